```python
import math
import jax, jax.numpy as jnp
from jax import lax
import numpy as np

D_MODEL = 1024
BATCH = 8
SEQ = 2048
DEPTH = 1
DEC_BATCH = 128
DEC_SEQ = 4
PAST_LEN = 16384
PAGE_SIZE = 128

W_A = D_MODEL // 2
H_A = 8
BW_A = W_A // H_A
CONV_A = 4
C_RG = 8.0
W_B = D_MODEL - W_A
H_B = 4
HD_B = W_B // H_B
CHUNK = 128
MIX_W = W_A + W_B
D_FF = 3 * D_MODEL
CONV_F = 3
PLE_DIM = 256
EPS = 1e-6

kernel_name = "hymba_style_rglru_sgu_convffn_step"


def _rmsnorm(x, g):
    x32 = x.astype(jnp.float32)
    y = x32 * lax.rsqrt(jnp.mean(x32 * x32, axis=-1, keepdims=True) + EPS)
    return y.astype(x.dtype) * g


def _layernorm(x, g, b):
    x32 = x.astype(jnp.float32)
    mu = jnp.mean(x32, axis=-1, keepdims=True)
    xc = x32 - mu
    y = xc * lax.rsqrt(jnp.mean(xc * xc, axis=-1, keepdims=True) + EPS)
    return y.astype(x.dtype) * g + b


def _causal_dwconv(x, buf, w, b):
    k = w.shape[0]
    L = x.shape[1]
    xp = jnp.concatenate([buf.astype(x.dtype), x], axis=1)
    out = b
    for j in range(k):
        out = out + xp[:, j:j + L] * w[j]
    return out, xp[:, xp.shape[1] - (k - 1):]


def _rglru(x, h0, wa, ba, wx, bx, a_param, reset_first):
    B, L, W = x.shape
    xb = x.reshape(B, L, H_A, BW_A)
    r = jax.nn.sigmoid(jnp.einsum('blhi,hij->blhj', xb, wa).reshape(B, L, W) + ba)
    i = jax.nn.sigmoid(jnp.einsum('blhi,hij->blhj', xb, wx).reshape(B, L, W) + bx)
    log_a = (-C_RG * r.astype(jnp.float32)) * jax.nn.softplus(-a_param.astype(jnp.float32))
    a = jnp.exp(log_a)
    mult = jnp.sqrt(-jnp.expm1(2.0 * log_a))
    if reset_first:
        mult = mult.at[:, 0].set(1.0)
    u = x.astype(jnp.float32) * i.astype(jnp.float32) * mult

    def step(h, inp):
        a_t, u_t = inp
        h = a_t * h + u_t
        return h, h

    hT, hs = lax.scan(step, h0.astype(jnp.float32), (jnp.swapaxes(a, 0, 1), jnp.swapaxes(u, 0, 1)))
    return jnp.swapaxes(hs, 0, 1).astype(x.dtype), hT


def _chunk_sgu(u, v, w_s, b_s):
    B, L, _ = v.shape
    cl = min(L, CHUNK)
    nc = L // cl
    mask = jnp.tril(jnp.ones((cl, cl), dtype=bool))
    w = jnp.where(mask, w_s[:, :cl, :cl], 0.0)
    vc = v.reshape(B, nc, cl, H_B, HD_B)
    mixed = jnp.einsum('hts,bcshd->bcthd', w, vc) + jnp.transpose(b_s[:, :cl])[None, None, :, :, None]
    return u * mixed.reshape(B, L, W_B)


def _layer(h, p, h0, conv_buf, ffn_buf, reset_first, lw):
    n1 = _rmsnorm(h, lw['g_mix_norm'])
    z = n1 @ lw['w_in']
    xa = z[..., :W_A]
    ga = z[..., W_A:2 * W_A]
    ub = z[..., 2 * W_A:2 * W_A + W_B]
    vb = z[..., 2 * W_A + W_B:]
    xa_c, conv_new = _causal_dwconv(xa, conv_buf, lw['conv_a_w'], lw['conv_a_b'])
    ya, hT = _rglru(xa_c, h0, lw['lru_wa'], lw['lru_ba'], lw['lru_wx'], lw['lru_bx'],
                    lw['lru_a_param'], reset_first)
    ya = _rmsnorm(ya * jax.nn.gelu(ga), lw['g_out_a'])
    vn = _layernorm(jax.nn.gelu(vb), lw['ln_v_g'], lw['ln_v_b'])
    yb = _rmsnorm(_chunk_sgu(jax.nn.gelu(ub), vn, lw['sgu_w'], lw['sgu_b']), lw['g_out_b'])
    h = h + jnp.concatenate([ya, yb], axis=-1) @ lw['w_out']
    n2 = _rmsnorm(h, lw['g_ffn_norm'])
    up = n2 @ lw['w_up']
    up_c, ffn_new = _causal_dwconv(up, ffn_buf, lw['ffn_conv_w'], lw['ffn_conv_b'])
    h = h + (jax.nn.gelu(up_c[..., :D_FF]) * up_c[..., D_FF:]) @ lw['w_down']
    gate = jax.nn.sigmoid(_rmsnorm(h, lw['g_ple_norm']) @ lw['w_ple_gate'])
    h = h + (p @ lw['w_ple']) * gate
    return h, hT, conv_new, vn, ffn_new


def setup_inputs(seed: int = 0) -> dict:
    key = jax.random.key(seed)
    ks = jax.random.split(key, 40)
    f32 = jnp.float32
    nrm = lambda k, s, sc: jax.random.normal(k, s, f32) * sc
    gain = lambda k, s: 1.0 + 0.02 * jax.random.normal(k, s, f32)
    u = jax.random.uniform(ks[12], (DEPTH, W_A), f32, minval=0.9, maxval=0.999)
    return {
        "x_prompt": nrm(ks[0], (BATCH, SEQ, D_MODEL), 1.0),
        "x_sample": nrm(ks[1], (DEC_BATCH, DEC_SEQ, D_MODEL), 1.0),
        "p_prompt": nrm(ks[2], (DEPTH, BATCH, SEQ, PLE_DIM), 1.0),
        "p_sample": nrm(ks[3], (DEPTH, DEC_BATCH, DEC_SEQ, PLE_DIM), 1.0),
        "state_rglru_h": nrm(ks[4], (DEPTH, DEC_BATCH, W_A), 0.5),
        "state_rglru_conv": nrm(ks[5], (DEPTH, DEC_BATCH, CONV_A - 1, W_A), 1.0),
        "state_ffn_conv": nrm(ks[6], (DEPTH, DEC_BATCH, CONV_F - 1, 2 * D_FF), 1.0),
        "g_mix_norm": gain(ks[7], (DEPTH, D_MODEL)),
        "w_in": nrm(ks[8], (DEPTH, D_MODEL, 2 * W_A + 2 * W_B), D_MODEL ** -0.5),
        "conv_a_w": nrm(ks[9], (DEPTH, CONV_A, W_A), CONV_A ** -0.5),
        "conv_a_b": nrm(ks[10], (DEPTH, W_A), 0.02),
        "lru_wa": nrm(ks[11], (DEPTH, H_A, BW_A, BW_A), BW_A ** -0.5),
        "lru_ba": nrm(ks[13], (DEPTH, W_A), 0.02),
        "lru_wx": nrm(ks[14], (DEPTH, H_A, BW_A, BW_A), BW_A ** -0.5),
        "lru_bx": nrm(ks[15], (DEPTH, W_A), 0.02),
        "lru_a_param": jnp.log(u) - jnp.log1p(-u),
        "g_out_a": gain(ks[16], (DEPTH, W_A)),
        "ln_v_g": gain(ks[17], (DEPTH, W_B)),
        "ln_v_b": nrm(ks[18], (DEPTH, W_B), 0.02),
        "sgu_w": nrm(ks[19], (DEPTH, H_B, CHUNK, CHUNK), CHUNK ** -0.5),
        "sgu_b": gain(ks[20], (DEPTH, H_B, CHUNK)),
        "g_out_b": gain(ks[21], (DEPTH, W_B)),
        "w_out": nrm(ks[22], (DEPTH, MIX_W, D_MODEL), MIX_W ** -0.5),
        "g_ffn_norm": gain(ks[23], (DEPTH, D_MODEL)),
        "w_up": nrm(ks[24], (DEPTH, D_MODEL, 2 * D_FF), D_MODEL ** -0.5),
        "ffn_conv_w": nrm(ks[25], (DEPTH, CONV_F, 2 * D_FF), CONV_F ** -0.5),
        "ffn_conv_b": nrm(ks[26], (DEPTH, 2 * D_FF), 0.02),
        "w_down": nrm(ks[27], (DEPTH, D_FF, D_MODEL), D_FF ** -0.5),
        "g_ple_norm": gain(ks[28], (DEPTH, D_MODEL)),
        "w_ple_gate": nrm(ks[29], (DEPTH, D_MODEL, D_MODEL), D_MODEL ** -0.5),
        "w_ple": nrm(ks[30], (DEPTH, PLE_DIM, D_MODEL), PLE_DIM ** -0.5),
        "g_final": gain(ks[31], (D_MODEL,)),
    }


def reference(x_prompt, x_sample, p_prompt, p_sample, state_rglru_h, state_rglru_conv,
              state_ffn_conv, g_mix_norm, w_in, conv_a_w, conv_a_b, lru_wa, lru_ba, lru_wx,
              lru_bx, lru_a_param, g_out_a, ln_v_g, ln_v_b, sgu_w, sgu_b, g_out_b, w_out,
              g_ffn_norm, w_up, ffn_conv_w, ffn_conv_b, w_down, g_ple_norm, w_ple_gate,
              w_ple, g_final):
    hp = x_prompt
    hs = x_sample
    Bp = x_prompt.shape[0]
    hP, hS, cP, cS, vS, fP, fS = [], [], [], [], [], [], []
    for i in range(DEPTH):
        lw = {
            'g_mix_norm': g_mix_norm[i], 'w_in': w_in[i], 'conv_a_w': conv_a_w[i],
            'conv_a_b': conv_a_b[i], 'lru_wa': lru_wa[i], 'lru_ba': lru_ba[i],
            'lru_wx': lru_wx[i], 'lru_bx': lru_bx[i], 'lru_a_param': lru_a_param[i],
            'g_out_a': g_out_a[i], 'ln_v_g': ln_v_g[i], 'ln_v_b': ln_v_b[i],
            'sgu_w': sgu_w[i], 'sgu_b': sgu_b[i], 'g_out_b': g_out_b[i], 'w_out': w_out[i],
            'g_ffn_norm': g_ffn_norm[i], 'w_up': w_up[i], 'ffn_conv_w': ffn_conv_w[i],
            'ffn_conv_b': ffn_conv_b[i], 'w_down': w_down[i], 'g_ple_norm': g_ple_norm[i],
            'w_ple_gate': w_ple_gate[i], 'w_ple': w_ple[i],
        }
        z_h = jnp.zeros((Bp, W_A), jnp.float32)
        z_c = jnp.zeros((Bp, CONV_A - 1, W_A), hp.dtype)
        z_f = jnp.zeros((Bp, CONV_F - 1, 2 * D_FF), hp.dtype)
        hp, h_p, c_p, _, f_p = _layer(hp, p_prompt[i], z_h, z_c, z_f, True, lw)
        hs, h_s, c_s, v_s, f_s = _layer(hs, p_sample[i], state_rglru_h[i], state_rglru_conv[i],
                                        state_ffn_conv[i], False, lw)
        hP.append(h_p); hS.append(h_s); cP.append(c_p); cS.append(c_s)
        vS.append(v_s); fP.append(f_p); fS.append(f_s)
    y_prompt = _rmsnorm(hp, g_final)
    y_sample = _rmsnorm(hs, g_final)
    new_rglru_h_prompt = jnp.stack(hP)
    new_rglru_h_sample = jnp.stack(hS)
    new_rglru_conv_prompt = jnp.stack(cP)
    new_rglru_conv_sample = jnp.stack(cS)
    new_sgu_v_sample = jnp.stack(vS)
    new_ffn_conv_prompt = jnp.stack(fP)
    new_ffn_conv_sample = jnp.stack(fS)
    return (y_prompt, y_sample, new_rglru_h_prompt, new_rglru_h_sample, new_rglru_conv_prompt,
            new_rglru_conv_sample, new_sgu_v_sample, new_ffn_conv_prompt, new_ffn_conv_sample)
```

```python
import functools

import jax
import jax.numpy as jnp
from jax import lax
from jax.experimental import pallas as pl
from jax.experimental.pallas import tpu as pltpu

F32 = jnp.float32
BF16 = jnp.bfloat16

D_MODEL = 1024
W_A = 512
W_B = 512
H_A = 8
BW_A = W_A // H_A
H_B = 4
HD_B = W_B // H_B
CHUNK = 128
D_FF = 3072
CONV_A = 4
CONV_F = 3
C_RG = 8.0
EPS = 1e-6

SUBLANES = 8
HIST = SUBLANES
GATE_HALF = W_A // 2
FF_CHUNK = 512
N_FF_CHUNKS = D_FF // FF_CHUNK
SLAB = 32
PROMPT_BLOCK_ROWS = 256
SAMPLE_GROUP = 32
VMEM_LIMIT_BYTES = 56 * 1024 * 1024

assert 2 * FF_CHUNK == D_MODEL


def _rms(x, g):
    ms = jnp.mean(x * x, axis=-1, keepdims=True)
    return x * lax.rsqrt(ms + EPS) * g


def _layernorm(x, g, b):
    mu = jnp.mean(x, axis=-1, keepdims=True)
    xc = x - mu
    return xc * lax.rsqrt(jnp.mean(xc * xc, axis=-1, keepdims=True) + EPS) * g + b


def _sigmoid(x):
    return 1.0 / (1.0 + jnp.exp(-x))


def _softplus(x):
    return jnp.maximum(x, 0.0) + jnp.log(1.0 + jnp.exp(-jnp.abs(x)))


def _dot(a, b):
    return jnp.dot(a, b, preferred_element_type=F32)


def _norm_to_bf16(src_ref, g_ref, dst_ref, rows):
    g = g_ref[...]
    for r0 in range(0, rows, SLAB):
        dst_ref[r0:r0 + SLAB, :] = _rms(src_ref[r0:r0 + SLAB, :], g).astype(BF16)


def _gate_stage(xc_ref, xcb_ref, g_ref, a_ref, u_ref, w_gate, ba, bx, apar, rows, reset_row):
    for hf in range(2):
        c0 = hf * GATE_HALF
        g_ref[:, 2 * c0:2 * c0 + 2 * GATE_HALF] = _dot(xcb_ref[:, c0:c0 + GATE_HALF], w_gate[hf])
    sp = _softplus(-apar[...])
    for r0 in range(0, rows, SLAB):
        for hf in range(2):
            c0 = hf * GATE_HALF
            cs = slice(c0, c0 + GATE_HALF)
            r = _sigmoid(g_ref[r0:r0 + SLAB, 2 * c0:2 * c0 + GATE_HALF] + ba[:, cs])
            i = _sigmoid(g_ref[r0:r0 + SLAB, 2 * c0 + GATE_HALF:2 * c0 + 2 * GATE_HALF] + bx[:, cs])
            a = jnp.exp((-C_RG) * r * sp[:, cs])
            mult = jnp.sqrt(1.0 - a * a)
            if reset_row is not None and r0 == 0:
                row = lax.broadcasted_iota(jnp.int32, (SLAB, GATE_HALF), 0)
                mult = jnp.where(row == reset_row, 1.0, mult)
            a_ref[r0:r0 + SLAB, cs] = a
            u_ref[r0:r0 + SLAB, cs] = xc_ref[r0:r0 + SLAB, cs] * i * mult


def _branch_a_out(hs_ref, z_ref, g_oa, ymix_ref, rows):
    g = g_oa[...]
    for r0 in range(0, rows, SLAB):
        ga = z_ref[r0:r0 + SLAB, 0:W_A]
        ymix_ref[r0:r0 + SLAB, 0:W_A] = _rms(hs_ref[r0:r0 + SLAB, :] * jax.nn.gelu(ga), g).astype(BF16)


def _vn_stage(z_ref, lng, lnb, rows, write):
    g = lng[...]
    b = lnb[...]
    for r0 in range(0, rows, SLAB):
        vb = z_ref[r0:r0 + SLAB, W_A + W_B:W_A + 2 * W_B]
        write(r0, _layernorm(jax.nn.gelu(vb), g, b))


def _ffn_and_ple(h_ref, n_ref, up_ref, act_ref, acc_ref, z_ref, load_p, g_ffn, w_up, fcw, fcb,
                 w_down, g_ple, w_pg, w_ple, g_fin, rows, shift, hist_rows, load_hist, store_hist,
                 write_y):
    _norm_to_bf16(h_ref, g_ffn, n_ref, rows)
    acc_ref[...] = h_ref[...]

    def chunk(c, carry):
        load_hist(c)
        up_ref[hist_rows:hist_rows + rows, :] = _dot(n_ref[...], w_up[c])
        w = fcw[c]
        b = fcb[c]
        for r0 in range(0, rows, SLAB):
            halves = []
            for cs in (slice(0, FF_CHUNK), slice(FF_CHUNK, 2 * FF_CHUNK)):
                cv = b[:, cs]
                for j in range(CONV_F):
                    off = hist_rows + r0 - (CONV_F - 1 - j) * shift
                    cv = cv + up_ref[off:off + SLAB, cs] * w[j:j + 1, cs]
                halves.append(cv)
            act_ref[r0:r0 + SLAB, :] = (jax.nn.gelu(halves[0]) * halves[1]).astype(BF16)
        store_hist(c)
        acc_ref[...] += _dot(act_ref[...], w_down[c])
        return carry

    lax.fori_loop(0, N_FF_CHUNKS, chunk, 0)

    _norm_to_bf16(acc_ref, g_ple, n_ref, rows)
    z_ref[:, 0:D_MODEL] = _dot(n_ref[...], w_pg[...])
    up_ref[0:rows, :] = _dot(load_p().astype(BF16), w_ple[...])
    gf = g_fin[...]
    for r0 in range(0, rows, SLAB):
        gate = _sigmoid(z_ref[r0:r0 + SLAB, 0:D_MODEL])
        h3 = acc_ref[r0:r0 + SLAB, :] + up_ref[r0:r0 + SLAB, :] * gate
        write_y(r0, _rms(h3, gf))


def _prompt_kernel(x_ref, p_ref, g_mix, w_in, caw, cab, w_gate, ba, bx, apar, g_oa, lng, lnb, sgw,
                   sgb, g_ob, w_out, g_ffn, w_up, fcw, fcb, w_down, g_ple, w_pg, w_ple, g_fin,
                   y_ref, ht_ref, cnew_ref, fnew_ref,
                   n_ref, z_ref, xa_ref, xc_ref, xcb_ref, g_ref, a_ref, u_ref, hs_ref, carry_ref,
                   vn_ref, ymix_ref, h_ref, up_ref, fhist_ref, act_ref, acc_ref, *, tm):
    t = pl.program_id(1)
    last_t = pl.num_programs(1) - 1

    @pl.when(t == 0)
    def _():
        xa_ref[0:HIST, :] = jnp.zeros((HIST, W_A), F32)
        carry_ref[...] = jnp.zeros((SUBLANES, W_A), F32)
        fhist_ref[...] = jnp.zeros(fhist_ref.shape, F32)

    h_ref[...] = x_ref[0]
    _norm_to_bf16(h_ref, g_mix, n_ref, tm)
    xa_ref[HIST:HIST + tm, :] = _dot(n_ref[...], w_in[:, 0:W_A])
    z_ref[...] = _dot(n_ref[...], w_in[:, W_A:])

    cw = caw[...]
    cb = cab[...]
    for r0 in range(0, tm, SLAB):
        cv = cb
        for j in range(CONV_A):
            off = HIST + r0 - (CONV_A - 1 - j)
            cv = cv + xa_ref[off:off + SLAB, :] * cw[j:j + 1, :]
        xc_ref[r0:r0 + SLAB, :] = cv
        xcb_ref[r0:r0 + SLAB, :] = cv.astype(BF16)

    @pl.when(t == last_t)
    def _():
        cnew_ref[0] = xa_ref[HIST + tm - (CONV_A - 1):HIST + tm, :]

    xa_ref[0:HIST, :] = xa_ref[tm:tm + HIST, :]

    reset_row = jnp.where(t == 0, 0, -1)
    _gate_stage(xc_ref, xcb_ref, g_ref, a_ref, u_ref, w_gate, ba, bx, apar, tm, reset_row)

    row = lax.broadcasted_iota(jnp.int32, (SUBLANES, W_A), 0)

    def scan_group(g, cbr):
        r0 = pl.multiple_of(g * SUBLANES, SUBLANES)
        a = a_ref[pl.ds(r0, SUBLANES), :]
        u = u_ref[pl.ds(r0, SUBLANES), :]
        for d in (1, 2, 4):
            keep = row >= d
            a_s = jnp.where(keep, pltpu.roll(a, d, 0), 1.0)
            u_s = jnp.where(keep, pltpu.roll(u, d, 0), 0.0)
            u = a * u_s + u
            a = a * a_s
        h = a * cbr + u
        hs_ref[pl.ds(r0, SUBLANES), :] = h
        return jnp.broadcast_to(h[SUBLANES - 1:SUBLANES, :], (SUBLANES, W_A))

    carry_ref[...] = lax.fori_loop(0, tm // SUBLANES, scan_group, carry_ref[...], unroll=4)

    @pl.when(t == last_t)
    def _():
        ht_ref[0] = carry_ref[0:1, :]

    _branch_a_out(hs_ref, z_ref, g_oa, ymix_ref, tm)

    def write_vn(r0, vn):
        vn_ref[r0:r0 + SLAB, :] = vn.astype(BF16)

    _vn_stage(z_ref, lng, lnb, tm, write_vn)

    ti = lax.broadcasted_iota(jnp.int32, (CHUNK, CHUNK), 0)
    si = lax.broadcasted_iota(jnp.int32, (CHUNK, CHUNK), 1)
    gob = g_ob[...]
    wmix = [jnp.where(si <= ti, sgw[hd], 0.0).astype(BF16) for hd in range(H_B)]
    for c0 in range(0, tm, CHUNK):
        heads = []
        for hd in range(H_B):
            cs = slice(hd * HD_B, (hd + 1) * HD_B)
            mixed = _dot(wmix[hd], vn_ref[c0:c0 + CHUNK, cs]) + sgb[:, cs]
            ub = z_ref[c0:c0 + CHUNK, W_A + hd * HD_B:W_A + (hd + 1) * HD_B]
            heads.append(jax.nn.gelu(ub) * mixed)
        yb = jnp.concatenate(heads, axis=-1)
        ymix_ref[c0:c0 + CHUNK, W_A:] = _rms(yb, gob).astype(BF16)

    h_ref[...] += _dot(ymix_ref[...], w_out[...])

    def load_hist(c):
        up_ref[0:HIST, :] = fhist_ref[c]

    def store_hist(c):
        fhist_ref[c] = up_ref[tm:tm + HIST, :]

        @pl.when(t == last_t)
        def _():
            fnew_ref[0, c] = up_ref[HIST + tm - (CONV_F - 1):HIST + tm, :]

    def write_y(r0, y):
        y_ref[0, r0:r0 + SLAB, :] = y

    _ffn_and_ple(h_ref, n_ref, up_ref, act_ref, acc_ref, z_ref, lambda: p_ref[0], g_ffn,
                 w_up, fcw, fcb, w_down, g_ple, w_pg, w_ple, g_fin, tm, 1, HIST, load_hist,
                 store_hist, write_y)


def _sample_kernel(x_ref, p_ref, h0_ref, chist_ref, fh_ref, g_mix, w_in, caw, cab, w_gate, ba, bx,
                   apar, g_oa, lng, lnb, sgw, sgb, g_ob, w_out, g_ffn, w_up, fcw, fcb, w_down,
                   g_ple, w_pg, w_ple, g_fin,
                   y_ref, ht_ref, cnew_ref, vn_out_ref, fnew_ref,
                   n_ref, z_ref, xa_ref, xc_ref, xcb_ref, g_ref, a_ref, u_ref, hs_ref, ymix_ref,
                   h_ref, p_buf, up_ref, act_ref, acc_ref, *, nb, steps):
    rows = nb * steps
    ahist = (CONV_A - 1) * nb
    fhist = (CONV_F - 1) * nb

    for s in range(steps):
        h_ref[s * nb:(s + 1) * nb, :] = x_ref[s]
        p_buf[s * nb:(s + 1) * nb, :] = p_ref[s]
    _norm_to_bf16(h_ref, g_mix, n_ref, rows)
    for k in range(CONV_A - 1):
        xa_ref[k * nb:(k + 1) * nb, :] = chist_ref[k]
    xa_ref[ahist:ahist + rows, :] = _dot(n_ref[...], w_in[:, 0:W_A])
    z_ref[...] = _dot(n_ref[...], w_in[:, W_A:])

    cw = caw[...]
    cb = cab[...]
    for r0 in range(0, rows, SLAB):
        cv = cb
        for j in range(CONV_A):
            off = r0 + j * nb
            cv = cv + xa_ref[off:off + SLAB, :] * cw[j:j + 1, :]
        xc_ref[r0:r0 + SLAB, :] = cv
        xcb_ref[r0:r0 + SLAB, :] = cv.astype(BF16)
    for k in range(CONV_A - 1):
        cnew_ref[k] = xa_ref[rows + k * nb:rows + (k + 1) * nb, :]

    _gate_stage(xc_ref, xcb_ref, g_ref, a_ref, u_ref, w_gate, ba, bx, apar, rows, None)

    for b0 in range(0, nb, SLAB):
        h = h0_ref[b0:b0 + SLAB, :]
        for s in range(steps):
            r0 = s * nb + b0
            h = a_ref[r0:r0 + SLAB, :] * h + u_ref[r0:r0 + SLAB, :]
            hs_ref[r0:r0 + SLAB, :] = h
        ht_ref[b0:b0 + SLAB, :] = h

    _branch_a_out(hs_ref, z_ref, g_oa, ymix_ref, rows)

    def write_vn(r0, vn):
        vn_out_ref[r0 // nb, r0 % nb:r0 % nb + SLAB, :] = vn

    _vn_stage(z_ref, lng, lnb, rows, write_vn)

    gob = g_ob[...]
    for b0 in range(0, nb, SLAB):
        for tt in range(steps):
            r0 = tt * nb + b0
            mixed = sgb[tt:tt + 1, :]
            for s in range(tt + 1):
                k = tt * steps + s
                mixed = mixed + vn_out_ref[s, b0:b0 + SLAB, :] * sgw[k:k + 1, :]
            yb = jax.nn.gelu(z_ref[r0:r0 + SLAB, W_A:W_A + W_B]) * mixed
            ymix_ref[r0:r0 + SLAB, W_A:] = _rms(yb, gob).astype(BF16)

    h_ref[...] += _dot(ymix_ref[...], w_out[...])

    def load_hist(c):
        for k in range(CONV_F - 1):
            up_ref[k * nb:(k + 1) * nb, :] = fh_ref[c, k]

    def store_hist(c):
        for k in range(CONV_F - 1):
            fnew_ref[c, k] = up_ref[rows + k * nb:rows + (k + 1) * nb, :]

    def write_y(r0, y):
        y_ref[r0 // nb, r0 % nb:r0 % nb + SLAB, :] = y

    _ffn_and_ple(h_ref, n_ref, up_ref, act_ref, acc_ref, z_ref, lambda: p_buf[...], g_ffn,
                 w_up, fcw, fcb, w_down, g_ple, w_pg, w_ple, g_fin, rows, nb, fhist, load_hist,
                 store_hist, write_y)


def _const_spec(shape, grid_rank):
    zeros = (0,) * len(shape)
    if grid_rank == 2:
        index_map = lambda b, t: zeros
    else:
        index_map = lambda i: zeros
    return pl.BlockSpec(shape, index_map, pipeline_mode=pl.Buffered(1))


def _block_diag_gate(wa, wx):
    hh = H_A // 2
    eye = jnp.eye(hh, dtype=wa.dtype)

    def bd(w):
        return jnp.einsum('hij,hg->higj', w, eye).reshape(GATE_HALF, GATE_HALF)

    return jnp.stack([jnp.concatenate([bd(wa[h * hh:(h + 1) * hh]), bd(wx[h * hh:(h + 1) * hh])], axis=1)
                      for h in range(2)])


def _ff_cols_to_chunks(a):
    lead = a.shape[:-1]
    a = a.reshape(lead + (2, N_FF_CHUNKS, FF_CHUNK))
    a = jnp.moveaxis(a, -2, 0)
    return a.reshape((N_FF_CHUNKS,) + lead + (2 * FF_CHUNK,))


def _ff_chunks_to_cols(a):
    lead = a.shape[1:-1]
    a = a.reshape((N_FF_CHUNKS,) + lead + (2, FF_CHUNK))
    a = jnp.moveaxis(a, 0, -2)
    return a.reshape(lead + (2 * D_FF,))


def kernel(x_prompt, x_sample, p_prompt, p_sample, state_rglru_h, state_rglru_conv, state_ffn_conv, g_mix_norm, w_in, conv_a_w, conv_a_b, lru_wa, lru_ba, lru_wx, lru_bx, lru_a_param, g_out_a, ln_v_g, ln_v_b, sgu_w, sgu_b, g_out_b, w_out, g_ffn_norm, w_up, ffn_conv_w, ffn_conv_b, w_down, g_ple_norm, w_ple_gate, w_ple, g_final):
    assert w_in.shape[0] == 1
    nbp, seq, _ = x_prompt.shape
    nbs, steps, _ = x_sample.shape
    ple = p_prompt.shape[-1]
    tm = PROMPT_BLOCK_ROWS
    nb = SAMPLE_GROUP
    assert seq % tm == 0 and tm % CHUNK == 0 and nbs % nb == 0 and nb % SLAB == 0 and steps <= CHUNK
    row = lambda v: v.reshape(1, -1)

    head = [row(g_mix_norm[0]), w_in[0].astype(BF16), conv_a_w[0], row(conv_a_b[0]),
            _block_diag_gate(lru_wa[0], lru_wx[0]).astype(BF16), row(lru_ba[0]), row(lru_bx[0]),
            row(lru_a_param[0]), row(g_out_a[0]), row(ln_v_g[0]), row(ln_v_b[0])]
    tail = [row(g_out_b[0]), w_out[0].astype(BF16), row(g_ffn_norm[0]),
            _ff_cols_to_chunks(w_up[0]).astype(BF16), _ff_cols_to_chunks(ffn_conv_w[0]),
            _ff_cols_to_chunks(row(ffn_conv_b[0])),
            w_down[0].reshape(N_FF_CHUNKS, FF_CHUNK, D_MODEL).astype(BF16),
            row(g_ple_norm[0]), w_ple_gate[0].astype(BF16), w_ple[0].astype(BF16), row(g_final)]

    sgb_p = jnp.repeat(jnp.transpose(sgu_b[0]), HD_B, axis=1)
    p_args = head + [sgu_w[0], sgb_p] + tail
    scratch_p = [
        pltpu.VMEM((tm, D_MODEL), BF16),
        pltpu.VMEM((tm, D_MODEL + W_A), F32),
        pltpu.VMEM((HIST + tm, W_A), F32),
        pltpu.VMEM((tm, W_A), F32),
        pltpu.VMEM((tm, W_A), BF16),
        pltpu.VMEM((tm, 2 * W_A), F32),
        pltpu.VMEM((tm, W_A), F32),
        pltpu.VMEM((tm, W_A), F32),
        pltpu.VMEM((tm, W_A), F32),
        pltpu.VMEM((SUBLANES, W_A), F32),
        pltpu.VMEM((tm, W_B), BF16),
        pltpu.VMEM((tm, W_A + W_B), BF16),
        pltpu.VMEM((tm, D_MODEL), F32),
        pltpu.VMEM((HIST + tm, 2 * FF_CHUNK), F32),
        pltpu.VMEM((N_FF_CHUNKS, HIST, 2 * FF_CHUNK), F32),
        pltpu.VMEM((tm, FF_CHUNK), BF16),
        pltpu.VMEM((tm, D_MODEL), F32),
    ]
    y_p, ht_p, cnew_p, fnew_p = pl.pallas_call(
        functools.partial(_prompt_kernel, tm=tm),
        grid=(nbp, seq // tm),
        in_specs=[pl.BlockSpec((1, tm, D_MODEL), lambda b, t: (b, t, 0)),
                  pl.BlockSpec((1, tm, ple), lambda b, t: (b, t, 0))]
                 + [_const_spec(a.shape, 2) for a in p_args],
        out_specs=[pl.BlockSpec((1, tm, D_MODEL), lambda b, t: (b, t, 0)),
                   pl.BlockSpec((1, 1, W_A), lambda b, t: (b, 0, 0)),
                   pl.BlockSpec((1, CONV_A - 1, W_A), lambda b, t: (b, 0, 0)),
                   pl.BlockSpec((1, N_FF_CHUNKS, CONV_F - 1, 2 * FF_CHUNK), lambda b, t: (b, 0, 0, 0))],
        out_shape=[jax.ShapeDtypeStruct((nbp, seq, D_MODEL), F32),
                   jax.ShapeDtypeStruct((nbp, 1, W_A), F32),
                   jax.ShapeDtypeStruct((nbp, CONV_A - 1, W_A), F32),
                   jax.ShapeDtypeStruct((nbp, N_FF_CHUNKS, CONV_F - 1, 2 * FF_CHUNK), F32)],
        scratch_shapes=scratch_p,
        compiler_params=pltpu.CompilerParams(dimension_semantics=("arbitrary", "arbitrary"),
                                             vmem_limit_bytes=VMEM_LIMIT_BYTES),
        name="prompt_layer",
    )(x_prompt, p_prompt[0], *p_args)
    fnew_p = _ff_chunks_to_cols(jnp.moveaxis(fnew_p, 1, 0))

    rows = nb * steps
    ahist = (CONV_A - 1) * nb
    fhist = (CONV_F - 1) * nb
    tmaj = lambda a: jnp.swapaxes(a, 0, 1)
    fh = _ff_cols_to_chunks(tmaj(state_ffn_conv[0]))
    sgw_s = jnp.repeat(jnp.transpose(sgu_w[0, :, :steps, :steps], (1, 2, 0)).reshape(steps * steps, H_B),
                       HD_B, axis=1)
    sgb_s = jnp.repeat(jnp.transpose(sgu_b[0, :, :steps]), HD_B, axis=1)
    w_args = head + [sgw_s, sgb_s] + tail
    scratch_s = [
        pltpu.VMEM((rows, D_MODEL), BF16),
        pltpu.VMEM((rows, D_MODEL + W_A), F32),
        pltpu.VMEM((ahist + rows, W_A), F32),
        pltpu.VMEM((rows, W_A), F32),
        pltpu.VMEM((rows, W_A), BF16),
        pltpu.VMEM((rows, 2 * W_A), F32),
        pltpu.VMEM((rows, W_A), F32),
        pltpu.VMEM((rows, W_A), F32),
        pltpu.VMEM((rows, W_A), F32),
        pltpu.VMEM((rows, W_A + W_B), BF16),
        pltpu.VMEM((rows, D_MODEL), F32),
        pltpu.VMEM((rows, ple), F32),
        pltpu.VMEM((fhist + rows, 2 * FF_CHUNK), F32),
        pltpu.VMEM((rows, FF_CHUNK), BF16),
        pltpu.VMEM((rows, D_MODEL), F32),
    ]
    y_s, ht_s, cnew_s, vn_s, fnew_s = pl.pallas_call(
        functools.partial(_sample_kernel, nb=nb, steps=steps),
        grid=(nbs // nb,),
        in_specs=[pl.BlockSpec((steps, nb, D_MODEL), lambda i: (0, i, 0)),
                  pl.BlockSpec((steps, nb, ple), lambda i: (0, i, 0)),
                  pl.BlockSpec((nb, W_A), lambda i: (i, 0)),
                  pl.BlockSpec((CONV_A - 1, nb, W_A), lambda i: (0, i, 0)),
                  pl.BlockSpec((N_FF_CHUNKS, CONV_F - 1, nb, 2 * FF_CHUNK), lambda i: (0, 0, i, 0))]
                 + [_const_spec(a.shape, 1) for a in w_args],
        out_specs=[pl.BlockSpec((steps, nb, D_MODEL), lambda i: (0, i, 0)),
                   pl.BlockSpec((nb, W_A), lambda i: (i, 0)),
                   pl.BlockSpec((CONV_A - 1, nb, W_A), lambda i: (0, i, 0)),
                   pl.BlockSpec((steps, nb, W_B), lambda i: (0, i, 0)),
                   pl.BlockSpec((N_FF_CHUNKS, CONV_F - 1, nb, 2 * FF_CHUNK), lambda i: (0, 0, i, 0))],
        out_shape=[jax.ShapeDtypeStruct((steps, nbs, D_MODEL), F32),
                   jax.ShapeDtypeStruct((nbs, W_A), F32),
                   jax.ShapeDtypeStruct((CONV_A - 1, nbs, W_A), F32),
                   jax.ShapeDtypeStruct((steps, nbs, W_B), F32),
                   jax.ShapeDtypeStruct((N_FF_CHUNKS, CONV_F - 1, nbs, 2 * FF_CHUNK), F32)],
        scratch_shapes=scratch_s,
        compiler_params=pltpu.CompilerParams(dimension_semantics=("arbitrary",),
                                             vmem_limit_bytes=VMEM_LIMIT_BYTES),
        name="sample_layer",
    )(tmaj(x_sample), tmaj(p_sample[0]), state_rglru_h[0], tmaj(state_rglru_conv[0]), fh, *w_args)
    fnew_s = tmaj(_ff_chunks_to_cols(fnew_s))

    return (y_p, tmaj(y_s), tmaj(ht_p), ht_s[None], cnew_p[None], tmaj(cnew_s)[None],
            tmaj(vn_s)[None], fnew_p[None], fnew_s[None])
```

```python
import functools

import jax
import jax.numpy as jnp
from jax import lax
from jax.experimental import pallas as pl
from jax.experimental.pallas import tpu as pltpu

F32 = jnp.float32
BF16 = jnp.bfloat16

D_MODEL = 1024
W_A = 512
W_B = 512
H_A = 8
BW_A = W_A // H_A
H_B = 4
HD_B = W_B // H_B
CHUNK = 128
D_FF = 3072
CONV_A = 4
CONV_F = 3
C_RG = 8.0
EPS = 1e-6

SUBLANES = 8
HIST = SUBLANES
GATE_HALF = W_A // 2
FF_CHUNK = 512
N_FF_CHUNKS = D_FF // FF_CHUNK
SLAB = 32
MXU_PIECE = 256
PROMPT_BLOCK_ROWS = 256
SAMPLE_GROUP = 32
VMEM_LIMIT_BYTES = 56 * 1024 * 1024

assert 2 * FF_CHUNK == D_MODEL


def _rms(x, g):
    ms = jnp.mean(x * x, axis=-1, keepdims=True)
    return x * lax.rsqrt(ms + EPS) * g


def _layernorm(x, g, b):
    mu = jnp.mean(x, axis=-1, keepdims=True)
    xc = x - mu
    return xc * lax.rsqrt(jnp.mean(xc * xc, axis=-1, keepdims=True) + EPS) * g + b


def _sigmoid(x):
    return 1.0 / (1.0 + jnp.exp(-x))


def _softplus(x):
    return jnp.maximum(x, 0.0) + jnp.log(1.0 + jnp.exp(-jnp.abs(x)))


def _dot(a, b):
    return jnp.dot(a, b, preferred_element_type=F32)


def _norm_to_bf16(src_ref, g_ref, dst_ref, rows):
    g = g_ref[...]
    for r0 in range(0, rows, SLAB):
        dst_ref[r0:r0 + SLAB, :] = _rms(src_ref[r0:r0 + SLAB, :], g).astype(BF16)


def _gate_stage(xc_ref, xcb_ref, g_ref, a_ref, u_ref, w_gate, ba, bx, apar, rows, reset_row):
    for hf in range(2):
        c0 = hf * GATE_HALF
        g_ref[:, 2 * c0:2 * c0 + 2 * GATE_HALF] = _dot(xcb_ref[:, c0:c0 + GATE_HALF], w_gate[hf])
    sp = _softplus(-apar[...])
    for r0 in range(0, rows, SLAB):
        for hf in range(2):
            c0 = hf * GATE_HALF
            cs = slice(c0, c0 + GATE_HALF)
            r = _sigmoid(g_ref[r0:r0 + SLAB, 2 * c0:2 * c0 + GATE_HALF] + ba[:, cs])
            i = _sigmoid(g_ref[r0:r0 + SLAB, 2 * c0 + GATE_HALF:2 * c0 + 2 * GATE_HALF] + bx[:, cs])
            a = jnp.exp((-C_RG) * r * sp[:, cs])
            mult = jnp.sqrt(1.0 - a * a)
            if reset_row is not None and r0 == 0:
                row = lax.broadcasted_iota(jnp.int32, (SLAB, GATE_HALF), 0)
                mult = jnp.where(row == reset_row, 1.0, mult)
            a_ref[r0:r0 + SLAB, cs] = a
            u_ref[r0:r0 + SLAB, cs] = xc_ref[r0:r0 + SLAB, cs] * i * mult


def _branch_a_out(hs_ref, z_ref, g_oa, ymix_ref, rows):
    g = g_oa[...]
    for r0 in range(0, rows, SLAB):
        ga = z_ref[r0:r0 + SLAB, 0:W_A]
        ymix_ref[r0:r0 + SLAB, 0:W_A] = _rms(hs_ref[r0:r0 + SLAB, :] * jax.nn.gelu(ga), g).astype(BF16)


def _vn_stage(z_ref, lng, lnb, rows, write):
    g = lng[...]
    b = lnb[...]
    for r0 in range(0, rows, SLAB):
        vb = z_ref[r0:r0 + SLAB, W_A + W_B:W_A + 2 * W_B]
        write(r0, _layernorm(jax.nn.gelu(vb), g, b))


def _ffn_and_ple(h_ref, n_ref, up2_ref, act2_ref, acc_ref, z_ref, pe_ref, load_p, g_ffn, w_up, fcw, fcb,
                 w_down, g_ple, w_pg, w_ple, g_fin, rows, shift, hist_rows, load_hist, store_hist,
                 write_y):
    _norm_to_bf16(h_ref, g_ffn, n_ref, rows)
    acc_ref[...] = h_ref[...]

    def up_piece(c, j):
        up_ref = up2_ref[c % 2]
        cs = slice(j * MXU_PIECE, (j + 1) * MXU_PIECE)
        if j == 0:
            load_hist(c, up_ref)
        up_ref[hist_rows:hist_rows + rows, cs] = _dot(n_ref[...], w_up[c, :, cs])

    def down_piece(c, j):
        cs = slice(j * MXU_PIECE, (j + 1) * MXU_PIECE)
        acc_ref[:, cs] += _dot(act2_ref[c % 2][...], w_down[c, :, cs])

    def conv_act_slab(c, r0):
        up_ref = up2_ref[c % 2]
        w = fcw[c]
        b = fcb[c]
        halves = []
        for cs in (slice(0, FF_CHUNK), slice(FF_CHUNK, 2 * FF_CHUNK)):
            cv = b[:, cs]
            for j in range(CONV_F):
                off = hist_rows + r0 - (CONV_F - 1 - j) * shift
                cv = cv + up_ref[off:off + SLAB, cs] * w[j:j + 1, cs]
            halves.append(cv)
        act2_ref[c % 2][r0:r0 + SLAB, :] = (jax.nn.gelu(halves[0]) * halves[1]).astype(BF16)

    n_up = 2 * FF_CHUNK // MXU_PIECE
    n_down = D_MODEL // MXU_PIECE
    for j in range(n_up):
        up_piece(0, j)
    pe_ref[...] = _dot(load_p().astype(BF16), w_ple[...])
    slabs = list(range(0, rows, SLAB))
    for c in range(N_FF_CHUNKS + 1):
        pieces = []
        for j in range(max(n_up, n_down)):
            if c + 1 < N_FF_CHUNKS and j < n_up:
                pieces.append(functools.partial(up_piece, c + 1, j))
            if c >= 1 and j < n_down:
                pieces.append(functools.partial(down_piece, c - 1, j))
        if c < N_FF_CHUNKS:
            done = 0
            for i, r0 in enumerate(slabs):
                conv_act_slab(c, r0)
                upto = (i + 1) * len(pieces) // len(slabs)
                for piece in pieces[done:upto]:
                    piece()
                done = upto
            store_hist(c, up2_ref[c % 2])
        else:
            for piece in pieces:
                piece()

    _norm_to_bf16(acc_ref, g_ple, n_ref, rows)
    z_ref[:, 0:D_MODEL] = _dot(n_ref[...], w_pg[...])
    gf = g_fin[...]
    for r0 in range(0, rows, SLAB):
        gate = _sigmoid(z_ref[r0:r0 + SLAB, 0:D_MODEL])
        h3 = acc_ref[r0:r0 + SLAB, :] + pe_ref[r0:r0 + SLAB, :] * gate
        write_y(r0, _rms(h3, gf))


def _prompt_kernel(x_ref, p_ref, g_mix, w_in, caw, cab, w_gate, ba, bx, apar, g_oa, lng, lnb, sgw,
                   sgb, g_ob, w_out, g_ffn, w_up, fcw, fcb, w_down, g_ple, w_pg, w_ple, g_fin,
                   y_ref, ht_ref, cnew_ref, fnew_ref,
                   n_ref, z_ref, xa_ref, xc_ref, xcb_ref, g_ref, a_ref, u_ref, hs_ref, carry_ref,
                   vn_ref, ymix_ref, h_ref, up_ref, upb_ref, fhist_ref, act_ref, actb_ref, acc_ref, *, tm):
    t = pl.program_id(1)

    @pl.when(t == 0)
    def _():
        xa_ref[0:HIST, :] = jnp.zeros((HIST, W_A), F32)
        carry_ref[...] = jnp.zeros((SUBLANES, W_A), F32)
        fhist_ref[...] = jnp.zeros(fhist_ref.shape, F32)

    h_ref[...] = x_ref[0]
    _norm_to_bf16(h_ref, g_mix, n_ref, tm)
    xa_ref[HIST:HIST + tm, :] = _dot(n_ref[...], w_in[:, 0:W_A])
    z_ref[...] = _dot(n_ref[...], w_in[:, W_A:])

    cw = caw[...]
    cb = cab[...]
    for r0 in range(0, tm, SLAB):
        cv = cb
        for j in range(CONV_A):
            off = HIST + r0 - (CONV_A - 1 - j)
            cv = cv + xa_ref[off:off + SLAB, :] * cw[j:j + 1, :]
        xc_ref[r0:r0 + SLAB, :] = cv
        xcb_ref[r0:r0 + SLAB, :] = cv.astype(BF16)

    cnew_ref[0] = xa_ref[HIST + tm - (CONV_A - 1):HIST + tm, :]

    xa_ref[0:HIST, :] = xa_ref[tm:tm + HIST, :]

    reset_row = jnp.where(t == 0, 0, -1)
    _gate_stage(xc_ref, xcb_ref, g_ref, a_ref, u_ref, w_gate, ba, bx, apar, tm, reset_row)

    row = lax.broadcasted_iota(jnp.int32, (SUBLANES, W_A), 0)

    def scan_group(g, cbr):
        r0 = pl.multiple_of(g * SUBLANES, SUBLANES)
        a = a_ref[pl.ds(r0, SUBLANES), :]
        u = u_ref[pl.ds(r0, SUBLANES), :]
        for d in (1, 2, 4):
            keep = row >= d
            a_s = jnp.where(keep, pltpu.roll(a, d, 0), 1.0)
            u_s = jnp.where(keep, pltpu.roll(u, d, 0), 0.0)
            u = a * u_s + u
            a = a * a_s
        h = a * cbr + u
        hs_ref[pl.ds(r0, SUBLANES), :] = h
        return jnp.broadcast_to(h[SUBLANES - 1:SUBLANES, :], (SUBLANES, W_A))

    carry_ref[...] = lax.fori_loop(0, tm // SUBLANES, scan_group, carry_ref[...], unroll=4)

    ht_ref[0] = carry_ref[0:1, :]

    _branch_a_out(hs_ref, z_ref, g_oa, ymix_ref, tm)

    def write_vn(r0, vn):
        vn_ref[r0:r0 + SLAB, :] = vn.astype(BF16)

    _vn_stage(z_ref, lng, lnb, tm, write_vn)

    ti = lax.broadcasted_iota(jnp.int32, (CHUNK, CHUNK), 0)
    si = lax.broadcasted_iota(jnp.int32, (CHUNK, CHUNK), 1)
    gob = g_ob[...]
    wmix = [jnp.where(si <= ti, sgw[hd], 0.0).astype(BF16) for hd in range(H_B)]
    for c0 in range(0, tm, CHUNK):
        heads = []
        for hd in range(H_B):
            cs = slice(hd * HD_B, (hd + 1) * HD_B)
            mixed = _dot(wmix[hd], vn_ref[c0:c0 + CHUNK, cs]) + sgb[:, cs]
            ub = z_ref[c0:c0 + CHUNK, W_A + hd * HD_B:W_A + (hd + 1) * HD_B]
            heads.append(jax.nn.gelu(ub) * mixed)
        yb = jnp.concatenate(heads, axis=-1)
        ymix_ref[c0:c0 + CHUNK, W_A:] = _rms(yb, gob).astype(BF16)

    h_ref[...] += _dot(ymix_ref[...], w_out[...])

    def load_hist(c, up_ref):
        up_ref[0:HIST, :] = fhist_ref[c]

    def store_hist(c, up_ref):
        fhist_ref[c] = up_ref[tm:tm + HIST, :]
        fnew_ref[0, c] = up_ref[HIST + tm - (CONV_F - 1):HIST + tm, :]

    def write_y(r0, y):
        y_ref[0, r0:r0 + SLAB, :] = y

    _ffn_and_ple(h_ref, n_ref, (up_ref, upb_ref), (act_ref, actb_ref), acc_ref, z_ref, g_ref,
                 lambda: p_ref[0], g_ffn,
                 w_up, fcw, fcb, w_down, g_ple, w_pg, w_ple, g_fin, tm, 1, HIST, load_hist,
                 store_hist, write_y)


def _sample_kernel(x_ref, p_ref, h0_ref, chist_ref, fh_ref, g_mix, w_in, caw, cab, w_gate, ba, bx,
                   apar, g_oa, lng, lnb, sgw, sgb, g_ob, w_out, g_ffn, w_up, fcw, fcb, w_down,
                   g_ple, w_pg, w_ple, g_fin,
                   y_ref, ht_ref, cnew_ref, vn_out_ref, fnew_ref,
                   n_ref, z_ref, xa_ref, xc_ref, xcb_ref, g_ref, a_ref, u_ref, hs_ref, ymix_ref,
                   h_ref, p_buf, up_ref, upb_ref, act_ref, actb_ref, acc_ref, *, nb, steps):
    rows = nb * steps
    ahist = (CONV_A - 1) * nb
    fhist = (CONV_F - 1) * nb

    for s in range(steps):
        h_ref[s * nb:(s + 1) * nb, :] = x_ref[s]
        p_buf[s * nb:(s + 1) * nb, :] = p_ref[s]
    _norm_to_bf16(h_ref, g_mix, n_ref, rows)
    for k in range(CONV_A - 1):
        xa_ref[k * nb:(k + 1) * nb, :] = chist_ref[k]
    xa_ref[ahist:ahist + rows, :] = _dot(n_ref[...], w_in[:, 0:W_A])
    z_ref[...] = _dot(n_ref[...], w_in[:, W_A:])

    cw = caw[...]
    cb = cab[...]
    for r0 in range(0, rows, SLAB):
        cv = cb
        for j in range(CONV_A):
            off = r0 + j * nb
            cv = cv + xa_ref[off:off + SLAB, :] * cw[j:j + 1, :]
        xc_ref[r0:r0 + SLAB, :] = cv
        xcb_ref[r0:r0 + SLAB, :] = cv.astype(BF16)
    for k in range(CONV_A - 1):
        cnew_ref[k] = xa_ref[rows + k * nb:rows + (k + 1) * nb, :]

    _gate_stage(xc_ref, xcb_ref, g_ref, a_ref, u_ref, w_gate, ba, bx, apar, rows, None)

    for b0 in range(0, nb, SLAB):
        h = h0_ref[b0:b0 + SLAB, :]
        for s in range(steps):
            r0 = s * nb + b0
            h = a_ref[r0:r0 + SLAB, :] * h + u_ref[r0:r0 + SLAB, :]
            hs_ref[r0:r0 + SLAB, :] = h
        ht_ref[b0:b0 + SLAB, :] = h

    _branch_a_out(hs_ref, z_ref, g_oa, ymix_ref, rows)

    def write_vn(r0, vn):
        vn_out_ref[r0 // nb, r0 % nb:r0 % nb + SLAB, :] = vn

    _vn_stage(z_ref, lng, lnb, rows, write_vn)

    gob = g_ob[...]
    for b0 in range(0, nb, SLAB):
        for tt in range(steps):
            r0 = tt * nb + b0
            mixed = sgb[tt:tt + 1, :]
            for s in range(tt + 1):
                k = tt * steps + s
                mixed = mixed + vn_out_ref[s, b0:b0 + SLAB, :] * sgw[k:k + 1, :]
            yb = jax.nn.gelu(z_ref[r0:r0 + SLAB, W_A:W_A + W_B]) * mixed
            ymix_ref[r0:r0 + SLAB, W_A:] = _rms(yb, gob).astype(BF16)

    h_ref[...] += _dot(ymix_ref[...], w_out[...])

    def load_hist(c, up_ref):
        for k in range(CONV_F - 1):
            up_ref[k * nb:(k + 1) * nb, :] = fh_ref[c, k]

    def store_hist(c, up_ref):
        for k in range(CONV_F - 1):
            fnew_ref[c, k] = up_ref[rows + k * nb:rows + (k + 1) * nb, :]

    def write_y(r0, y):
        y_ref[r0 // nb, r0 % nb:r0 % nb + SLAB, :] = y

    _ffn_and_ple(h_ref, n_ref, (up_ref, upb_ref), (act_ref, actb_ref), acc_ref, z_ref, g_ref,
                 lambda: p_buf[...], g_ffn,
                 w_up, fcw, fcb, w_down, g_ple, w_pg, w_ple, g_fin, rows, nb, fhist, load_hist,
                 store_hist, write_y)


def _const_spec(shape, grid_rank):
    zeros = (0,) * len(shape)
    if grid_rank == 2:
        index_map = lambda b, t: zeros
    else:
        index_map = lambda i: zeros
    return pl.BlockSpec(shape, index_map, pipeline_mode=pl.Buffered(1))


def _block_diag_gate(wa, wx):
    hh = H_A // 2
    eye = jnp.eye(hh, dtype=wa.dtype)

    def bd(w):
        return jnp.einsum('hij,hg->higj', w, eye).reshape(GATE_HALF, GATE_HALF)

    return jnp.stack([jnp.concatenate([bd(wa[h * hh:(h + 1) * hh]), bd(wx[h * hh:(h + 1) * hh])], axis=1)
                      for h in range(2)])


def _ff_cols_to_chunks(a):
    lead = a.shape[:-1]
    a = a.reshape(lead + (2, N_FF_CHUNKS, FF_CHUNK))
    a = jnp.moveaxis(a, -2, 0)
    return a.reshape((N_FF_CHUNKS,) + lead + (2 * FF_CHUNK,))


def _ff_chunks_to_cols(a):
    lead = a.shape[1:-1]
    a = a.reshape((N_FF_CHUNKS,) + lead + (2, FF_CHUNK))
    a = jnp.moveaxis(a, 0, -2)
    return a.reshape(lead + (2 * D_FF,))


def kernel(x_prompt, x_sample, p_prompt, p_sample, state_rglru_h, state_rglru_conv, state_ffn_conv, g_mix_norm, w_in, conv_a_w, conv_a_b, lru_wa, lru_ba, lru_wx, lru_bx, lru_a_param, g_out_a, ln_v_g, ln_v_b, sgu_w, sgu_b, g_out_b, w_out, g_ffn_norm, w_up, ffn_conv_w, ffn_conv_b, w_down, g_ple_norm, w_ple_gate, w_ple, g_final):
    assert w_in.shape[0] == 1
    nbp, seq, _ = x_prompt.shape
    nbs, steps, _ = x_sample.shape
    ple = p_prompt.shape[-1]
    tm = PROMPT_BLOCK_ROWS
    nb = SAMPLE_GROUP
    assert seq % tm == 0 and tm % CHUNK == 0 and nbs % nb == 0 and nb % SLAB == 0 and steps <= CHUNK
    row = lambda v: v.reshape(1, -1)

    head = [row(g_mix_norm[0]), w_in[0].astype(BF16), conv_a_w[0], row(conv_a_b[0]),
            _block_diag_gate(lru_wa[0], lru_wx[0]).astype(BF16), row(lru_ba[0]), row(lru_bx[0]),
            row(lru_a_param[0]), row(g_out_a[0]), row(ln_v_g[0]), row(ln_v_b[0])]
    tail = [row(g_out_b[0]), w_out[0].astype(BF16), row(g_ffn_norm[0]),
            _ff_cols_to_chunks(w_up[0]).astype(BF16), _ff_cols_to_chunks(ffn_conv_w[0]),
            _ff_cols_to_chunks(row(ffn_conv_b[0])),
            w_down[0].reshape(N_FF_CHUNKS, FF_CHUNK, D_MODEL).astype(BF16),
            row(g_ple_norm[0]), w_ple_gate[0].astype(BF16), w_ple[0].astype(BF16), row(g_final)]

    sgb_p = jnp.repeat(jnp.transpose(sgu_b[0]), HD_B, axis=1)
    p_args = head + [sgu_w[0], sgb_p] + tail
    scratch_p = [
        pltpu.VMEM((tm, D_MODEL), BF16),
        pltpu.VMEM((tm, D_MODEL + W_A), F32),
        pltpu.VMEM((HIST + tm, W_A), F32),
        pltpu.VMEM((tm, W_A), F32),
        pltpu.VMEM((tm, W_A), BF16),
        pltpu.VMEM((tm, 2 * W_A), F32),
        pltpu.VMEM((tm, W_A), F32),
        pltpu.VMEM((tm, W_A), F32),
        pltpu.VMEM((tm, W_A), F32),
        pltpu.VMEM((SUBLANES, W_A), F32),
        pltpu.VMEM((tm, W_B), BF16),
        pltpu.VMEM((tm, W_A + W_B), BF16),
        pltpu.VMEM((tm, D_MODEL), F32),
        pltpu.VMEM((HIST + tm, 2 * FF_CHUNK), F32),
        pltpu.VMEM((HIST + tm, 2 * FF_CHUNK), F32),
        pltpu.VMEM((N_FF_CHUNKS, HIST, 2 * FF_CHUNK), F32),
        pltpu.VMEM((tm, FF_CHUNK), BF16),
        pltpu.VMEM((tm, FF_CHUNK), BF16),
        pltpu.VMEM((tm, D_MODEL), F32),
    ]
    y_p, ht_p, cnew_p, fnew_p = pl.pallas_call(
        functools.partial(_prompt_kernel, tm=tm),
        grid=(nbp, seq // tm),
        in_specs=[pl.BlockSpec((1, tm, D_MODEL), lambda b, t: (b, t, 0)),
                  pl.BlockSpec((1, tm, ple), lambda b, t: (b, t, 0))]
                 + [_const_spec(a.shape, 2) for a in p_args],
        out_specs=[pl.BlockSpec((1, tm, D_MODEL), lambda b, t: (b, t, 0)),
                   pl.BlockSpec((1, 1, W_A), lambda b, t: (b, 0, 0)),
                   pl.BlockSpec((1, CONV_A - 1, W_A), lambda b, t: (b, 0, 0)),
                   pl.BlockSpec((1, N_FF_CHUNKS, CONV_F - 1, 2 * FF_CHUNK), lambda b, t: (b, 0, 0, 0))],
        out_shape=[jax.ShapeDtypeStruct((nbp, seq, D_MODEL), F32),
                   jax.ShapeDtypeStruct((nbp, 1, W_A), F32),
                   jax.ShapeDtypeStruct((nbp, CONV_A - 1, W_A), F32),
                   jax.ShapeDtypeStruct((nbp, N_FF_CHUNKS, CONV_F - 1, 2 * FF_CHUNK), F32)],
        scratch_shapes=scratch_p,
        compiler_params=pltpu.CompilerParams(dimension_semantics=("arbitrary", "arbitrary"),
                                             vmem_limit_bytes=VMEM_LIMIT_BYTES),
        name="prompt_layer",
    )(x_prompt, p_prompt[0], *p_args)
    fnew_p = _ff_chunks_to_cols(jnp.moveaxis(fnew_p, 1, 0))

    rows = nb * steps
    ahist = (CONV_A - 1) * nb
    fhist = (CONV_F - 1) * nb
    tmaj = lambda a: jnp.swapaxes(a, 0, 1)
    fh = _ff_cols_to_chunks(tmaj(state_ffn_conv[0]))
    sgw_s = jnp.repeat(jnp.transpose(sgu_w[0, :, :steps, :steps], (1, 2, 0)).reshape(steps * steps, H_B),
                       HD_B, axis=1)
    sgb_s = jnp.repeat(jnp.transpose(sgu_b[0, :, :steps]), HD_B, axis=1)
    w_args = head + [sgw_s, sgb_s] + tail
    scratch_s = [
        pltpu.VMEM((rows, D_MODEL), BF16),
        pltpu.VMEM((rows, D_MODEL + W_A), F32),
        pltpu.VMEM((ahist + rows, W_A), F32),
        pltpu.VMEM((rows, W_A), F32),
        pltpu.VMEM((rows, W_A), BF16),
        pltpu.VMEM((rows, 2 * W_A), F32),
        pltpu.VMEM((rows, W_A), F32),
        pltpu.VMEM((rows, W_A), F32),
        pltpu.VMEM((rows, W_A), F32),
        pltpu.VMEM((rows, W_A + W_B), BF16),
        pltpu.VMEM((rows, D_MODEL), F32),
        pltpu.VMEM((rows, ple), F32),
        pltpu.VMEM((fhist + rows, 2 * FF_CHUNK), F32),
        pltpu.VMEM((fhist + rows, 2 * FF_CHUNK), F32),
        pltpu.VMEM((rows, FF_CHUNK), BF16),
        pltpu.VMEM((rows, FF_CHUNK), BF16),
        pltpu.VMEM((rows, D_MODEL), F32),
    ]
    y_s, ht_s, cnew_s, vn_s, fnew_s = pl.pallas_call(
        functools.partial(_sample_kernel, nb=nb, steps=steps),
        grid=(nbs // nb,),
        in_specs=[pl.BlockSpec((steps, nb, D_MODEL), lambda i: (0, i, 0)),
                  pl.BlockSpec((steps, nb, ple), lambda i: (0, i, 0)),
                  pl.BlockSpec((nb, W_A), lambda i: (i, 0)),
                  pl.BlockSpec((CONV_A - 1, nb, W_A), lambda i: (0, i, 0)),
                  pl.BlockSpec((N_FF_CHUNKS, CONV_F - 1, nb, 2 * FF_CHUNK), lambda i: (0, 0, i, 0))]
                 + [_const_spec(a.shape, 1) for a in w_args],
        out_specs=[pl.BlockSpec((steps, nb, D_MODEL), lambda i: (0, i, 0)),
                   pl.BlockSpec((nb, W_A), lambda i: (i, 0)),
                   pl.BlockSpec((CONV_A - 1, nb, W_A), lambda i: (0, i, 0)),
                   pl.BlockSpec((steps, nb, W_B), lambda i: (0, i, 0)),
                   pl.BlockSpec((N_FF_CHUNKS, CONV_F - 1, nb, 2 * FF_CHUNK), lambda i: (0, 0, i, 0))],
        out_shape=[jax.ShapeDtypeStruct((steps, nbs, D_MODEL), F32),
                   jax.ShapeDtypeStruct((nbs, W_A), F32),
                   jax.ShapeDtypeStruct((CONV_A - 1, nbs, W_A), F32),
                   jax.ShapeDtypeStruct((steps, nbs, W_B), F32),
                   jax.ShapeDtypeStruct((N_FF_CHUNKS, CONV_F - 1, nbs, 2 * FF_CHUNK), F32)],
        scratch_shapes=scratch_s,
        compiler_params=pltpu.CompilerParams(dimension_semantics=("arbitrary",),
                                             vmem_limit_bytes=VMEM_LIMIT_BYTES),
        name="sample_layer",
    )(tmaj(x_sample), tmaj(p_sample[0]), state_rglru_h[0], tmaj(state_rglru_conv[0]), fh, *w_args)
    fnew_s = tmaj(_ff_chunks_to_cols(fnew_s))

    return (y_p, tmaj(y_s), tmaj(ht_p), ht_s[None], cnew_p[None], tmaj(cnew_s)[None],
            tmaj(vn_s)[None], fnew_p[None], fnew_s[None])
```

```python
import functools

import jax
import jax.numpy as jnp
from jax import lax
from jax.experimental import pallas as pl
from jax.experimental.pallas import tpu as pltpu

F32 = jnp.float32
BF16 = jnp.bfloat16

D_MODEL = 1024
W_A = 512
W_B = 512
H_A = 8
BW_A = W_A // H_A
H_B = 4
HD_B = W_B // H_B
CHUNK = 128
D_FF = 3072
CONV_A = 4
CONV_F = 3
C_RG = 8.0
EPS = 1e-6

SUBLANES = 8
HIST = SUBLANES
GATE_HALF = W_A // 2
FF_CHUNK = 512
N_FF_CHUNKS = D_FF // FF_CHUNK
SLAB = 32
MXU_PIECE = 256
PROMPT_BLOCK_ROWS = 256
SAMPLE_GROUP = 32
VMEM_LIMIT_BYTES = 56 * 1024 * 1024

assert 2 * FF_CHUNK == D_MODEL


def _rms(x, g):
    ms = jnp.mean(x * x, axis=-1, keepdims=True)
    return x * lax.rsqrt(ms + EPS) * g


def _layernorm(x, g, b):
    mu = jnp.mean(x, axis=-1, keepdims=True)
    xc = x - mu
    return xc * lax.rsqrt(jnp.mean(xc * xc, axis=-1, keepdims=True) + EPS) * g + b


def _sigmoid(x):
    return 1.0 / (1.0 + jnp.exp(-x))


def _softplus(x):
    return jnp.maximum(x, 0.0) + jnp.log(1.0 + jnp.exp(-jnp.abs(x)))


def _dot(a, b):
    return jnp.dot(a, b, preferred_element_type=F32)


def _ff_cols(c, half):
    start = half * D_FF + c * FF_CHUNK
    return slice(start, start + FF_CHUNK)


def _norm_to_bf16(src_ref, g_ref, dst_ref, rows):
    g = g_ref[...]
    for r0 in range(0, rows, SLAB):
        dst_ref[r0:r0 + SLAB, :] = _rms(src_ref[r0:r0 + SLAB, :], g).astype(BF16)


def _gate_stage(xc_ref, xcb_ref, g_ref, a_ref, u_ref, w_gate, ba, bx, apar, rows, reset_row):
    for hf in range(2):
        c0 = hf * GATE_HALF
        g_ref[:, 2 * c0:2 * c0 + 2 * GATE_HALF] = _dot(xcb_ref[:, c0:c0 + GATE_HALF], w_gate[hf])
    sp = _softplus(-apar[...])
    for r0 in range(0, rows, SLAB):
        for hf in range(2):
            c0 = hf * GATE_HALF
            cs = slice(c0, c0 + GATE_HALF)
            r = _sigmoid(g_ref[r0:r0 + SLAB, 2 * c0:2 * c0 + GATE_HALF] + ba[:, cs])
            i = _sigmoid(g_ref[r0:r0 + SLAB, 2 * c0 + GATE_HALF:2 * c0 + 2 * GATE_HALF] + bx[:, cs])
            a = jnp.exp((-C_RG) * r * sp[:, cs])
            mult = jnp.sqrt(1.0 - a * a)
            if reset_row is not None and r0 == 0:
                row = lax.broadcasted_iota(jnp.int32, (SLAB, GATE_HALF), 0)
                mult = jnp.where(row == reset_row, 1.0, mult)
            a_ref[r0:r0 + SLAB, cs] = a
            u_ref[r0:r0 + SLAB, cs] = xc_ref[r0:r0 + SLAB, cs] * i * mult


def _branch_a_out(hs_ref, z_ref, g_oa, ymix_ref, rows):
    g = g_oa[...]
    for r0 in range(0, rows, SLAB):
        ga = z_ref[r0:r0 + SLAB, 0:W_A]
        ymix_ref[r0:r0 + SLAB, 0:W_A] = _rms(hs_ref[r0:r0 + SLAB, :] * jax.nn.gelu(ga), g).astype(BF16)


def _vn_stage(z_ref, lng, lnb, rows, write):
    g = lng[...]
    b = lnb[...]
    for r0 in range(0, rows, SLAB):
        vb = z_ref[r0:r0 + SLAB, W_A + W_B:W_A + 2 * W_B]
        write(r0, _layernorm(jax.nn.gelu(vb), g, b))


def _ffn_and_ple(h_ref, n_ref, up2_ref, act2_ref, acc_ref, z_ref, pe_ref, load_p, g_ffn, w_up, fcw, fcb,
                 w_down, g_ple, w_pg, w_ple, g_fin, rows, shift, hist_rows, load_hist, store_hist,
                 write_y):
    _norm_to_bf16(h_ref, g_ffn, n_ref, rows)
    acc_ref[...] = h_ref[...]

    def up_piece(c, j):
        up_ref = up2_ref[c % 2]
        cs = slice(j * MXU_PIECE, (j + 1) * MXU_PIECE)
        half, off = divmod(j * MXU_PIECE, FF_CHUNK)
        w0 = half * D_FF + c * FF_CHUNK + off
        if j == 0:
            load_hist(c, up_ref)
        up_ref[hist_rows:hist_rows + rows, cs] = _dot(n_ref[...], w_up[:, w0:w0 + MXU_PIECE])

    def down_piece(c, j):
        cs = slice(j * MXU_PIECE, (j + 1) * MXU_PIECE)
        acc_ref[:, cs] += _dot(act2_ref[c % 2][...], w_down[c * FF_CHUNK:(c + 1) * FF_CHUNK, cs])

    def conv_act_slab(c, r0):
        up_ref = up2_ref[c % 2]
        halves = []
        for half in range(2):
            cs = slice(half * FF_CHUNK, (half + 1) * FF_CHUNK)
            ws = _ff_cols(c, half)
            cv = fcb[:, ws]
            for j in range(CONV_F):
                off = hist_rows + r0 - (CONV_F - 1 - j) * shift
                cv = cv + up_ref[off:off + SLAB, cs] * fcw[j:j + 1, ws]
            halves.append(cv)
        act2_ref[c % 2][r0:r0 + SLAB, :] = (jax.nn.gelu(halves[0]) * halves[1]).astype(BF16)

    n_up = 2 * FF_CHUNK // MXU_PIECE
    n_down = D_MODEL // MXU_PIECE
    for j in range(n_up):
        up_piece(0, j)
    pe_ref[...] = _dot(load_p().astype(BF16), w_ple[...])
    slabs = list(range(0, rows, SLAB))
    for c in range(N_FF_CHUNKS + 1):
        pieces = []
        for j in range(max(n_up, n_down)):
            if c + 1 < N_FF_CHUNKS and j < n_up:
                pieces.append(functools.partial(up_piece, c + 1, j))
            if c >= 1 and j < n_down:
                pieces.append(functools.partial(down_piece, c - 1, j))
        if c < N_FF_CHUNKS:
            done = 0
            for i, r0 in enumerate(slabs):
                conv_act_slab(c, r0)
                upto = (i + 1) * len(pieces) // len(slabs)
                for piece in pieces[done:upto]:
                    piece()
                done = upto
            store_hist(c, up2_ref[c % 2])
        else:
            for piece in pieces:
                piece()

    _norm_to_bf16(acc_ref, g_ple, n_ref, rows)
    z_ref[:, 0:D_MODEL] = _dot(n_ref[...], w_pg[...])
    gf = g_fin[...]
    for r0 in range(0, rows, SLAB):
        gate = _sigmoid(z_ref[r0:r0 + SLAB, 0:D_MODEL])
        h3 = acc_ref[r0:r0 + SLAB, :] + pe_ref[r0:r0 + SLAB, :] * gate
        write_y(r0, _rms(h3, gf))


def _prompt_kernel(x_ref, p_ref, g_mix, w_in, caw, cab, w_gate, ba, bx, apar, g_oa, lng, lnb, sgw,
                   sgb, g_ob, w_out, g_ffn, w_up, fcw, fcb, w_down, g_ple, w_pg, w_ple, g_fin,
                   y_ref, ht_ref, cnew_ref, fnew_ref,
                   n_ref, z_ref, xa_ref, xc_ref, xcb_ref, g_ref, a_ref, u_ref, hs_ref, carry_ref,
                   vn_ref, ymix_ref, h_ref, up_ref, upb_ref, fhist_ref, act_ref, actb_ref, acc_ref, *, tm):
    t = pl.program_id(1)

    @pl.when(t == 0)
    def _():
        xa_ref[0:HIST, :] = jnp.zeros((HIST, W_A), F32)
        carry_ref[...] = jnp.zeros((SUBLANES, W_A), F32)
        fhist_ref[...] = jnp.zeros(fhist_ref.shape, F32)

    h_ref[...] = x_ref[0]
    _norm_to_bf16(h_ref, g_mix, n_ref, tm)
    xa_ref[HIST:HIST + tm, :] = _dot(n_ref[...], w_in[:, 0:W_A])
    z_ref[...] = _dot(n_ref[...], w_in[:, W_A:])

    cw = caw[...]
    cb = cab[...]
    for r0 in range(0, tm, SLAB):
        cv = cb
        for j in range(CONV_A):
            off = HIST + r0 - (CONV_A - 1 - j)
            cv = cv + xa_ref[off:off + SLAB, :] * cw[j:j + 1, :]
        xc_ref[r0:r0 + SLAB, :] = cv
        xcb_ref[r0:r0 + SLAB, :] = cv.astype(BF16)

    cnew_ref[0] = xa_ref[HIST + tm - (CONV_A - 1):HIST + tm, :]

    xa_ref[0:HIST, :] = xa_ref[tm:tm + HIST, :]

    reset_row = jnp.where(t == 0, 0, -1)
    _gate_stage(xc_ref, xcb_ref, g_ref, a_ref, u_ref, w_gate, ba, bx, apar, tm, reset_row)

    row = lax.broadcasted_iota(jnp.int32, (SUBLANES, W_A), 0)

    def scan_group(g, cbr):
        r0 = pl.multiple_of(g * SUBLANES, SUBLANES)
        a = a_ref[pl.ds(r0, SUBLANES), :]
        u = u_ref[pl.ds(r0, SUBLANES), :]
        for d in (1, 2, 4):
            keep = row >= d
            a_s = jnp.where(keep, pltpu.roll(a, d, 0), 1.0)
            u_s = jnp.where(keep, pltpu.roll(u, d, 0), 0.0)
            u = a * u_s + u
            a = a * a_s
        h = a * cbr + u
        hs_ref[pl.ds(r0, SUBLANES), :] = h
        return jnp.broadcast_to(h[SUBLANES - 1:SUBLANES, :], (SUBLANES, W_A))

    carry_ref[...] = lax.fori_loop(0, tm // SUBLANES, scan_group, carry_ref[...], unroll=4)

    ht_ref[0] = carry_ref[0:1, :]

    _branch_a_out(hs_ref, z_ref, g_oa, ymix_ref, tm)

    def write_vn(r0, vn):
        vn_ref[r0:r0 + SLAB, :] = vn.astype(BF16)

    _vn_stage(z_ref, lng, lnb, tm, write_vn)

    ti = lax.broadcasted_iota(jnp.int32, (CHUNK, CHUNK), 0)
    si = lax.broadcasted_iota(jnp.int32, (CHUNK, CHUNK), 1)
    gob = g_ob[...]
    wmix = [jnp.where(si <= ti, sgw[hd], 0.0).astype(BF16) for hd in range(H_B)]
    for c0 in range(0, tm, CHUNK):
        heads = []
        for hd in range(H_B):
            cs = slice(hd * HD_B, (hd + 1) * HD_B)
            mixed = _dot(wmix[hd], vn_ref[c0:c0 + CHUNK, cs]) + sgb[:, cs]
            ub = z_ref[c0:c0 + CHUNK, W_A + hd * HD_B:W_A + (hd + 1) * HD_B]
            heads.append(jax.nn.gelu(ub) * mixed)
        yb = jnp.concatenate(heads, axis=-1)
        ymix_ref[c0:c0 + CHUNK, W_A:] = _rms(yb, gob).astype(BF16)

    h_ref[...] += _dot(ymix_ref[...], w_out[...])

    def load_hist(c, up_ref):
        for half in range(2):
            up_ref[0:HIST, half * FF_CHUNK:(half + 1) * FF_CHUNK] = fhist_ref[:, _ff_cols(c, half)]

    def store_hist(c, up_ref):
        for half in range(2):
            cs = slice(half * FF_CHUNK, (half + 1) * FF_CHUNK)
            fhist_ref[:, _ff_cols(c, half)] = up_ref[tm:tm + HIST, cs]
            fnew_ref[0, :, _ff_cols(c, half)] = up_ref[HIST + tm - (CONV_F - 1):HIST + tm, cs]

    def write_y(r0, y):
        y_ref[0, r0:r0 + SLAB, :] = y

    _ffn_and_ple(h_ref, n_ref, (up_ref, upb_ref), (act_ref, actb_ref), acc_ref, z_ref, g_ref,
                 lambda: p_ref[0], g_ffn,
                 w_up, fcw, fcb, w_down, g_ple, w_pg, w_ple, g_fin, tm, 1, HIST, load_hist,
                 store_hist, write_y)


def _sample_kernel(x_ref, p_ref, h0_ref, chist_ref, fh_ref, g_mix, w_in, caw, cab, w_gate, ba, bx,
                   apar, g_oa, lng, lnb, sgw, sgb, g_ob, w_out, g_ffn, w_up, fcw, fcb, w_down,
                   g_ple, w_pg, w_ple, g_fin,
                   y_ref, ht_ref, cnew_ref, vn_out_ref, fnew_ref,
                   n_ref, z_ref, xa_ref, xc_ref, xcb_ref, g_ref, a_ref, u_ref, hs_ref, ymix_ref,
                   h_ref, p_buf, up_ref, upb_ref, act_ref, actb_ref, acc_ref, *, nb, steps):
    rows = nb * steps
    ahist = (CONV_A - 1) * nb
    fhist = (CONV_F - 1) * nb

    for s in range(steps):
        h_ref[s * nb:(s + 1) * nb, :] = x_ref[s]
        p_buf[s * nb:(s + 1) * nb, :] = p_ref[s]
    _norm_to_bf16(h_ref, g_mix, n_ref, rows)
    for k in range(CONV_A - 1):
        xa_ref[k * nb:(k + 1) * nb, :] = chist_ref[k]
    xa_ref[ahist:ahist + rows, :] = _dot(n_ref[...], w_in[:, 0:W_A])
    z_ref[...] = _dot(n_ref[...], w_in[:, W_A:])

    cw = caw[...]
    cb = cab[...]
    for r0 in range(0, rows, SLAB):
        cv = cb
        for j in range(CONV_A):
            off = r0 + j * nb
            cv = cv + xa_ref[off:off + SLAB, :] * cw[j:j + 1, :]
        xc_ref[r0:r0 + SLAB, :] = cv
        xcb_ref[r0:r0 + SLAB, :] = cv.astype(BF16)
    for k in range(CONV_A - 1):
        cnew_ref[k] = xa_ref[rows + k * nb:rows + (k + 1) * nb, :]

    _gate_stage(xc_ref, xcb_ref, g_ref, a_ref, u_ref, w_gate, ba, bx, apar, rows, None)

    for b0 in range(0, nb, SLAB):
        h = h0_ref[b0:b0 + SLAB, :]
        for s in range(steps):
            r0 = s * nb + b0
            h = a_ref[r0:r0 + SLAB, :] * h + u_ref[r0:r0 + SLAB, :]
            hs_ref[r0:r0 + SLAB, :] = h
        ht_ref[b0:b0 + SLAB, :] = h

    _branch_a_out(hs_ref, z_ref, g_oa, ymix_ref, rows)

    def write_vn(r0, vn):
        vn_out_ref[r0 // nb, r0 % nb:r0 % nb + SLAB, :] = vn

    _vn_stage(z_ref, lng, lnb, rows, write_vn)

    gob = g_ob[...]
    for b0 in range(0, nb, SLAB):
        for tt in range(steps):
            r0 = tt * nb + b0
            mixed = sgb[tt:tt + 1, :]
            for s in range(tt + 1):
                k = tt * steps + s
                mixed = mixed + vn_out_ref[s, b0:b0 + SLAB, :] * sgw[k:k + 1, :]
            yb = jax.nn.gelu(z_ref[r0:r0 + SLAB, W_A:W_A + W_B]) * mixed
            ymix_ref[r0:r0 + SLAB, W_A:] = _rms(yb, gob).astype(BF16)

    h_ref[...] += _dot(ymix_ref[...], w_out[...])

    def load_hist(c, up_ref):
        for k in range(CONV_F - 1):
            for half in range(2):
                cs = slice(half * FF_CHUNK, (half + 1) * FF_CHUNK)
                up_ref[k * nb:(k + 1) * nb, cs] = fh_ref[k, :, _ff_cols(c, half)]

    def store_hist(c, up_ref):
        for k in range(CONV_F - 1):
            for half in range(2):
                cs = slice(half * FF_CHUNK, (half + 1) * FF_CHUNK)
                fnew_ref[k, :, _ff_cols(c, half)] = up_ref[rows + k * nb:rows + (k + 1) * nb, cs]

    def write_y(r0, y):
        y_ref[r0 // nb, r0 % nb:r0 % nb + SLAB, :] = y

    _ffn_and_ple(h_ref, n_ref, (up_ref, upb_ref), (act_ref, actb_ref), acc_ref, z_ref, g_ref,
                 lambda: p_buf[...], g_ffn,
                 w_up, fcw, fcb, w_down, g_ple, w_pg, w_ple, g_fin, rows, nb, fhist, load_hist,
                 store_hist, write_y)


def _const_spec(shape, grid_rank):
    zeros = (0,) * len(shape)
    if grid_rank == 2:
        index_map = lambda b, t: zeros
    else:
        index_map = lambda i: zeros
    return pl.BlockSpec(shape, index_map, pipeline_mode=pl.Buffered(1))


def _block_diag_gate(wa, wx):
    hh = H_A // 2
    eye = jnp.eye(hh, dtype=wa.dtype)

    def bd(w):
        return jnp.einsum('hij,hg->higj', w, eye).reshape(GATE_HALF, GATE_HALF)

    return jnp.stack([jnp.concatenate([bd(wa[h * hh:(h + 1) * hh]), bd(wx[h * hh:(h + 1) * hh])], axis=1)
                      for h in range(2)])


def kernel(x_prompt, x_sample, p_prompt, p_sample, state_rglru_h, state_rglru_conv, state_ffn_conv, g_mix_norm, w_in, conv_a_w, conv_a_b, lru_wa, lru_ba, lru_wx, lru_bx, lru_a_param, g_out_a, ln_v_g, ln_v_b, sgu_w, sgu_b, g_out_b, w_out, g_ffn_norm, w_up, ffn_conv_w, ffn_conv_b, w_down, g_ple_norm, w_ple_gate, w_ple, g_final):
    assert w_in.shape[0] == 1
    nbp, seq, _ = x_prompt.shape
    nbs, steps, _ = x_sample.shape
    ple = p_prompt.shape[-1]
    tm = PROMPT_BLOCK_ROWS
    nb = SAMPLE_GROUP
    assert seq % tm == 0 and tm % CHUNK == 0 and nbs % nb == 0 and nb % SLAB == 0 and steps <= CHUNK
    row = lambda v: v.reshape(1, -1)

    head = [row(g_mix_norm[0]), w_in[0].astype(BF16), conv_a_w[0], row(conv_a_b[0]),
            _block_diag_gate(lru_wa[0], lru_wx[0]).astype(BF16), row(lru_ba[0]), row(lru_bx[0]),
            row(lru_a_param[0]), row(g_out_a[0]), row(ln_v_g[0]), row(ln_v_b[0])]
    tail = [row(g_out_b[0]), w_out[0].astype(BF16), row(g_ffn_norm[0]),
            w_up[0].astype(BF16), ffn_conv_w[0], row(ffn_conv_b[0]), w_down[0].astype(BF16),
            row(g_ple_norm[0]), w_ple_gate[0].astype(BF16), w_ple[0].astype(BF16), row(g_final)]

    sgb_p = jnp.repeat(jnp.transpose(sgu_b[0]), HD_B, axis=1)
    p_args = head + [sgu_w[0], sgb_p] + tail
    scratch_p = [
        pltpu.VMEM((tm, D_MODEL), BF16),
        pltpu.VMEM((tm, D_MODEL + W_A), F32),
        pltpu.VMEM((HIST + tm, W_A), F32),
        pltpu.VMEM((tm, W_A), F32),
        pltpu.VMEM((tm, W_A), BF16),
        pltpu.VMEM((tm, 2 * W_A), F32),
        pltpu.VMEM((tm, W_A), F32),
        pltpu.VMEM((tm, W_A), F32),
        pltpu.VMEM((tm, W_A), F32),
        pltpu.VMEM((SUBLANES, W_A), F32),
        pltpu.VMEM((tm, W_B), BF16),
        pltpu.VMEM((tm, W_A + W_B), BF16),
        pltpu.VMEM((tm, D_MODEL), F32),
        pltpu.VMEM((HIST + tm, 2 * FF_CHUNK), F32),
        pltpu.VMEM((HIST + tm, 2 * FF_CHUNK), F32),
        pltpu.VMEM((HIST, 2 * D_FF), F32),
        pltpu.VMEM((tm, FF_CHUNK), BF16),
        pltpu.VMEM((tm, FF_CHUNK), BF16),
        pltpu.VMEM((tm, D_MODEL), F32),
    ]
    y_p, ht_p, cnew_p, fnew_p = pl.pallas_call(
        functools.partial(_prompt_kernel, tm=tm),
        grid=(nbp, seq // tm),
        in_specs=[pl.BlockSpec((1, tm, D_MODEL), lambda b, t: (b, t, 0)),
                  pl.BlockSpec((1, tm, ple), lambda b, t: (b, t, 0))]
                 + [_const_spec(a.shape, 2) for a in p_args],
        out_specs=[pl.BlockSpec((1, tm, D_MODEL), lambda b, t: (b, t, 0)),
                   pl.BlockSpec((1, 1, W_A), lambda b, t: (b, 0, 0)),
                   pl.BlockSpec((1, CONV_A - 1, W_A), lambda b, t: (b, 0, 0)),
                   pl.BlockSpec((1, CONV_F - 1, 2 * D_FF), lambda b, t: (b, 0, 0))],
        out_shape=[jax.ShapeDtypeStruct((nbp, seq, D_MODEL), F32),
                   jax.ShapeDtypeStruct((nbp, 1, W_A), F32),
                   jax.ShapeDtypeStruct((nbp, CONV_A - 1, W_A), F32),
                   jax.ShapeDtypeStruct((nbp, CONV_F - 1, 2 * D_FF), F32)],
        scratch_shapes=scratch_p,
        compiler_params=pltpu.CompilerParams(dimension_semantics=("arbitrary", "arbitrary"),
                                             vmem_limit_bytes=VMEM_LIMIT_BYTES),
        name="prompt_layer",
    )(x_prompt, p_prompt[0], *p_args)

    rows = nb * steps
    ahist = (CONV_A - 1) * nb
    fhist = (CONV_F - 1) * nb
    tmaj = lambda a: jnp.swapaxes(a, 0, 1)
    sgw_s = jnp.repeat(jnp.transpose(sgu_w[0, :, :steps, :steps], (1, 2, 0)).reshape(steps * steps, H_B),
                       HD_B, axis=1)
    sgb_s = jnp.repeat(jnp.transpose(sgu_b[0, :, :steps]), HD_B, axis=1)
    w_args = head + [sgw_s, sgb_s] + tail
    scratch_s = [
        pltpu.VMEM((rows, D_MODEL), BF16),
        pltpu.VMEM((rows, D_MODEL + W_A), F32),
        pltpu.VMEM((ahist + rows, W_A), F32),
        pltpu.VMEM((rows, W_A), F32),
        pltpu.VMEM((rows, W_A), BF16),
        pltpu.VMEM((rows, 2 * W_A), F32),
        pltpu.VMEM((rows, W_A), F32),
        pltpu.VMEM((rows, W_A), F32),
        pltpu.VMEM((rows, W_A), F32),
        pltpu.VMEM((rows, W_A + W_B), BF16),
        pltpu.VMEM((rows, D_MODEL), F32),
        pltpu.VMEM((rows, ple), F32),
        pltpu.VMEM((fhist + rows, 2 * FF_CHUNK), F32),
        pltpu.VMEM((fhist + rows, 2 * FF_CHUNK), F32),
        pltpu.VMEM((rows, FF_CHUNK), BF16),
        pltpu.VMEM((rows, FF_CHUNK), BF16),
        pltpu.VMEM((rows, D_MODEL), F32),
    ]
    y_s, ht_s, cnew_s, vn_s, fnew_s = pl.pallas_call(
        functools.partial(_sample_kernel, nb=nb, steps=steps),
        grid=(nbs // nb,),
        in_specs=[pl.BlockSpec((steps, nb, D_MODEL), lambda i: (0, i, 0)),
                  pl.BlockSpec((steps, nb, ple), lambda i: (0, i, 0)),
                  pl.BlockSpec((nb, W_A), lambda i: (i, 0)),
                  pl.BlockSpec((CONV_A - 1, nb, W_A), lambda i: (0, i, 0)),
                  pl.BlockSpec((CONV_F - 1, nb, 2 * D_FF), lambda i: (0, i, 0))]
                 + [_const_spec(a.shape, 1) for a in w_args],
        out_specs=[pl.BlockSpec((steps, nb, D_MODEL), lambda i: (0, i, 0)),
                   pl.BlockSpec((nb, W_A), lambda i: (i, 0)),
                   pl.BlockSpec((CONV_A - 1, nb, W_A), lambda i: (0, i, 0)),
                   pl.BlockSpec((steps, nb, W_B), lambda i: (0, i, 0)),
                   pl.BlockSpec((CONV_F - 1, nb, 2 * D_FF), lambda i: (0, i, 0))],
        out_shape=[jax.ShapeDtypeStruct((steps, nbs, D_MODEL), F32),
                   jax.ShapeDtypeStruct((nbs, W_A), F32),
                   jax.ShapeDtypeStruct((CONV_A - 1, nbs, W_A), F32),
                   jax.ShapeDtypeStruct((steps, nbs, W_B), F32),
                   jax.ShapeDtypeStruct((CONV_F - 1, nbs, 2 * D_FF), F32)],
        scratch_shapes=scratch_s,
        compiler_params=pltpu.CompilerParams(dimension_semantics=("arbitrary",),
                                             vmem_limit_bytes=VMEM_LIMIT_BYTES),
        name="sample_layer",
    )(tmaj(x_sample), tmaj(p_sample[0]), state_rglru_h[0], tmaj(state_rglru_conv[0]),
      tmaj(state_ffn_conv[0]), *w_args)

    return (y_p, tmaj(y_s), tmaj(ht_p), ht_s[None], cnew_p[None], tmaj(cnew_s)[None],
            tmaj(vn_s)[None], fnew_p[None], tmaj(fnew_s)[None])
```

```python
import functools

import jax
import jax.numpy as jnp
from jax import lax
from jax.experimental import pallas as pl
from jax.experimental.pallas import tpu as pltpu

F32 = jnp.float32
BF16 = jnp.bfloat16

D_MODEL = 1024
W_A = 512
W_B = 512
H_A = 8
BW_A = W_A // H_A
H_B = 4
HD_B = W_B // H_B
CHUNK = 128
D_FF = 3072
CONV_A = 4
CONV_F = 3
C_RG = 8.0
EPS = 1e-6

SUBLANES = 8
HIST = SUBLANES
GATE_HALF = W_A // 2
FF_CHUNK = 512
N_FF_CHUNKS = D_FF // FF_CHUNK
SLAB = 32
MXU_PIECE = 256
PROMPT_BLOCK_ROWS = 256
SAMPLE_GROUP = 32
VMEM_LIMIT_BYTES = 56 * 1024 * 1024

assert 2 * FF_CHUNK == D_MODEL


def _rms(x, g):
    ms = jnp.mean(x * x, axis=-1, keepdims=True)
    return x * lax.rsqrt(ms + EPS) * g


def _layernorm(x, g, b):
    mu = jnp.mean(x, axis=-1, keepdims=True)
    xc = x - mu
    return xc * lax.rsqrt(jnp.mean(xc * xc, axis=-1, keepdims=True) + EPS) * g + b


def _sigmoid(x):
    return 1.0 / (1.0 + jnp.exp(-x))


def _softplus(x):
    return jnp.maximum(x, 0.0) + jnp.log(1.0 + jnp.exp(-jnp.abs(x)))


def _dot(a, b):
    return jnp.dot(a, b, preferred_element_type=F32)


def _wdot(a, w_words):
    return _dot(a, pltpu.bitcast(w_words, BF16))


def _ff_cols(c, half):
    start = half * D_FF + c * FF_CHUNK
    return slice(start, start + FF_CHUNK)


def _norm_to_bf16(src_ref, g_ref, dst_ref, rows):
    g = g_ref[...]
    for r0 in range(0, rows, SLAB):
        dst_ref[r0:r0 + SLAB, :] = _rms(src_ref[r0:r0 + SLAB, :], g).astype(BF16)


def _gate_stage(xc_ref, xcb_ref, g_ref, a_ref, u_ref, w_gate, ba, bx, apar, rows, reset_row):
    for hf in range(2):
        c0 = hf * GATE_HALF
        g_ref[:, 2 * c0:2 * c0 + 2 * GATE_HALF] = _wdot(xcb_ref[:, c0:c0 + GATE_HALF], w_gate[hf])
    sp = _softplus(-apar[...])
    for r0 in range(0, rows, SLAB):
        for hf in range(2):
            c0 = hf * GATE_HALF
            cs = slice(c0, c0 + GATE_HALF)
            r = _sigmoid(g_ref[r0:r0 + SLAB, 2 * c0:2 * c0 + GATE_HALF] + ba[:, cs])
            i = _sigmoid(g_ref[r0:r0 + SLAB, 2 * c0 + GATE_HALF:2 * c0 + 2 * GATE_HALF] + bx[:, cs])
            a = jnp.exp((-C_RG) * r * sp[:, cs])
            mult = jnp.sqrt(1.0 - a * a)
            if reset_row is not None and r0 == 0:
                row = lax.broadcasted_iota(jnp.int32, (SLAB, GATE_HALF), 0)
                mult = jnp.where(row == reset_row, 1.0, mult)
            a_ref[r0:r0 + SLAB, cs] = a
            u_ref[r0:r0 + SLAB, cs] = xc_ref[r0:r0 + SLAB, cs] * i * mult


def _branch_a_out(hs_ref, z_ref, g_oa, ymix_ref, rows):
    g = g_oa[...]
    for r0 in range(0, rows, SLAB):
        ga = z_ref[r0:r0 + SLAB, 0:W_A]
        ymix_ref[r0:r0 + SLAB, 0:W_A] = _rms(hs_ref[r0:r0 + SLAB, :] * jax.nn.gelu(ga), g).astype(BF16)


def _vn_stage(z_ref, lng, lnb, rows, write):
    g = lng[...]
    b = lnb[...]
    for r0 in range(0, rows, SLAB):
        vb = z_ref[r0:r0 + SLAB, W_A + W_B:W_A + 2 * W_B]
        write(r0, _layernorm(jax.nn.gelu(vb), g, b))


def _ffn_and_ple(h_ref, n_ref, up2_ref, act2_ref, acc_ref, z_ref, pe_ref, load_p, g_ffn, w_up, fcw, fcb,
                 w_down, g_ple, w_pg, w_ple, g_fin, rows, shift, hist_rows, load_hist, store_hist,
                 write_y):
    _norm_to_bf16(h_ref, g_ffn, n_ref, rows)
    acc_ref[...] = h_ref[...]

    def up_piece(c, j):
        up_ref = up2_ref[c % 2]
        cs = slice(j * MXU_PIECE, (j + 1) * MXU_PIECE)
        half, off = divmod(j * MXU_PIECE, FF_CHUNK)
        w0 = half * D_FF + c * FF_CHUNK + off
        if j == 0:
            load_hist(c, up_ref)
        up_ref[hist_rows:hist_rows + rows, cs] = _wdot(n_ref[...], w_up[:, w0:w0 + MXU_PIECE])

    def down_piece(c, j):
        cs = slice(j * MXU_PIECE, (j + 1) * MXU_PIECE)
        acc_ref[:, cs] += _wdot(act2_ref[c % 2][...], w_down[c * FF_CHUNK // 2:(c + 1) * FF_CHUNK // 2, cs])

    def conv_act_slab(c, r0):
        up_ref = up2_ref[c % 2]
        halves = []
        for half in range(2):
            cs = slice(half * FF_CHUNK, (half + 1) * FF_CHUNK)
            ws = _ff_cols(c, half)
            cv = fcb[:, ws]
            for j in range(CONV_F):
                off = hist_rows + r0 - (CONV_F - 1 - j) * shift
                cv = cv + up_ref[off:off + SLAB, cs] * fcw[j:j + 1, ws]
            halves.append(cv)
        act2_ref[c % 2][r0:r0 + SLAB, :] = (jax.nn.gelu(halves[0]) * halves[1]).astype(BF16)

    n_up = 2 * FF_CHUNK // MXU_PIECE
    n_down = D_MODEL // MXU_PIECE
    for j in range(n_up):
        up_piece(0, j)
    pe_ref[...] = _wdot(load_p().astype(BF16), w_ple[...])
    slabs = list(range(0, rows, SLAB))
    for c in range(N_FF_CHUNKS + 1):
        pieces = []
        for j in range(max(n_up, n_down)):
            if c + 1 < N_FF_CHUNKS and j < n_up:
                pieces.append(functools.partial(up_piece, c + 1, j))
            if c >= 1 and j < n_down:
                pieces.append(functools.partial(down_piece, c - 1, j))
        if c < N_FF_CHUNKS:
            done = 0
            for i, r0 in enumerate(slabs):
                conv_act_slab(c, r0)
                upto = (i + 1) * len(pieces) // len(slabs)
                for piece in pieces[done:upto]:
                    piece()
                done = upto
            store_hist(c, up2_ref[c % 2])
        else:
            for piece in pieces:
                piece()

    _norm_to_bf16(acc_ref, g_ple, n_ref, rows)
    z_ref[:, 0:D_MODEL] = _wdot(n_ref[...], w_pg[...])
    gf = g_fin[...]
    for r0 in range(0, rows, SLAB):
        gate = _sigmoid(z_ref[r0:r0 + SLAB, 0:D_MODEL])
        h3 = acc_ref[r0:r0 + SLAB, :] + pe_ref[r0:r0 + SLAB, :] * gate
        write_y(r0, _rms(h3, gf))


def _prompt_kernel(x_ref, p_ref, g_mix, w_in, caw, cab, w_gate, ba, bx, apar, g_oa, lng, lnb, sgw,
                   sgb, g_ob, w_out, g_ffn, w_up, fcw, fcb, w_down, g_ple, w_pg, w_ple, g_fin,
                   y_ref, ht_ref, cnew_ref, fnew_ref,
                   n_ref, z_ref, xa_ref, xc_ref, xcb_ref, g_ref, a_ref, u_ref, hs_ref, carry_ref,
                   vn_ref, ymix_ref, h_ref, up_ref, upb_ref, fhist_ref, act_ref, actb_ref, acc_ref, *, tm):
    t = pl.program_id(1)

    @pl.when(t == 0)
    def _():
        xa_ref[0:HIST, :] = jnp.zeros((HIST, W_A), F32)
        carry_ref[...] = jnp.zeros((SUBLANES, W_A), F32)
        fhist_ref[...] = jnp.zeros(fhist_ref.shape, F32)

    h_ref[...] = x_ref[0]
    _norm_to_bf16(h_ref, g_mix, n_ref, tm)
    xa_ref[HIST:HIST + tm, :] = _wdot(n_ref[...], w_in[:, 0:W_A])
    z_ref[...] = _wdot(n_ref[...], w_in[:, W_A:])

    cw = caw[...]
    cb = cab[...]
    for r0 in range(0, tm, SLAB):
        cv = cb
        for j in range(CONV_A):
            off = HIST + r0 - (CONV_A - 1 - j)
            cv = cv + xa_ref[off:off + SLAB, :] * cw[j:j + 1, :]
        xc_ref[r0:r0 + SLAB, :] = cv
        xcb_ref[r0:r0 + SLAB, :] = cv.astype(BF16)

    cnew_ref[0] = xa_ref[HIST + tm - (CONV_A - 1):HIST + tm, :]

    xa_ref[0:HIST, :] = xa_ref[tm:tm + HIST, :]

    reset_row = jnp.where(t == 0, 0, -1)
    _gate_stage(xc_ref, xcb_ref, g_ref, a_ref, u_ref, w_gate, ba, bx, apar, tm, reset_row)

    row = lax.broadcasted_iota(jnp.int32, (SUBLANES, W_A), 0)

    def scan_group(g, cbr):
        r0 = pl.multiple_of(g * SUBLANES, SUBLANES)
        a = a_ref[pl.ds(r0, SUBLANES), :]
        u = u_ref[pl.ds(r0, SUBLANES), :]
        for d in (1, 2, 4):
            keep = row >= d
            a_s = jnp.where(keep, pltpu.roll(a, d, 0), 1.0)
            u_s = jnp.where(keep, pltpu.roll(u, d, 0), 0.0)
            u = a * u_s + u
            a = a * a_s
        h = a * cbr + u
        hs_ref[pl.ds(r0, SUBLANES), :] = h
        return jnp.broadcast_to(h[SUBLANES - 1:SUBLANES, :], (SUBLANES, W_A))

    carry_ref[...] = lax.fori_loop(0, tm // SUBLANES, scan_group, carry_ref[...], unroll=4)

    ht_ref[0] = carry_ref[0:1, :]

    _branch_a_out(hs_ref, z_ref, g_oa, ymix_ref, tm)

    def write_vn(r0, vn):
        vn_ref[r0:r0 + SLAB, :] = vn.astype(BF16)

    _vn_stage(z_ref, lng, lnb, tm, write_vn)

    ti = lax.broadcasted_iota(jnp.int32, (CHUNK, CHUNK), 0)
    si = lax.broadcasted_iota(jnp.int32, (CHUNK, CHUNK), 1)
    gob = g_ob[...]
    wmix = [jnp.where(si <= ti, sgw[hd], 0.0).astype(BF16) for hd in range(H_B)]
    for c0 in range(0, tm, CHUNK):
        heads = []
        for hd in range(H_B):
            cs = slice(hd * HD_B, (hd + 1) * HD_B)
            mixed = _dot(wmix[hd], vn_ref[c0:c0 + CHUNK, cs]) + sgb[:, cs]
            ub = z_ref[c0:c0 + CHUNK, W_A + hd * HD_B:W_A + (hd + 1) * HD_B]
            heads.append(jax.nn.gelu(ub) * mixed)
        yb = jnp.concatenate(heads, axis=-1)
        ymix_ref[c0:c0 + CHUNK, W_A:] = _rms(yb, gob).astype(BF16)

    h_ref[...] += _wdot(ymix_ref[...], w_out[...])

    def load_hist(c, up_ref):
        for half in range(2):
            up_ref[0:HIST, half * FF_CHUNK:(half + 1) * FF_CHUNK] = fhist_ref[:, _ff_cols(c, half)]

    def store_hist(c, up_ref):
        for half in range(2):
            cs = slice(half * FF_CHUNK, (half + 1) * FF_CHUNK)
            fhist_ref[:, _ff_cols(c, half)] = up_ref[tm:tm + HIST, cs]
            fnew_ref[0, :, _ff_cols(c, half)] = up_ref[HIST + tm - (CONV_F - 1):HIST + tm, cs]

    def write_y(r0, y):
        y_ref[0, r0:r0 + SLAB, :] = y

    _ffn_and_ple(h_ref, n_ref, (up_ref, upb_ref), (act_ref, actb_ref), acc_ref, z_ref, g_ref,
                 lambda: p_ref[0], g_ffn,
                 w_up, fcw, fcb, w_down, g_ple, w_pg, w_ple, g_fin, tm, 1, HIST, load_hist,
                 store_hist, write_y)


def _sample_kernel(x_ref, p_ref, h0_ref, chist_ref, fh_ref, g_mix, w_in, caw, cab, w_gate, ba, bx,
                   apar, g_oa, lng, lnb, sgw, sgb, g_ob, w_out, g_ffn, w_up, fcw, fcb, w_down,
                   g_ple, w_pg, w_ple, g_fin,
                   y_ref, ht_ref, cnew_ref, vn_out_ref, fnew_ref,
                   n_ref, z_ref, xa_ref, xc_ref, xcb_ref, g_ref, a_ref, u_ref, hs_ref, ymix_ref,
                   h_ref, p_buf, up_ref, upb_ref, act_ref, actb_ref, acc_ref, *, nb, steps):
    rows = nb * steps
    ahist = (CONV_A - 1) * nb
    fhist = (CONV_F - 1) * nb

    for s in range(steps):
        h_ref[s * nb:(s + 1) * nb, :] = x_ref[s]
        p_buf[s * nb:(s + 1) * nb, :] = p_ref[s]
    _norm_to_bf16(h_ref, g_mix, n_ref, rows)
    for k in range(CONV_A - 1):
        xa_ref[k * nb:(k + 1) * nb, :] = chist_ref[k]
    xa_ref[ahist:ahist + rows, :] = _wdot(n_ref[...], w_in[:, 0:W_A])
    z_ref[...] = _wdot(n_ref[...], w_in[:, W_A:])

    cw = caw[...]
    cb = cab[...]
    for r0 in range(0, rows, SLAB):
        cv = cb
        for j in range(CONV_A):
            off = r0 + j * nb
            cv = cv + xa_ref[off:off + SLAB, :] * cw[j:j + 1, :]
        xc_ref[r0:r0 + SLAB, :] = cv
        xcb_ref[r0:r0 + SLAB, :] = cv.astype(BF16)
    for k in range(CONV_A - 1):
        cnew_ref[k] = xa_ref[rows + k * nb:rows + (k + 1) * nb, :]

    _gate_stage(xc_ref, xcb_ref, g_ref, a_ref, u_ref, w_gate, ba, bx, apar, rows, None)

    for b0 in range(0, nb, SLAB):
        h = h0_ref[b0:b0 + SLAB, :]
        for s in range(steps):
            r0 = s * nb + b0
            h = a_ref[r0:r0 + SLAB, :] * h + u_ref[r0:r0 + SLAB, :]
            hs_ref[r0:r0 + SLAB, :] = h
        ht_ref[b0:b0 + SLAB, :] = h

    _branch_a_out(hs_ref, z_ref, g_oa, ymix_ref, rows)

    def write_vn(r0, vn):
        vn_out_ref[r0 // nb, r0 % nb:r0 % nb + SLAB, :] = vn

    _vn_stage(z_ref, lng, lnb, rows, write_vn)

    gob = g_ob[...]
    for b0 in range(0, nb, SLAB):
        for tt in range(steps):
            r0 = tt * nb + b0
            mixed = sgb[tt:tt + 1, :]
            for s in range(tt + 1):
                k = tt * steps + s
                mixed = mixed + vn_out_ref[s, b0:b0 + SLAB, :] * sgw[k:k + 1, :]
            yb = jax.nn.gelu(z_ref[r0:r0 + SLAB, W_A:W_A + W_B]) * mixed
            ymix_ref[r0:r0 + SLAB, W_A:] = _rms(yb, gob).astype(BF16)

    h_ref[...] += _wdot(ymix_ref[...], w_out[...])

    def load_hist(c, up_ref):
        for k in range(CONV_F - 1):
            for half in range(2):
                cs = slice(half * FF_CHUNK, (half + 1) * FF_CHUNK)
                up_ref[k * nb:(k + 1) * nb, cs] = fh_ref[k, :, _ff_cols(c, half)]

    def store_hist(c, up_ref):
        for k in range(CONV_F - 1):
            for half in range(2):
                cs = slice(half * FF_CHUNK, (half + 1) * FF_CHUNK)
                fnew_ref[k, :, _ff_cols(c, half)] = up_ref[rows + k * nb:rows + (k + 1) * nb, cs]

    def write_y(r0, y):
        y_ref[r0 // nb, r0 % nb:r0 % nb + SLAB, :] = y

    _ffn_and_ple(h_ref, n_ref, (up_ref, upb_ref), (act_ref, actb_ref), acc_ref, z_ref, g_ref,
                 lambda: p_buf[...], g_ffn,
                 w_up, fcw, fcb, w_down, g_ple, w_pg, w_ple, g_fin, rows, nb, fhist, load_hist,
                 store_hist, write_y)


def _const_spec(shape, grid_rank):
    zeros = (0,) * len(shape)
    if grid_rank == 2:
        index_map = lambda b, t: zeros
    else:
        index_map = lambda i: zeros
    return pl.BlockSpec(shape, index_map, pipeline_mode=pl.Buffered(1))


def _pack_bf16_rows(w):
    k, n = w.shape[-2:]
    pairs = w.astype(BF16).reshape(w.shape[:-2] + (k // 2, 2, n))
    return lax.bitcast_convert_type(jnp.swapaxes(pairs, -1, -2), jnp.uint32)


def _block_diag_gate(wa, wx):
    hh = H_A // 2
    eye = jnp.eye(hh, dtype=wa.dtype)

    def bd(w):
        return jnp.einsum('hij,hg->higj', w, eye).reshape(GATE_HALF, GATE_HALF)

    return jnp.stack([jnp.concatenate([bd(wa[h * hh:(h + 1) * hh]), bd(wx[h * hh:(h + 1) * hh])], axis=1)
                      for h in range(2)])


def kernel(x_prompt, x_sample, p_prompt, p_sample, state_rglru_h, state_rglru_conv, state_ffn_conv, g_mix_norm, w_in, conv_a_w, conv_a_b, lru_wa, lru_ba, lru_wx, lru_bx, lru_a_param, g_out_a, ln_v_g, ln_v_b, sgu_w, sgu_b, g_out_b, w_out, g_ffn_norm, w_up, ffn_conv_w, ffn_conv_b, w_down, g_ple_norm, w_ple_gate, w_ple, g_final):
    assert w_in.shape[0] == 1
    nbp, seq, _ = x_prompt.shape
    nbs, steps, _ = x_sample.shape
    ple = p_prompt.shape[-1]
    tm = PROMPT_BLOCK_ROWS
    nb = SAMPLE_GROUP
    assert seq % tm == 0 and tm % CHUNK == 0 and nbs % nb == 0 and nb % SLAB == 0 and steps <= CHUNK
    row = lambda v: v.reshape(1, -1)

    head = [row(g_mix_norm[0]), _pack_bf16_rows(w_in[0]), conv_a_w[0], row(conv_a_b[0]),
            _pack_bf16_rows(_block_diag_gate(lru_wa[0], lru_wx[0])), row(lru_ba[0]), row(lru_bx[0]),
            row(lru_a_param[0]), row(g_out_a[0]), row(ln_v_g[0]), row(ln_v_b[0])]
    tail = [row(g_out_b[0]), _pack_bf16_rows(w_out[0]), row(g_ffn_norm[0]),
            _pack_bf16_rows(w_up[0]), ffn_conv_w[0], row(ffn_conv_b[0]), _pack_bf16_rows(w_down[0]),
            row(g_ple_norm[0]), _pack_bf16_rows(w_ple_gate[0]), _pack_bf16_rows(w_ple[0]), row(g_final)]

    sgb_p = jnp.repeat(jnp.transpose(sgu_b[0]), HD_B, axis=1)
    p_args = head + [sgu_w[0], sgb_p] + tail
    scratch_p = [
        pltpu.VMEM((tm, D_MODEL), BF16),
        pltpu.VMEM((tm, D_MODEL + W_A), F32),
        pltpu.VMEM((HIST + tm, W_A), F32),
        pltpu.VMEM((tm, W_A), F32),
        pltpu.VMEM((tm, W_A), BF16),
        pltpu.VMEM((tm, 2 * W_A), F32),
        pltpu.VMEM((tm, W_A), F32),
        pltpu.VMEM((tm, W_A), F32),
        pltpu.VMEM((tm, W_A), F32),
        pltpu.VMEM((SUBLANES, W_A), F32),
        pltpu.VMEM((tm, W_B), BF16),
        pltpu.VMEM((tm, W_A + W_B), BF16),
        pltpu.VMEM((tm, D_MODEL), F32),
        pltpu.VMEM((HIST + tm, 2 * FF_CHUNK), F32),
        pltpu.VMEM((HIST + tm, 2 * FF_CHUNK), F32),
        pltpu.VMEM((HIST, 2 * D_FF), F32),
        pltpu.VMEM((tm, FF_CHUNK), BF16),
        pltpu.VMEM((tm, FF_CHUNK), BF16),
        pltpu.VMEM((tm, D_MODEL), F32),
    ]
    y_p, ht_p, cnew_p, fnew_p = pl.pallas_call(
        functools.partial(_prompt_kernel, tm=tm),
        grid=(nbp, seq // tm),
        in_specs=[pl.BlockSpec((1, tm, D_MODEL), lambda b, t: (b, t, 0)),
                  pl.BlockSpec((1, tm, ple), lambda b, t: (b, t, 0))]
                 + [_const_spec(a.shape, 2) for a in p_args],
        out_specs=[pl.BlockSpec((1, tm, D_MODEL), lambda b, t: (b, t, 0)),
                   pl.BlockSpec((1, 1, W_A), lambda b, t: (b, 0, 0)),
                   pl.BlockSpec((1, CONV_A - 1, W_A), lambda b, t: (b, 0, 0)),
                   pl.BlockSpec((1, CONV_F - 1, 2 * D_FF), lambda b, t: (b, 0, 0))],
        out_shape=[jax.ShapeDtypeStruct((nbp, seq, D_MODEL), F32),
                   jax.ShapeDtypeStruct((nbp, 1, W_A), F32),
                   jax.ShapeDtypeStruct((nbp, CONV_A - 1, W_A), F32),
                   jax.ShapeDtypeStruct((nbp, CONV_F - 1, 2 * D_FF), F32)],
        scratch_shapes=scratch_p,
        compiler_params=pltpu.CompilerParams(dimension_semantics=("arbitrary", "arbitrary"),
                                             vmem_limit_bytes=VMEM_LIMIT_BYTES),
        name="prompt_layer",
    )(x_prompt, p_prompt[0], *p_args)

    rows = nb * steps
    ahist = (CONV_A - 1) * nb
    fhist = (CONV_F - 1) * nb
    tmaj = lambda a: jnp.swapaxes(a, 0, 1)
    sgw_s = jnp.repeat(jnp.transpose(sgu_w[0, :, :steps, :steps], (1, 2, 0)).reshape(steps * steps, H_B),
                       HD_B, axis=1)
    sgb_s = jnp.repeat(jnp.transpose(sgu_b[0, :, :steps]), HD_B, axis=1)
    w_args = head + [sgw_s, sgb_s] + tail
    scratch_s = [
        pltpu.VMEM((rows, D_MODEL), BF16),
        pltpu.VMEM((rows, D_MODEL + W_A), F32),
        pltpu.VMEM((ahist + rows, W_A), F32),
        pltpu.VMEM((rows, W_A), F32),
        pltpu.VMEM((rows, W_A), BF16),
        pltpu.VMEM((rows, 2 * W_A), F32),
        pltpu.VMEM((rows, W_A), F32),
        pltpu.VMEM((rows, W_A), F32),
        pltpu.VMEM((rows, W_A), F32),
        pltpu.VMEM((rows, W_A + W_B), BF16),
        pltpu.VMEM((rows, D_MODEL), F32),
        pltpu.VMEM((rows, ple), F32),
        pltpu.VMEM((fhist + rows, 2 * FF_CHUNK), F32),
        pltpu.VMEM((fhist + rows, 2 * FF_CHUNK), F32),
        pltpu.VMEM((rows, FF_CHUNK), BF16),
        pltpu.VMEM((rows, FF_CHUNK), BF16),
        pltpu.VMEM((rows, D_MODEL), F32),
    ]
    y_s, ht_s, cnew_s, vn_s, fnew_s = pl.pallas_call(
        functools.partial(_sample_kernel, nb=nb, steps=steps),
        grid=(nbs // nb,),
        in_specs=[pl.BlockSpec((steps, nb, D_MODEL), lambda i: (0, i, 0)),
                  pl.BlockSpec((steps, nb, ple), lambda i: (0, i, 0)),
                  pl.BlockSpec((nb, W_A), lambda i: (i, 0)),
                  pl.BlockSpec((CONV_A - 1, nb, W_A), lambda i: (0, i, 0)),
                  pl.BlockSpec((CONV_F - 1, nb, 2 * D_FF), lambda i: (0, i, 0))]
                 + [_const_spec(a.shape, 1) for a in w_args],
        out_specs=[pl.BlockSpec((steps, nb, D_MODEL), lambda i: (0, i, 0)),
                   pl.BlockSpec((nb, W_A), lambda i: (i, 0)),
                   pl.BlockSpec((CONV_A - 1, nb, W_A), lambda i: (0, i, 0)),
                   pl.BlockSpec((steps, nb, W_B), lambda i: (0, i, 0)),
                   pl.BlockSpec((CONV_F - 1, nb, 2 * D_FF), lambda i: (0, i, 0))],
        out_shape=[jax.ShapeDtypeStruct((steps, nbs, D_MODEL), F32),
                   jax.ShapeDtypeStruct((nbs, W_A), F32),
                   jax.ShapeDtypeStruct((CONV_A - 1, nbs, W_A), F32),
                   jax.ShapeDtypeStruct((steps, nbs, W_B), F32),
                   jax.ShapeDtypeStruct((CONV_F - 1, nbs, 2 * D_FF), F32)],
        scratch_shapes=scratch_s,
        compiler_params=pltpu.CompilerParams(dimension_semantics=("arbitrary",),
                                             vmem_limit_bytes=VMEM_LIMIT_BYTES),
        name="sample_layer",
    )(tmaj(x_sample), tmaj(p_sample[0]), state_rglru_h[0], tmaj(state_rglru_conv[0]),
      tmaj(state_ffn_conv[0]), *w_args)

    return (y_p, tmaj(y_s), tmaj(ht_p), ht_s[None], cnew_p[None], tmaj(cnew_s)[None],
            tmaj(vn_s)[None], fnew_p[None], tmaj(fnew_s)[None])
```

```python
import functools

import jax
import jax.numpy as jnp
from jax import lax
from jax.experimental import pallas as pl
from jax.experimental.pallas import tpu as pltpu

F32 = jnp.float32
BF16 = jnp.bfloat16

D_MODEL = 1024
W_A = 512
W_B = 512
H_A = 8
BW_A = W_A // H_A
H_B = 4
HD_B = W_B // H_B
CHUNK = 128
D_FF = 3072
CONV_A = 4
CONV_F = 3
C_RG = 8.0
EPS = 1e-6

SUBLANES = 8
HIST = SUBLANES
GATE_HALF = W_A // 2
FF_CHUNK = 512
N_FF_CHUNKS = D_FF // FF_CHUNK
SLAB = 32
MXU_PIECE = 256
PROMPT_BLOCK_ROWS = 256
SAMPLE_GROUP = 32
VMEM_LIMIT_BYTES = 56 * 1024 * 1024

assert 2 * FF_CHUNK == D_MODEL


def _rms(x, g):
    ms = jnp.mean(x * x, axis=-1, keepdims=True)
    return x * lax.rsqrt(ms + EPS) * g


def _layernorm(x, g, b):
    mu = jnp.mean(x, axis=-1, keepdims=True)
    xc = x - mu
    return xc * lax.rsqrt(jnp.mean(xc * xc, axis=-1, keepdims=True) + EPS) * g + b


def _sigmoid(x):
    return 1.0 / (1.0 + jnp.exp(-x))


def _softplus(x):
    return jnp.maximum(x, 0.0) + jnp.log(1.0 + jnp.exp(-jnp.abs(x)))


def _dot(a, b):
    return jnp.dot(a, b, preferred_element_type=F32)


def _wdot(a, w_words):
    return _dot(a, pltpu.bitcast(w_words, BF16))


def _ff_cols(c, half):
    start = half * D_FF + c * FF_CHUNK
    return slice(start, start + FF_CHUNK)


def _norm_to_bf16(src_ref, g_ref, dst_ref, rows):
    g = g_ref[...]
    for r0 in range(0, rows, SLAB):
        dst_ref[r0:r0 + SLAB, :] = _rms(src_ref[r0:r0 + SLAB, :], g).astype(BF16)


def _gate_stage(xc_ref, xcb_ref, g_ref, a_ref, u_ref, w_gate, ba, bx, apar, rows, reset_row):
    for hf in range(2):
        c0 = hf * GATE_HALF
        g_ref[:, 2 * c0:2 * c0 + 2 * GATE_HALF] = _wdot(xcb_ref[:, c0:c0 + GATE_HALF], w_gate[hf])
    sp = _softplus(-apar[...])
    for r0 in range(0, rows, SLAB):
        for hf in range(2):
            c0 = hf * GATE_HALF
            cs = slice(c0, c0 + GATE_HALF)
            r = _sigmoid(g_ref[r0:r0 + SLAB, 2 * c0:2 * c0 + GATE_HALF] + ba[:, cs])
            i = _sigmoid(g_ref[r0:r0 + SLAB, 2 * c0 + GATE_HALF:2 * c0 + 2 * GATE_HALF] + bx[:, cs])
            a = jnp.exp((-C_RG) * r * sp[:, cs])
            mult = jnp.sqrt(1.0 - a * a)
            if reset_row is not None and r0 == 0:
                row = lax.broadcasted_iota(jnp.int32, (SLAB, GATE_HALF), 0)
                mult = jnp.where(row == reset_row, 1.0, mult)
            a_ref[r0:r0 + SLAB, cs] = a
            u_ref[r0:r0 + SLAB, cs] = xc_ref[r0:r0 + SLAB, cs] * i * mult


def _branch_a_out(hs_ref, z_ref, g_oa, ymix_ref, rows):
    g = g_oa[...]
    for r0 in range(0, rows, SLAB):
        ga = z_ref[r0:r0 + SLAB, 0:W_A]
        ymix_ref[r0:r0 + SLAB, 0:W_A] = _rms(hs_ref[r0:r0 + SLAB, :] * jax.nn.gelu(ga), g).astype(BF16)


def _vn_stage(z_ref, lng, lnb, rows, write):
    g = lng[...]
    b = lnb[...]
    for r0 in range(0, rows, SLAB):
        vb = z_ref[r0:r0 + SLAB, W_A + W_B:W_A + 2 * W_B]
        write(r0, _layernorm(jax.nn.gelu(vb), g, b))


def _ffn_and_ple(h_ref, n_ref, up2_ref, act2_ref, acc_ref, z_ref, pe_ref, load_p, g_ffn, w_up, fcw, fcb,
                 w_down, g_ple, w_pg, w_ple, g_fin, rows, shift, hist_rows, load_hist, store_hist,
                 write_y):
    _norm_to_bf16(h_ref, g_ffn, n_ref, rows)
    acc_ref[...] = h_ref[...]

    def up_piece(c, j):
        up_ref = up2_ref[c % 2]
        cs = slice(j * MXU_PIECE, (j + 1) * MXU_PIECE)
        half, off = divmod(j * MXU_PIECE, FF_CHUNK)
        w0 = half * D_FF + c * FF_CHUNK + off
        if j == 0:
            load_hist(c, up_ref)
        up_ref[hist_rows:hist_rows + rows, cs] = _wdot(n_ref[...], w_up[:, w0:w0 + MXU_PIECE])

    def down_piece(c, j):
        cs = slice(j * MXU_PIECE, (j + 1) * MXU_PIECE)
        acc_ref[:, cs] += _wdot(act2_ref[c % 2][...], w_down[c * FF_CHUNK // 2:(c + 1) * FF_CHUNK // 2, cs])

    def conv_act_slab(c, r0):
        up_ref = up2_ref[c % 2]
        halves = []
        for half in range(2):
            cs = slice(half * FF_CHUNK, (half + 1) * FF_CHUNK)
            ws = _ff_cols(c, half)
            cv = fcb[:, ws]
            for j in range(CONV_F):
                off = hist_rows + r0 - (CONV_F - 1 - j) * shift
                cv = cv + up_ref[off:off + SLAB, cs] * fcw[j:j + 1, ws]
            halves.append(cv)
        act2_ref[c % 2][r0:r0 + SLAB, :] = (jax.nn.gelu(halves[0]) * halves[1]).astype(BF16)

    n_up = 2 * FF_CHUNK // MXU_PIECE
    n_down = D_MODEL // MXU_PIECE
    for j in range(n_up):
        up_piece(0, j)
    pe_ref[...] = _wdot(load_p().astype(BF16), w_ple[...])
    slabs = list(range(0, rows, SLAB))
    for c in range(N_FF_CHUNKS + 1):
        pieces = []
        for j in range(max(n_up, n_down)):
            if c + 1 < N_FF_CHUNKS and j < n_up:
                pieces.append(functools.partial(up_piece, c + 1, j))
            if c >= 1 and j < n_down:
                pieces.append(functools.partial(down_piece, c - 1, j))
        if c < N_FF_CHUNKS:
            done = 0
            for i, r0 in enumerate(slabs):
                conv_act_slab(c, r0)
                upto = (i + 1) * len(pieces) // len(slabs)
                for piece in pieces[done:upto]:
                    piece()
                done = upto
            store_hist(c, up2_ref[c % 2])
        else:
            for piece in pieces:
                piece()

    _norm_to_bf16(acc_ref, g_ple, n_ref, rows)
    z_ref[:, 0:D_MODEL] = _wdot(n_ref[...], w_pg[...])
    gf = g_fin[...]
    for r0 in range(0, rows, SLAB):
        gate = _sigmoid(z_ref[r0:r0 + SLAB, 0:D_MODEL])
        h3 = acc_ref[r0:r0 + SLAB, :] + pe_ref[r0:r0 + SLAB, :] * gate
        write_y(r0, _rms(h3, gf))


def _prompt_kernel(x_ref, p_ref, g_mix, w_in, caw, cab, w_gate, ba, bx, apar, g_oa, lng, lnb, sgw,
                   sgb, g_ob, w_out, g_ffn, w_up, fcw, fcb, w_down, g_ple, w_pg, w_ple, g_fin,
                   y_ref, ht_ref, cnew_ref, fnew_ref,
                   n_ref, z_ref, xa_ref, xc_ref, xcb_ref, g_ref, a_ref, u_ref, hs_ref, carry_ref,
                   vn_ref, ymix_ref, h_ref, up_ref, upb_ref, fhist_ref, act_ref, actb_ref, acc_ref, *, tm):
    t = pl.program_id(1)

    @pl.when(t == 0)
    def _():
        xa_ref[0:HIST, :] = jnp.zeros((HIST, W_A), F32)
        carry_ref[...] = jnp.zeros((SUBLANES, W_A), F32)
        fhist_ref[...] = jnp.zeros(fhist_ref.shape, F32)

    h_ref[...] = x_ref[0]
    _norm_to_bf16(h_ref, g_mix, n_ref, tm)
    xa_ref[HIST:HIST + tm, :] = _wdot(n_ref[...], w_in[:, 0:W_A])
    z_ref[...] = _wdot(n_ref[...], w_in[:, W_A:])

    cw = caw[...]
    cb = cab[...]
    for r0 in range(0, tm, SLAB):
        cv = cb
        for j in range(CONV_A):
            off = HIST + r0 - (CONV_A - 1 - j)
            cv = cv + xa_ref[off:off + SLAB, :] * cw[j:j + 1, :]
        xc_ref[r0:r0 + SLAB, :] = cv
        xcb_ref[r0:r0 + SLAB, :] = cv.astype(BF16)

    cnew_ref[0] = xa_ref[HIST + tm - (CONV_A - 1):HIST + tm, :]

    xa_ref[0:HIST, :] = xa_ref[tm:tm + HIST, :]

    reset_row = jnp.where(t == 0, 0, -1)
    _gate_stage(xc_ref, xcb_ref, g_ref, a_ref, u_ref, w_gate, ba, bx, apar, tm, reset_row)

    row = lax.broadcasted_iota(jnp.int32, (SUBLANES, W_A), 0)

    def scan_group(g, cbr):
        r0 = pl.multiple_of(g * SUBLANES, SUBLANES)
        a = a_ref[pl.ds(r0, SUBLANES), :]
        u = u_ref[pl.ds(r0, SUBLANES), :]
        for d in (1, 2, 4):
            keep = row >= d
            a_s = jnp.where(keep, pltpu.roll(a, d, 0), 1.0)
            u_s = jnp.where(keep, pltpu.roll(u, d, 0), 0.0)
            u = a * u_s + u
            a = a * a_s
        h = a * cbr + u
        hs_ref[pl.ds(r0, SUBLANES), :] = h
        return jnp.broadcast_to(h[SUBLANES - 1:SUBLANES, :], (SUBLANES, W_A))

    carry_ref[...] = lax.fori_loop(0, tm // SUBLANES, scan_group, carry_ref[...], unroll=4)

    ht_ref[0] = carry_ref[0:1, :]

    _branch_a_out(hs_ref, z_ref, g_oa, ymix_ref, tm)

    def write_vn(r0, vn):
        vn_ref[r0:r0 + SLAB, :] = vn.astype(BF16)

    _vn_stage(z_ref, lng, lnb, tm, write_vn)

    ti = lax.broadcasted_iota(jnp.int32, (CHUNK, CHUNK), 0)
    si = lax.broadcasted_iota(jnp.int32, (CHUNK, CHUNK), 1)
    gob = g_ob[...]
    wmix = [jnp.where(si <= ti, sgw[hd], 0.0).astype(BF16) for hd in range(H_B)]
    for c0 in range(0, tm, CHUNK):
        heads = []
        for hd in range(H_B):
            cs = slice(hd * HD_B, (hd + 1) * HD_B)
            mixed = _dot(wmix[hd], vn_ref[c0:c0 + CHUNK, cs]) + sgb[:, cs]
            ub = z_ref[c0:c0 + CHUNK, W_A + hd * HD_B:W_A + (hd + 1) * HD_B]
            heads.append(jax.nn.gelu(ub) * mixed)
        yb = jnp.concatenate(heads, axis=-1)
        ymix_ref[c0:c0 + CHUNK, W_A:] = _rms(yb, gob).astype(BF16)

    h_ref[...] += _wdot(ymix_ref[...], w_out[...])

    def load_hist(c, up_ref):
        for half in range(2):
            up_ref[0:HIST, half * FF_CHUNK:(half + 1) * FF_CHUNK] = fhist_ref[:, _ff_cols(c, half)]

    def store_hist(c, up_ref):
        for half in range(2):
            cs = slice(half * FF_CHUNK, (half + 1) * FF_CHUNK)
            fhist_ref[:, _ff_cols(c, half)] = up_ref[tm:tm + HIST, cs]
            fnew_ref[0, :, _ff_cols(c, half)] = up_ref[HIST + tm - (CONV_F - 1):HIST + tm, cs]

    def write_y(r0, y):
        y_ref[0, r0:r0 + SLAB, :] = y

    _ffn_and_ple(h_ref, n_ref, (up_ref, upb_ref), (act_ref, actb_ref), acc_ref, z_ref, g_ref,
                 lambda: p_ref[0], g_ffn,
                 w_up, fcw, fcb, w_down, g_ple, w_pg, w_ple, g_fin, tm, 1, HIST, load_hist,
                 store_hist, write_y)


def _sample_kernel(x_ref, p_ref, h0_ref, chist_ref, fh_ref, g_mix, w_in, caw, cab, w_gate, ba, bx,
                   apar, g_oa, lng, lnb, sgw, sgb, g_ob, w_out, g_ffn, w_up, fcw, fcb, w_down,
                   g_ple, w_pg, w_ple, g_fin,
                   y_ref, ht_ref, cnew_ref, vn_out_ref, fnew_ref,
                   n_ref, z_ref, xa_ref, xc_ref, xcb_ref, g_ref, a_ref, u_ref, hs_ref, ymix_ref,
                   h_ref, p_buf, up_ref, upb_ref, act_ref, actb_ref, acc_ref, *, nb, steps):
    rows = nb * steps
    ahist = (CONV_A - 1) * nb
    fhist = (CONV_F - 1) * nb

    for s in range(steps):
        h_ref[s * nb:(s + 1) * nb, :] = x_ref[s]
        p_buf[s * nb:(s + 1) * nb, :] = p_ref[s]
    _norm_to_bf16(h_ref, g_mix, n_ref, rows)
    for k in range(CONV_A - 1):
        xa_ref[k * nb:(k + 1) * nb, :] = chist_ref[k]
    xa_ref[ahist:ahist + rows, :] = _wdot(n_ref[...], w_in[:, 0:W_A])
    z_ref[...] = _wdot(n_ref[...], w_in[:, W_A:])

    cw = caw[...]
    cb = cab[...]
    for r0 in range(0, rows, SLAB):
        cv = cb
        for j in range(CONV_A):
            off = r0 + j * nb
            cv = cv + xa_ref[off:off + SLAB, :] * cw[j:j + 1, :]
        xc_ref[r0:r0 + SLAB, :] = cv
        xcb_ref[r0:r0 + SLAB, :] = cv.astype(BF16)
    for k in range(CONV_A - 1):
        cnew_ref[k] = xa_ref[rows + k * nb:rows + (k + 1) * nb, :]

    _gate_stage(xc_ref, xcb_ref, g_ref, a_ref, u_ref, w_gate, ba, bx, apar, rows, None)

    for b0 in range(0, nb, SLAB):
        h = h0_ref[b0:b0 + SLAB, :]
        for s in range(steps):
            r0 = s * nb + b0
            h = a_ref[r0:r0 + SLAB, :] * h + u_ref[r0:r0 + SLAB, :]
            hs_ref[r0:r0 + SLAB, :] = h
        ht_ref[b0:b0 + SLAB, :] = h

    _branch_a_out(hs_ref, z_ref, g_oa, ymix_ref, rows)

    def write_vn(r0, vn):
        vn_out_ref[r0 // nb, r0 % nb:r0 % nb + SLAB, :] = vn

    _vn_stage(z_ref, lng, lnb, rows, write_vn)

    gob = g_ob[...]
    for b0 in range(0, nb, SLAB):
        for tt in range(steps):
            r0 = tt * nb + b0
            mixed = sgb[tt:tt + 1, :]
            for s in range(tt + 1):
                k = tt * steps + s
                mixed = mixed + vn_out_ref[s, b0:b0 + SLAB, :] * sgw[k:k + 1, :]
            yb = jax.nn.gelu(z_ref[r0:r0 + SLAB, W_A:W_A + W_B]) * mixed
            ymix_ref[r0:r0 + SLAB, W_A:] = _rms(yb, gob).astype(BF16)

    h_ref[...] += _wdot(ymix_ref[...], w_out[...])

    def load_hist(c, up_ref):
        for k in range(CONV_F - 1):
            for half in range(2):
                cs = slice(half * FF_CHUNK, (half + 1) * FF_CHUNK)
                up_ref[k * nb:(k + 1) * nb, cs] = fh_ref[k, :, _ff_cols(c, half)]

    def store_hist(c, up_ref):
        for k in range(CONV_F - 1):
            for half in range(2):
                cs = slice(half * FF_CHUNK, (half + 1) * FF_CHUNK)
                fnew_ref[k, :, _ff_cols(c, half)] = up_ref[rows + k * nb:rows + (k + 1) * nb, cs]

    def write_y(r0, y):
        y_ref[r0 // nb, r0 % nb:r0 % nb + SLAB, :] = y

    _ffn_and_ple(h_ref, n_ref, (up_ref, upb_ref), (act_ref, actb_ref), acc_ref, z_ref, g_ref,
                 lambda: p_buf[...], g_ffn,
                 w_up, fcw, fcb, w_down, g_ple, w_pg, w_ple, g_fin, rows, nb, fhist, load_hist,
                 store_hist, write_y)


def _const_spec(shape, grid_rank):
    zeros = (0,) * len(shape)
    if grid_rank == 2:
        index_map = lambda b, t: zeros
    else:
        index_map = lambda i: zeros
    return pl.BlockSpec(shape, index_map, pipeline_mode=pl.Buffered(1))


def _pack_bf16_rows(w):
    def half(rows):
        return lax.bitcast_convert_type(rows.astype(BF16), jnp.uint16).astype(jnp.uint32)

    return half(w[..., 0::2, :]) | (half(w[..., 1::2, :]) << 16)


def _block_diag_gate(wa, wx):
    hh = H_A // 2
    eye = jnp.eye(hh, dtype=wa.dtype)

    def bd(w):
        return jnp.einsum('hij,hg->higj', w, eye).reshape(GATE_HALF, GATE_HALF)

    return jnp.stack([jnp.concatenate([bd(wa[h * hh:(h + 1) * hh]), bd(wx[h * hh:(h + 1) * hh])], axis=1)
                      for h in range(2)])


def kernel(x_prompt, x_sample, p_prompt, p_sample, state_rglru_h, state_rglru_conv, state_ffn_conv, g_mix_norm, w_in, conv_a_w, conv_a_b, lru_wa, lru_ba, lru_wx, lru_bx, lru_a_param, g_out_a, ln_v_g, ln_v_b, sgu_w, sgu_b, g_out_b, w_out, g_ffn_norm, w_up, ffn_conv_w, ffn_conv_b, w_down, g_ple_norm, w_ple_gate, w_ple, g_final):
    assert w_in.shape[0] == 1
    nbp, seq, _ = x_prompt.shape
    nbs, steps, _ = x_sample.shape
    ple = p_prompt.shape[-1]
    tm = PROMPT_BLOCK_ROWS
    nb = SAMPLE_GROUP
    assert seq % tm == 0 and tm % CHUNK == 0 and nbs % nb == 0 and nb % SLAB == 0 and steps <= CHUNK
    row = lambda v: v.reshape(1, -1)

    head = [row(g_mix_norm[0]), _pack_bf16_rows(w_in[0]), conv_a_w[0], row(conv_a_b[0]),
            _pack_bf16_rows(_block_diag_gate(lru_wa[0], lru_wx[0])), row(lru_ba[0]), row(lru_bx[0]),
            row(lru_a_param[0]), row(g_out_a[0]), row(ln_v_g[0]), row(ln_v_b[0])]
    tail = [row(g_out_b[0]), _pack_bf16_rows(w_out[0]), row(g_ffn_norm[0]),
            _pack_bf16_rows(w_up[0]), ffn_conv_w[0], row(ffn_conv_b[0]), _pack_bf16_rows(w_down[0]),
            row(g_ple_norm[0]), _pack_bf16_rows(w_ple_gate[0]), _pack_bf16_rows(w_ple[0]), row(g_final)]

    sgb_p = jnp.repeat(jnp.transpose(sgu_b[0]), HD_B, axis=1)
    p_args = head + [sgu_w[0], sgb_p] + tail
    scratch_p = [
        pltpu.VMEM((tm, D_MODEL), BF16),
        pltpu.VMEM((tm, D_MODEL + W_A), F32),
        pltpu.VMEM((HIST + tm, W_A), F32),
        pltpu.VMEM((tm, W_A), F32),
        pltpu.VMEM((tm, W_A), BF16),
        pltpu.VMEM((tm, 2 * W_A), F32),
        pltpu.VMEM((tm, W_A), F32),
        pltpu.VMEM((tm, W_A), F32),
        pltpu.VMEM((tm, W_A), F32),
        pltpu.VMEM((SUBLANES, W_A), F32),
        pltpu.VMEM((tm, W_B), BF16),
        pltpu.VMEM((tm, W_A + W_B), BF16),
        pltpu.VMEM((tm, D_MODEL), F32),
        pltpu.VMEM((HIST + tm, 2 * FF_CHUNK), F32),
        pltpu.VMEM((HIST + tm, 2 * FF_CHUNK), F32),
        pltpu.VMEM((HIST, 2 * D_FF), F32),
        pltpu.VMEM((tm, FF_CHUNK), BF16),
        pltpu.VMEM((tm, FF_CHUNK), BF16),
        pltpu.VMEM((tm, D_MODEL), F32),
    ]
    y_p, ht_p, cnew_p, fnew_p = pl.pallas_call(
        functools.partial(_prompt_kernel, tm=tm),
        grid=(nbp, seq // tm),
        in_specs=[pl.BlockSpec((1, tm, D_MODEL), lambda b, t: (b, t, 0)),
                  pl.BlockSpec((1, tm, ple), lambda b, t: (b, t, 0))]
                 + [_const_spec(a.shape, 2) for a in p_args],
        out_specs=[pl.BlockSpec((1, tm, D_MODEL), lambda b, t: (b, t, 0)),
                   pl.BlockSpec((1, 1, W_A), lambda b, t: (b, 0, 0)),
                   pl.BlockSpec((1, CONV_A - 1, W_A), lambda b, t: (b, 0, 0)),
                   pl.BlockSpec((1, CONV_F - 1, 2 * D_FF), lambda b, t: (b, 0, 0))],
        out_shape=[jax.ShapeDtypeStruct((nbp, seq, D_MODEL), F32),
                   jax.ShapeDtypeStruct((nbp, 1, W_A), F32),
                   jax.ShapeDtypeStruct((nbp, CONV_A - 1, W_A), F32),
                   jax.ShapeDtypeStruct((nbp, CONV_F - 1, 2 * D_FF), F32)],
        scratch_shapes=scratch_p,
        compiler_params=pltpu.CompilerParams(dimension_semantics=("arbitrary", "arbitrary"),
                                             vmem_limit_bytes=VMEM_LIMIT_BYTES),
        name="prompt_layer",
    )(x_prompt, p_prompt[0], *p_args)

    rows = nb * steps
    ahist = (CONV_A - 1) * nb
    fhist = (CONV_F - 1) * nb
    tmaj = lambda a: jnp.swapaxes(a, 0, 1)
    sgw_s = jnp.repeat(jnp.transpose(sgu_w[0, :, :steps, :steps], (1, 2, 0)).reshape(steps * steps, H_B),
                       HD_B, axis=1)
    sgb_s = jnp.repeat(jnp.transpose(sgu_b[0, :, :steps]), HD_B, axis=1)
    w_args = head + [sgw_s, sgb_s] + tail
    scratch_s = [
        pltpu.VMEM((rows, D_MODEL), BF16),
        pltpu.VMEM((rows, D_MODEL + W_A), F32),
        pltpu.VMEM((ahist + rows, W_A), F32),
        pltpu.VMEM((rows, W_A), F32),
        pltpu.VMEM((rows, W_A), BF16),
        pltpu.VMEM((rows, 2 * W_A), F32),
        pltpu.VMEM((rows, W_A), F32),
        pltpu.VMEM((rows, W_A), F32),
        pltpu.VMEM((rows, W_A), F32),
        pltpu.VMEM((rows, W_A + W_B), BF16),
        pltpu.VMEM((rows, D_MODEL), F32),
        pltpu.VMEM((rows, ple), F32),
        pltpu.VMEM((fhist + rows, 2 * FF_CHUNK), F32),
        pltpu.VMEM((fhist + rows, 2 * FF_CHUNK), F32),
        pltpu.VMEM((rows, FF_CHUNK), BF16),
        pltpu.VMEM((rows, FF_CHUNK), BF16),
        pltpu.VMEM((rows, D_MODEL), F32),
    ]
    y_s, ht_s, cnew_s, vn_s, fnew_s = pl.pallas_call(
        functools.partial(_sample_kernel, nb=nb, steps=steps),
        grid=(nbs // nb,),
        in_specs=[pl.BlockSpec((steps, nb, D_MODEL), lambda i: (0, i, 0)),
                  pl.BlockSpec((steps, nb, ple), lambda i: (0, i, 0)),
                  pl.BlockSpec((nb, W_A), lambda i: (i, 0)),
                  pl.BlockSpec((CONV_A - 1, nb, W_A), lambda i: (0, i, 0)),
                  pl.BlockSpec((CONV_F - 1, nb, 2 * D_FF), lambda i: (0, i, 0))]
                 + [_const_spec(a.shape, 1) for a in w_args],
        out_specs=[pl.BlockSpec((steps, nb, D_MODEL), lambda i: (0, i, 0)),
                   pl.BlockSpec((nb, W_A), lambda i: (i, 0)),
                   pl.BlockSpec((CONV_A - 1, nb, W_A), lambda i: (0, i, 0)),
                   pl.BlockSpec((steps, nb, W_B), lambda i: (0, i, 0)),
                   pl.BlockSpec((CONV_F - 1, nb, 2 * D_FF), lambda i: (0, i, 0))],
        out_shape=[jax.ShapeDtypeStruct((steps, nbs, D_MODEL), F32),
                   jax.ShapeDtypeStruct((nbs, W_A), F32),
                   jax.ShapeDtypeStruct((CONV_A - 1, nbs, W_A), F32),
                   jax.ShapeDtypeStruct((steps, nbs, W_B), F32),
                   jax.ShapeDtypeStruct((CONV_F - 1, nbs, 2 * D_FF), F32)],
        scratch_shapes=scratch_s,
        compiler_params=pltpu.CompilerParams(dimension_semantics=("arbitrary",),
                                             vmem_limit_bytes=VMEM_LIMIT_BYTES),
        name="sample_layer",
    )(tmaj(x_sample), tmaj(p_sample[0]), state_rglru_h[0], tmaj(state_rglru_conv[0]),
      tmaj(state_ffn_conv[0]), *w_args)

    return (y_p, tmaj(y_s), tmaj(ht_p), ht_s[None], cnew_p[None], tmaj(cnew_s)[None],
            tmaj(vn_s)[None], fnew_p[None], tmaj(fnew_s)[None])
```

```python
import functools

import jax
import jax.numpy as jnp
from jax import lax
from jax.experimental import pallas as pl
from jax.experimental.pallas import tpu as pltpu

F32 = jnp.float32
BF16 = jnp.bfloat16

D_MODEL = 1024
W_A = 512
W_B = 512
H_A = 8
BW_A = W_A // H_A
H_B = 4
HD_B = W_B // H_B
CHUNK = 128
D_FF = 3072
CONV_A = 4
CONV_F = 3
C_RG = 8.0
EPS = 1e-6

SUBLANES = 8
HIST = SUBLANES
GATE_HALF = W_A // 2
FF_CHUNK = 512
N_FF_CHUNKS = D_FF // FF_CHUNK
SLAB = 32
MXU_PIECE = 256
PACK_BLOCK_ROWS = 512
PACK_BLOCK_COLS = 2048
PROMPT_BLOCK_ROWS = 256
SAMPLE_GROUP = 32
VMEM_LIMIT_BYTES = 56 * 1024 * 1024

assert 2 * FF_CHUNK == D_MODEL


def _rms(x, g):
    ms = jnp.mean(x * x, axis=-1, keepdims=True)
    return x * lax.rsqrt(ms + EPS) * g


def _layernorm(x, g, b):
    mu = jnp.mean(x, axis=-1, keepdims=True)
    xc = x - mu
    return xc * lax.rsqrt(jnp.mean(xc * xc, axis=-1, keepdims=True) + EPS) * g + b


def _sigmoid(x):
    return 1.0 / (1.0 + jnp.exp(-x))


def _softplus(x):
    return jnp.maximum(x, 0.0) + jnp.log(1.0 + jnp.exp(-jnp.abs(x)))


def _dot(a, b):
    return jnp.dot(a, b, preferred_element_type=F32)


def _wdot(a, w_words):
    return _dot(a, pltpu.bitcast(w_words, BF16))


def _ff_cols(c, half):
    start = half * D_FF + c * FF_CHUNK
    return slice(start, start + FF_CHUNK)


def _norm_to_bf16(src_ref, g_ref, dst_ref, rows):
    g = g_ref[...]
    for r0 in range(0, rows, SLAB):
        dst_ref[r0:r0 + SLAB, :] = _rms(src_ref[r0:r0 + SLAB, :], g).astype(BF16)


def _gate_stage(xc_ref, xcb_ref, g_ref, a_ref, u_ref, w_gate, ba, bx, apar, rows, reset_row):
    for hf in range(2):
        c0 = hf * GATE_HALF
        g_ref[:, 2 * c0:2 * c0 + 2 * GATE_HALF] = _wdot(xcb_ref[:, c0:c0 + GATE_HALF], w_gate[hf])
    sp = _softplus(-apar[...])
    for r0 in range(0, rows, SLAB):
        for hf in range(2):
            c0 = hf * GATE_HALF
            cs = slice(c0, c0 + GATE_HALF)
            r = _sigmoid(g_ref[r0:r0 + SLAB, 2 * c0:2 * c0 + GATE_HALF] + ba[:, cs])
            i = _sigmoid(g_ref[r0:r0 + SLAB, 2 * c0 + GATE_HALF:2 * c0 + 2 * GATE_HALF] + bx[:, cs])
            a = jnp.exp((-C_RG) * r * sp[:, cs])
            mult = jnp.sqrt(1.0 - a * a)
            if reset_row is not None and r0 == 0:
                row = lax.broadcasted_iota(jnp.int32, (SLAB, GATE_HALF), 0)
                mult = jnp.where(row == reset_row, 1.0, mult)
            a_ref[r0:r0 + SLAB, cs] = a
            u_ref[r0:r0 + SLAB, cs] = xc_ref[r0:r0 + SLAB, cs] * i * mult


def _branch_a_out(hs_ref, z_ref, g_oa, ymix_ref, rows):
    g = g_oa[...]
    for r0 in range(0, rows, SLAB):
        ga = z_ref[r0:r0 + SLAB, 0:W_A]
        ymix_ref[r0:r0 + SLAB, 0:W_A] = _rms(hs_ref[r0:r0 + SLAB, :] * jax.nn.gelu(ga), g).astype(BF16)


def _vn_stage(z_ref, lng, lnb, rows, write):
    g = lng[...]
    b = lnb[...]
    for r0 in range(0, rows, SLAB):
        vb = z_ref[r0:r0 + SLAB, W_A + W_B:W_A + 2 * W_B]
        write(r0, _layernorm(jax.nn.gelu(vb), g, b))


def _ffn_and_ple(h_ref, n_ref, up2_ref, act2_ref, acc_ref, z_ref, pe_ref, load_p, g_ffn, w_up, fcw, fcb,
                 w_down, g_ple, w_pg, w_ple, g_fin, rows, shift, hist_rows, load_hist, store_hist,
                 write_y):
    _norm_to_bf16(h_ref, g_ffn, n_ref, rows)
    acc_ref[...] = h_ref[...]

    def up_piece(c, j):
        up_ref = up2_ref[c % 2]
        cs = slice(j * MXU_PIECE, (j + 1) * MXU_PIECE)
        half, off = divmod(j * MXU_PIECE, FF_CHUNK)
        w0 = half * D_FF + c * FF_CHUNK + off
        if j == 0:
            load_hist(c, up_ref)
        up_ref[hist_rows:hist_rows + rows, cs] = _wdot(n_ref[...], w_up[:, w0:w0 + MXU_PIECE])

    def down_piece(c, j):
        cs = slice(j * MXU_PIECE, (j + 1) * MXU_PIECE)
        acc_ref[:, cs] += _wdot(act2_ref[c % 2][...], w_down[c * FF_CHUNK // 2:(c + 1) * FF_CHUNK // 2, cs])

    def conv_act_slab(c, r0):
        up_ref = up2_ref[c % 2]
        halves = []
        for half in range(2):
            cs = slice(half * FF_CHUNK, (half + 1) * FF_CHUNK)
            ws = _ff_cols(c, half)
            cv = fcb[:, ws]
            for j in range(CONV_F):
                off = hist_rows + r0 - (CONV_F - 1 - j) * shift
                cv = cv + up_ref[off:off + SLAB, cs] * fcw[j:j + 1, ws]
            halves.append(cv)
        act2_ref[c % 2][r0:r0 + SLAB, :] = (jax.nn.gelu(halves[0]) * halves[1]).astype(BF16)

    n_up = 2 * FF_CHUNK // MXU_PIECE
    n_down = D_MODEL // MXU_PIECE
    for j in range(n_up):
        up_piece(0, j)
    pe_ref[...] = _wdot(load_p().astype(BF16), w_ple[...])
    slabs = list(range(0, rows, SLAB))
    for c in range(N_FF_CHUNKS + 1):
        pieces = []
        for j in range(max(n_up, n_down)):
            if c + 1 < N_FF_CHUNKS and j < n_up:
                pieces.append(functools.partial(up_piece, c + 1, j))
            if c >= 1 and j < n_down:
                pieces.append(functools.partial(down_piece, c - 1, j))
        if c < N_FF_CHUNKS:
            done = 0
            for i, r0 in enumerate(slabs):
                conv_act_slab(c, r0)
                upto = (i + 1) * len(pieces) // len(slabs)
                for piece in pieces[done:upto]:
                    piece()
                done = upto
            store_hist(c, up2_ref[c % 2])
        else:
            for piece in pieces:
                piece()

    _norm_to_bf16(acc_ref, g_ple, n_ref, rows)
    z_ref[:, 0:D_MODEL] = _wdot(n_ref[...], w_pg[...])
    gf = g_fin[...]
    for r0 in range(0, rows, SLAB):
        gate = _sigmoid(z_ref[r0:r0 + SLAB, 0:D_MODEL])
        h3 = acc_ref[r0:r0 + SLAB, :] + pe_ref[r0:r0 + SLAB, :] * gate
        write_y(r0, _rms(h3, gf))


def _prompt_kernel(x_ref, p_ref, g_mix, w_in, caw, cab, w_gate, ba, bx, apar, g_oa, lng, lnb, sgw,
                   sgb, g_ob, w_out, g_ffn, w_up, fcw, fcb, w_down, g_ple, w_pg, w_ple, g_fin,
                   y_ref, ht_ref, cnew_ref, fnew_ref,
                   n_ref, z_ref, xa_ref, xc_ref, xcb_ref, g_ref, a_ref, u_ref, hs_ref, carry_ref,
                   vn_ref, ymix_ref, h_ref, up_ref, upb_ref, fhist_ref, act_ref, actb_ref, acc_ref, *, tm):
    t = pl.program_id(1)

    @pl.when(t == 0)
    def _():
        xa_ref[0:HIST, :] = jnp.zeros((HIST, W_A), F32)
        carry_ref[...] = jnp.zeros((SUBLANES, W_A), F32)
        fhist_ref[...] = jnp.zeros(fhist_ref.shape, F32)

    h_ref[...] = x_ref[0]
    _norm_to_bf16(h_ref, g_mix, n_ref, tm)
    xa_ref[HIST:HIST + tm, :] = _wdot(n_ref[...], w_in[:, 0:W_A])
    z_ref[...] = _wdot(n_ref[...], w_in[:, W_A:])

    cw = caw[...]
    cb = cab[...]
    for r0 in range(0, tm, SLAB):
        cv = cb
        for j in range(CONV_A):
            off = HIST + r0 - (CONV_A - 1 - j)
            cv = cv + xa_ref[off:off + SLAB, :] * cw[j:j + 1, :]
        xc_ref[r0:r0 + SLAB, :] = cv
        xcb_ref[r0:r0 + SLAB, :] = cv.astype(BF16)

    cnew_ref[0] = xa_ref[HIST + tm - (CONV_A - 1):HIST + tm, :]

    xa_ref[0:HIST, :] = xa_ref[tm:tm + HIST, :]

    reset_row = jnp.where(t == 0, 0, -1)
    _gate_stage(xc_ref, xcb_ref, g_ref, a_ref, u_ref, w_gate, ba, bx, apar, tm, reset_row)

    row = lax.broadcasted_iota(jnp.int32, (SUBLANES, W_A), 0)

    def scan_group(g, cbr):
        r0 = pl.multiple_of(g * SUBLANES, SUBLANES)
        a = a_ref[pl.ds(r0, SUBLANES), :]
        u = u_ref[pl.ds(r0, SUBLANES), :]
        for d in (1, 2, 4):
            keep = row >= d
            a_s = jnp.where(keep, pltpu.roll(a, d, 0), 1.0)
            u_s = jnp.where(keep, pltpu.roll(u, d, 0), 0.0)
            u = a * u_s + u
            a = a * a_s
        h = a * cbr + u
        hs_ref[pl.ds(r0, SUBLANES), :] = h
        return jnp.broadcast_to(h[SUBLANES - 1:SUBLANES, :], (SUBLANES, W_A))

    carry_ref[...] = lax.fori_loop(0, tm // SUBLANES, scan_group, carry_ref[...], unroll=4)

    ht_ref[0] = carry_ref[0:1, :]

    _branch_a_out(hs_ref, z_ref, g_oa, ymix_ref, tm)

    def write_vn(r0, vn):
        vn_ref[r0:r0 + SLAB, :] = vn.astype(BF16)

    _vn_stage(z_ref, lng, lnb, tm, write_vn)

    ti = lax.broadcasted_iota(jnp.int32, (CHUNK, CHUNK), 0)
    si = lax.broadcasted_iota(jnp.int32, (CHUNK, CHUNK), 1)
    gob = g_ob[...]
    wmix = [jnp.where(si <= ti, sgw[hd], 0.0).astype(BF16) for hd in range(H_B)]
    for c0 in range(0, tm, CHUNK):
        heads = []
        for hd in range(H_B):
            cs = slice(hd * HD_B, (hd + 1) * HD_B)
            mixed = _dot(wmix[hd], vn_ref[c0:c0 + CHUNK, cs]) + sgb[:, cs]
            ub = z_ref[c0:c0 + CHUNK, W_A + hd * HD_B:W_A + (hd + 1) * HD_B]
            heads.append(jax.nn.gelu(ub) * mixed)
        yb = jnp.concatenate(heads, axis=-1)
        ymix_ref[c0:c0 + CHUNK, W_A:] = _rms(yb, gob).astype(BF16)

    h_ref[...] += _wdot(ymix_ref[...], w_out[...])

    def load_hist(c, up_ref):
        for half in range(2):
            up_ref[0:HIST, half * FF_CHUNK:(half + 1) * FF_CHUNK] = fhist_ref[:, _ff_cols(c, half)]

    def store_hist(c, up_ref):
        for half in range(2):
            cs = slice(half * FF_CHUNK, (half + 1) * FF_CHUNK)
            fhist_ref[:, _ff_cols(c, half)] = up_ref[tm:tm + HIST, cs]
            fnew_ref[0, :, _ff_cols(c, half)] = up_ref[HIST + tm - (CONV_F - 1):HIST + tm, cs]

    def write_y(r0, y):
        y_ref[0, r0:r0 + SLAB, :] = y

    _ffn_and_ple(h_ref, n_ref, (up_ref, upb_ref), (act_ref, actb_ref), acc_ref, z_ref, g_ref,
                 lambda: p_ref[0], g_ffn,
                 w_up, fcw, fcb, w_down, g_ple, w_pg, w_ple, g_fin, tm, 1, HIST, load_hist,
                 store_hist, write_y)


def _sample_kernel(x_ref, p_ref, h0_ref, chist_ref, fh_ref, g_mix, w_in, caw, cab, w_gate, ba, bx,
                   apar, g_oa, lng, lnb, sgw, sgb, g_ob, w_out, g_ffn, w_up, fcw, fcb, w_down,
                   g_ple, w_pg, w_ple, g_fin,
                   y_ref, ht_ref, cnew_ref, vn_out_ref, fnew_ref,
                   n_ref, z_ref, xa_ref, xc_ref, xcb_ref, g_ref, a_ref, u_ref, hs_ref, ymix_ref,
                   h_ref, p_buf, up_ref, upb_ref, act_ref, actb_ref, acc_ref, *, nb, steps):
    rows = nb * steps
    ahist = (CONV_A - 1) * nb
    fhist = (CONV_F - 1) * nb

    for s in range(steps):
        h_ref[s * nb:(s + 1) * nb, :] = x_ref[s]
        p_buf[s * nb:(s + 1) * nb, :] = p_ref[s]
    _norm_to_bf16(h_ref, g_mix, n_ref, rows)
    for k in range(CONV_A - 1):
        xa_ref[k * nb:(k + 1) * nb, :] = chist_ref[k]
    xa_ref[ahist:ahist + rows, :] = _wdot(n_ref[...], w_in[:, 0:W_A])
    z_ref[...] = _wdot(n_ref[...], w_in[:, W_A:])

    cw = caw[...]
    cb = cab[...]
    for r0 in range(0, rows, SLAB):
        cv = cb
        for j in range(CONV_A):
            off = r0 + j * nb
            cv = cv + xa_ref[off:off + SLAB, :] * cw[j:j + 1, :]
        xc_ref[r0:r0 + SLAB, :] = cv
        xcb_ref[r0:r0 + SLAB, :] = cv.astype(BF16)
    for k in range(CONV_A - 1):
        cnew_ref[k] = xa_ref[rows + k * nb:rows + (k + 1) * nb, :]

    _gate_stage(xc_ref, xcb_ref, g_ref, a_ref, u_ref, w_gate, ba, bx, apar, rows, None)

    for b0 in range(0, nb, SLAB):
        h = h0_ref[b0:b0 + SLAB, :]
        for s in range(steps):
            r0 = s * nb + b0
            h = a_ref[r0:r0 + SLAB, :] * h + u_ref[r0:r0 + SLAB, :]
            hs_ref[r0:r0 + SLAB, :] = h
        ht_ref[b0:b0 + SLAB, :] = h

    _branch_a_out(hs_ref, z_ref, g_oa, ymix_ref, rows)

    def write_vn(r0, vn):
        vn_out_ref[r0 // nb, r0 % nb:r0 % nb + SLAB, :] = vn

    _vn_stage(z_ref, lng, lnb, rows, write_vn)

    gob = g_ob[...]
    for b0 in range(0, nb, SLAB):
        for tt in range(steps):
            r0 = tt * nb + b0
            mixed = sgb[tt:tt + 1, :]
            for s in range(tt + 1):
                k = tt * steps + s
                mixed = mixed + vn_out_ref[s, b0:b0 + SLAB, :] * sgw[k:k + 1, :]
            yb = jax.nn.gelu(z_ref[r0:r0 + SLAB, W_A:W_A + W_B]) * mixed
            ymix_ref[r0:r0 + SLAB, W_A:] = _rms(yb, gob).astype(BF16)

    h_ref[...] += _wdot(ymix_ref[...], w_out[...])

    def load_hist(c, up_ref):
        for k in range(CONV_F - 1):
            for half in range(2):
                cs = slice(half * FF_CHUNK, (half + 1) * FF_CHUNK)
                up_ref[k * nb:(k + 1) * nb, cs] = fh_ref[k, :, _ff_cols(c, half)]

    def store_hist(c, up_ref):
        for k in range(CONV_F - 1):
            for half in range(2):
                cs = slice(half * FF_CHUNK, (half + 1) * FF_CHUNK)
                fnew_ref[k, :, _ff_cols(c, half)] = up_ref[rows + k * nb:rows + (k + 1) * nb, cs]

    def write_y(r0, y):
        y_ref[r0 // nb, r0 % nb:r0 % nb + SLAB, :] = y

    _ffn_and_ple(h_ref, n_ref, (up_ref, upb_ref), (act_ref, actb_ref), acc_ref, z_ref, g_ref,
                 lambda: p_buf[...], g_ffn,
                 w_up, fcw, fcb, w_down, g_ple, w_pg, w_ple, g_fin, rows, nb, fhist, load_hist,
                 store_hist, write_y)


def _const_spec(shape, grid_rank):
    zeros = (0,) * len(shape)
    if grid_rank == 2:
        index_map = lambda b, t: zeros
    else:
        index_map = lambda i: zeros
    return pl.BlockSpec(shape, index_map, pipeline_mode=pl.Buffered(1))


def _pack_bf16_rows(w):
    lead = w.shape[:-2]
    n = w.shape[-1]
    w2 = w.reshape((-1, n))
    k = w2.shape[0]
    bk = min(k, PACK_BLOCK_ROWS)
    bn = min(n, PACK_BLOCK_COLS)
    assert k % bk == 0 and n % bn == 0 and w.shape[-2] % 2 == 0
    words = pl.pallas_call(
        _pack_kernel,
        grid=(k // bk, n // bn),
        in_specs=[pl.BlockSpec((bk, bn), lambda i, j: (i, j))],
        out_specs=pl.BlockSpec((bk // 2, bn), lambda i, j: (i, j)),
        out_shape=jax.ShapeDtypeStruct((k // 2, n), jnp.uint32),
        name="pack_weight",
    )(w2)
    return words.reshape(lead + (w.shape[-2] // 2, n))


def _pack_kernel(w_ref, o_ref):
    o_ref[...] = pltpu.bitcast(w_ref[...].astype(BF16), jnp.uint32)


def _block_diag_gate(wa, wx):
    hh = H_A // 2
    eye = jnp.eye(hh, dtype=wa.dtype)

    def bd(w):
        return jnp.einsum('hij,hg->higj', w, eye).reshape(GATE_HALF, GATE_HALF)

    return jnp.stack([jnp.concatenate([bd(wa[h * hh:(h + 1) * hh]), bd(wx[h * hh:(h + 1) * hh])], axis=1)
                      for h in range(2)])


def kernel(x_prompt, x_sample, p_prompt, p_sample, state_rglru_h, state_rglru_conv, state_ffn_conv, g_mix_norm, w_in, conv_a_w, conv_a_b, lru_wa, lru_ba, lru_wx, lru_bx, lru_a_param, g_out_a, ln_v_g, ln_v_b, sgu_w, sgu_b, g_out_b, w_out, g_ffn_norm, w_up, ffn_conv_w, ffn_conv_b, w_down, g_ple_norm, w_ple_gate, w_ple, g_final):
    assert w_in.shape[0] == 1
    nbp, seq, _ = x_prompt.shape
    nbs, steps, _ = x_sample.shape
    ple = p_prompt.shape[-1]
    tm = PROMPT_BLOCK_ROWS
    nb = SAMPLE_GROUP
    assert seq % tm == 0 and tm % CHUNK == 0 and nbs % nb == 0 and nb % SLAB == 0 and steps <= CHUNK
    row = lambda v: v.reshape(1, -1)

    head = [row(g_mix_norm[0]), _pack_bf16_rows(w_in[0]), conv_a_w[0], row(conv_a_b[0]),
            _pack_bf16_rows(_block_diag_gate(lru_wa[0], lru_wx[0])), row(lru_ba[0]), row(lru_bx[0]),
            row(lru_a_param[0]), row(g_out_a[0]), row(ln_v_g[0]), row(ln_v_b[0])]
    tail = [row(g_out_b[0]), _pack_bf16_rows(w_out[0]), row(g_ffn_norm[0]),
            _pack_bf16_rows(w_up[0]), ffn_conv_w[0], row(ffn_conv_b[0]), _pack_bf16_rows(w_down[0]),
            row(g_ple_norm[0]), _pack_bf16_rows(w_ple_gate[0]), _pack_bf16_rows(w_ple[0]), row(g_final)]

    sgb_p = jnp.repeat(jnp.transpose(sgu_b[0]), HD_B, axis=1)
    p_args = head + [sgu_w[0], sgb_p] + tail
    scratch_p = [
        pltpu.VMEM((tm, D_MODEL), BF16),
        pltpu.VMEM((tm, D_MODEL + W_A), F32),
        pltpu.VMEM((HIST + tm, W_A), F32),
        pltpu.VMEM((tm, W_A), F32),
        pltpu.VMEM((tm, W_A), BF16),
        pltpu.VMEM((tm, 2 * W_A), F32),
        pltpu.VMEM((tm, W_A), F32),
        pltpu.VMEM((tm, W_A), F32),
        pltpu.VMEM((tm, W_A), F32),
        pltpu.VMEM((SUBLANES, W_A), F32),
        pltpu.VMEM((tm, W_B), BF16),
        pltpu.VMEM((tm, W_A + W_B), BF16),
        pltpu.VMEM((tm, D_MODEL), F32),
        pltpu.VMEM((HIST + tm, 2 * FF_CHUNK), F32),
        pltpu.VMEM((HIST + tm, 2 * FF_CHUNK), F32),
        pltpu.VMEM((HIST, 2 * D_FF), F32),
        pltpu.VMEM((tm, FF_CHUNK), BF16),
        pltpu.VMEM((tm, FF_CHUNK), BF16),
        pltpu.VMEM((tm, D_MODEL), F32),
    ]
    y_p, ht_p, cnew_p, fnew_p = pl.pallas_call(
        functools.partial(_prompt_kernel, tm=tm),
        grid=(nbp, seq // tm),
        in_specs=[pl.BlockSpec((1, tm, D_MODEL), lambda b, t: (b, t, 0)),
                  pl.BlockSpec((1, tm, ple), lambda b, t: (b, t, 0))]
                 + [_const_spec(a.shape, 2) for a in p_args],
        out_specs=[pl.BlockSpec((1, tm, D_MODEL), lambda b, t: (b, t, 0)),
                   pl.BlockSpec((1, 1, W_A), lambda b, t: (b, 0, 0)),
                   pl.BlockSpec((1, CONV_A - 1, W_A), lambda b, t: (b, 0, 0)),
                   pl.BlockSpec((1, CONV_F - 1, 2 * D_FF), lambda b, t: (b, 0, 0))],
        out_shape=[jax.ShapeDtypeStruct((nbp, seq, D_MODEL), F32),
                   jax.ShapeDtypeStruct((nbp, 1, W_A), F32),
                   jax.ShapeDtypeStruct((nbp, CONV_A - 1, W_A), F32),
                   jax.ShapeDtypeStruct((nbp, CONV_F - 1, 2 * D_FF), F32)],
        scratch_shapes=scratch_p,
        compiler_params=pltpu.CompilerParams(dimension_semantics=("arbitrary", "arbitrary"),
                                             vmem_limit_bytes=VMEM_LIMIT_BYTES),
        name="prompt_layer",
    )(x_prompt, p_prompt[0], *p_args)

    rows = nb * steps
    ahist = (CONV_A - 1) * nb
    fhist = (CONV_F - 1) * nb
    tmaj = lambda a: jnp.swapaxes(a, 0, 1)
    sgw_s = jnp.repeat(jnp.transpose(sgu_w[0, :, :steps, :steps], (1, 2, 0)).reshape(steps * steps, H_B),
                       HD_B, axis=1)
    sgb_s = jnp.repeat(jnp.transpose(sgu_b[0, :, :steps]), HD_B, axis=1)
    w_args = head + [sgw_s, sgb_s] + tail
    scratch_s = [
        pltpu.VMEM((rows, D_MODEL), BF16),
        pltpu.VMEM((rows, D_MODEL + W_A), F32),
        pltpu.VMEM((ahist + rows, W_A), F32),
        pltpu.VMEM((rows, W_A), F32),
        pltpu.VMEM((rows, W_A), BF16),
        pltpu.VMEM((rows, 2 * W_A), F32),
        pltpu.VMEM((rows, W_A), F32),
        pltpu.VMEM((rows, W_A), F32),
        pltpu.VMEM((rows, W_A), F32),
        pltpu.VMEM((rows, W_A + W_B), BF16),
        pltpu.VMEM((rows, D_MODEL), F32),
        pltpu.VMEM((rows, ple), F32),
        pltpu.VMEM((fhist + rows, 2 * FF_CHUNK), F32),
        pltpu.VMEM((fhist + rows, 2 * FF_CHUNK), F32),
        pltpu.VMEM((rows, FF_CHUNK), BF16),
        pltpu.VMEM((rows, FF_CHUNK), BF16),
        pltpu.VMEM((rows, D_MODEL), F32),
    ]
    y_s, ht_s, cnew_s, vn_s, fnew_s = pl.pallas_call(
        functools.partial(_sample_kernel, nb=nb, steps=steps),
        grid=(nbs // nb,),
        in_specs=[pl.BlockSpec((steps, nb, D_MODEL), lambda i: (0, i, 0)),
                  pl.BlockSpec((steps, nb, ple), lambda i: (0, i, 0)),
                  pl.BlockSpec((nb, W_A), lambda i: (i, 0)),
                  pl.BlockSpec((CONV_A - 1, nb, W_A), lambda i: (0, i, 0)),
                  pl.BlockSpec((CONV_F - 1, nb, 2 * D_FF), lambda i: (0, i, 0))]
                 + [_const_spec(a.shape, 1) for a in w_args],
        out_specs=[pl.BlockSpec((steps, nb, D_MODEL), lambda i: (0, i, 0)),
                   pl.BlockSpec((nb, W_A), lambda i: (i, 0)),
                   pl.BlockSpec((CONV_A - 1, nb, W_A), lambda i: (0, i, 0)),
                   pl.BlockSpec((steps, nb, W_B), lambda i: (0, i, 0)),
                   pl.BlockSpec((CONV_F - 1, nb, 2 * D_FF), lambda i: (0, i, 0))],
        out_shape=[jax.ShapeDtypeStruct((steps, nbs, D_MODEL), F32),
                   jax.ShapeDtypeStruct((nbs, W_A), F32),
                   jax.ShapeDtypeStruct((CONV_A - 1, nbs, W_A), F32),
                   jax.ShapeDtypeStruct((steps, nbs, W_B), F32),
                   jax.ShapeDtypeStruct((CONV_F - 1, nbs, 2 * D_FF), F32)],
        scratch_shapes=scratch_s,
        compiler_params=pltpu.CompilerParams(dimension_semantics=("arbitrary",),
                                             vmem_limit_bytes=VMEM_LIMIT_BYTES),
        name="sample_layer",
    )(tmaj(x_sample), tmaj(p_sample[0]), state_rglru_h[0], tmaj(state_rglru_conv[0]),
      tmaj(state_ffn_conv[0]), *w_args)

    return (y_p, tmaj(y_s), tmaj(ht_p), ht_s[None], cnew_p[None], tmaj(cnew_s)[None],
            tmaj(vn_s)[None], fnew_p[None], tmaj(fnew_s)[None])
```

```python
import functools

import jax
import jax.numpy as jnp
from jax import lax
from jax.experimental import pallas as pl
from jax.experimental.pallas import tpu as pltpu

F32 = jnp.float32
BF16 = jnp.bfloat16

D_MODEL = 1024
W_A = 512
W_B = 512
H_A = 8
BW_A = W_A // H_A
H_B = 4
HD_B = W_B // H_B
CHUNK = 128
D_FF = 3072
CONV_A = 4
CONV_F = 3
C_RG = 8.0
EPS = 1e-6

SUBLANES = 8
HIST = SUBLANES
GATE_HALF = W_A // 2
FF_CHUNK = 512
N_FF_CHUNKS = D_FF // FF_CHUNK
SLAB = 32
MXU_PIECE = 256
PACK_BLOCK_ROWS = 512
PACK_BLOCK_COLS = 2048
PROMPT_BLOCK_ROWS = 256
SAMPLE_GROUP = 32
VMEM_LIMIT_BYTES = 56 * 1024 * 1024

assert 2 * FF_CHUNK == D_MODEL


def _rms(x, g):
    ms = jnp.mean(x * x, axis=-1, keepdims=True)
    return x * lax.rsqrt(ms + EPS) * g


def _layernorm(x, g, b):
    mu = jnp.mean(x, axis=-1, keepdims=True)
    xc = x - mu
    return xc * lax.rsqrt(jnp.mean(xc * xc, axis=-1, keepdims=True) + EPS) * g + b


def _sigmoid(x):
    return 1.0 / (1.0 + jnp.exp(-x))


def _softplus(x):
    return jnp.maximum(x, 0.0) + jnp.log(1.0 + jnp.exp(-jnp.abs(x)))


def _dot(a, b):
    return jnp.dot(a, b, preferred_element_type=F32)


def _fresh_rows(ref, salt):
    zero = jnp.minimum(pl.program_id(0), 0) * salt
    return ref[pl.ds(pl.multiple_of(zero, 16), ref.shape[0]), :]


def _wdot(a, w_words):
    return _dot(a, pltpu.bitcast(w_words, BF16))


def _ff_cols(c, half):
    start = half * D_FF + c * FF_CHUNK
    return slice(start, start + FF_CHUNK)


def _norm_to_bf16(src_ref, g_ref, dst_ref, rows):
    g = g_ref[...]
    for r0 in range(0, rows, SLAB):
        dst_ref[r0:r0 + SLAB, :] = _rms(src_ref[r0:r0 + SLAB, :], g).astype(BF16)


def _gate_stage(xc_ref, xcb_ref, g_ref, a_ref, u_ref, w_gate, ba, bx, apar, rows, reset_row):
    for hf in range(2):
        c0 = hf * GATE_HALF
        g_ref[:, 2 * c0:2 * c0 + 2 * GATE_HALF] = _wdot(xcb_ref[:, c0:c0 + GATE_HALF], w_gate[hf])
    sp = _softplus(-apar[...])
    for r0 in range(0, rows, SLAB):
        for hf in range(2):
            c0 = hf * GATE_HALF
            cs = slice(c0, c0 + GATE_HALF)
            r = _sigmoid(g_ref[r0:r0 + SLAB, 2 * c0:2 * c0 + GATE_HALF] + ba[:, cs])
            i = _sigmoid(g_ref[r0:r0 + SLAB, 2 * c0 + GATE_HALF:2 * c0 + 2 * GATE_HALF] + bx[:, cs])
            a = jnp.exp((-C_RG) * r * sp[:, cs])
            mult = jnp.sqrt(1.0 - a * a)
            if reset_row is not None and r0 == 0:
                row = lax.broadcasted_iota(jnp.int32, (SLAB, GATE_HALF), 0)
                mult = jnp.where(row == reset_row, 1.0, mult)
            a_ref[r0:r0 + SLAB, cs] = a
            u_ref[r0:r0 + SLAB, cs] = xc_ref[r0:r0 + SLAB, cs] * i * mult


def _branch_a_out(hs_ref, z_ref, g_oa, ymix_ref, rows):
    g = g_oa[...]
    for r0 in range(0, rows, SLAB):
        ga = z_ref[r0:r0 + SLAB, 0:W_A]
        ymix_ref[r0:r0 + SLAB, 0:W_A] = _rms(hs_ref[r0:r0 + SLAB, :] * jax.nn.gelu(ga), g).astype(BF16)


def _vn_stage(z_ref, lng, lnb, rows, write):
    g = lng[...]
    b = lnb[...]
    for r0 in range(0, rows, SLAB):
        vb = z_ref[r0:r0 + SLAB, W_A + W_B:W_A + 2 * W_B]
        write(r0, _layernorm(jax.nn.gelu(vb), g, b))


def _ffn_and_ple(h_ref, n_ref, up2_ref, act2_ref, acc_ref, z_ref, pe_ref, load_p, g_ffn, w_up, fcw, fcb,
                 w_down, g_ple, w_pg, w_ple, g_fin, rows, shift, hist_rows, load_hist, store_hist,
                 write_y):
    _norm_to_bf16(h_ref, g_ffn, n_ref, rows)
    acc_ref[...] = h_ref[...]

    def up_piece(c, j):
        up_ref = up2_ref[c % 2]
        cs = slice(j * MXU_PIECE, (j + 1) * MXU_PIECE)
        half, off = divmod(j * MXU_PIECE, FF_CHUNK)
        w0 = half * D_FF + c * FF_CHUNK + off
        if j == 0:
            load_hist(c, up_ref)
        lhs = _fresh_rows(n_ref, 1 + c * n_up + j)
        up_ref[hist_rows:hist_rows + rows, cs] = _wdot(lhs, w_up[:, w0:w0 + MXU_PIECE])

    def down_piece(c, j):
        cs = slice(j * MXU_PIECE, (j + 1) * MXU_PIECE)
        lhs = _fresh_rows(act2_ref[c % 2], 1 + c * n_down + j)
        acc_ref[:, cs] += _wdot(lhs, w_down[c * FF_CHUNK // 2:(c + 1) * FF_CHUNK // 2, cs])

    def conv_act_slab(c, r0):
        up_ref = up2_ref[c % 2]
        halves = []
        for half in range(2):
            cs = slice(half * FF_CHUNK, (half + 1) * FF_CHUNK)
            ws = _ff_cols(c, half)
            cv = fcb[:, ws]
            for j in range(CONV_F):
                off = hist_rows + r0 - (CONV_F - 1 - j) * shift
                cv = cv + up_ref[off:off + SLAB, cs] * fcw[j:j + 1, ws]
            halves.append(cv)
        act2_ref[c % 2][r0:r0 + SLAB, :] = (jax.nn.gelu(halves[0]) * halves[1]).astype(BF16)

    n_up = 2 * FF_CHUNK // MXU_PIECE
    n_down = D_MODEL // MXU_PIECE
    for j in range(n_up):
        up_piece(0, j)
    pe_ref[...] = _wdot(load_p().astype(BF16), w_ple[...])
    slabs = list(range(0, rows, SLAB))
    for c in range(N_FF_CHUNKS + 1):
        pieces = []
        for j in range(max(n_up, n_down)):
            if c + 1 < N_FF_CHUNKS and j < n_up:
                pieces.append(functools.partial(up_piece, c + 1, j))
            if c >= 1 and j < n_down:
                pieces.append(functools.partial(down_piece, c - 1, j))
        if c < N_FF_CHUNKS:
            done = 0
            for i, r0 in enumerate(slabs):
                conv_act_slab(c, r0)
                upto = (i + 1) * len(pieces) // len(slabs)
                for piece in pieces[done:upto]:
                    piece()
                done = upto
            store_hist(c, up2_ref[c % 2])
        else:
            for piece in pieces:
                piece()

    _norm_to_bf16(acc_ref, g_ple, n_ref, rows)
    z_ref[:, 0:D_MODEL] = _wdot(n_ref[...], w_pg[...])
    gf = g_fin[...]
    for r0 in range(0, rows, SLAB):
        gate = _sigmoid(z_ref[r0:r0 + SLAB, 0:D_MODEL])
        h3 = acc_ref[r0:r0 + SLAB, :] + pe_ref[r0:r0 + SLAB, :] * gate
        write_y(r0, _rms(h3, gf))


def _prompt_kernel(x_ref, p_ref, g_mix, w_in, caw, cab, w_gate, ba, bx, apar, g_oa, lng, lnb, sgw,
                   sgb, g_ob, w_out, g_ffn, w_up, fcw, fcb, w_down, g_ple, w_pg, w_ple, g_fin,
                   y_ref, ht_ref, cnew_ref, fnew_ref,
                   n_ref, z_ref, xa_ref, xc_ref, xcb_ref, g_ref, a_ref, u_ref, hs_ref, carry_ref,
                   vn_ref, ymix_ref, h_ref, up_ref, upb_ref, fhist_ref, act_ref, actb_ref, acc_ref, *, tm):
    t = pl.program_id(1)

    @pl.when(t == 0)
    def _():
        xa_ref[0:HIST, :] = jnp.zeros((HIST, W_A), F32)
        carry_ref[...] = jnp.zeros((SUBLANES, W_A), F32)
        fhist_ref[...] = jnp.zeros(fhist_ref.shape, F32)

    h_ref[...] = x_ref[0]
    _norm_to_bf16(h_ref, g_mix, n_ref, tm)
    xa_ref[HIST:HIST + tm, :] = _wdot(n_ref[...], w_in[:, 0:W_A])
    z_ref[...] = _wdot(n_ref[...], w_in[:, W_A:])

    cw = caw[...]
    cb = cab[...]
    for r0 in range(0, tm, SLAB):
        cv = cb
        for j in range(CONV_A):
            off = HIST + r0 - (CONV_A - 1 - j)
            cv = cv + xa_ref[off:off + SLAB, :] * cw[j:j + 1, :]
        xc_ref[r0:r0 + SLAB, :] = cv
        xcb_ref[r0:r0 + SLAB, :] = cv.astype(BF16)

    cnew_ref[0] = xa_ref[HIST + tm - (CONV_A - 1):HIST + tm, :]

    xa_ref[0:HIST, :] = xa_ref[tm:tm + HIST, :]

    reset_row = jnp.where(t == 0, 0, -1)
    _gate_stage(xc_ref, xcb_ref, g_ref, a_ref, u_ref, w_gate, ba, bx, apar, tm, reset_row)

    row = lax.broadcasted_iota(jnp.int32, (SUBLANES, W_A), 0)

    def scan_group(g, cbr):
        r0 = pl.multiple_of(g * SUBLANES, SUBLANES)
        a = a_ref[pl.ds(r0, SUBLANES), :]
        u = u_ref[pl.ds(r0, SUBLANES), :]
        for d in (1, 2, 4):
            keep = row >= d
            a_s = jnp.where(keep, pltpu.roll(a, d, 0), 1.0)
            u_s = jnp.where(keep, pltpu.roll(u, d, 0), 0.0)
            u = a * u_s + u
            a = a * a_s
        h = a * cbr + u
        hs_ref[pl.ds(r0, SUBLANES), :] = h
        return jnp.broadcast_to(h[SUBLANES - 1:SUBLANES, :], (SUBLANES, W_A))

    carry_ref[...] = lax.fori_loop(0, tm // SUBLANES, scan_group, carry_ref[...], unroll=4)

    ht_ref[0] = carry_ref[0:1, :]

    _branch_a_out(hs_ref, z_ref, g_oa, ymix_ref, tm)

    def write_vn(r0, vn):
        vn_ref[r0:r0 + SLAB, :] = vn.astype(BF16)

    _vn_stage(z_ref, lng, lnb, tm, write_vn)

    ti = lax.broadcasted_iota(jnp.int32, (CHUNK, CHUNK), 0)
    si = lax.broadcasted_iota(jnp.int32, (CHUNK, CHUNK), 1)
    gob = g_ob[...]
    wmix = [jnp.where(si <= ti, sgw[hd], 0.0).astype(BF16) for hd in range(H_B)]
    for c0 in range(0, tm, CHUNK):
        heads = []
        for hd in range(H_B):
            cs = slice(hd * HD_B, (hd + 1) * HD_B)
            mixed = _dot(wmix[hd], vn_ref[c0:c0 + CHUNK, cs]) + sgb[:, cs]
            ub = z_ref[c0:c0 + CHUNK, W_A + hd * HD_B:W_A + (hd + 1) * HD_B]
            heads.append(jax.nn.gelu(ub) * mixed)
        yb = jnp.concatenate(heads, axis=-1)
        ymix_ref[c0:c0 + CHUNK, W_A:] = _rms(yb, gob).astype(BF16)

    h_ref[...] += _wdot(ymix_ref[...], w_out[...])

    def load_hist(c, up_ref):
        for half in range(2):
            up_ref[0:HIST, half * FF_CHUNK:(half + 1) * FF_CHUNK] = fhist_ref[:, _ff_cols(c, half)]

    def store_hist(c, up_ref):
        for half in range(2):
            cs = slice(half * FF_CHUNK, (half + 1) * FF_CHUNK)
            fhist_ref[:, _ff_cols(c, half)] = up_ref[tm:tm + HIST, cs]
            fnew_ref[0, :, _ff_cols(c, half)] = up_ref[HIST + tm - (CONV_F - 1):HIST + tm, cs]

    def write_y(r0, y):
        y_ref[0, r0:r0 + SLAB, :] = y

    _ffn_and_ple(h_ref, n_ref, (up_ref, upb_ref), (act_ref, actb_ref), acc_ref, z_ref, g_ref,
                 lambda: p_ref[0], g_ffn,
                 w_up, fcw, fcb, w_down, g_ple, w_pg, w_ple, g_fin, tm, 1, HIST, load_hist,
                 store_hist, write_y)


def _sample_kernel(x_ref, p_ref, h0_ref, chist_ref, fh_ref, g_mix, w_in, caw, cab, w_gate, ba, bx,
                   apar, g_oa, lng, lnb, sgw, sgb, g_ob, w_out, g_ffn, w_up, fcw, fcb, w_down,
                   g_ple, w_pg, w_ple, g_fin,
                   y_ref, ht_ref, cnew_ref, vn_out_ref, fnew_ref,
                   n_ref, z_ref, xa_ref, xc_ref, xcb_ref, g_ref, a_ref, u_ref, hs_ref, ymix_ref,
                   h_ref, p_buf, up_ref, upb_ref, act_ref, actb_ref, acc_ref, *, nb, steps):
    rows = nb * steps
    ahist = (CONV_A - 1) * nb
    fhist = (CONV_F - 1) * nb

    for s in range(steps):
        h_ref[s * nb:(s + 1) * nb, :] = x_ref[s]
        p_buf[s * nb:(s + 1) * nb, :] = p_ref[s]
    _norm_to_bf16(h_ref, g_mix, n_ref, rows)
    for k in range(CONV_A - 1):
        xa_ref[k * nb:(k + 1) * nb, :] = chist_ref[k]
    xa_ref[ahist:ahist + rows, :] = _wdot(n_ref[...], w_in[:, 0:W_A])
    z_ref[...] = _wdot(n_ref[...], w_in[:, W_A:])

    cw = caw[...]
    cb = cab[...]
    for r0 in range(0, rows, SLAB):
        cv = cb
        for j in range(CONV_A):
            off = r0 + j * nb
            cv = cv + xa_ref[off:off + SLAB, :] * cw[j:j + 1, :]
        xc_ref[r0:r0 + SLAB, :] = cv
        xcb_ref[r0:r0 + SLAB, :] = cv.astype(BF16)
    for k in range(CONV_A - 1):
        cnew_ref[k] = xa_ref[rows + k * nb:rows + (k + 1) * nb, :]

    _gate_stage(xc_ref, xcb_ref, g_ref, a_ref, u_ref, w_gate, ba, bx, apar, rows, None)

    for b0 in range(0, nb, SLAB):
        h = h0_ref[b0:b0 + SLAB, :]
        for s in range(steps):
            r0 = s * nb + b0
            h = a_ref[r0:r0 + SLAB, :] * h + u_ref[r0:r0 + SLAB, :]
            hs_ref[r0:r0 + SLAB, :] = h
        ht_ref[b0:b0 + SLAB, :] = h

    _branch_a_out(hs_ref, z_ref, g_oa, ymix_ref, rows)

    def write_vn(r0, vn):
        vn_out_ref[r0 // nb, r0 % nb:r0 % nb + SLAB, :] = vn

    _vn_stage(z_ref, lng, lnb, rows, write_vn)

    gob = g_ob[...]
    for b0 in range(0, nb, SLAB):
        for tt in range(steps):
            r0 = tt * nb + b0
            mixed = sgb[tt:tt + 1, :]
            for s in range(tt + 1):
                k = tt * steps + s
                mixed = mixed + vn_out_ref[s, b0:b0 + SLAB, :] * sgw[k:k + 1, :]
            yb = jax.nn.gelu(z_ref[r0:r0 + SLAB, W_A:W_A + W_B]) * mixed
            ymix_ref[r0:r0 + SLAB, W_A:] = _rms(yb, gob).astype(BF16)

    h_ref[...] += _wdot(ymix_ref[...], w_out[...])

    def load_hist(c, up_ref):
        for k in range(CONV_F - 1):
            for half in range(2):
                cs = slice(half * FF_CHUNK, (half + 1) * FF_CHUNK)
                up_ref[k * nb:(k + 1) * nb, cs] = fh_ref[k, :, _ff_cols(c, half)]

    def store_hist(c, up_ref):
        for k in range(CONV_F - 1):
            for half in range(2):
                cs = slice(half * FF_CHUNK, (half + 1) * FF_CHUNK)
                fnew_ref[k, :, _ff_cols(c, half)] = up_ref[rows + k * nb:rows + (k + 1) * nb, cs]

    def write_y(r0, y):
        y_ref[r0 // nb, r0 % nb:r0 % nb + SLAB, :] = y

    _ffn_and_ple(h_ref, n_ref, (up_ref, upb_ref), (act_ref, actb_ref), acc_ref, z_ref, g_ref,
                 lambda: p_buf[...], g_ffn,
                 w_up, fcw, fcb, w_down, g_ple, w_pg, w_ple, g_fin, rows, nb, fhist, load_hist,
                 store_hist, write_y)


def _const_spec(shape, grid_rank):
    zeros = (0,) * len(shape)
    if grid_rank == 2:
        index_map = lambda b, t: zeros
    else:
        index_map = lambda i: zeros
    return pl.BlockSpec(shape, index_map, pipeline_mode=pl.Buffered(1))


def _pack_bf16_rows(w):
    lead = w.shape[:-2]
    n = w.shape[-1]
    w2 = w.reshape((-1, n))
    k = w2.shape[0]
    bk = min(k, PACK_BLOCK_ROWS)
    bn = min(n, PACK_BLOCK_COLS)
    assert k % bk == 0 and n % bn == 0 and w.shape[-2] % 2 == 0
    words = pl.pallas_call(
        _pack_kernel,
        grid=(k // bk, n // bn),
        in_specs=[pl.BlockSpec((bk, bn), lambda i, j: (i, j))],
        out_specs=pl.BlockSpec((bk // 2, bn), lambda i, j: (i, j)),
        out_shape=jax.ShapeDtypeStruct((k // 2, n), jnp.uint32),
        name="pack_weight",
    )(w2)
    return words.reshape(lead + (w.shape[-2] // 2, n))


def _pack_kernel(w_ref, o_ref):
    o_ref[...] = pltpu.bitcast(w_ref[...].astype(BF16), jnp.uint32)


def _block_diag_gate(wa, wx):
    hh = H_A // 2
    eye = jnp.eye(hh, dtype=wa.dtype)

    def bd(w):
        return jnp.einsum('hij,hg->higj', w, eye).reshape(GATE_HALF, GATE_HALF)

    return jnp.stack([jnp.concatenate([bd(wa[h * hh:(h + 1) * hh]), bd(wx[h * hh:(h + 1) * hh])], axis=1)
                      for h in range(2)])


def kernel(x_prompt, x_sample, p_prompt, p_sample, state_rglru_h, state_rglru_conv, state_ffn_conv, g_mix_norm, w_in, conv_a_w, conv_a_b, lru_wa, lru_ba, lru_wx, lru_bx, lru_a_param, g_out_a, ln_v_g, ln_v_b, sgu_w, sgu_b, g_out_b, w_out, g_ffn_norm, w_up, ffn_conv_w, ffn_conv_b, w_down, g_ple_norm, w_ple_gate, w_ple, g_final):
    assert w_in.shape[0] == 1
    nbp, seq, _ = x_prompt.shape
    nbs, steps, _ = x_sample.shape
    ple = p_prompt.shape[-1]
    tm = PROMPT_BLOCK_ROWS
    nb = SAMPLE_GROUP
    assert seq % tm == 0 and tm % CHUNK == 0 and nbs % nb == 0 and nb % SLAB == 0 and steps <= CHUNK
    row = lambda v: v.reshape(1, -1)

    head = [row(g_mix_norm[0]), _pack_bf16_rows(w_in[0]), conv_a_w[0], row(conv_a_b[0]),
            _pack_bf16_rows(_block_diag_gate(lru_wa[0], lru_wx[0])), row(lru_ba[0]), row(lru_bx[0]),
            row(lru_a_param[0]), row(g_out_a[0]), row(ln_v_g[0]), row(ln_v_b[0])]
    tail = [row(g_out_b[0]), _pack_bf16_rows(w_out[0]), row(g_ffn_norm[0]),
            _pack_bf16_rows(w_up[0]), ffn_conv_w[0], row(ffn_conv_b[0]), _pack_bf16_rows(w_down[0]),
            row(g_ple_norm[0]), _pack_bf16_rows(w_ple_gate[0]), _pack_bf16_rows(w_ple[0]), row(g_final)]

    sgb_p = jnp.repeat(jnp.transpose(sgu_b[0]), HD_B, axis=1)
    p_args = head + [sgu_w[0], sgb_p] + tail
    scratch_p = [
        pltpu.VMEM((tm, D_MODEL), BF16),
        pltpu.VMEM((tm, D_MODEL + W_A), F32),
        pltpu.VMEM((HIST + tm, W_A), F32),
        pltpu.VMEM((tm, W_A), F32),
        pltpu.VMEM((tm, W_A), BF16),
        pltpu.VMEM((tm, 2 * W_A), F32),
        pltpu.VMEM((tm, W_A), F32),
        pltpu.VMEM((tm, W_A), F32),
        pltpu.VMEM((tm, W_A), F32),
        pltpu.VMEM((SUBLANES, W_A), F32),
        pltpu.VMEM((tm, W_B), BF16),
        pltpu.VMEM((tm, W_A + W_B), BF16),
        pltpu.VMEM((tm, D_MODEL), F32),
        pltpu.VMEM((HIST + tm, 2 * FF_CHUNK), F32),
        pltpu.VMEM((HIST + tm, 2 * FF_CHUNK), F32),
        pltpu.VMEM((HIST, 2 * D_FF), F32),
        pltpu.VMEM((tm, FF_CHUNK), BF16),
        pltpu.VMEM((tm, FF_CHUNK), BF16),
        pltpu.VMEM((tm, D_MODEL), F32),
    ]
    y_p, ht_p, cnew_p, fnew_p = pl.pallas_call(
        functools.partial(_prompt_kernel, tm=tm),
        grid=(nbp, seq // tm),
        in_specs=[pl.BlockSpec((1, tm, D_MODEL), lambda b, t: (b, t, 0)),
                  pl.BlockSpec((1, tm, ple), lambda b, t: (b, t, 0))]
                 + [_const_spec(a.shape, 2) for a in p_args],
        out_specs=[pl.BlockSpec((1, tm, D_MODEL), lambda b, t: (b, t, 0)),
                   pl.BlockSpec((1, 1, W_A), lambda b, t: (b, 0, 0)),
                   pl.BlockSpec((1, CONV_A - 1, W_A), lambda b, t: (b, 0, 0)),
                   pl.BlockSpec((1, CONV_F - 1, 2 * D_FF), lambda b, t: (b, 0, 0))],
        out_shape=[jax.ShapeDtypeStruct((nbp, seq, D_MODEL), F32),
                   jax.ShapeDtypeStruct((nbp, 1, W_A), F32),
                   jax.ShapeDtypeStruct((nbp, CONV_A - 1, W_A), F32),
                   jax.ShapeDtypeStruct((nbp, CONV_F - 1, 2 * D_FF), F32)],
        scratch_shapes=scratch_p,
        compiler_params=pltpu.CompilerParams(dimension_semantics=("arbitrary", "arbitrary"),
                                             vmem_limit_bytes=VMEM_LIMIT_BYTES),
        name="prompt_layer",
    )(x_prompt, p_prompt[0], *p_args)

    rows = nb * steps
    ahist = (CONV_A - 1) * nb
    fhist = (CONV_F - 1) * nb
    tmaj = lambda a: jnp.swapaxes(a, 0, 1)
    sgw_s = jnp.repeat(jnp.transpose(sgu_w[0, :, :steps, :steps], (1, 2, 0)).reshape(steps * steps, H_B),
                       HD_B, axis=1)
    sgb_s = jnp.repeat(jnp.transpose(sgu_b[0, :, :steps]), HD_B, axis=1)
    w_args = head + [sgw_s, sgb_s] + tail
    scratch_s = [
        pltpu.VMEM((rows, D_MODEL), BF16),
        pltpu.VMEM((rows, D_MODEL + W_A), F32),
        pltpu.VMEM((ahist + rows, W_A), F32),
        pltpu.VMEM((rows, W_A), F32),
        pltpu.VMEM((rows, W_A), BF16),
        pltpu.VMEM((rows, 2 * W_A), F32),
        pltpu.VMEM((rows, W_A), F32),
        pltpu.VMEM((rows, W_A), F32),
        pltpu.VMEM((rows, W_A), F32),
        pltpu.VMEM((rows, W_A + W_B), BF16),
        pltpu.VMEM((rows, D_MODEL), F32),
        pltpu.VMEM((rows, ple), F32),
        pltpu.VMEM((fhist + rows, 2 * FF_CHUNK), F32),
        pltpu.VMEM((fhist + rows, 2 * FF_CHUNK), F32),
        pltpu.VMEM((rows, FF_CHUNK), BF16),
        pltpu.VMEM((rows, FF_CHUNK), BF16),
        pltpu.VMEM((rows, D_MODEL), F32),
    ]
    y_s, ht_s, cnew_s, vn_s, fnew_s = pl.pallas_call(
        functools.partial(_sample_kernel, nb=nb, steps=steps),
        grid=(nbs // nb,),
        in_specs=[pl.BlockSpec((steps, nb, D_MODEL), lambda i: (0, i, 0)),
                  pl.BlockSpec((steps, nb, ple), lambda i: (0, i, 0)),
                  pl.BlockSpec((nb, W_A), lambda i: (i, 0)),
                  pl.BlockSpec((CONV_A - 1, nb, W_A), lambda i: (0, i, 0)),
                  pl.BlockSpec((CONV_F - 1, nb, 2 * D_FF), lambda i: (0, i, 0))]
                 + [_const_spec(a.shape, 1) for a in w_args],
        out_specs=[pl.BlockSpec((steps, nb, D_MODEL), lambda i: (0, i, 0)),
                   pl.BlockSpec((nb, W_A), lambda i: (i, 0)),
                   pl.BlockSpec((CONV_A - 1, nb, W_A), lambda i: (0, i, 0)),
                   pl.BlockSpec((steps, nb, W_B), lambda i: (0, i, 0)),
                   pl.BlockSpec((CONV_F - 1, nb, 2 * D_FF), lambda i: (0, i, 0))],
        out_shape=[jax.ShapeDtypeStruct((steps, nbs, D_MODEL), F32),
                   jax.ShapeDtypeStruct((nbs, W_A), F32),
                   jax.ShapeDtypeStruct((CONV_A - 1, nbs, W_A), F32),
                   jax.ShapeDtypeStruct((steps, nbs, W_B), F32),
                   jax.ShapeDtypeStruct((CONV_F - 1, nbs, 2 * D_FF), F32)],
        scratch_shapes=scratch_s,
        compiler_params=pltpu.CompilerParams(dimension_semantics=("arbitrary",),
                                             vmem_limit_bytes=VMEM_LIMIT_BYTES),
        name="sample_layer",
    )(tmaj(x_sample), tmaj(p_sample[0]), state_rglru_h[0], tmaj(state_rglru_conv[0]),
      tmaj(state_ffn_conv[0]), *w_args)

    return (y_p, tmaj(y_s), tmaj(ht_p), ht_s[None], cnew_p[None], tmaj(cnew_s)[None],
            tmaj(vn_s)[None], fnew_p[None], tmaj(fnew_s)[None])
```

```python
import functools

import jax
import jax.numpy as jnp
from jax import lax
from jax.experimental import pallas as pl
from jax.experimental.pallas import tpu as pltpu

F32 = jnp.float32
BF16 = jnp.bfloat16

D_MODEL = 1024
W_A = 512
W_B = 512
H_A = 8
BW_A = W_A // H_A
H_B = 4
HD_B = W_B // H_B
CHUNK = 128
D_FF = 3072
CONV_A = 4
CONV_F = 3
C_RG = 8.0
EPS = 1e-6

SUBLANES = 8
HIST = SUBLANES
GATE_HALF = W_A // 2
FF_CHUNK = 512
N_FF_CHUNKS = D_FF // FF_CHUNK
SLAB = 32
MXU_PIECE = 256
PACK_BLOCK_ROWS = 512
PACK_BLOCK_COLS = 2048
PROMPT_BLOCK_ROWS = 256
SAMPLE_GROUP = 32
VMEM_LIMIT_BYTES = 56 * 1024 * 1024


def _bc(tile, rows):
    return jnp.concatenate([tile] * (rows // SUBLANES), axis=0)


def _rms(x, g):
    ms = jnp.mean(x * x, axis=-1, keepdims=True)
    return x * lax.rsqrt(ms + EPS) * _bc(g, x.shape[0])


def _layernorm(x, g, b):
    mu = jnp.mean(x, axis=-1, keepdims=True)
    xc = x - mu
    rows = x.shape[0]
    return xc * lax.rsqrt(jnp.mean(xc * xc, axis=-1, keepdims=True) + EPS) * _bc(g, rows) + _bc(b, rows)


def _sigmoid(x):
    return 1.0 / (1.0 + jnp.exp(-x))


def _softplus(x):
    return jnp.maximum(x, 0.0) + jnp.log(1.0 + jnp.exp(-jnp.abs(x)))


def _dot(a, b):
    return jnp.dot(a, b, preferred_element_type=F32)


def _fresh_rows(ref, salt):
    zero = jnp.minimum(pl.program_id(0), 0) * salt
    return ref[pl.ds(pl.multiple_of(zero, 16), ref.shape[0]), :]


def _wdot(a, w_words):
    return _dot(a, pltpu.bitcast(w_words, BF16))


def _ff_cols(c, half):
    start = half * D_FF + c * FF_CHUNK
    return slice(start, start + FF_CHUNK)


def _norm_to_bf16(src_ref, g_ref, dst_ref, rows):
    g = g_ref[...]
    for r0 in range(0, rows, SLAB):
        dst_ref[r0:r0 + SLAB, :] = _rms(src_ref[r0:r0 + SLAB, :], g).astype(BF16)


def _gate_stage(xc_ref, xcb_ref, g_ref, a_ref, u_ref, w_gate, ba, bx, apar, rows):
    for hf in range(2):
        c0 = hf * GATE_HALF
        g_ref[:, 2 * c0:2 * c0 + 2 * GATE_HALF] = _wdot(xcb_ref[:, c0:c0 + GATE_HALF], w_gate[hf])
    sp = _softplus(-apar[...])
    for r0 in range(0, rows, SLAB):
        for hf in range(2):
            c0 = hf * GATE_HALF
            cs = slice(c0, c0 + GATE_HALF)
            r = _sigmoid(g_ref[r0:r0 + SLAB, 2 * c0:2 * c0 + GATE_HALF] + _bc(ba[:, cs], SLAB))
            i = _sigmoid(g_ref[r0:r0 + SLAB, 2 * c0 + GATE_HALF:2 * c0 + 2 * GATE_HALF] + _bc(bx[:, cs], SLAB))
            a = jnp.exp((-C_RG) * r * _bc(sp[:, cs], SLAB))
            mult = jnp.sqrt(1.0 - a * a)
            a_ref[r0:r0 + SLAB, cs] = a
            u_ref[r0:r0 + SLAB, cs] = xc_ref[r0:r0 + SLAB, cs] * i * mult


def _branch_a_out(hs_ref, z_ref, g_oa, ymix_ref, rows):
    g = g_oa[...]
    for r0 in range(0, rows, SLAB):
        ga = z_ref[r0:r0 + SLAB, 0:W_A]
        ymix_ref[r0:r0 + SLAB, 0:W_A] = _rms(hs_ref[r0:r0 + SLAB, :] * jax.nn.gelu(ga), g).astype(BF16)


def _vn_stage(z_ref, lng, lnb, rows, write):
    g = lng[...]
    b = lnb[...]
    for r0 in range(0, rows, SLAB):
        vb = z_ref[r0:r0 + SLAB, W_A + W_B:W_A + 2 * W_B]
        write(r0, _layernorm(jax.nn.gelu(vb), g, b))


def _ffn_items(h_ref, n_ref, up2_ref, act2_ref, acc_ref, gz_ref, pe_ref, load_p, g_ffn, w_up, fcw, fcb,
               w_down, g_ple, w_pg, w_ple, g_fin, rows, shift, hist_rows, load_hist, store_hist, write_y):
    n_up = 2 * FF_CHUNK // MXU_PIECE
    n_down = D_MODEL // MXU_PIECE
    slabs = list(range(0, rows, SLAB))
    items = []

    def norm_in(r0):
        h = h_ref[r0:r0 + SLAB, :]
        acc_ref[r0:r0 + SLAB, :] = h
        n_ref[r0:r0 + SLAB, :] = _rms(h, g_ffn[...]).astype(BF16)

    def up_piece(c, j):
        up_ref = up2_ref[c % 2]
        cs = slice(j * MXU_PIECE, (j + 1) * MXU_PIECE)
        half, off = divmod(j * MXU_PIECE, FF_CHUNK)
        w0 = half * D_FF + c * FF_CHUNK + off
        if j == 0:
            load_hist(c, up_ref)
        lhs = _fresh_rows(n_ref, 1 + c * n_up + j)
        up_ref[hist_rows:hist_rows + rows, cs] = _wdot(lhs, w_up[:, w0:w0 + MXU_PIECE])

    def down_piece(c, j):
        cs = slice(j * MXU_PIECE, (j + 1) * MXU_PIECE)
        lhs = _fresh_rows(act2_ref[c % 2], 1 + c * n_down + j)
        acc_ref[:, cs] += _wdot(lhs, w_down[c * FF_CHUNK // 2:(c + 1) * FF_CHUNK // 2, cs])

    def conv_act_slab(c, r0, last):
        up_ref = up2_ref[c % 2]
        halves = []
        for half in range(2):
            cs = slice(half * FF_CHUNK, (half + 1) * FF_CHUNK)
            ws = _ff_cols(c, half)
            cv = _bc(fcb[:, ws], SLAB)
            for j in range(CONV_F):
                off = hist_rows + r0 - (CONV_F - 1 - j) * shift
                cv = cv + up_ref[off:off + SLAB, cs] * _bc(fcw[j, :, ws], SLAB)
            halves.append(cv)
        act2_ref[c % 2][r0:r0 + SLAB, :] = (jax.nn.gelu(halves[0]) * halves[1]).astype(BF16)
        if last:
            store_hist(c, up_ref)

    def embed():
        pe_ref[...] = _wdot(load_p().astype(BF16), w_ple[...])

    def norm_mid(r0):
        n_ref[r0:r0 + SLAB, :] = _rms(acc_ref[r0:r0 + SLAB, :], g_ple[...]).astype(BF16)

    def gate_piece(j):
        cs = slice(j * MXU_PIECE, (j + 1) * MXU_PIECE)
        gz_ref[:, cs] = _wdot(_fresh_rows(n_ref, 101 + j), w_pg[:, cs])

    def finish(r0):
        gate = _sigmoid(gz_ref[r0:r0 + SLAB, 0:D_MODEL])
        h3 = acc_ref[r0:r0 + SLAB, :] + pe_ref[r0:r0 + SLAB, :] * gate
        write_y(r0, _rms(h3, g_fin[...]))

    mm = rows
    for r0 in slabs:
        items.append((SLAB * 3 // 2, functools.partial(norm_in, r0)))
    for j in range(n_up):
        items.append((mm, functools.partial(up_piece, 0, j)))
    items.append((mm // 2, embed))
    for c in range(N_FF_CHUNKS + 1):
        pieces = []
        for j in range(max(n_up, n_down)):
            if c + 1 < N_FF_CHUNKS and j < n_up:
                pieces.append((mm, functools.partial(up_piece, c + 1, j)))
            if c >= 1 and j < n_down:
                pieces.append((mm // 2, functools.partial(down_piece, c - 1, j)))
        if c < N_FF_CHUNKS:
            done = 0
            for i, r0 in enumerate(slabs):
                items.append((SLAB * 5, functools.partial(conv_act_slab, c, r0, i == len(slabs) - 1)))
                upto = (i + 1) * len(pieces) // len(slabs)
                items.extend(pieces[done:upto])
                done = upto
        else:
            items.extend(pieces)
    for r0 in slabs:
        items.append((SLAB * 3 // 2, functools.partial(norm_mid, r0)))
    for j in range(n_down):
        items.append((mm, functools.partial(gate_piece, j)))
    for r0 in slabs:
        items.append((SLAB * 3, functools.partial(finish, r0)))
    return items


def _run_items(*streams):
    totals = [sum(c for c, _ in s) for s in streams]
    pos = [0] * len(streams)
    spent = [0.0] * len(streams)
    while any(p < len(s) for p, s in zip(pos, streams)):
        k = min((i for i in range(len(streams)) if pos[i] < len(streams[i])),
                key=lambda i: spent[i] / totals[i])
        cost, fn = streams[k][pos[k]]
        fn()
        pos[k] += 1
        spent[k] += cost


def _prompt_mixer_items(x_ref, h_mid_ref, n_ref, z_ref, xa_ref, xc_ref, xcb_ref, g_ref, a_ref, u_ref, hs_ref,
                        carry_ref, sp_ref, vn_ref, ymix_ref, g_mix, w_in, caw, cab, w_gate, ba, bx, apar,
                        g_oa, lng, lnb, sgw, sgb, g_ob, w_out, tm, reset_row):
    slabs = list(range(0, tm, SLAB))
    mm = tm
    items = []

    def norm_in(r0):
        n_ref[r0:r0 + SLAB, :] = _rms(x_ref[0, r0:r0 + SLAB, :], g_mix[...]).astype(BF16)

    def in_piece(j):
        c0 = j * MXU_PIECE
        res = _wdot(_fresh_rows(n_ref, 201 + j), w_in[:, c0:c0 + MXU_PIECE])
        if c0 < W_A:
            xa_ref[HIST:HIST + tm, c0:c0 + MXU_PIECE] = res
        else:
            z_ref[:, c0 - W_A:c0 - W_A + MXU_PIECE] = res

    def conv(r0, last):
        cv = _bc(cab[...], SLAB)
        for j in range(CONV_A):
            off = HIST + r0 - (CONV_A - 1 - j)
            cv = cv + xa_ref[off:off + SLAB, :] * _bc(caw[j], SLAB)
        xc_ref[r0:r0 + SLAB, :] = cv
        xcb_ref[r0:r0 + SLAB, :] = cv.astype(BF16)
        if last:
            xa_ref[0:HIST, :] = xa_ref[tm:tm + HIST, :]
            sp_ref[...] = _softplus(-apar[...])

    def gate_piece(hf):
        c0 = hf * GATE_HALF
        zero = jnp.minimum(pl.program_id(0), 0) * (211 + hf)
        lhs = xcb_ref[pl.ds(pl.multiple_of(zero, 16), tm), c0:c0 + GATE_HALF]
        g_ref[:, 2 * c0:2 * c0 + 2 * GATE_HALF] = _wdot(lhs, w_gate[hf])

    def gate_ew(r0, hf):
        c0 = hf * GATE_HALF
        cs = slice(c0, c0 + GATE_HALF)
        r = _sigmoid(g_ref[r0:r0 + SLAB, 2 * c0:2 * c0 + GATE_HALF] + _bc(ba[:, cs], SLAB))
        i = _sigmoid(g_ref[r0:r0 + SLAB, 2 * c0 + GATE_HALF:2 * c0 + 2 * GATE_HALF] + _bc(bx[:, cs], SLAB))
        a = jnp.exp((-C_RG) * r * _bc(sp_ref[:, cs], SLAB))
        mult = jnp.sqrt(1.0 - a * a)
        if r0 == 0:
            row = lax.broadcasted_iota(jnp.int32, (SLAB, GATE_HALF), 0)
            mult = jnp.where(row == reset_row, 1.0, mult)
        a_ref[r0:r0 + SLAB, cs] = a
        u_ref[r0:r0 + SLAB, cs] = xc_ref[r0:r0 + SLAB, cs] * i * mult

    def scan_group(r0):
        row = lax.broadcasted_iota(jnp.int32, (SUBLANES, W_A), 0)
        a = a_ref[r0:r0 + SUBLANES, :]
        u = u_ref[r0:r0 + SUBLANES, :]
        for d in (1, 2, 4):
            keep = row >= d
            a_s = jnp.where(keep, pltpu.roll(a, d, 0), 1.0)
            u_s = jnp.where(keep, pltpu.roll(u, d, 0), 0.0)
            u = a * u_s + u
            a = a * a_s
        h = a * carry_ref[...] + u
        hs_ref[r0:r0 + SUBLANES, :] = h
        carry_ref[...] = jnp.broadcast_to(h[SUBLANES - 1:SUBLANES, :], (SUBLANES, W_A))

    def a_out(r0):
        ga = z_ref[r0:r0 + SLAB, 0:W_A]
        ymix_ref[r0:r0 + SLAB, 0:W_A] = _rms(hs_ref[r0:r0 + SLAB, :] * jax.nn.gelu(ga), g_oa[...]).astype(BF16)

    def vn(r0):
        vb = z_ref[r0:r0 + SLAB, W_A + W_B:W_A + 2 * W_B]
        vn_ref[r0:r0 + SLAB, :] = _layernorm(jax.nn.gelu(vb), lng[...], lnb[...]).astype(BF16)

    def sgu(c0):
        ti = lax.broadcasted_iota(jnp.int32, (CHUNK, CHUNK), 0)
        si = lax.broadcasted_iota(jnp.int32, (CHUNK, CHUNK), 1)
        heads = []
        for hd in range(H_B):
            cs = slice(hd * HD_B, (hd + 1) * HD_B)
            wmix = jnp.where(si <= ti, sgw[hd], 0.0).astype(BF16)
            mixed = _dot(wmix, vn_ref[c0:c0 + CHUNK, cs]) + sgb[:, cs]
            ub = z_ref[c0:c0 + CHUNK, W_A + hd * HD_B:W_A + (hd + 1) * HD_B]
            heads.append(jax.nn.gelu(ub) * mixed)
        yb = jnp.concatenate(heads, axis=-1)
        ymix_ref[c0:c0 + CHUNK, W_A:] = _rms(yb, g_ob[...]).astype(BF16)

    def out_piece(j):
        cs = slice(j * MXU_PIECE, (j + 1) * MXU_PIECE)
        h_mid_ref[:, cs] = x_ref[0, :, cs] + _wdot(_fresh_rows(ymix_ref, 221 + j), w_out[:, cs])

    for r0 in slabs:
        items.append((SLAB * 3 // 2, functools.partial(norm_in, r0)))
    for j in range((2 * W_A + 2 * W_B) // MXU_PIECE):
        items.append((mm, functools.partial(in_piece, j)))
    for i, r0 in enumerate(slabs):
        items.append((SLAB * 2, functools.partial(conv, r0, i == len(slabs) - 1)))
    for hf in range(2):
        items.append((mm // 2, functools.partial(gate_piece, hf)))
    for r0 in slabs:
        for hf in range(2):
            items.append((SLAB * 3 // 2, functools.partial(gate_ew, r0, hf)))
    for r0 in range(0, tm, SUBLANES):
        items.append((SUBLANES * 3, functools.partial(scan_group, r0)))
    for r0 in slabs:
        items.append((SLAB * 2, functools.partial(a_out, r0)))
    for r0 in slabs:
        items.append((SLAB * 2, functools.partial(vn, r0)))
    for c0 in range(0, tm, CHUNK):
        items.append((CHUNK * 5 // 2, functools.partial(sgu, c0)))
    for j in range(D_MODEL // MXU_PIECE):
        items.append((mm, functools.partial(out_piece, j)))
    return items


def _prompt_kernel(x_ref, p_ref, g_mix, w_in, caw, cab, w_gate, ba, bx, apar, g_oa, lng, lnb, sgw,
                   sgb, g_ob, w_out, g_ffn, w_up, fcw, fcb, w_down, g_ple, w_pg, w_ple, g_fin,
                   y_ref, ht_ref, cnew_ref, fnew_ref,
                   h_mid_ref, n1_ref, z_ref, xa_ref, xc_ref, xcb_ref, g_ref, a_ref, u_ref, hs_ref, carry_ref,
                   sp_ref, vn_ref, ymix_ref, n2_ref, up_ref, upb_ref, fhist_ref, act_ref, actb_ref, acc_ref,
                   gz_ref, pe_ref, *, tm):
    t = pl.program_id(1)

    @pl.when(t == 0)
    def _():
        xa_ref[0:HIST, :] = jnp.zeros((HIST, W_A), F32)
        carry_ref[...] = jnp.zeros((SUBLANES, W_A), F32)
        fhist_ref[...] = jnp.zeros(fhist_ref.shape, F32)

    reset_row = jnp.where(t == 0, 0, -1)
    mixer = _prompt_mixer_items(x_ref, h_mid_ref, n1_ref, z_ref, xa_ref, xc_ref, xcb_ref, g_ref, a_ref, u_ref,
                                hs_ref, carry_ref, sp_ref, vn_ref, ymix_ref, g_mix, w_in, caw, cab, w_gate,
                                ba, bx, apar, g_oa, lng, lnb, sgw, sgb, g_ob, w_out, tm, reset_row)

    def load_hist(c, up):
        for half in range(2):
            up[0:HIST, half * FF_CHUNK:(half + 1) * FF_CHUNK] = fhist_ref[:, _ff_cols(c, half)]

    def store_hist(c, up):
        for half in range(2):
            cs = slice(half * FF_CHUNK, (half + 1) * FF_CHUNK)
            fhist_ref[:, _ff_cols(c, half)] = up[tm:tm + HIST, cs]
            fnew_ref[0, :, _ff_cols(c, half)] = up[HIST + tm - (CONV_F - 1):HIST + tm, cs]

    def write_y(r0, y):
        y_ref[0, r0:r0 + SLAB, :] = y

    ffn = _ffn_items(h_mid_ref, n2_ref, (up_ref, upb_ref), (act_ref, actb_ref), acc_ref, gz_ref, pe_ref,
                     lambda: p_ref[0], g_ffn, w_up, fcw, fcb, w_down, g_ple, w_pg, w_ple, g_fin, tm, 1, HIST,
                     load_hist, store_hist, write_y)

    _run_items(mixer)
    _run_items(ffn)

    ht_ref[0] = carry_ref[0:1, :]
    cnew_ref[0] = xa_ref[HIST + tm - (CONV_A - 1):HIST + tm, :]


def _sample_kernel(x_ref, p_ref, h0_ref, chist_ref, fh_ref, g_mix, w_in, caw, cab, w_gate, ba, bx,
                   apar, g_oa, lng, lnb, sgw, sgb, g_ob, w_out, g_ffn, w_up, fcw, fcb, w_down,
                   g_ple, w_pg, w_ple, g_fin,
                   y_ref, ht_ref, cnew_ref, vn_out_ref, fnew_ref,
                   n_ref, z_ref, xa_ref, xc_ref, xcb_ref, g_ref, a_ref, u_ref, hs_ref, ymix_ref,
                   h_ref, p_buf, up_ref, upb_ref, act_ref, actb_ref, acc_ref, *, nb, steps):
    rows = nb * steps
    ahist = (CONV_A - 1) * nb
    fhist = (CONV_F - 1) * nb

    for s in range(steps):
        h_ref[s * nb:(s + 1) * nb, :] = x_ref[s]
        p_buf[s * nb:(s + 1) * nb, :] = p_ref[s]
    _norm_to_bf16(h_ref, g_mix, n_ref, rows)
    for k in range(CONV_A - 1):
        xa_ref[k * nb:(k + 1) * nb, :] = chist_ref[k]
    xa_ref[ahist:ahist + rows, :] = _wdot(n_ref[...], w_in[:, 0:W_A])
    z_ref[...] = _wdot(n_ref[...], w_in[:, W_A:])

    for r0 in range(0, rows, SLAB):
        cv = _bc(cab[...], SLAB)
        for j in range(CONV_A):
            off = r0 + j * nb
            cv = cv + xa_ref[off:off + SLAB, :] * _bc(caw[j], SLAB)
        xc_ref[r0:r0 + SLAB, :] = cv
        xcb_ref[r0:r0 + SLAB, :] = cv.astype(BF16)
    for k in range(CONV_A - 1):
        cnew_ref[k] = xa_ref[rows + k * nb:rows + (k + 1) * nb, :]

    _gate_stage(xc_ref, xcb_ref, g_ref, a_ref, u_ref, w_gate, ba, bx, apar, rows)

    for b0 in range(0, nb, SLAB):
        h = h0_ref[b0:b0 + SLAB, :]
        for s in range(steps):
            r0 = s * nb + b0
            h = a_ref[r0:r0 + SLAB, :] * h + u_ref[r0:r0 + SLAB, :]
            hs_ref[r0:r0 + SLAB, :] = h
        ht_ref[b0:b0 + SLAB, :] = h

    _branch_a_out(hs_ref, z_ref, g_oa, ymix_ref, rows)

    def write_vn(r0, vn):
        vn_out_ref[r0 // nb, r0 % nb:r0 % nb + SLAB, :] = vn

    _vn_stage(z_ref, lng, lnb, rows, write_vn)

    gob = g_ob[...]
    for b0 in range(0, nb, SLAB):
        for tt in range(steps):
            r0 = tt * nb + b0
            mixed = _bc(sgb[tt], SLAB)
            for s in range(tt + 1):
                k = tt * steps + s
                mixed = mixed + vn_out_ref[s, b0:b0 + SLAB, :] * _bc(sgw[k], SLAB)
            yb = jax.nn.gelu(z_ref[r0:r0 + SLAB, W_A:W_A + W_B]) * mixed
            ymix_ref[r0:r0 + SLAB, W_A:] = _rms(yb, gob).astype(BF16)

    h_ref[...] += _wdot(ymix_ref[...], w_out[...])

    def load_hist(c, up):
        for k in range(CONV_F - 1):
            for half in range(2):
                cs = slice(half * FF_CHUNK, (half + 1) * FF_CHUNK)
                up[k * nb:(k + 1) * nb, cs] = fh_ref[k, :, _ff_cols(c, half)]

    def store_hist(c, up):
        for k in range(CONV_F - 1):
            for half in range(2):
                cs = slice(half * FF_CHUNK, (half + 1) * FF_CHUNK)
                fnew_ref[k, :, _ff_cols(c, half)] = up[rows + k * nb:rows + (k + 1) * nb, cs]

    def write_y(r0, y):
        y_ref[r0 // nb, r0 % nb:r0 % nb + SLAB, :] = y

    _run_items(_ffn_items(h_ref, n_ref, (up_ref, upb_ref), (act_ref, actb_ref), acc_ref, z_ref, g_ref,
                          lambda: p_buf[...], g_ffn, w_up, fcw, fcb, w_down, g_ple, w_pg, w_ple, g_fin,
                          rows, nb, fhist, load_hist, store_hist, write_y))


def _const_spec(shape):
    zeros = (0,) * len(shape)
    return pl.BlockSpec(shape, lambda *_: zeros, pipeline_mode=pl.Buffered(1))


def _pack_bf16_rows(w):
    lead = w.shape[:-2]
    n = w.shape[-1]
    w2 = w.reshape((-1, n))
    k = w2.shape[0]
    bk = min(k, PACK_BLOCK_ROWS)
    bn = min(n, PACK_BLOCK_COLS)
    assert k % bk == 0 and n % bn == 0 and w.shape[-2] % 2 == 0
    words = pl.pallas_call(
        _pack_kernel,
        grid=(k // bk, n // bn),
        in_specs=[pl.BlockSpec((bk, bn), lambda i, j: (i, j))],
        out_specs=pl.BlockSpec((bk // 2, bn), lambda i, j: (i, j)),
        out_shape=jax.ShapeDtypeStruct((k // 2, n), jnp.uint32),
        name="pack_weight",
    )(w2)
    return words.reshape(lead + (w.shape[-2] // 2, n))


def _pack_kernel(w_ref, o_ref):
    o_ref[...] = pltpu.bitcast(w_ref[...].astype(BF16), jnp.uint32)


def _block_diag_gate(wa, wx):
    hh = H_A // 2
    eye = jnp.eye(hh, dtype=wa.dtype)

    def bd(w):
        return jnp.einsum('hij,hg->higj', w, eye).reshape(GATE_HALF, GATE_HALF)

    return jnp.stack([jnp.concatenate([bd(wa[h * hh:(h + 1) * hh]), bd(wx[h * hh:(h + 1) * hh])], axis=1)
                      for h in range(2)])


def kernel(x_prompt, x_sample, p_prompt, p_sample, state_rglru_h, state_rglru_conv, state_ffn_conv, g_mix_norm, w_in, conv_a_w, conv_a_b, lru_wa, lru_ba, lru_wx, lru_bx, lru_a_param, g_out_a, ln_v_g, ln_v_b, sgu_w, sgu_b, g_out_b, w_out, g_ffn_norm, w_up, ffn_conv_w, ffn_conv_b, w_down, g_ple_norm, w_ple_gate, w_ple, g_final):
    assert w_in.shape[0] == 1
    nbp, seq, _ = x_prompt.shape
    nbs, steps, _ = x_sample.shape
    ple = p_prompt.shape[-1]
    tm = PROMPT_BLOCK_ROWS
    nb = SAMPLE_GROUP
    assert seq % tm == 0 and tm % CHUNK == 0 and nbs % nb == 0 and nb % SLAB == 0 and steps <= CHUNK
    row = lambda v: jnp.broadcast_to(v.reshape(1, -1), (SUBLANES, v.size))
    row_set = lambda m: jnp.broadcast_to(m[:, None, :], (m.shape[0], SUBLANES, m.shape[1]))

    head = [row(g_mix_norm[0]), _pack_bf16_rows(w_in[0]), row_set(conv_a_w[0]), row(conv_a_b[0]),
            _pack_bf16_rows(_block_diag_gate(lru_wa[0], lru_wx[0])), row(lru_ba[0]), row(lru_bx[0]),
            row(lru_a_param[0]), row(g_out_a[0]), row(ln_v_g[0]), row(ln_v_b[0])]
    tail = [row(g_out_b[0]), _pack_bf16_rows(w_out[0]), row(g_ffn_norm[0]),
            _pack_bf16_rows(w_up[0]), row_set(ffn_conv_w[0]), row(ffn_conv_b[0]), _pack_bf16_rows(w_down[0]),
            row(g_ple_norm[0]), _pack_bf16_rows(w_ple_gate[0]), _pack_bf16_rows(w_ple[0]), row(g_final)]

    sgb_p = jnp.repeat(jnp.transpose(sgu_b[0]), HD_B, axis=1)
    p_args = head + [sgu_w[0], sgb_p] + tail
    scratch_p = [
        pltpu.VMEM((tm, D_MODEL), F32),
        pltpu.VMEM((tm, D_MODEL), BF16),
        pltpu.VMEM((tm, D_MODEL + W_A), F32),
        pltpu.VMEM((HIST + tm, W_A), F32),
        pltpu.VMEM((tm, W_A), F32),
        pltpu.VMEM((tm, W_A), BF16),
        pltpu.VMEM((tm, 2 * W_A), F32),
        pltpu.VMEM((tm, W_A), F32),
        pltpu.VMEM((tm, W_A), F32),
        pltpu.VMEM((tm, W_A), F32),
        pltpu.VMEM((SUBLANES, W_A), F32),
        pltpu.VMEM((SUBLANES, W_A), F32),
        pltpu.VMEM((tm, W_B), BF16),
        pltpu.VMEM((tm, W_A + W_B), BF16),
        pltpu.VMEM((tm, D_MODEL), BF16),
        pltpu.VMEM((HIST + tm, 2 * FF_CHUNK), F32),
        pltpu.VMEM((HIST + tm, 2 * FF_CHUNK), F32),
        pltpu.VMEM((HIST, 2 * D_FF), F32),
        pltpu.VMEM((tm, FF_CHUNK), BF16),
        pltpu.VMEM((tm, FF_CHUNK), BF16),
        pltpu.VMEM((tm, D_MODEL), F32),
        pltpu.VMEM((tm, D_MODEL), F32),
        pltpu.VMEM((tm, D_MODEL), F32),
    ]
    y_p, ht_p, cnew_p, fnew_p = pl.pallas_call(
        functools.partial(_prompt_kernel, tm=tm),
        grid=(nbp, seq // tm),
        in_specs=[pl.BlockSpec((1, tm, D_MODEL), lambda b, t: (b, t, 0)),
                  pl.BlockSpec((1, tm, ple), lambda b, t: (b, t, 0))]
                 + [_const_spec(a.shape) for a in p_args],
        out_specs=[pl.BlockSpec((1, tm, D_MODEL), lambda b, t: (b, t, 0)),
                   pl.BlockSpec((1, 1, W_A), lambda b, t: (b, 0, 0)),
                   pl.BlockSpec((1, CONV_A - 1, W_A), lambda b, t: (b, 0, 0)),
                   pl.BlockSpec((1, CONV_F - 1, 2 * D_FF), lambda b, t: (b, 0, 0))],
        out_shape=[jax.ShapeDtypeStruct((nbp, seq, D_MODEL), F32),
                   jax.ShapeDtypeStruct((nbp, 1, W_A), F32),
                   jax.ShapeDtypeStruct((nbp, CONV_A - 1, W_A), F32),
                   jax.ShapeDtypeStruct((nbp, CONV_F - 1, 2 * D_FF), F32)],
        scratch_shapes=scratch_p,
        compiler_params=pltpu.CompilerParams(dimension_semantics=("arbitrary", "arbitrary"),
                                             vmem_limit_bytes=VMEM_LIMIT_BYTES),
        name="prompt_layer",
    )(x_prompt, p_prompt[0], *p_args)

    rows = nb * steps
    ahist = (CONV_A - 1) * nb
    fhist = (CONV_F - 1) * nb
    tmaj = lambda a: jnp.swapaxes(a, 0, 1)
    sgw_s = jnp.repeat(jnp.transpose(sgu_w[0, :, :steps, :steps], (1, 2, 0)).reshape(steps * steps, H_B),
                       HD_B, axis=1)
    sgb_s = jnp.repeat(jnp.transpose(sgu_b[0, :, :steps]), HD_B, axis=1)
    w_args = head + [row_set(sgw_s), row_set(sgb_s)] + tail
    scratch_s = [
        pltpu.VMEM((rows, D_MODEL), BF16),
        pltpu.VMEM((rows, D_MODEL + W_A), F32),
        pltpu.VMEM((ahist + rows, W_A), F32),
        pltpu.VMEM((rows, W_A), F32),
        pltpu.VMEM((rows, W_A), BF16),
        pltpu.VMEM((rows, 2 * W_A), F32),
        pltpu.VMEM((rows, W_A), F32),
        pltpu.VMEM((rows, W_A), F32),
        pltpu.VMEM((rows, W_A), F32),
        pltpu.VMEM((rows, W_A + W_B), BF16),
        pltpu.VMEM((rows, D_MODEL), F32),
        pltpu.VMEM((rows, ple), F32),
        pltpu.VMEM((fhist + rows, 2 * FF_CHUNK), F32),
        pltpu.VMEM((fhist + rows, 2 * FF_CHUNK), F32),
        pltpu.VMEM((rows, FF_CHUNK), BF16),
        pltpu.VMEM((rows, FF_CHUNK), BF16),
        pltpu.VMEM((rows, D_MODEL), F32),
    ]
    y_s, ht_s, cnew_s, vn_s, fnew_s = pl.pallas_call(
        functools.partial(_sample_kernel, nb=nb, steps=steps),
        grid=(nbs // nb,),
        in_specs=[pl.BlockSpec((steps, nb, D_MODEL), lambda i: (0, i, 0)),
                  pl.BlockSpec((steps, nb, ple), lambda i: (0, i, 0)),
                  pl.BlockSpec((nb, W_A), lambda i: (i, 0)),
                  pl.BlockSpec((CONV_A - 1, nb, W_A), lambda i: (0, i, 0)),
                  pl.BlockSpec((CONV_F - 1, nb, 2 * D_FF), lambda i: (0, i, 0))]
                 + [_const_spec(a.shape) for a in w_args],
        out_specs=[pl.BlockSpec((steps, nb, D_MODEL), lambda i: (0, i, 0)),
                   pl.BlockSpec((nb, W_A), lambda i: (i, 0)),
                   pl.BlockSpec((CONV_A - 1, nb, W_A), lambda i: (0, i, 0)),
                   pl.BlockSpec((steps, nb, W_B), lambda i: (0, i, 0)),
                   pl.BlockSpec((CONV_F - 1, nb, 2 * D_FF), lambda i: (0, i, 0))],
        out_shape=[jax.ShapeDtypeStruct((steps, nbs, D_MODEL), F32),
                   jax.ShapeDtypeStruct((nbs, W_A), F32),
                   jax.ShapeDtypeStruct((CONV_A - 1, nbs, W_A), F32),
                   jax.ShapeDtypeStruct((steps, nbs, W_B), F32),
                   jax.ShapeDtypeStruct((CONV_F - 1, nbs, 2 * D_FF), F32)],
        scratch_shapes=scratch_s,
        compiler_params=pltpu.CompilerParams(dimension_semantics=("arbitrary",),
                                             vmem_limit_bytes=VMEM_LIMIT_BYTES),
        name="sample_layer",
    )(tmaj(x_sample), tmaj(p_sample[0]), state_rglru_h[0], tmaj(state_rglru_conv[0]),
      tmaj(state_ffn_conv[0]), *w_args)

    return (y_p, tmaj(y_s), tmaj(ht_p), ht_s[None], cnew_p[None], tmaj(cnew_s)[None],
            tmaj(vn_s)[None], fnew_p[None], tmaj(fnew_s)[None])
```

```python
import functools

import jax
import jax.numpy as jnp
from jax import lax
from jax.experimental import pallas as pl
from jax.experimental.pallas import tpu as pltpu

F32 = jnp.float32
BF16 = jnp.bfloat16

D_MODEL = 1024
W_A = 512
W_B = 512
H_A = 8
BW_A = W_A // H_A
H_B = 4
HD_B = W_B // H_B
CHUNK = 128
D_FF = 3072
CONV_A = 4
CONV_F = 3
C_RG = 8.0
EPS = 1e-6

SUBLANES = 8
HIST = SUBLANES
GATE_HALF = W_A // 2
FF_CHUNK = 512
N_FF_CHUNKS = D_FF // FF_CHUNK
SLAB = 32
MXU_PIECE = 256
PACK_BLOCK_ROWS = 512
PACK_BLOCK_COLS = 2048
PROMPT_BLOCK_ROWS = 256
SAMPLE_GROUP = 32
VMEM_LIMIT_BYTES = 56 * 1024 * 1024


def _bc(tile, rows):
    return jnp.concatenate([tile] * (rows // SUBLANES), axis=0)


def _rms(x, g):
    ms = jnp.mean(x * x, axis=-1, keepdims=True)
    return x * lax.rsqrt(ms + EPS) * _bc(g, x.shape[0])


def _layernorm(x, g, b):
    mu = jnp.mean(x, axis=-1, keepdims=True)
    xc = x - mu
    rows = x.shape[0]
    return xc * lax.rsqrt(jnp.mean(xc * xc, axis=-1, keepdims=True) + EPS) * _bc(g, rows) + _bc(b, rows)


def _sigmoid(x):
    return 1.0 / (1.0 + jnp.exp(-x))


def _softplus(x):
    return jnp.maximum(x, 0.0) + jnp.log(1.0 + jnp.exp(-jnp.abs(x)))


def _dot(a, b):
    return jnp.dot(a, b, preferred_element_type=F32)


def _fresh_rows(ref, salt, row0=0, rows=None):
    rows = ref.shape[0] - row0 if rows is None else rows
    zero = jnp.minimum(pl.program_id(0), 0) * salt
    return ref[pl.ds(pl.multiple_of(zero + row0, 16), rows), :]


def _wdot(a, w_words):
    return _dot(a, pltpu.bitcast(w_words, BF16))


def _ff_cols(c, half):
    start = half * D_FF + c * FF_CHUNK
    return slice(start, start + FF_CHUNK)


def _norm_to_bf16(src_ref, g_ref, dst_ref, rows):
    g = g_ref[...]
    for r0 in range(0, rows, SLAB):
        dst_ref[r0:r0 + SLAB, :] = _rms(src_ref[r0:r0 + SLAB, :], g).astype(BF16)


def _gate_stage(xc_ref, xcb_ref, g_ref, a_ref, u_ref, w_gate, ba, bx, apar, rows):
    for hf in range(2):
        c0 = hf * GATE_HALF
        g_ref[:, 2 * c0:2 * c0 + 2 * GATE_HALF] = _wdot(xcb_ref[:, c0:c0 + GATE_HALF], w_gate[hf])
    sp = _softplus(-apar[...])
    for r0 in range(0, rows, SLAB):
        for hf in range(2):
            c0 = hf * GATE_HALF
            cs = slice(c0, c0 + GATE_HALF)
            r = _sigmoid(g_ref[r0:r0 + SLAB, 2 * c0:2 * c0 + GATE_HALF] + _bc(ba[:, cs], SLAB))
            i = _sigmoid(g_ref[r0:r0 + SLAB, 2 * c0 + GATE_HALF:2 * c0 + 2 * GATE_HALF] + _bc(bx[:, cs], SLAB))
            a = jnp.exp((-C_RG) * r * _bc(sp[:, cs], SLAB))
            mult = jnp.sqrt(1.0 - a * a)
            a_ref[r0:r0 + SLAB, cs] = a
            u_ref[r0:r0 + SLAB, cs] = xc_ref[r0:r0 + SLAB, cs] * i * mult


def _branch_a_out(hs_ref, z_ref, g_oa, ymix_ref, rows):
    g = g_oa[...]
    for r0 in range(0, rows, SLAB):
        ga = z_ref[r0:r0 + SLAB, 0:W_A]
        ymix_ref[r0:r0 + SLAB, 0:W_A] = _rms(hs_ref[r0:r0 + SLAB, :] * jax.nn.gelu(ga), g).astype(BF16)


def _vn_stage(z_ref, lng, lnb, rows, write):
    g = lng[...]
    b = lnb[...]
    for r0 in range(0, rows, SLAB):
        vb = z_ref[r0:r0 + SLAB, W_A + W_B:W_A + 2 * W_B]
        write(r0, _layernorm(jax.nn.gelu(vb), g, b))


def _ffn_items(h_ref, n_ref, n_mid_ref, up2_ref, act2_ref, acc_ref, gz_ref, pe_ref, load_p, g_ffn, w_up, fcw, fcb,
               w_down, g_ple, w_pg, w_ple, g_fin, rows, shift, hist_rows, load_hist, store_hist, write_y):
    n_up = 2 * FF_CHUNK // MXU_PIECE
    n_down = D_MODEL // MXU_PIECE
    slabs = list(range(0, rows, SLAB))
    items = []

    def norm_in(r0):
        h = h_ref[r0:r0 + SLAB, :]
        acc_ref[r0:r0 + SLAB, :] = h
        n_ref[r0:r0 + SLAB, :] = _rms(h, g_ffn[...]).astype(BF16)

    def up_piece(c, j):
        up_ref = up2_ref[c % 2]
        cs = slice(j * MXU_PIECE, (j + 1) * MXU_PIECE)
        half, off = divmod(j * MXU_PIECE, FF_CHUNK)
        w0 = half * D_FF + c * FF_CHUNK + off
        if j == 0:
            load_hist(c, up_ref)
        lhs = _fresh_rows(n_ref, 1 + c * n_up + j)
        up_ref[hist_rows:hist_rows + rows, cs] = _wdot(lhs, w_up[:, w0:w0 + MXU_PIECE])

    def down_piece(c, j):
        cs = slice(j * MXU_PIECE, (j + 1) * MXU_PIECE)
        lhs = _fresh_rows(act2_ref[c % len(act2_ref)], 1 + c * n_down + j)
        acc_ref[:, cs] += _wdot(lhs, w_down[c * FF_CHUNK // 2:(c + 1) * FF_CHUNK // 2, cs])

    def conv_act_slab(c, r0, last):
        up_ref = up2_ref[c % 2]
        halves = []
        for half in range(2):
            cs = slice(half * FF_CHUNK, (half + 1) * FF_CHUNK)
            ws = _ff_cols(c, half)
            cv = _bc(fcb[:, ws], SLAB)
            for j in range(CONV_F):
                off = hist_rows + r0 - (CONV_F - 1 - j) * shift
                cv = cv + up_ref[off:off + SLAB, cs] * _bc(fcw[j, :, ws], SLAB)
            halves.append(cv)
        act2_ref[c % len(act2_ref)][r0:r0 + SLAB, :] = (jax.nn.gelu(halves[0]) * halves[1]).astype(BF16)
        if last:
            store_hist(c, up_ref)

    def embed():
        pe_ref[...] = _wdot(load_p().astype(BF16), w_ple[...])

    def norm_mid(r0):
        n_mid_ref[r0:r0 + SLAB, :] = _rms(acc_ref[r0:r0 + SLAB, :], g_ple[...]).astype(BF16)

    def gate_piece(j):
        cs = slice(j * MXU_PIECE, (j + 1) * MXU_PIECE)
        gz_ref[:, cs] = _wdot(_fresh_rows(n_mid_ref, 101 + j), w_pg[:, cs])

    def finish(r0):
        gate = _sigmoid(gz_ref[r0:r0 + SLAB, 0:D_MODEL])
        h3 = acc_ref[r0:r0 + SLAB, :] + pe_ref[r0:r0 + SLAB, :] * gate
        write_y(r0, _rms(h3, g_fin[...]))

    mm = rows
    for r0 in slabs:
        items.append((SLAB * 3 // 2, functools.partial(norm_in, r0)))
    for j in range(n_up):
        items.append((mm, functools.partial(up_piece, 0, j)))
    items.append((mm // 2, embed))
    for c in range(N_FF_CHUNKS + 1):
        pieces = []
        for j in range(max(n_up, n_down)):
            if c + 1 < N_FF_CHUNKS and j < n_up:
                pieces.append((mm, functools.partial(up_piece, c + 1, j)))
            if c >= 1 and j < n_down:
                pieces.append((mm // 2, functools.partial(down_piece, c - 1, j)))
        if c < N_FF_CHUNKS:
            done = 0
            for i, r0 in enumerate(slabs):
                items.append((SLAB * 5, functools.partial(conv_act_slab, c, r0, i == len(slabs) - 1)))
                upto = (i + 1) * len(pieces) // len(slabs)
                items.extend(pieces[done:upto])
                done = upto
        else:
            items.extend(pieces)
    for r0 in slabs:
        items.append((SLAB * 3 // 2, functools.partial(norm_mid, r0)))
    for j in range(n_down):
        items.append((mm, functools.partial(gate_piece, j)))
    for r0 in slabs:
        items.append((SLAB * 3, functools.partial(finish, r0)))
    return items


def _run_items(*streams):
    totals = [sum(c for c, _ in s) for s in streams]
    pos = [0] * len(streams)
    spent = [0.0] * len(streams)
    while any(p < len(s) for p, s in zip(pos, streams)):
        k = min((i for i in range(len(streams)) if pos[i] < len(streams[i])),
                key=lambda i: spent[i] / totals[i])
        cost, fn = streams[k][pos[k]]
        fn()
        pos[k] += 1
        spent[k] += cost


def _merge_items(a, b):
    out = []
    ta, tb = sum(c for c, _ in a), sum(c for c, _ in b)
    ia = ib = 0
    ca = cb = 0.0
    while ia < len(a) or ib < len(b):
        if ib >= len(b) or (ia < len(a) and ca / ta <= cb / tb):
            out.append(a[ia]); ca += a[ia][0]; ia += 1
        else:
            out.append(b[ib]); cb += b[ib][0]; ib += 1
    return out


def _prompt_mixer_items(x_ref, h_mid_ref, n_ref, z_ref, xa_ref, xc_ref, xcb_ref, g_ref, a_ref, u_ref, hs_ref,
                        carry_ref, sp_ref, vn_ref, ymix_ref, g_mix, w_in, caw, cab, w_gate, ba, bx, apar,
                        g_oa, lng, lnb, sgw, sgb, g_ob, w_out, tm, reset_row):
    half = tm // 2
    assert half % CHUNK == 0 and half % SLAB == 0
    mm = half

    def norm_in(r0):
        n_ref[r0:r0 + SLAB, :] = _rms(x_ref[0, r0:r0 + SLAB, :], g_mix[...]).astype(BF16)

    def in_piece(lo, j):
        c0 = j * MXU_PIECE
        res = _wdot(_fresh_rows(n_ref, 201 + j, lo, half), w_in[:, c0:c0 + MXU_PIECE])
        if c0 < W_A:
            xa_ref[HIST + lo:HIST + lo + half, c0:c0 + MXU_PIECE] = res
        else:
            z_ref[lo:lo + half, c0 - W_A:c0 - W_A + MXU_PIECE] = res

    def conv(r0):
        if r0 == 0:
            sp_ref[...] = _softplus(-apar[...])
        cv = _bc(cab[...], SLAB)
        for j in range(CONV_A):
            off = HIST + r0 - (CONV_A - 1 - j)
            cv = cv + xa_ref[off:off + SLAB, :] * _bc(caw[j], SLAB)
        xc_ref[r0:r0 + SLAB, :] = cv
        xcb_ref[r0:r0 + SLAB, :] = cv.astype(BF16)
        if r0 + SLAB == tm:
            xa_ref[0:HIST, :] = xa_ref[tm:tm + HIST, :]

    def gate_piece(lo, hf):
        c0 = hf * GATE_HALF
        zero = jnp.minimum(pl.program_id(0), 0) * (211 + hf)
        lhs = xcb_ref[pl.ds(pl.multiple_of(zero + lo, 16), half), c0:c0 + GATE_HALF]
        g_ref[lo:lo + half, 2 * c0:2 * c0 + 2 * GATE_HALF] = _wdot(lhs, w_gate[hf])

    def gate_ew(r0, hf):
        c0 = hf * GATE_HALF
        cs = slice(c0, c0 + GATE_HALF)
        r = _sigmoid(g_ref[r0:r0 + SLAB, 2 * c0:2 * c0 + GATE_HALF] + _bc(ba[:, cs], SLAB))
        i = _sigmoid(g_ref[r0:r0 + SLAB, 2 * c0 + GATE_HALF:2 * c0 + 2 * GATE_HALF] + _bc(bx[:, cs], SLAB))
        a = jnp.exp((-C_RG) * r * _bc(sp_ref[:, cs], SLAB))
        mult = jnp.sqrt(1.0 - a * a)
        if r0 == 0:
            row = lax.broadcasted_iota(jnp.int32, (SLAB, GATE_HALF), 0)
            mult = jnp.where(row == reset_row, 1.0, mult)
        a_ref[r0:r0 + SLAB, cs] = a
        u_ref[r0:r0 + SLAB, cs] = xc_ref[r0:r0 + SLAB, cs] * i * mult

    def scan_group(r0):
        row = lax.broadcasted_iota(jnp.int32, (SUBLANES, W_A), 0)
        a = a_ref[r0:r0 + SUBLANES, :]
        u = u_ref[r0:r0 + SUBLANES, :]
        for d in (1, 2, 4):
            keep = row >= d
            a_s = jnp.where(keep, pltpu.roll(a, d, 0), 1.0)
            u_s = jnp.where(keep, pltpu.roll(u, d, 0), 0.0)
            u = a * u_s + u
            a = a * a_s
        h = a * carry_ref[...] + u
        hs_ref[r0:r0 + SUBLANES, :] = h
        carry_ref[...] = jnp.broadcast_to(h[SUBLANES - 1:SUBLANES, :], (SUBLANES, W_A))

    def a_out(r0):
        ga = z_ref[r0:r0 + SLAB, 0:W_A]
        ymix_ref[r0:r0 + SLAB, 0:W_A] = _rms(hs_ref[r0:r0 + SLAB, :] * jax.nn.gelu(ga), g_oa[...]).astype(BF16)

    def vn(r0):
        vb = z_ref[r0:r0 + SLAB, W_A + W_B:W_A + 2 * W_B]
        vn_ref[r0:r0 + SLAB, :] = _layernorm(jax.nn.gelu(vb), lng[...], lnb[...]).astype(BF16)

    def sgu(c0):
        ti = lax.broadcasted_iota(jnp.int32, (CHUNK, CHUNK), 0)
        si = lax.broadcasted_iota(jnp.int32, (CHUNK, CHUNK), 1)
        heads = []
        for hd in range(H_B):
            cs = slice(hd * HD_B, (hd + 1) * HD_B)
            wmix = jnp.where(si <= ti, sgw[hd], 0.0).astype(BF16)
            mixed = _dot(wmix, vn_ref[c0:c0 + CHUNK, cs]) + sgb[:, cs]
            ub = z_ref[c0:c0 + CHUNK, W_A + hd * HD_B:W_A + (hd + 1) * HD_B]
            heads.append(jax.nn.gelu(ub) * mixed)
        yb = jnp.concatenate(heads, axis=-1)
        ymix_ref[c0:c0 + CHUNK, W_A:] = _rms(yb, g_ob[...]).astype(BF16)

    def out_piece(lo, j):
        cs = slice(j * MXU_PIECE, (j + 1) * MXU_PIECE)
        lhs = _fresh_rows(ymix_ref, 221 + j, lo, half)
        h_mid_ref[lo:lo + half, cs] = x_ref[0, lo:lo + half, cs] + _wdot(lhs, w_out[:, cs])

    def stage_in(lo):
        items = [(SLAB * 3 // 2, functools.partial(norm_in, r0)) for r0 in range(lo, lo + half, SLAB)]
        for j in (0, 1, 6, 7, 4, 5, 2, 3):
            items.append((mm, functools.partial(in_piece, lo, j)))
        return items

    def stage_mid(lo):
        slabs = list(range(lo, lo + half, SLAB))
        items = [(SLAB * 2, functools.partial(conv, r0)) for r0 in slabs]
        items += [(mm // 2, functools.partial(gate_piece, lo, hf)) for hf in range(2)]
        items += [(SLAB * 3 // 2, functools.partial(gate_ew, r0, hf)) for r0 in slabs for hf in range(2)]
        items += [(SUBLANES * 3, functools.partial(scan_group, r0)) for r0 in range(lo, lo + half, SUBLANES)]
        items += [(SLAB * 2, functools.partial(a_out, r0)) for r0 in slabs]
        items += [(SLAB * 2, functools.partial(vn, r0)) for r0 in slabs]
        items += [(CHUNK * 5 // 2, functools.partial(sgu, c0)) for c0 in range(lo, lo + half, CHUNK)]
        return items

    def stage_out(lo):
        return [(mm, functools.partial(out_piece, lo, j)) for j in range(D_MODEL // MXU_PIECE)]

    return (stage_in(0) + _merge_items(stage_mid(0), stage_in(half))
            + _merge_items(stage_mid(half), stage_out(0)) + stage_out(half))


def _prompt_kernel(x_ref, p_ref, g_mix, w_in, caw, cab, w_gate, ba, bx, apar, g_oa, lng, lnb, sgw,
                   sgb, g_ob, w_out, g_ffn, w_up, fcw, fcb, w_down, g_ple, w_pg, w_ple, g_fin,
                   y_ref, ht_ref, cnew_ref, fnew_ref,
                   h_mid_ref, n1_ref, z_ref, xa_ref, xc_ref, xcb_ref, g_ref, a_ref, u_ref, hs_ref, carry_ref,
                   sp_ref, vn_ref, ymix_ref, n2_ref, up_ref, upb_ref, fhist_ref, act_ref, actb_ref, acc_ref,
                   gz_ref, pe_ref, *, tm):
    t = pl.program_id(1)

    @pl.when(t == 0)
    def _():
        xa_ref[0:HIST, :] = jnp.zeros((HIST, W_A), F32)
        carry_ref[...] = jnp.zeros((SUBLANES, W_A), F32)
        fhist_ref[...] = jnp.zeros(fhist_ref.shape, F32)

    reset_row = jnp.where(t == 0, 0, -1)
    mixer = _prompt_mixer_items(x_ref, h_mid_ref, n1_ref, z_ref, xa_ref, xc_ref, xcb_ref, g_ref, a_ref, u_ref,
                                hs_ref, carry_ref, sp_ref, vn_ref, ymix_ref, g_mix, w_in, caw, cab, w_gate,
                                ba, bx, apar, g_oa, lng, lnb, sgw, sgb, g_ob, w_out, tm, reset_row)

    def load_hist(c, up):
        for half in range(2):
            up[0:HIST, half * FF_CHUNK:(half + 1) * FF_CHUNK] = fhist_ref[:, _ff_cols(c, half)]

    def store_hist(c, up):
        for half in range(2):
            cs = slice(half * FF_CHUNK, (half + 1) * FF_CHUNK)
            fhist_ref[:, _ff_cols(c, half)] = up[tm:tm + HIST, cs]
            fnew_ref[0, :, _ff_cols(c, half)] = up[HIST + tm - (CONV_F - 1):HIST + tm, cs]

    def write_y(r0, y):
        y_ref[0, r0:r0 + SLAB, :] = y

    ffn = _ffn_items(h_mid_ref, n2_ref, n2_ref, (up_ref, upb_ref), (act_ref, actb_ref), acc_ref, gz_ref,
                     pe_ref, lambda: p_ref[0], g_ffn, w_up, fcw, fcb, w_down, g_ple, w_pg, w_ple, g_fin, tm, 1,
                     HIST, load_hist, store_hist, write_y)

    _run_items(mixer)
    _run_items(ffn)

    ht_ref[0] = carry_ref[0:1, :]
    cnew_ref[0] = xa_ref[HIST + tm - (CONV_A - 1):HIST + tm, :]


def _sample_kernel(x_ref, p_ref, h0_ref, chist_ref, fh_ref, g_mix, w_in, caw, cab, w_gate, ba, bx,
                   apar, g_oa, lng, lnb, sgw, sgb, g_ob, w_out, g_ffn, w_up, fcw, fcb, w_down,
                   g_ple, w_pg, w_ple, g_fin,
                   y_ref, ht_ref, cnew_ref, vn_out_ref, fnew_ref,
                   n_ref, z_ref, xa_ref, xc_ref, xcb_ref, g_ref, a_ref, u_ref, hs_ref, ymix_ref,
                   h_ref, p_buf, up_ref, upb_ref, act_ref, actb_ref, acc_ref, *, nb, steps):
    rows = nb * steps
    ahist = (CONV_A - 1) * nb
    fhist = (CONV_F - 1) * nb

    for s in range(steps):
        h_ref[s * nb:(s + 1) * nb, :] = x_ref[s]
        p_buf[s * nb:(s + 1) * nb, :] = p_ref[s]
    _norm_to_bf16(h_ref, g_mix, n_ref, rows)
    for k in range(CONV_A - 1):
        xa_ref[k * nb:(k + 1) * nb, :] = chist_ref[k]
    xa_ref[ahist:ahist + rows, :] = _wdot(n_ref[...], w_in[:, 0:W_A])
    z_ref[...] = _wdot(n_ref[...], w_in[:, W_A:])

    for r0 in range(0, rows, SLAB):
        cv = _bc(cab[...], SLAB)
        for j in range(CONV_A):
            off = r0 + j * nb
            cv = cv + xa_ref[off:off + SLAB, :] * _bc(caw[j], SLAB)
        xc_ref[r0:r0 + SLAB, :] = cv
        xcb_ref[r0:r0 + SLAB, :] = cv.astype(BF16)
    for k in range(CONV_A - 1):
        cnew_ref[k] = xa_ref[rows + k * nb:rows + (k + 1) * nb, :]

    _gate_stage(xc_ref, xcb_ref, g_ref, a_ref, u_ref, w_gate, ba, bx, apar, rows)

    for b0 in range(0, nb, SLAB):
        h = h0_ref[b0:b0 + SLAB, :]
        for s in range(steps):
            r0 = s * nb + b0
            h = a_ref[r0:r0 + SLAB, :] * h + u_ref[r0:r0 + SLAB, :]
            hs_ref[r0:r0 + SLAB, :] = h
        ht_ref[b0:b0 + SLAB, :] = h

    _branch_a_out(hs_ref, z_ref, g_oa, ymix_ref, rows)

    def write_vn(r0, vn):
        vn_out_ref[r0 // nb, r0 % nb:r0 % nb + SLAB, :] = vn

    _vn_stage(z_ref, lng, lnb, rows, write_vn)

    gob = g_ob[...]
    for b0 in range(0, nb, SLAB):
        for tt in range(steps):
            r0 = tt * nb + b0
            mixed = _bc(sgb[tt], SLAB)
            for s in range(tt + 1):
                k = tt * steps + s
                mixed = mixed + vn_out_ref[s, b0:b0 + SLAB, :] * _bc(sgw[k], SLAB)
            yb = jax.nn.gelu(z_ref[r0:r0 + SLAB, W_A:W_A + W_B]) * mixed
            ymix_ref[r0:r0 + SLAB, W_A:] = _rms(yb, gob).astype(BF16)

    h_ref[...] += _wdot(ymix_ref[...], w_out[...])

    def load_hist(c, up):
        for k in range(CONV_F - 1):
            for half in range(2):
                cs = slice(half * FF_CHUNK, (half + 1) * FF_CHUNK)
                up[k * nb:(k + 1) * nb, cs] = fh_ref[k, :, _ff_cols(c, half)]

    def store_hist(c, up):
        for k in range(CONV_F - 1):
            for half in range(2):
                cs = slice(half * FF_CHUNK, (half + 1) * FF_CHUNK)
                fnew_ref[k, :, _ff_cols(c, half)] = up[rows + k * nb:rows + (k + 1) * nb, cs]

    def write_y(r0, y):
        y_ref[r0 // nb, r0 % nb:r0 % nb + SLAB, :] = y

    _run_items(_ffn_items(h_ref, n_ref, n_ref, (up_ref, upb_ref), (act_ref, actb_ref), acc_ref, z_ref, g_ref,
                          lambda: p_buf[...], g_ffn, w_up, fcw, fcb, w_down, g_ple, w_pg, w_ple, g_fin,
                          rows, nb, fhist, load_hist, store_hist, write_y))


def _const_spec(shape):
    zeros = (0,) * len(shape)
    return pl.BlockSpec(shape, lambda *_: zeros, pipeline_mode=pl.Buffered(1))


def _pack_bf16_rows(w):
    lead = w.shape[:-2]
    n = w.shape[-1]
    w2 = w.reshape((-1, n))
    k = w2.shape[0]
    bk = min(k, PACK_BLOCK_ROWS)
    bn = min(n, PACK_BLOCK_COLS)
    assert k % bk == 0 and n % bn == 0 and w.shape[-2] % 2 == 0
    words = pl.pallas_call(
        _pack_kernel,
        grid=(k // bk, n // bn),
        in_specs=[pl.BlockSpec((bk, bn), lambda i, j: (i, j))],
        out_specs=pl.BlockSpec((bk // 2, bn), lambda i, j: (i, j)),
        out_shape=jax.ShapeDtypeStruct((k // 2, n), jnp.uint32),
        name="pack_weight",
    )(w2)
    return words.reshape(lead + (w.shape[-2] // 2, n))


def _pack_kernel(w_ref, o_ref):
    o_ref[...] = pltpu.bitcast(w_ref[...].astype(BF16), jnp.uint32)


def _block_diag_gate(wa, wx):
    hh = H_A // 2
    eye = jnp.eye(hh, dtype=wa.dtype)

    def bd(w):
        return jnp.einsum('hij,hg->higj', w, eye).reshape(GATE_HALF, GATE_HALF)

    return jnp.stack([jnp.concatenate([bd(wa[h * hh:(h + 1) * hh]), bd(wx[h * hh:(h + 1) * hh])], axis=1)
                      for h in range(2)])


def kernel(x_prompt, x_sample, p_prompt, p_sample, state_rglru_h, state_rglru_conv, state_ffn_conv, g_mix_norm, w_in, conv_a_w, conv_a_b, lru_wa, lru_ba, lru_wx, lru_bx, lru_a_param, g_out_a, ln_v_g, ln_v_b, sgu_w, sgu_b, g_out_b, w_out, g_ffn_norm, w_up, ffn_conv_w, ffn_conv_b, w_down, g_ple_norm, w_ple_gate, w_ple, g_final):
    assert w_in.shape[0] == 1
    nbp, seq, _ = x_prompt.shape
    nbs, steps, _ = x_sample.shape
    ple = p_prompt.shape[-1]
    tm = PROMPT_BLOCK_ROWS
    nb = SAMPLE_GROUP
    assert seq % tm == 0 and tm % CHUNK == 0 and nbs % nb == 0 and nb % SLAB == 0 and steps <= CHUNK
    row = lambda v: jnp.broadcast_to(v.reshape(1, -1), (SUBLANES, v.size))
    row_set = lambda m: jnp.broadcast_to(m[:, None, :], (m.shape[0], SUBLANES, m.shape[1]))

    head = [row(g_mix_norm[0]), _pack_bf16_rows(w_in[0]), row_set(conv_a_w[0]), row(conv_a_b[0]),
            _pack_bf16_rows(_block_diag_gate(lru_wa[0], lru_wx[0])), row(lru_ba[0]), row(lru_bx[0]),
            row(lru_a_param[0]), row(g_out_a[0]), row(ln_v_g[0]), row(ln_v_b[0])]
    tail = [row(g_out_b[0]), _pack_bf16_rows(w_out[0]), row(g_ffn_norm[0]),
            _pack_bf16_rows(w_up[0]), row_set(ffn_conv_w[0]), row(ffn_conv_b[0]), _pack_bf16_rows(w_down[0]),
            row(g_ple_norm[0]), _pack_bf16_rows(w_ple_gate[0]), _pack_bf16_rows(w_ple[0]), row(g_final)]

    sgb_p = jnp.repeat(jnp.transpose(sgu_b[0]), HD_B, axis=1)
    p_args = head + [sgu_w[0], sgb_p] + tail
    scratch_p = [
        pltpu.VMEM((tm, D_MODEL), F32),
        pltpu.VMEM((tm, D_MODEL), BF16),
        pltpu.VMEM((tm, D_MODEL + W_A), F32),
        pltpu.VMEM((HIST + tm, W_A), F32),
        pltpu.VMEM((tm, W_A), F32),
        pltpu.VMEM((tm, W_A), BF16),
        pltpu.VMEM((tm, 2 * W_A), F32),
        pltpu.VMEM((tm, W_A), F32),
        pltpu.VMEM((tm, W_A), F32),
        pltpu.VMEM((tm, W_A), F32),
        pltpu.VMEM((SUBLANES, W_A), F32),
        pltpu.VMEM((SUBLANES, W_A), F32),
        pltpu.VMEM((tm, W_B), BF16),
        pltpu.VMEM((tm, W_A + W_B), BF16),
        pltpu.VMEM((tm, D_MODEL), BF16),
        pltpu.VMEM((HIST + tm, 2 * FF_CHUNK), F32),
        pltpu.VMEM((HIST + tm, 2 * FF_CHUNK), F32),
        pltpu.VMEM((HIST, 2 * D_FF), F32),
        pltpu.VMEM((tm, FF_CHUNK), BF16),
        pltpu.VMEM((tm, FF_CHUNK), BF16),
        pltpu.VMEM((tm, D_MODEL), F32),
        pltpu.VMEM((tm, D_MODEL), F32),
        pltpu.VMEM((tm, D_MODEL), F32),
    ]
    y_p, ht_p, cnew_p, fnew_p = pl.pallas_call(
        functools.partial(_prompt_kernel, tm=tm),
        grid=(nbp, seq // tm),
        in_specs=[pl.BlockSpec((1, tm, D_MODEL), lambda b, t: (b, t, 0)),
                  pl.BlockSpec((1, tm, ple), lambda b, t: (b, t, 0))]
                 + [_const_spec(a.shape) for a in p_args],
        out_specs=[pl.BlockSpec((1, tm, D_MODEL), lambda b, t: (b, t, 0)),
                   pl.BlockSpec((1, 1, W_A), lambda b, t: (b, 0, 0)),
                   pl.BlockSpec((1, CONV_A - 1, W_A), lambda b, t: (b, 0, 0)),
                   pl.BlockSpec((1, CONV_F - 1, 2 * D_FF), lambda b, t: (b, 0, 0))],
        out_shape=[jax.ShapeDtypeStruct((nbp, seq, D_MODEL), F32),
                   jax.ShapeDtypeStruct((nbp, 1, W_A), F32),
                   jax.ShapeDtypeStruct((nbp, CONV_A - 1, W_A), F32),
                   jax.ShapeDtypeStruct((nbp, CONV_F - 1, 2 * D_FF), F32)],
        scratch_shapes=scratch_p,
        compiler_params=pltpu.CompilerParams(dimension_semantics=("arbitrary", "arbitrary"),
                                             vmem_limit_bytes=VMEM_LIMIT_BYTES),
        name="prompt_layer",
    )(x_prompt, p_prompt[0], *p_args)

    rows = nb * steps
    ahist = (CONV_A - 1) * nb
    fhist = (CONV_F - 1) * nb
    tmaj = lambda a: jnp.swapaxes(a, 0, 1)
    sgw_s = jnp.repeat(jnp.transpose(sgu_w[0, :, :steps, :steps], (1, 2, 0)).reshape(steps * steps, H_B),
                       HD_B, axis=1)
    sgb_s = jnp.repeat(jnp.transpose(sgu_b[0, :, :steps]), HD_B, axis=1)
    w_args = head + [row_set(sgw_s), row_set(sgb_s)] + tail
    scratch_s = [
        pltpu.VMEM((rows, D_MODEL), BF16),
        pltpu.VMEM((rows, D_MODEL + W_A), F32),
        pltpu.VMEM((ahist + rows, W_A), F32),
        pltpu.VMEM((rows, W_A), F32),
        pltpu.VMEM((rows, W_A), BF16),
        pltpu.VMEM((rows, 2 * W_A), F32),
        pltpu.VMEM((rows, W_A), F32),
        pltpu.VMEM((rows, W_A), F32),
        pltpu.VMEM((rows, W_A), F32),
        pltpu.VMEM((rows, W_A + W_B), BF16),
        pltpu.VMEM((rows, D_MODEL), F32),
        pltpu.VMEM((rows, ple), F32),
        pltpu.VMEM((fhist + rows, 2 * FF_CHUNK), F32),
        pltpu.VMEM((fhist + rows, 2 * FF_CHUNK), F32),
        pltpu.VMEM((rows, FF_CHUNK), BF16),
        pltpu.VMEM((rows, FF_CHUNK), BF16),
        pltpu.VMEM((rows, D_MODEL), F32),
    ]
    y_s, ht_s, cnew_s, vn_s, fnew_s = pl.pallas_call(
        functools.partial(_sample_kernel, nb=nb, steps=steps),
        grid=(nbs // nb,),
        in_specs=[pl.BlockSpec((steps, nb, D_MODEL), lambda i: (0, i, 0)),
                  pl.BlockSpec((steps, nb, ple), lambda i: (0, i, 0)),
                  pl.BlockSpec((nb, W_A), lambda i: (i, 0)),
                  pl.BlockSpec((CONV_A - 1, nb, W_A), lambda i: (0, i, 0)),
                  pl.BlockSpec((CONV_F - 1, nb, 2 * D_FF), lambda i: (0, i, 0))]
                 + [_const_spec(a.shape) for a in w_args],
        out_specs=[pl.BlockSpec((steps, nb, D_MODEL), lambda i: (0, i, 0)),
                   pl.BlockSpec((nb, W_A), lambda i: (i, 0)),
                   pl.BlockSpec((CONV_A - 1, nb, W_A), lambda i: (0, i, 0)),
                   pl.BlockSpec((steps, nb, W_B), lambda i: (0, i, 0)),
                   pl.BlockSpec((CONV_F - 1, nb, 2 * D_FF), lambda i: (0, i, 0))],
        out_shape=[jax.ShapeDtypeStruct((steps, nbs, D_MODEL), F32),
                   jax.ShapeDtypeStruct((nbs, W_A), F32),
                   jax.ShapeDtypeStruct((CONV_A - 1, nbs, W_A), F32),
                   jax.ShapeDtypeStruct((steps, nbs, W_B), F32),
                   jax.ShapeDtypeStruct((CONV_F - 1, nbs, 2 * D_FF), F32)],
        scratch_shapes=scratch_s,
        compiler_params=pltpu.CompilerParams(dimension_semantics=("arbitrary",),
                                             vmem_limit_bytes=VMEM_LIMIT_BYTES),
        name="sample_layer",
    )(tmaj(x_sample), tmaj(p_sample[0]), state_rglru_h[0], tmaj(state_rglru_conv[0]),
      tmaj(state_ffn_conv[0]), *w_args)

    return (y_p, tmaj(y_s), tmaj(ht_p), ht_s[None], cnew_p[None], tmaj(cnew_s)[None],
            tmaj(vn_s)[None], fnew_p[None], tmaj(fnew_s)[None])
```

```python
import functools

import jax
import jax.numpy as jnp
from jax import lax
from jax.experimental import pallas as pl
from jax.experimental.pallas import tpu as pltpu

F32 = jnp.float32
BF16 = jnp.bfloat16

D_MODEL = 1024
W_A = 512
W_B = 512
H_A = 8
BW_A = W_A // H_A
H_B = 4
HD_B = W_B // H_B
CHUNK = 128
D_FF = 3072
CONV_A = 4
CONV_F = 3
C_RG = 8.0
EPS = 1e-6

SUBLANES = 8
HIST = SUBLANES
GATE_HALF = W_A // 2
FF_CHUNK = 512
N_FF_CHUNKS = D_FF // FF_CHUNK
SLAB = 32
MXU_PIECE = 256
PACK_BLOCK_ROWS = 512
PACK_BLOCK_COLS = 2048
PROMPT_BLOCK_ROWS = 256
SAMPLE_GROUP = 32
VMEM_LIMIT_BYTES = 56 * 1024 * 1024


def _bc(tile, rows):
    return jnp.concatenate([tile] * (rows // SUBLANES), axis=0)


class _Vec:
    def __init__(self, ref, off, width, n):
        self.ref, self.off, self.width, self.n = ref, off, width, n

    def __getitem__(self, idx):
        full = slice(0, self.width)
        if self.n is None:
            j, cols = 0, (full if idx is Ellipsis else idx[1])
        else:
            j, cols = (idx, full) if isinstance(idx, int) else (idx[0], idx[2])
        start = self.off + j * self.width
        return self.ref[:, start + cols.start:start + cols.stop]


def _pack_vectors(named):
    layout, flat, off = {}, [], 0
    for name, a in named:
        n, width = (None, a.shape[0]) if a.ndim == 1 else a.shape
        layout[name] = (off, width, n)
        flat.append(a.reshape(-1))
        off += a.size
    block = jnp.broadcast_to(jnp.concatenate(flat)[None, :], (SUBLANES, off))
    return block, layout


def _rms(x, g):
    ms = jnp.mean(x * x, axis=-1, keepdims=True)
    return x * lax.rsqrt(ms + EPS) * _bc(g, x.shape[0])


def _layernorm(x, g, b):
    mu = jnp.mean(x, axis=-1, keepdims=True)
    xc = x - mu
    rows = x.shape[0]
    return xc * lax.rsqrt(jnp.mean(xc * xc, axis=-1, keepdims=True) + EPS) * _bc(g, rows) + _bc(b, rows)


def _sigmoid(x):
    return 1.0 / (1.0 + jnp.exp(-x))


def _softplus(x):
    return jnp.maximum(x, 0.0) + jnp.log(1.0 + jnp.exp(-jnp.abs(x)))


def _dot(a, b):
    return jnp.dot(a, b, preferred_element_type=F32)


def _fresh_rows(ref, salt, cols=slice(None)):
    zero = jnp.minimum(pl.program_id(0), 0) * salt
    return ref[pl.ds(pl.multiple_of(zero, 16), ref.shape[0]), cols]


def _wdot(a, w_words):
    return _dot(a, pltpu.bitcast(w_words, BF16))


def _ff_cols(c, half):
    start = half * D_FF + c * FF_CHUNK
    return slice(start, start + FF_CHUNK)


def _norm_to_bf16(src_ref, g_ref, dst_ref, rows):
    g = g_ref[...]
    for r0 in range(0, rows, SLAB):
        dst_ref[r0:r0 + SLAB, :] = _rms(src_ref[r0:r0 + SLAB, :], g).astype(BF16)


def _gate_stage(xc_ref, xcb_ref, g_ref, a_ref, u_ref, w_gate, ba, bx, apar, rows):
    for hf in range(2):
        c0 = hf * GATE_HALF
        g_ref[:, 2 * c0:2 * c0 + 2 * GATE_HALF] = _wdot(xcb_ref[:, c0:c0 + GATE_HALF], w_gate[hf])
    sp = _softplus(-apar[...])
    for r0 in range(0, rows, SLAB):
        for hf in range(2):
            c0 = hf * GATE_HALF
            cs = slice(c0, c0 + GATE_HALF)
            r = _sigmoid(g_ref[r0:r0 + SLAB, 2 * c0:2 * c0 + GATE_HALF] + _bc(ba[:, cs], SLAB))
            i = _sigmoid(g_ref[r0:r0 + SLAB, 2 * c0 + GATE_HALF:2 * c0 + 2 * GATE_HALF] + _bc(bx[:, cs], SLAB))
            a = jnp.exp((-C_RG) * r * _bc(sp[:, cs], SLAB))
            mult = jnp.sqrt(1.0 - a * a)
            a_ref[r0:r0 + SLAB, cs] = a
            u_ref[r0:r0 + SLAB, cs] = xc_ref[r0:r0 + SLAB, cs] * i * mult


def _branch_a_out(hs_ref, z_ref, g_oa, ymix_ref, rows):
    g = g_oa[...]
    for r0 in range(0, rows, SLAB):
        ga = z_ref[r0:r0 + SLAB, 0:W_A]
        ymix_ref[r0:r0 + SLAB, 0:W_A] = _rms(hs_ref[r0:r0 + SLAB, :] * jax.nn.gelu(ga), g).astype(BF16)


def _vn_stage(z_ref, lng, lnb, rows, write):
    g = lng[...]
    b = lnb[...]
    for r0 in range(0, rows, SLAB):
        vb = z_ref[r0:r0 + SLAB, W_A + W_B:W_A + 2 * W_B]
        write(r0, _layernorm(jax.nn.gelu(vb), g, b))


def _ffn_items(h_ref, n_ref, up2_ref, act2_ref, acc_ref, gz_ref, pe_ref, load_p, g_ffn, w_up, fcw, fcb,
               w_down, g_ple, w_pg, w_ple, g_fin, rows, shift, hist_rows, load_hist, store_hist, write_y):
    n_up = 2 * FF_CHUNK // MXU_PIECE
    n_down = D_MODEL // MXU_PIECE
    slabs = list(range(0, rows, SLAB))
    items = []

    def norm_in(r0):
        h = h_ref[r0:r0 + SLAB, :]
        acc_ref[r0:r0 + SLAB, :] = h
        n_ref[r0:r0 + SLAB, :] = _rms(h, g_ffn[...]).astype(BF16)

    def up_piece(c, j):
        up_ref = up2_ref[c % 2]
        cs = slice(j * MXU_PIECE, (j + 1) * MXU_PIECE)
        half, off = divmod(j * MXU_PIECE, FF_CHUNK)
        w0 = half * D_FF + c * FF_CHUNK + off
        if j == 0:
            load_hist(c, up_ref)
        lhs = _fresh_rows(n_ref, 1 + c * n_up + j)
        up_ref[hist_rows:hist_rows + rows, cs] = _wdot(lhs, w_up[:, w0:w0 + MXU_PIECE])

    def down_piece(c, j):
        cs = slice(j * MXU_PIECE, (j + 1) * MXU_PIECE)
        lhs = _fresh_rows(act2_ref[c % 2], 1 + c * n_down + j)
        acc_ref[:, cs] += _wdot(lhs, w_down[c * FF_CHUNK // 2:(c + 1) * FF_CHUNK // 2, cs])

    def conv_act_slab(c, r0, last):
        up_ref = up2_ref[c % 2]
        halves = []
        for half in range(2):
            cs = slice(half * FF_CHUNK, (half + 1) * FF_CHUNK)
            ws = _ff_cols(c, half)
            cv = _bc(fcb[:, ws], SLAB)
            for j in range(CONV_F):
                off = hist_rows + r0 - (CONV_F - 1 - j) * shift
                cv = cv + up_ref[off:off + SLAB, cs] * _bc(fcw[j, :, ws], SLAB)
            halves.append(cv)
        act2_ref[c % 2][r0:r0 + SLAB, :] = (jax.nn.gelu(halves[0]) * halves[1]).astype(BF16)
        if last:
            store_hist(c, up_ref)

    def embed():
        pe_ref[...] = _wdot(load_p().astype(BF16), w_ple[...])

    def norm_mid(r0):
        n_ref[r0:r0 + SLAB, :] = _rms(acc_ref[r0:r0 + SLAB, :], g_ple[...]).astype(BF16)

    def gate_piece(j):
        cs = slice(j * MXU_PIECE, (j + 1) * MXU_PIECE)
        gz_ref[:, cs] = _wdot(_fresh_rows(n_ref, 1 + N_FF_CHUNKS * n_up + j), w_pg[:, cs])

    def finish(r0):
        gate = _sigmoid(gz_ref[r0:r0 + SLAB, 0:D_MODEL])
        h3 = acc_ref[r0:r0 + SLAB, :] + pe_ref[r0:r0 + SLAB, :] * gate
        write_y(r0, _rms(h3, g_fin[...]))

    items += [functools.partial(norm_in, r0) for r0 in slabs]
    items += [functools.partial(up_piece, 0, j) for j in range(n_up)]
    items.append(embed)
    for c in range(N_FF_CHUNKS + 1):
        pieces = []
        for j in range(max(n_up, n_down)):
            if c + 1 < N_FF_CHUNKS and j < n_up:
                pieces.append(functools.partial(up_piece, c + 1, j))
            if c >= 1 and j < n_down:
                pieces.append(functools.partial(down_piece, c - 1, j))
        if c < N_FF_CHUNKS:
            done = 0
            for i, r0 in enumerate(slabs):
                items.append(functools.partial(conv_act_slab, c, r0, i == len(slabs) - 1))
                upto = (i + 1) * len(pieces) // len(slabs)
                items += pieces[done:upto]
                done = upto
        else:
            items += pieces
    items += [functools.partial(norm_mid, r0) for r0 in slabs]
    items += [functools.partial(gate_piece, j) for j in range(n_down)]
    items += [functools.partial(finish, r0) for r0 in slabs]
    return items


def _run(items):
    for item in items:
        item()


def _prompt_mixer_items(x_ref, h_mid_ref, n_ref, z_ref, xa_ref, xc_ref, xcb_ref, g_ref, a_ref, u_ref, hs_ref,
                        carry_ref, sp_ref, vn_ref, ymix_ref, g_mix, w_in, caw, cab, w_gate, ba, bx, apar,
                        g_oa, lng, lnb, sgw, sgb, g_ob, w_out, tm, reset_row):
    assert tm % CHUNK == 0 and tm % SLAB == 0
    slabs = list(range(0, tm, SLAB))

    def norm_in(r0):
        n_ref[r0:r0 + SLAB, :] = _rms(x_ref[0, r0:r0 + SLAB, :], g_mix[...]).astype(BF16)

    def in_piece(j):
        c0 = j * MXU_PIECE
        res = _wdot(_fresh_rows(n_ref, 1 + j), w_in[:, c0:c0 + MXU_PIECE])
        if c0 < W_A:
            xa_ref[HIST:HIST + tm, c0:c0 + MXU_PIECE] = res
        else:
            z_ref[:, c0 - W_A:c0 - W_A + MXU_PIECE] = res

    def conv(r0):
        if r0 == 0:
            sp_ref[...] = _softplus(-apar[...])
        cv = _bc(cab[...], SLAB)
        for j in range(CONV_A):
            off = HIST + r0 - (CONV_A - 1 - j)
            cv = cv + xa_ref[off:off + SLAB, :] * _bc(caw[j], SLAB)
        xc_ref[r0:r0 + SLAB, :] = cv
        xcb_ref[r0:r0 + SLAB, :] = cv.astype(BF16)
        if r0 + SLAB == tm:
            xa_ref[0:HIST, :] = xa_ref[tm:tm + HIST, :]

    def gate_piece(hf):
        c0 = hf * GATE_HALF
        lhs = _fresh_rows(xcb_ref, 1 + hf, slice(c0, c0 + GATE_HALF))
        g_ref[:, 2 * c0:2 * c0 + 2 * GATE_HALF] = _wdot(lhs, w_gate[hf])

    def gate_ew(r0, hf):
        c0 = hf * GATE_HALF
        cs = slice(c0, c0 + GATE_HALF)
        r = _sigmoid(g_ref[r0:r0 + SLAB, 2 * c0:2 * c0 + GATE_HALF] + _bc(ba[:, cs], SLAB))
        i = _sigmoid(g_ref[r0:r0 + SLAB, 2 * c0 + GATE_HALF:2 * c0 + 2 * GATE_HALF] + _bc(bx[:, cs], SLAB))
        a = jnp.exp((-C_RG) * r * _bc(sp_ref[:, cs], SLAB))
        mult = jnp.sqrt(1.0 - a * a)
        if r0 == 0:
            row = lax.broadcasted_iota(jnp.int32, (SLAB, GATE_HALF), 0)
            mult = jnp.where(row == reset_row, 1.0, mult)
        a_ref[r0:r0 + SLAB, cs] = a
        u_ref[r0:r0 + SLAB, cs] = xc_ref[r0:r0 + SLAB, cs] * i * mult

    def scan_group(r0):
        row = lax.broadcasted_iota(jnp.int32, (SUBLANES, W_A), 0)
        a = a_ref[r0:r0 + SUBLANES, :]
        u = u_ref[r0:r0 + SUBLANES, :]
        for d in (1, 2, 4):
            keep = row >= d
            a_s = jnp.where(keep, pltpu.roll(a, d, 0), 1.0)
            u_s = jnp.where(keep, pltpu.roll(u, d, 0), 0.0)
            u = a * u_s + u
            a = a * a_s
        h = a * carry_ref[...] + u
        hs_ref[r0:r0 + SUBLANES, :] = h
        carry_ref[...] = jnp.broadcast_to(h[SUBLANES - 1:SUBLANES, :], (SUBLANES, W_A))

    def a_out(r0):
        ga = z_ref[r0:r0 + SLAB, 0:W_A]
        ymix_ref[r0:r0 + SLAB, 0:W_A] = _rms(hs_ref[r0:r0 + SLAB, :] * jax.nn.gelu(ga), g_oa[...]).astype(BF16)

    def vn(r0):
        vb = z_ref[r0:r0 + SLAB, W_A + W_B:W_A + 2 * W_B]
        vn_ref[r0:r0 + SLAB, :] = _layernorm(jax.nn.gelu(vb), lng[...], lnb[...]).astype(BF16)

    def sgu(c0):
        ti = lax.broadcasted_iota(jnp.int32, (CHUNK, CHUNK), 0)
        si = lax.broadcasted_iota(jnp.int32, (CHUNK, CHUNK), 1)
        heads = []
        for hd in range(H_B):
            cs = slice(hd * HD_B, (hd + 1) * HD_B)
            wmix = jnp.where(si <= ti, sgw[hd], 0.0).astype(BF16)
            mixed = _dot(wmix, vn_ref[c0:c0 + CHUNK, cs]) + sgb[:, cs]
            ub = z_ref[c0:c0 + CHUNK, W_A + hd * HD_B:W_A + (hd + 1) * HD_B]
            heads.append(jax.nn.gelu(ub) * mixed)
        yb = jnp.concatenate(heads, axis=-1)
        ymix_ref[c0:c0 + CHUNK, W_A:] = _rms(yb, g_ob[...]).astype(BF16)

    def out_piece(j):
        cs = slice(j * MXU_PIECE, (j + 1) * MXU_PIECE)
        h_mid_ref[:, cs] = x_ref[0, :, cs] + _wdot(_fresh_rows(ymix_ref, 1 + j), w_out[:, cs])

    items = [functools.partial(norm_in, r0) for r0 in slabs]
    items += [functools.partial(in_piece, j) for j in range((2 * W_A + 2 * W_B) // MXU_PIECE)]
    items += [functools.partial(conv, r0) for r0 in slabs]
    items += [functools.partial(gate_piece, hf) for hf in range(2)]
    items += [functools.partial(gate_ew, r0, hf) for r0 in slabs for hf in range(2)]
    items += [functools.partial(scan_group, r0) for r0 in range(0, tm, SUBLANES)]
    items += [functools.partial(a_out, r0) for r0 in slabs]
    items += [functools.partial(vn, r0) for r0 in slabs]
    items += [functools.partial(sgu, c0) for c0 in range(0, tm, CHUNK)]
    items += [functools.partial(out_piece, j) for j in range(D_MODEL // MXU_PIECE)]
    return items


def _prompt_kernel(x_ref, p_ref, vecs, w_in, w_gate, sgw, sgb, w_out, w_up, w_down, w_pg, w_ple,
                   y_ref, ht_ref, cnew_ref, fnew_ref,
                   h_mid_ref, n1_ref, z_ref, xa_ref, xc_ref, xcb_ref, g_ref, a_ref, u_ref, hs_ref, carry_ref,
                   sp_ref, vn_ref, ymix_ref, n2_ref, up_ref, upb_ref, fhist_ref, act_ref, actb_ref, acc_ref,
                   gz_ref, pe_ref, *, tm, layout):
    v = {name: _Vec(vecs, *spec) for name, spec in layout.items()}
    g_mix, caw, cab, ba, bx, apar = v['g_mix'], v['caw'], v['cab'], v['ba'], v['bx'], v['apar']
    g_oa, lng, lnb, g_ob = v['g_oa'], v['lng'], v['lnb'], v['g_ob']
    g_ffn, fcw, fcb, g_ple, g_fin = v['g_ffn'], v['fcw'], v['fcb'], v['g_ple'], v['g_fin']
    t = pl.program_id(1)

    @pl.when(t == 0)
    def _():
        xa_ref[0:HIST, :] = jnp.zeros((HIST, W_A), F32)
        carry_ref[...] = jnp.zeros((SUBLANES, W_A), F32)
        fhist_ref[...] = jnp.zeros(fhist_ref.shape, F32)

    reset_row = jnp.where(t == 0, 0, -1)
    mixer = _prompt_mixer_items(x_ref, h_mid_ref, n1_ref, z_ref, xa_ref, xc_ref, xcb_ref, g_ref, a_ref, u_ref,
                                hs_ref, carry_ref, sp_ref, vn_ref, ymix_ref, g_mix, w_in, caw, cab, w_gate,
                                ba, bx, apar, g_oa, lng, lnb, sgw, sgb, g_ob, w_out, tm, reset_row)

    def load_hist(c, up):
        for half in range(2):
            up[0:HIST, half * FF_CHUNK:(half + 1) * FF_CHUNK] = fhist_ref[:, _ff_cols(c, half)]

    def store_hist(c, up):
        for half in range(2):
            cs = slice(half * FF_CHUNK, (half + 1) * FF_CHUNK)
            fhist_ref[:, _ff_cols(c, half)] = up[tm:tm + HIST, cs]
            fnew_ref[0, :, _ff_cols(c, half)] = up[HIST + tm - (CONV_F - 1):HIST + tm, cs]

    def write_y(r0, y):
        y_ref[0, r0:r0 + SLAB, :] = y

    ffn = _ffn_items(h_mid_ref, n2_ref, (up_ref, upb_ref), (act_ref, actb_ref), acc_ref, gz_ref, pe_ref,
                     lambda: p_ref[0], g_ffn, w_up, fcw, fcb, w_down, g_ple, w_pg, w_ple, g_fin, tm, 1, HIST,
                     load_hist, store_hist, write_y)
    _run(mixer)
    _run(ffn)

    ht_ref[0] = carry_ref[0:1, :]
    cnew_ref[0] = xa_ref[HIST + tm - (CONV_A - 1):HIST + tm, :]


def _sample_kernel(x_ref, p_ref, h0_ref, chist_ref, fh_ref, vecs, w_in, w_gate, w_out, w_up, w_down,
                   w_pg, w_ple,
                   y_ref, ht_ref, cnew_ref, vn_out_ref, fnew_ref,
                   n_ref, z_ref, xa_ref, xc_ref, xcb_ref, g_ref, a_ref, u_ref, hs_ref, ymix_ref,
                   h_ref, p_buf, up_ref, upb_ref, act_ref, actb_ref, acc_ref, *, nb, steps, layout):
    v = {name: _Vec(vecs, *spec) for name, spec in layout.items()}
    g_mix, caw, cab, ba, bx, apar = v['g_mix'], v['caw'], v['cab'], v['ba'], v['bx'], v['apar']
    g_oa, lng, lnb, sgw, sgb, g_ob = v['g_oa'], v['lng'], v['lnb'], v['sgw'], v['sgb'], v['g_ob']
    g_ffn, fcw, fcb, g_ple, g_fin = v['g_ffn'], v['fcw'], v['fcb'], v['g_ple'], v['g_fin']
    rows = nb * steps
    ahist = (CONV_A - 1) * nb
    fhist = (CONV_F - 1) * nb

    for s in range(steps):
        h_ref[s * nb:(s + 1) * nb, :] = x_ref[s]
        p_buf[s * nb:(s + 1) * nb, :] = p_ref[s]
    _norm_to_bf16(h_ref, g_mix, n_ref, rows)
    for k in range(CONV_A - 1):
        xa_ref[k * nb:(k + 1) * nb, :] = chist_ref[k]
    xa_ref[ahist:ahist + rows, :] = _wdot(n_ref[...], w_in[:, 0:W_A])
    z_ref[...] = _wdot(n_ref[...], w_in[:, W_A:])

    for r0 in range(0, rows, SLAB):
        cv = _bc(cab[...], SLAB)
        for j in range(CONV_A):
            off = r0 + j * nb
            cv = cv + xa_ref[off:off + SLAB, :] * _bc(caw[j], SLAB)
        xc_ref[r0:r0 + SLAB, :] = cv
        xcb_ref[r0:r0 + SLAB, :] = cv.astype(BF16)
    for k in range(CONV_A - 1):
        cnew_ref[k] = xa_ref[rows + k * nb:rows + (k + 1) * nb, :]

    _gate_stage(xc_ref, xcb_ref, g_ref, a_ref, u_ref, w_gate, ba, bx, apar, rows)

    for b0 in range(0, nb, SLAB):
        h = h0_ref[b0:b0 + SLAB, :]
        for s in range(steps):
            r0 = s * nb + b0
            h = a_ref[r0:r0 + SLAB, :] * h + u_ref[r0:r0 + SLAB, :]
            hs_ref[r0:r0 + SLAB, :] = h
        ht_ref[b0:b0 + SLAB, :] = h

    _branch_a_out(hs_ref, z_ref, g_oa, ymix_ref, rows)

    def write_vn(r0, vn):
        vn_out_ref[r0 // nb, r0 % nb:r0 % nb + SLAB, :] = vn

    _vn_stage(z_ref, lng, lnb, rows, write_vn)

    gob = g_ob[...]
    for b0 in range(0, nb, SLAB):
        for tt in range(steps):
            r0 = tt * nb + b0
            mixed = _bc(sgb[tt], SLAB)
            for s in range(tt + 1):
                k = tt * steps + s
                mixed = mixed + vn_out_ref[s, b0:b0 + SLAB, :] * _bc(sgw[k], SLAB)
            yb = jax.nn.gelu(z_ref[r0:r0 + SLAB, W_A:W_A + W_B]) * mixed
            ymix_ref[r0:r0 + SLAB, W_A:] = _rms(yb, gob).astype(BF16)

    h_ref[...] += _wdot(ymix_ref[...], w_out[...])

    def load_hist(c, up):
        for k in range(CONV_F - 1):
            for half in range(2):
                cs = slice(half * FF_CHUNK, (half + 1) * FF_CHUNK)
                up[k * nb:(k + 1) * nb, cs] = fh_ref[k, :, _ff_cols(c, half)]

    def store_hist(c, up):
        for k in range(CONV_F - 1):
            for half in range(2):
                cs = slice(half * FF_CHUNK, (half + 1) * FF_CHUNK)
                fnew_ref[k, :, _ff_cols(c, half)] = up[rows + k * nb:rows + (k + 1) * nb, cs]

    def write_y(r0, y):
        y_ref[r0 // nb, r0 % nb:r0 % nb + SLAB, :] = y

    _run(_ffn_items(h_ref, n_ref, (up_ref, upb_ref), (act_ref, actb_ref), acc_ref, z_ref, g_ref,
                          lambda: p_buf[...], g_ffn, w_up, fcw, fcb, w_down, g_ple, w_pg, w_ple, g_fin,
                          rows, nb, fhist, load_hist, store_hist, write_y))


def _const_spec(shape):
    zeros = (0,) * len(shape)
    return pl.BlockSpec(shape, lambda *_: zeros, pipeline_mode=pl.Buffered(1))


def _pack_bf16_rows(w):
    lead = w.shape[:-2]
    n = w.shape[-1]
    w2 = w.reshape((-1, n))
    k = w2.shape[0]
    bk = min(k, PACK_BLOCK_ROWS)
    bn = min(n, PACK_BLOCK_COLS)
    assert k % bk == 0 and n % bn == 0 and w.shape[-2] % 2 == 0
    words = pl.pallas_call(
        _pack_kernel,
        grid=(k // bk, n // bn),
        in_specs=[pl.BlockSpec((bk, bn), lambda i, j: (i, j))],
        out_specs=pl.BlockSpec((bk // 2, bn), lambda i, j: (i, j)),
        out_shape=jax.ShapeDtypeStruct((k // 2, n), jnp.uint32),
        name="pack_weight",
    )(w2)
    return words.reshape(lead + (w.shape[-2] // 2, n))


def _pack_kernel(w_ref, o_ref):
    o_ref[...] = pltpu.bitcast(w_ref[...].astype(BF16), jnp.uint32)


def _block_diag_gate(wa, wx):
    hh = H_A // 2
    eye = jnp.eye(hh, dtype=wa.dtype)

    def bd(w):
        return jnp.einsum('hij,hg->higj', w, eye).reshape(GATE_HALF, GATE_HALF)

    return jnp.stack([jnp.concatenate([bd(wa[h * hh:(h + 1) * hh]), bd(wx[h * hh:(h + 1) * hh])], axis=1)
                      for h in range(2)])


def kernel(x_prompt, x_sample, p_prompt, p_sample, state_rglru_h, state_rglru_conv, state_ffn_conv, g_mix_norm, w_in, conv_a_w, conv_a_b, lru_wa, lru_ba, lru_wx, lru_bx, lru_a_param, g_out_a, ln_v_g, ln_v_b, sgu_w, sgu_b, g_out_b, w_out, g_ffn_norm, w_up, ffn_conv_w, ffn_conv_b, w_down, g_ple_norm, w_ple_gate, w_ple, g_final):
    assert w_in.shape[0] == 1
    nbp, seq, _ = x_prompt.shape
    nbs, steps, _ = x_sample.shape
    ple = p_prompt.shape[-1]
    tm = PROMPT_BLOCK_ROWS
    nb = SAMPLE_GROUP
    assert seq % tm == 0 and tm % CHUNK == 0 and nbs % nb == 0 and nb % SLAB == 0 and steps <= CHUNK
    common = [('g_mix', g_mix_norm[0]), ('caw', conv_a_w[0]), ('cab', conv_a_b[0]), ('ba', lru_ba[0]),
              ('bx', lru_bx[0]), ('apar', lru_a_param[0]), ('g_oa', g_out_a[0]), ('lng', ln_v_g[0]),
              ('lnb', ln_v_b[0]), ('g_ob', g_out_b[0]), ('g_ffn', g_ffn_norm[0]), ('fcw', ffn_conv_w[0]),
              ('fcb', ffn_conv_b[0]), ('g_ple', g_ple_norm[0]), ('g_fin', g_final)]
    mats = [_pack_bf16_rows(w_in[0]), _pack_bf16_rows(_block_diag_gate(lru_wa[0], lru_wx[0]))]
    mats_tail = [_pack_bf16_rows(w_out[0]), _pack_bf16_rows(w_up[0]), _pack_bf16_rows(w_down[0]),
                 _pack_bf16_rows(w_ple_gate[0]), _pack_bf16_rows(w_ple[0])]

    sgb_p = jnp.repeat(jnp.transpose(sgu_b[0]), HD_B, axis=1)
    vecs_p, layout_p = _pack_vectors(common)
    p_args = [vecs_p] + mats + [sgu_w[0], sgb_p] + mats_tail
    scratch_p = [
        pltpu.VMEM((tm, D_MODEL), F32),
        pltpu.VMEM((tm, D_MODEL), BF16),
        pltpu.VMEM((tm, D_MODEL + W_A), F32),
        pltpu.VMEM((HIST + tm, W_A), F32),
        pltpu.VMEM((tm, W_A), F32),
        pltpu.VMEM((tm, W_A), BF16),
        pltpu.VMEM((tm, 2 * W_A), F32),
        pltpu.VMEM((tm, W_A), F32),
        pltpu.VMEM((tm, W_A), F32),
        pltpu.VMEM((tm, W_A), F32),
        pltpu.VMEM((SUBLANES, W_A), F32),
        pltpu.VMEM((SUBLANES, W_A), F32),
        pltpu.VMEM((tm, W_B), BF16),
        pltpu.VMEM((tm, W_A + W_B), BF16),
        pltpu.VMEM((tm, D_MODEL), BF16),
        pltpu.VMEM((HIST + tm, 2 * FF_CHUNK), F32),
        pltpu.VMEM((HIST + tm, 2 * FF_CHUNK), F32),
        pltpu.VMEM((HIST, 2 * D_FF), F32),
        pltpu.VMEM((tm, FF_CHUNK), BF16),
        pltpu.VMEM((tm, FF_CHUNK), BF16),
        pltpu.VMEM((tm, D_MODEL), F32),
        pltpu.VMEM((tm, D_MODEL), F32),
        pltpu.VMEM((tm, D_MODEL), F32),
    ]
    y_p, ht_p, cnew_p, fnew_p = pl.pallas_call(
        functools.partial(_prompt_kernel, tm=tm, layout=layout_p),
        grid=(nbp, seq // tm),
        in_specs=[pl.BlockSpec((1, tm, D_MODEL), lambda b, t: (b, t, 0)),
                  pl.BlockSpec((1, tm, ple), lambda b, t: (b, t, 0))]
                 + [_const_spec(a.shape) for a in p_args],
        out_specs=[pl.BlockSpec((1, tm, D_MODEL), lambda b, t: (b, t, 0)),
                   pl.BlockSpec((1, 1, W_A), lambda b, t: (b, 0, 0)),
                   pl.BlockSpec((1, CONV_A - 1, W_A), lambda b, t: (b, 0, 0)),
                   pl.BlockSpec((1, CONV_F - 1, 2 * D_FF), lambda b, t: (b, 0, 0))],
        out_shape=[jax.ShapeDtypeStruct((nbp, seq, D_MODEL), F32),
                   jax.ShapeDtypeStruct((nbp, 1, W_A), F32),
                   jax.ShapeDtypeStruct((nbp, CONV_A - 1, W_A), F32),
                   jax.ShapeDtypeStruct((nbp, CONV_F - 1, 2 * D_FF), F32)],
        scratch_shapes=scratch_p,
        compiler_params=pltpu.CompilerParams(dimension_semantics=("arbitrary", "arbitrary"),
                                             vmem_limit_bytes=VMEM_LIMIT_BYTES),
        name="prompt_layer",
    )(x_prompt, p_prompt[0], *p_args)

    rows = nb * steps
    ahist = (CONV_A - 1) * nb
    fhist = (CONV_F - 1) * nb
    tmaj = lambda a: jnp.swapaxes(a, 0, 1)
    sgw_s = jnp.repeat(jnp.transpose(sgu_w[0, :, :steps, :steps], (1, 2, 0)).reshape(steps * steps, H_B),
                       HD_B, axis=1)
    sgb_s = jnp.repeat(jnp.transpose(sgu_b[0, :, :steps]), HD_B, axis=1)
    vecs_s, layout_s = _pack_vectors(common + [('sgw', sgw_s), ('sgb', sgb_s)])
    w_args = [vecs_s] + mats + mats_tail
    scratch_s = [
        pltpu.VMEM((rows, D_MODEL), BF16),
        pltpu.VMEM((rows, D_MODEL + W_A), F32),
        pltpu.VMEM((ahist + rows, W_A), F32),
        pltpu.VMEM((rows, W_A), F32),
        pltpu.VMEM((rows, W_A), BF16),
        pltpu.VMEM((rows, 2 * W_A), F32),
        pltpu.VMEM((rows, W_A), F32),
        pltpu.VMEM((rows, W_A), F32),
        pltpu.VMEM((rows, W_A), F32),
        pltpu.VMEM((rows, W_A + W_B), BF16),
        pltpu.VMEM((rows, D_MODEL), F32),
        pltpu.VMEM((rows, ple), F32),
        pltpu.VMEM((fhist + rows, 2 * FF_CHUNK), F32),
        pltpu.VMEM((fhist + rows, 2 * FF_CHUNK), F32),
        pltpu.VMEM((rows, FF_CHUNK), BF16),
        pltpu.VMEM((rows, FF_CHUNK), BF16),
        pltpu.VMEM((rows, D_MODEL), F32),
    ]
    y_s, ht_s, cnew_s, vn_s, fnew_s = pl.pallas_call(
        functools.partial(_sample_kernel, nb=nb, steps=steps, layout=layout_s),
        grid=(nbs // nb,),
        in_specs=[pl.BlockSpec((steps, nb, D_MODEL), lambda i: (0, i, 0)),
                  pl.BlockSpec((steps, nb, ple), lambda i: (0, i, 0)),
                  pl.BlockSpec((nb, W_A), lambda i: (i, 0)),
                  pl.BlockSpec((CONV_A - 1, nb, W_A), lambda i: (0, i, 0)),
                  pl.BlockSpec((CONV_F - 1, nb, 2 * D_FF), lambda i: (0, i, 0))]
                 + [_const_spec(a.shape) for a in w_args],
        out_specs=[pl.BlockSpec((steps, nb, D_MODEL), lambda i: (0, i, 0)),
                   pl.BlockSpec((nb, W_A), lambda i: (i, 0)),
                   pl.BlockSpec((CONV_A - 1, nb, W_A), lambda i: (0, i, 0)),
                   pl.BlockSpec((steps, nb, W_B), lambda i: (0, i, 0)),
                   pl.BlockSpec((CONV_F - 1, nb, 2 * D_FF), lambda i: (0, i, 0))],
        out_shape=[jax.ShapeDtypeStruct((steps, nbs, D_MODEL), F32),
                   jax.ShapeDtypeStruct((nbs, W_A), F32),
                   jax.ShapeDtypeStruct((CONV_A - 1, nbs, W_A), F32),
                   jax.ShapeDtypeStruct((steps, nbs, W_B), F32),
                   jax.ShapeDtypeStruct((CONV_F - 1, nbs, 2 * D_FF), F32)],
        scratch_shapes=scratch_s,
        compiler_params=pltpu.CompilerParams(dimension_semantics=("arbitrary",),
                                             vmem_limit_bytes=VMEM_LIMIT_BYTES),
        name="sample_layer",
    )(tmaj(x_sample), tmaj(p_sample[0]), state_rglru_h[0], tmaj(state_rglru_conv[0]),
      tmaj(state_ffn_conv[0]), *w_args)

    return (y_p, tmaj(y_s), tmaj(ht_p), ht_s[None], cnew_p[None], tmaj(cnew_s)[None],
            tmaj(vn_s)[None], fnew_p[None], tmaj(fnew_s)[None])
```

```python
import functools

import jax
import jax.numpy as jnp
from jax import lax
from jax.experimental import pallas as pl
from jax.experimental.pallas import tpu as pltpu

F32 = jnp.float32
BF16 = jnp.bfloat16

D_MODEL = 1024
W_A = 512
W_B = 512
H_A = 8
BW_A = W_A // H_A
H_B = 4
HD_B = W_B // H_B
CHUNK = 128
D_FF = 3072
CONV_A = 4
CONV_F = 3
C_RG = 8.0
EPS = 1e-6

SUBLANES = 8
HIST = SUBLANES
GATE_HALF = W_A // 2
FF_CHUNK = 512
N_FF_CHUNKS = D_FF // FF_CHUNK
SLAB = 32
MXU_PIECE = 256
PACK_BLOCK_ROWS = 512
PACK_BLOCK_COLS = 2048
PROMPT_BLOCK_ROWS = 256
SAMPLE_GROUP = 32
VMEM_LIMIT_BYTES = 56 * 1024 * 1024


def _bc(tile, rows):
    return jnp.concatenate([tile] * (rows // SUBLANES), axis=0)


class _Vec:
    def __init__(self, ref, off, width, n):
        self.ref, self.off, self.width, self.n = ref, off, width, n

    def __getitem__(self, idx):
        full = slice(0, self.width)
        if self.n is None:
            j, cols = 0, (full if idx is Ellipsis else idx[1])
        else:
            j, cols = (idx, full) if isinstance(idx, int) else (idx[0], idx[2])
        start = self.off + j * self.width
        return self.ref[:, start + cols.start:start + cols.stop]


def _pack_vectors(named):
    layout, flat, off = {}, [], 0
    for name, a in named:
        n, width = (None, a.shape[0]) if a.ndim == 1 else a.shape
        layout[name] = (off, width, n)
        flat.append(a.reshape(-1))
        off += a.size
    block = jnp.broadcast_to(jnp.concatenate(flat)[None, :], (SUBLANES, off))
    return block, layout


def _rms(x, g):
    ms = jnp.mean(x * x, axis=-1, keepdims=True)
    return x * lax.rsqrt(ms + EPS) * _bc(g, x.shape[0])


def _layernorm(x, g, b):
    mu = jnp.mean(x, axis=-1, keepdims=True)
    xc = x - mu
    rows = x.shape[0]
    return xc * lax.rsqrt(jnp.mean(xc * xc, axis=-1, keepdims=True) + EPS) * _bc(g, rows) + _bc(b, rows)


def _sigmoid(x):
    return 1.0 / (1.0 + jnp.exp(-x))


def _softplus(x):
    return jnp.maximum(x, 0.0) + jnp.log(1.0 + jnp.exp(-jnp.abs(x)))


def _dot(a, b):
    return jnp.dot(a, b, preferred_element_type=F32)


def _fresh_rows(ref, salt, cols=slice(None)):
    zero = jnp.minimum(pl.program_id(0), 0) * salt
    return ref[pl.ds(pl.multiple_of(zero, 16), ref.shape[0]), cols]


def _wdot(a, w_words):
    return _dot(a, pltpu.bitcast(w_words, BF16))


def _ff_cols(c, half):
    start = half * D_FF + c * FF_CHUNK
    return slice(start, start + FF_CHUNK)


def _norm_to_bf16(src_ref, g_ref, dst_ref, rows):
    g = g_ref[...]
    for r0 in range(0, rows, SLAB):
        dst_ref[r0:r0 + SLAB, :] = _rms(src_ref[r0:r0 + SLAB, :], g).astype(BF16)


def _gate_stage(xc_ref, xcb_ref, g_ref, a_ref, u_ref, w_gate, ba, bx, apar, rows):
    for hf in range(2):
        c0 = hf * GATE_HALF
        g_ref[:, 2 * c0:2 * c0 + 2 * GATE_HALF] = _wdot(xcb_ref[:, c0:c0 + GATE_HALF], w_gate[hf])
    sp = _softplus(-apar[...])
    for r0 in range(0, rows, SLAB):
        for hf in range(2):
            c0 = hf * GATE_HALF
            cs = slice(c0, c0 + GATE_HALF)
            r = _sigmoid(g_ref[r0:r0 + SLAB, 2 * c0:2 * c0 + GATE_HALF] + _bc(ba[:, cs], SLAB))
            i = _sigmoid(g_ref[r0:r0 + SLAB, 2 * c0 + GATE_HALF:2 * c0 + 2 * GATE_HALF] + _bc(bx[:, cs], SLAB))
            a = jnp.exp((-C_RG) * r * _bc(sp[:, cs], SLAB))
            mult = jnp.sqrt(1.0 - a * a)
            a_ref[r0:r0 + SLAB, cs] = a
            u_ref[r0:r0 + SLAB, cs] = xc_ref[r0:r0 + SLAB, cs] * i * mult


def _branch_a_out(hs_ref, z_ref, g_oa, ymix_ref, rows):
    g = g_oa[...]
    for r0 in range(0, rows, SLAB):
        ga = z_ref[r0:r0 + SLAB, 0:W_A]
        ymix_ref[r0:r0 + SLAB, 0:W_A] = _rms(hs_ref[r0:r0 + SLAB, :] * jax.nn.gelu(ga), g).astype(BF16)


def _vn_stage(z_ref, lng, lnb, rows, write):
    g = lng[...]
    b = lnb[...]
    for r0 in range(0, rows, SLAB):
        vb = z_ref[r0:r0 + SLAB, W_A + W_B:W_A + 2 * W_B]
        write(r0, _layernorm(jax.nn.gelu(vb), g, b))


def _ffn_items(h_ref, n_ref, up2_ref, act2_ref, acc_ref, gz_ref, pe_ref, load_p, g_ffn, w_up, fcw, fcb,
               w_down, g_ple, w_pg, w_ple, g_fin, rows, shift, hist_rows, load_hist, store_hist, write_y):
    n_up = 2 * FF_CHUNK // MXU_PIECE
    n_down = D_MODEL // MXU_PIECE
    slabs = list(range(0, rows, SLAB))
    items = []

    def norm_in(r0):
        h = h_ref[r0:r0 + SLAB, :]
        acc_ref[r0:r0 + SLAB, :] = h
        n_ref[r0:r0 + SLAB, :] = _rms(h, g_ffn[...]).astype(BF16)

    def up_piece(c, j):
        up_ref = up2_ref[c % 2]
        cs = slice(j * MXU_PIECE, (j + 1) * MXU_PIECE)
        half, off = divmod(j * MXU_PIECE, FF_CHUNK)
        w0 = half * D_FF + c * FF_CHUNK + off
        if j == 0:
            load_hist(c, up_ref)
        lhs = _fresh_rows(n_ref, 1 + c * n_up + j)
        up_ref[hist_rows:hist_rows + rows, cs] = _wdot(lhs, w_up[:, w0:w0 + MXU_PIECE])

    def down_piece(c, j):
        cs = slice(j * MXU_PIECE, (j + 1) * MXU_PIECE)
        lhs = _fresh_rows(act2_ref[c % 2], 1 + c * n_down + j)
        acc_ref[:, cs] += _wdot(lhs, w_down[c * FF_CHUNK // 2:(c + 1) * FF_CHUNK // 2, cs])

    def conv_act_slab(c, r0, last):
        up_ref = up2_ref[c % 2]
        halves = []
        for half in range(2):
            cs = slice(half * FF_CHUNK, (half + 1) * FF_CHUNK)
            ws = _ff_cols(c, half)
            cv = _bc(fcb[:, ws], SLAB)
            for j in range(CONV_F):
                off = hist_rows + r0 - (CONV_F - 1 - j) * shift
                cv = cv + up_ref[off:off + SLAB, cs] * _bc(fcw[j, :, ws], SLAB)
            halves.append(cv)
        act2_ref[c % 2][r0:r0 + SLAB, :] = (jax.nn.gelu(halves[0]) * halves[1]).astype(BF16)
        if last:
            store_hist(c, up_ref)

    def embed():
        pe_ref[...] = _wdot(load_p().astype(BF16), w_ple[...])

    def norm_mid(r0):
        n_ref[r0:r0 + SLAB, :] = _rms(acc_ref[r0:r0 + SLAB, :], g_ple[...]).astype(BF16)

    def gate_piece(j):
        cs = slice(j * MXU_PIECE, (j + 1) * MXU_PIECE)
        gz_ref[:, cs] = _wdot(_fresh_rows(n_ref, 1 + N_FF_CHUNKS * n_up + j), w_pg[:, cs])

    def finish(r0):
        gate = _sigmoid(gz_ref[r0:r0 + SLAB, 0:D_MODEL])
        h3 = acc_ref[r0:r0 + SLAB, :] + pe_ref[r0:r0 + SLAB, :] * gate
        write_y(r0, _rms(h3, g_fin[...]))

    items += [functools.partial(norm_in, r0) for r0 in slabs]
    items += [functools.partial(up_piece, 0, j) for j in range(n_up)]
    for c in range(N_FF_CHUNKS + 1):
        pieces = []
        for j in range(max(n_up, n_down)):
            if c + 1 < N_FF_CHUNKS and j < n_up:
                pieces.append(functools.partial(up_piece, c + 1, j))
            if c >= 1 and j < n_down:
                pieces.append(functools.partial(down_piece, c - 1, j))
        if c < N_FF_CHUNKS:
            done = 0
            for i, r0 in enumerate(slabs):
                items.append(functools.partial(conv_act_slab, c, r0, i == len(slabs) - 1))
                upto = (i + 1) * len(pieces) // len(slabs)
                items += pieces[done:upto]
                done = upto
        else:
            items += pieces
    items += [functools.partial(norm_mid, r0) for r0 in slabs]
    items += [functools.partial(gate_piece, j) for j in range(n_down)]
    items += [functools.partial(finish, r0) for r0 in slabs]
    return embed, items


def _run(items):
    for item in items:
        item()


def _prompt_mixer_items(x_ref, h_mid_ref, n_ref, z_ref, xa_ref, xc_ref, xcb_ref, g_ref, a_ref, u_ref, hs_ref,
                        carry_ref, sp_ref, vn_ref, ymix_ref, g_mix, w_in, caw, cab, w_gate, ba, bx, apar,
                        g_oa, lng, lnb, sgw, sgb, g_ob, w_out, tm, reset_row, extra_matmul):
    assert tm % CHUNK == 0 and tm % SLAB == 0
    slabs = list(range(0, tm, SLAB))

    def norm_in(r0):
        n_ref[r0:r0 + SLAB, :] = _rms(x_ref[0, r0:r0 + SLAB, :], g_mix[...]).astype(BF16)

    def in_piece(j):
        c0 = j * MXU_PIECE
        res = _wdot(_fresh_rows(n_ref, 1 + j), w_in[:, c0:c0 + MXU_PIECE])
        if c0 < W_A:
            xa_ref[HIST:HIST + tm, c0:c0 + MXU_PIECE] = res
        else:
            z_ref[:, c0 - W_A:c0 - W_A + MXU_PIECE] = res

    def conv(r0):
        if r0 == 0:
            sp_ref[...] = _softplus(-apar[...])
        cv = _bc(cab[...], SLAB)
        for j in range(CONV_A):
            off = HIST + r0 - (CONV_A - 1 - j)
            cv = cv + xa_ref[off:off + SLAB, :] * _bc(caw[j], SLAB)
        xc_ref[r0:r0 + SLAB, :] = cv
        xcb_ref[r0:r0 + SLAB, :] = cv.astype(BF16)
        if r0 + SLAB == tm:
            xa_ref[0:HIST, :] = xa_ref[tm:tm + HIST, :]

    def gate_piece(hf):
        c0 = hf * GATE_HALF
        lhs = _fresh_rows(xcb_ref, 1 + hf, slice(c0, c0 + GATE_HALF))
        g_ref[:, 2 * c0:2 * c0 + 2 * GATE_HALF] = _wdot(lhs, w_gate[hf])

    def gate_ew(r0, hf):
        c0 = hf * GATE_HALF
        cs = slice(c0, c0 + GATE_HALF)
        r = _sigmoid(g_ref[r0:r0 + SLAB, 2 * c0:2 * c0 + GATE_HALF] + _bc(ba[:, cs], SLAB))
        i = _sigmoid(g_ref[r0:r0 + SLAB, 2 * c0 + GATE_HALF:2 * c0 + 2 * GATE_HALF] + _bc(bx[:, cs], SLAB))
        a = jnp.exp((-C_RG) * r * _bc(sp_ref[:, cs], SLAB))
        mult = jnp.sqrt(1.0 - a * a)
        if r0 == 0:
            row = lax.broadcasted_iota(jnp.int32, (SLAB, GATE_HALF), 0)
            mult = jnp.where(row == reset_row, 1.0, mult)
        a_ref[r0:r0 + SLAB, cs] = a
        u_ref[r0:r0 + SLAB, cs] = xc_ref[r0:r0 + SLAB, cs] * i * mult

    def scan_group(r0):
        row = lax.broadcasted_iota(jnp.int32, (SUBLANES, W_A), 0)
        a = a_ref[r0:r0 + SUBLANES, :]
        u = u_ref[r0:r0 + SUBLANES, :]
        for d in (1, 2, 4):
            keep = row >= d
            a_s = jnp.where(keep, pltpu.roll(a, d, 0), 1.0)
            u_s = jnp.where(keep, pltpu.roll(u, d, 0), 0.0)
            u = a * u_s + u
            a = a * a_s
        h = a * carry_ref[...] + u
        hs_ref[r0:r0 + SUBLANES, :] = h
        carry_ref[...] = jnp.broadcast_to(h[SUBLANES - 1:SUBLANES, :], (SUBLANES, W_A))

    def a_out(r0):
        ga = z_ref[r0:r0 + SLAB, 0:W_A]
        ymix_ref[r0:r0 + SLAB, 0:W_A] = _rms(hs_ref[r0:r0 + SLAB, :] * jax.nn.gelu(ga), g_oa[...]).astype(BF16)

    def vn(r0):
        vb = z_ref[r0:r0 + SLAB, W_A + W_B:W_A + 2 * W_B]
        vn_ref[r0:r0 + SLAB, :] = _layernorm(jax.nn.gelu(vb), lng[...], lnb[...]).astype(BF16)

    def sgu(c0):
        ti = lax.broadcasted_iota(jnp.int32, (CHUNK, CHUNK), 0)
        si = lax.broadcasted_iota(jnp.int32, (CHUNK, CHUNK), 1)
        heads = []
        for hd in range(H_B):
            cs = slice(hd * HD_B, (hd + 1) * HD_B)
            wmix = jnp.where(si <= ti, sgw[hd], 0.0).astype(BF16)
            mixed = _dot(wmix, vn_ref[c0:c0 + CHUNK, cs]) + sgb[:, cs]
            ub = z_ref[c0:c0 + CHUNK, W_A + hd * HD_B:W_A + (hd + 1) * HD_B]
            heads.append(jax.nn.gelu(ub) * mixed)
        yb = jnp.concatenate(heads, axis=-1)
        ymix_ref[c0:c0 + CHUNK, W_A:] = _rms(yb, g_ob[...]).astype(BF16)

    def out_piece(j):
        cs = slice(j * MXU_PIECE, (j + 1) * MXU_PIECE)
        h_mid_ref[:, cs] = x_ref[0, :, cs] + _wdot(_fresh_rows(ymix_ref, 1 + j), w_out[:, cs])

    items = [functools.partial(norm_in, r0) for r0 in slabs]
    items += [functools.partial(in_piece, j) for j in range((2 * W_A + 2 * W_B) // MXU_PIECE)]
    items += [functools.partial(conv, r0) for r0 in slabs]
    items += [functools.partial(gate_piece, hf) for hf in range(2)] + [extra_matmul]
    items += [functools.partial(gate_ew, r0, hf) for r0 in slabs for hf in range(2)]
    items += [functools.partial(scan_group, r0) for r0 in range(0, tm, SUBLANES)]
    items += [functools.partial(a_out, r0) for r0 in slabs]
    items += [functools.partial(vn, r0) for r0 in slabs]
    items += [functools.partial(sgu, c0) for c0 in range(0, tm, CHUNK)]
    items += [functools.partial(out_piece, j) for j in range(D_MODEL // MXU_PIECE)]
    return items


def _prompt_kernel(x_ref, p_ref, vecs, w_in, w_gate, sgw, sgb, w_out, w_up, w_down, w_pg, w_ple,
                   y_ref, ht_ref, cnew_ref, fnew_ref,
                   h_mid_ref, n1_ref, z_ref, xa_ref, xc_ref, xcb_ref, g_ref, a_ref, u_ref, hs_ref, carry_ref,
                   sp_ref, vn_ref, ymix_ref, n2_ref, up_ref, upb_ref, fhist_ref, act_ref, actb_ref, acc_ref,
                   gz_ref, pe_ref, *, tm, layout):
    v = {name: _Vec(vecs, *spec) for name, spec in layout.items()}
    g_mix, caw, cab, ba, bx, apar = v['g_mix'], v['caw'], v['cab'], v['ba'], v['bx'], v['apar']
    g_oa, lng, lnb, g_ob = v['g_oa'], v['lng'], v['lnb'], v['g_ob']
    g_ffn, fcw, fcb, g_ple, g_fin = v['g_ffn'], v['fcw'], v['fcb'], v['g_ple'], v['g_fin']
    t = pl.program_id(1)

    @pl.when(t == 0)
    def _():
        xa_ref[0:HIST, :] = jnp.zeros((HIST, W_A), F32)
        carry_ref[...] = jnp.zeros((SUBLANES, W_A), F32)
        fhist_ref[...] = jnp.zeros(fhist_ref.shape, F32)

    def load_hist(c, up):
        for half in range(2):
            up[0:HIST, half * FF_CHUNK:(half + 1) * FF_CHUNK] = fhist_ref[:, _ff_cols(c, half)]

    def store_hist(c, up):
        for half in range(2):
            cs = slice(half * FF_CHUNK, (half + 1) * FF_CHUNK)
            fhist_ref[:, _ff_cols(c, half)] = up[tm:tm + HIST, cs]
            fnew_ref[0, :, _ff_cols(c, half)] = up[HIST + tm - (CONV_F - 1):HIST + tm, cs]

    def write_y(r0, y):
        y_ref[0, r0:r0 + SLAB, :] = y

    embed, ffn = _ffn_items(h_mid_ref, n2_ref, (up_ref, upb_ref), (act_ref, actb_ref), acc_ref, gz_ref, pe_ref,
                            lambda: p_ref[0], g_ffn, w_up, fcw, fcb, w_down, g_ple, w_pg, w_ple, g_fin, tm, 1,
                            HIST, load_hist, store_hist, write_y)
    reset_row = jnp.where(t == 0, 0, -1)
    mixer = _prompt_mixer_items(x_ref, h_mid_ref, n1_ref, z_ref, xa_ref, xc_ref, xcb_ref, g_ref, a_ref, u_ref,
                                hs_ref, carry_ref, sp_ref, vn_ref, ymix_ref, g_mix, w_in, caw, cab, w_gate,
                                ba, bx, apar, g_oa, lng, lnb, sgw, sgb, g_ob, w_out, tm, reset_row, embed)
    _run(mixer)
    _run(ffn)

    ht_ref[0] = carry_ref[0:1, :]
    cnew_ref[0] = xa_ref[HIST + tm - (CONV_A - 1):HIST + tm, :]


def _sample_kernel(x_ref, p_ref, h0_ref, chist_ref, fh_ref, vecs, w_in, w_gate, w_out, w_up, w_down,
                   w_pg, w_ple,
                   y_ref, ht_ref, cnew_ref, vn_out_ref, fnew_ref,
                   n_ref, z_ref, xa_ref, xc_ref, xcb_ref, g_ref, a_ref, u_ref, hs_ref, ymix_ref,
                   h_ref, p_buf, up_ref, upb_ref, act_ref, actb_ref, acc_ref, *, nb, steps, layout):
    v = {name: _Vec(vecs, *spec) for name, spec in layout.items()}
    g_mix, caw, cab, ba, bx, apar = v['g_mix'], v['caw'], v['cab'], v['ba'], v['bx'], v['apar']
    g_oa, lng, lnb, sgw, sgb, g_ob = v['g_oa'], v['lng'], v['lnb'], v['sgw'], v['sgb'], v['g_ob']
    g_ffn, fcw, fcb, g_ple, g_fin = v['g_ffn'], v['fcw'], v['fcb'], v['g_ple'], v['g_fin']
    rows = nb * steps
    ahist = (CONV_A - 1) * nb
    fhist = (CONV_F - 1) * nb

    for s in range(steps):
        h_ref[s * nb:(s + 1) * nb, :] = x_ref[s]
        p_buf[s * nb:(s + 1) * nb, :] = p_ref[s]
    _norm_to_bf16(h_ref, g_mix, n_ref, rows)
    for k in range(CONV_A - 1):
        xa_ref[k * nb:(k + 1) * nb, :] = chist_ref[k]
    xa_ref[ahist:ahist + rows, :] = _wdot(n_ref[...], w_in[:, 0:W_A])
    z_ref[...] = _wdot(n_ref[...], w_in[:, W_A:])

    for r0 in range(0, rows, SLAB):
        cv = _bc(cab[...], SLAB)
        for j in range(CONV_A):
            off = r0 + j * nb
            cv = cv + xa_ref[off:off + SLAB, :] * _bc(caw[j], SLAB)
        xc_ref[r0:r0 + SLAB, :] = cv
        xcb_ref[r0:r0 + SLAB, :] = cv.astype(BF16)
    for k in range(CONV_A - 1):
        cnew_ref[k] = xa_ref[rows + k * nb:rows + (k + 1) * nb, :]

    _gate_stage(xc_ref, xcb_ref, g_ref, a_ref, u_ref, w_gate, ba, bx, apar, rows)

    for b0 in range(0, nb, SLAB):
        h = h0_ref[b0:b0 + SLAB, :]
        for s in range(steps):
            r0 = s * nb + b0
            h = a_ref[r0:r0 + SLAB, :] * h + u_ref[r0:r0 + SLAB, :]
            hs_ref[r0:r0 + SLAB, :] = h
        ht_ref[b0:b0 + SLAB, :] = h

    _branch_a_out(hs_ref, z_ref, g_oa, ymix_ref, rows)

    def write_vn(r0, vn):
        vn_out_ref[r0 // nb, r0 % nb:r0 % nb + SLAB, :] = vn

    _vn_stage(z_ref, lng, lnb, rows, write_vn)

    gob = g_ob[...]
    for b0 in range(0, nb, SLAB):
        for tt in range(steps):
            r0 = tt * nb + b0
            mixed = _bc(sgb[tt], SLAB)
            for s in range(tt + 1):
                k = tt * steps + s
                mixed = mixed + vn_out_ref[s, b0:b0 + SLAB, :] * _bc(sgw[k], SLAB)
            yb = jax.nn.gelu(z_ref[r0:r0 + SLAB, W_A:W_A + W_B]) * mixed
            ymix_ref[r0:r0 + SLAB, W_A:] = _rms(yb, gob).astype(BF16)

    h_ref[...] += _wdot(ymix_ref[...], w_out[...])

    def load_hist(c, up):
        for k in range(CONV_F - 1):
            for half in range(2):
                cs = slice(half * FF_CHUNK, (half + 1) * FF_CHUNK)
                up[k * nb:(k + 1) * nb, cs] = fh_ref[k, :, _ff_cols(c, half)]

    def store_hist(c, up):
        for k in range(CONV_F - 1):
            for half in range(2):
                cs = slice(half * FF_CHUNK, (half + 1) * FF_CHUNK)
                fnew_ref[k, :, _ff_cols(c, half)] = up[rows + k * nb:rows + (k + 1) * nb, cs]

    def write_y(r0, y):
        y_ref[r0 // nb, r0 % nb:r0 % nb + SLAB, :] = y

    embed, items = _ffn_items(h_ref, n_ref, (up_ref, upb_ref), (act_ref, actb_ref), acc_ref, z_ref, g_ref,
                              lambda: p_buf[...], g_ffn, w_up, fcw, fcb, w_down, g_ple, w_pg, w_ple, g_fin,
                              rows, nb, fhist, load_hist, store_hist, write_y)
    _run([embed] + items)


def _const_spec(shape):
    zeros = (0,) * len(shape)
    return pl.BlockSpec(shape, lambda *_: zeros, pipeline_mode=pl.Buffered(1))


def _pack_bf16_rows(w):
    lead = w.shape[:-2]
    n = w.shape[-1]
    w2 = w.reshape((-1, n))
    k = w2.shape[0]
    bk = min(k, PACK_BLOCK_ROWS)
    bn = min(n, PACK_BLOCK_COLS)
    assert k % bk == 0 and n % bn == 0 and w.shape[-2] % 2 == 0
    words = pl.pallas_call(
        _pack_kernel,
        grid=(k // bk, n // bn),
        in_specs=[pl.BlockSpec((bk, bn), lambda i, j: (i, j))],
        out_specs=pl.BlockSpec((bk // 2, bn), lambda i, j: (i, j)),
        out_shape=jax.ShapeDtypeStruct((k // 2, n), jnp.uint32),
        name="pack_weight",
    )(w2)
    return words.reshape(lead + (w.shape[-2] // 2, n))


def _pack_kernel(w_ref, o_ref):
    o_ref[...] = pltpu.bitcast(w_ref[...].astype(BF16), jnp.uint32)


def _block_diag_gate(wa, wx):
    hh = H_A // 2
    eye = jnp.eye(hh, dtype=wa.dtype)

    def bd(w):
        return jnp.einsum('hij,hg->higj', w, eye).reshape(GATE_HALF, GATE_HALF)

    return jnp.stack([jnp.concatenate([bd(wa[h * hh:(h + 1) * hh]), bd(wx[h * hh:(h + 1) * hh])], axis=1)
                      for h in range(2)])


def kernel(x_prompt, x_sample, p_prompt, p_sample, state_rglru_h, state_rglru_conv, state_ffn_conv, g_mix_norm, w_in, conv_a_w, conv_a_b, lru_wa, lru_ba, lru_wx, lru_bx, lru_a_param, g_out_a, ln_v_g, ln_v_b, sgu_w, sgu_b, g_out_b, w_out, g_ffn_norm, w_up, ffn_conv_w, ffn_conv_b, w_down, g_ple_norm, w_ple_gate, w_ple, g_final):
    assert w_in.shape[0] == 1
    nbp, seq, _ = x_prompt.shape
    nbs, steps, _ = x_sample.shape
    ple = p_prompt.shape[-1]
    tm = PROMPT_BLOCK_ROWS
    nb = SAMPLE_GROUP
    assert seq % tm == 0 and tm % CHUNK == 0 and nbs % nb == 0 and nb % SLAB == 0 and steps <= CHUNK
    common = [('g_mix', g_mix_norm[0]), ('caw', conv_a_w[0]), ('cab', conv_a_b[0]), ('ba', lru_ba[0]),
              ('bx', lru_bx[0]), ('apar', lru_a_param[0]), ('g_oa', g_out_a[0]), ('lng', ln_v_g[0]),
              ('lnb', ln_v_b[0]), ('g_ob', g_out_b[0]), ('g_ffn', g_ffn_norm[0]), ('fcw', ffn_conv_w[0]),
              ('fcb', ffn_conv_b[0]), ('g_ple', g_ple_norm[0]), ('g_fin', g_final)]
    mats = [_pack_bf16_rows(w_in[0]), _pack_bf16_rows(_block_diag_gate(lru_wa[0], lru_wx[0]))]
    mats_tail = [_pack_bf16_rows(w_out[0]), _pack_bf16_rows(w_up[0]), _pack_bf16_rows(w_down[0]),
                 _pack_bf16_rows(w_ple_gate[0]), _pack_bf16_rows(w_ple[0])]

    sgb_p = jnp.repeat(jnp.transpose(sgu_b[0]), HD_B, axis=1)
    vecs_p, layout_p = _pack_vectors(common)
    p_args = [vecs_p] + mats + [sgu_w[0], sgb_p] + mats_tail
    scratch_p = [
        pltpu.VMEM((tm, D_MODEL), F32),
        pltpu.VMEM((tm, D_MODEL), BF16),
        pltpu.VMEM((tm, D_MODEL + W_A), F32),
        pltpu.VMEM((HIST + tm, W_A), F32),
        pltpu.VMEM((tm, W_A), F32),
        pltpu.VMEM((tm, W_A), BF16),
        pltpu.VMEM((tm, 2 * W_A), F32),
        pltpu.VMEM((tm, W_A), F32),
        pltpu.VMEM((tm, W_A), F32),
        pltpu.VMEM((tm, W_A), F32),
        pltpu.VMEM((SUBLANES, W_A), F32),
        pltpu.VMEM((SUBLANES, W_A), F32),
        pltpu.VMEM((tm, W_B), BF16),
        pltpu.VMEM((tm, W_A + W_B), BF16),
        pltpu.VMEM((tm, D_MODEL), BF16),
        pltpu.VMEM((HIST + tm, 2 * FF_CHUNK), F32),
        pltpu.VMEM((HIST + tm, 2 * FF_CHUNK), F32),
        pltpu.VMEM((HIST, 2 * D_FF), F32),
        pltpu.VMEM((tm, FF_CHUNK), BF16),
        pltpu.VMEM((tm, FF_CHUNK), BF16),
        pltpu.VMEM((tm, D_MODEL), F32),
        pltpu.VMEM((tm, D_MODEL), F32),
        pltpu.VMEM((tm, D_MODEL), F32),
    ]
    y_p, ht_p, cnew_p, fnew_p = pl.pallas_call(
        functools.partial(_prompt_kernel, tm=tm, layout=layout_p),
        grid=(nbp, seq // tm),
        in_specs=[pl.BlockSpec((1, tm, D_MODEL), lambda b, t: (b, t, 0)),
                  pl.BlockSpec((1, tm, ple), lambda b, t: (b, t, 0))]
                 + [_const_spec(a.shape) for a in p_args],
        out_specs=[pl.BlockSpec((1, tm, D_MODEL), lambda b, t: (b, t, 0)),
                   pl.BlockSpec((1, 1, W_A), lambda b, t: (b, 0, 0)),
                   pl.BlockSpec((1, CONV_A - 1, W_A), lambda b, t: (b, 0, 0)),
                   pl.BlockSpec((1, CONV_F - 1, 2 * D_FF), lambda b, t: (b, 0, 0))],
        out_shape=[jax.ShapeDtypeStruct((nbp, seq, D_MODEL), F32),
                   jax.ShapeDtypeStruct((nbp, 1, W_A), F32),
                   jax.ShapeDtypeStruct((nbp, CONV_A - 1, W_A), F32),
                   jax.ShapeDtypeStruct((nbp, CONV_F - 1, 2 * D_FF), F32)],
        scratch_shapes=scratch_p,
        compiler_params=pltpu.CompilerParams(dimension_semantics=("arbitrary", "arbitrary"),
                                             vmem_limit_bytes=VMEM_LIMIT_BYTES),
        name="prompt_layer",
    )(x_prompt, p_prompt[0], *p_args)

    rows = nb * steps
    ahist = (CONV_A - 1) * nb
    fhist = (CONV_F - 1) * nb
    tmaj = lambda a: jnp.swapaxes(a, 0, 1)
    sgw_s = jnp.repeat(jnp.transpose(sgu_w[0, :, :steps, :steps], (1, 2, 0)).reshape(steps * steps, H_B),
                       HD_B, axis=1)
    sgb_s = jnp.repeat(jnp.transpose(sgu_b[0, :, :steps]), HD_B, axis=1)
    vecs_s, layout_s = _pack_vectors(common + [('sgw', sgw_s), ('sgb', sgb_s)])
    w_args = [vecs_s] + mats + mats_tail
    scratch_s = [
        pltpu.VMEM((rows, D_MODEL), BF16),
        pltpu.VMEM((rows, D_MODEL + W_A), F32),
        pltpu.VMEM((ahist + rows, W_A), F32),
        pltpu.VMEM((rows, W_A), F32),
        pltpu.VMEM((rows, W_A), BF16),
        pltpu.VMEM((rows, 2 * W_A), F32),
        pltpu.VMEM((rows, W_A), F32),
        pltpu.VMEM((rows, W_A), F32),
        pltpu.VMEM((rows, W_A), F32),
        pltpu.VMEM((rows, W_A + W_B), BF16),
        pltpu.VMEM((rows, D_MODEL), F32),
        pltpu.VMEM((rows, ple), F32),
        pltpu.VMEM((fhist + rows, 2 * FF_CHUNK), F32),
        pltpu.VMEM((fhist + rows, 2 * FF_CHUNK), F32),
        pltpu.VMEM((rows, FF_CHUNK), BF16),
        pltpu.VMEM((rows, FF_CHUNK), BF16),
        pltpu.VMEM((rows, D_MODEL), F32),
    ]
    y_s, ht_s, cnew_s, vn_s, fnew_s = pl.pallas_call(
        functools.partial(_sample_kernel, nb=nb, steps=steps, layout=layout_s),
        grid=(nbs // nb,),
        in_specs=[pl.BlockSpec((steps, nb, D_MODEL), lambda i: (0, i, 0)),
                  pl.BlockSpec((steps, nb, ple), lambda i: (0, i, 0)),
                  pl.BlockSpec((nb, W_A), lambda i: (i, 0)),
                  pl.BlockSpec((CONV_A - 1, nb, W_A), lambda i: (0, i, 0)),
                  pl.BlockSpec((CONV_F - 1, nb, 2 * D_FF), lambda i: (0, i, 0))]
                 + [_const_spec(a.shape) for a in w_args],
        out_specs=[pl.BlockSpec((steps, nb, D_MODEL), lambda i: (0, i, 0)),
                   pl.BlockSpec((nb, W_A), lambda i: (i, 0)),
                   pl.BlockSpec((CONV_A - 1, nb, W_A), lambda i: (0, i, 0)),
                   pl.BlockSpec((steps, nb, W_B), lambda i: (0, i, 0)),
                   pl.BlockSpec((CONV_F - 1, nb, 2 * D_FF), lambda i: (0, i, 0))],
        out_shape=[jax.ShapeDtypeStruct((steps, nbs, D_MODEL), F32),
                   jax.ShapeDtypeStruct((nbs, W_A), F32),
                   jax.ShapeDtypeStruct((CONV_A - 1, nbs, W_A), F32),
                   jax.ShapeDtypeStruct((steps, nbs, W_B), F32),
                   jax.ShapeDtypeStruct((CONV_F - 1, nbs, 2 * D_FF), F32)],
        scratch_shapes=scratch_s,
        compiler_params=pltpu.CompilerParams(dimension_semantics=("arbitrary",),
                                             vmem_limit_bytes=VMEM_LIMIT_BYTES),
        name="sample_layer",
    )(tmaj(x_sample), tmaj(p_sample[0]), state_rglru_h[0], tmaj(state_rglru_conv[0]),
      tmaj(state_ffn_conv[0]), *w_args)

    return (y_p, tmaj(y_s), tmaj(ht_p), ht_s[None], cnew_p[None], tmaj(cnew_s)[None],
            tmaj(vn_s)[None], fnew_p[None], tmaj(fnew_s)[None])
```

```python
import functools

import jax
import jax.numpy as jnp
from jax import lax
from jax.experimental import pallas as pl
from jax.experimental.pallas import tpu as pltpu

F32 = jnp.float32
BF16 = jnp.bfloat16

D_MODEL = 1024
W_A = 512
W_B = 512
H_A = 8
BW_A = W_A // H_A
H_B = 4
HD_B = W_B // H_B
CHUNK = 128
D_FF = 3072
CONV_A = 4
CONV_F = 3
C_RG = 8.0
EPS = 1e-6

SUBLANES = 8
HIST = SUBLANES
GATE_HALF = W_A // 2
FF_CHUNK = 512
N_FF_CHUNKS = D_FF // FF_CHUNK
SLAB = 32
MXU_PIECE = 256
PACK_BLOCK_ROWS = 512
PACK_BLOCK_COLS = 2048
PROMPT_BLOCK_ROWS = 256
SAMPLE_GROUP = 32
VMEM_LIMIT_BYTES = 56 * 1024 * 1024


def _bc(tile, rows):
    return jnp.concatenate([tile] * (rows // SUBLANES), axis=0)


class _Vec:
    def __init__(self, ref, off, width, n):
        self.ref, self.off, self.width, self.n = ref, off, width, n

    def __getitem__(self, idx):
        full = slice(0, self.width)
        if self.n is None:
            j, cols = 0, (full if idx is Ellipsis else idx[1])
        else:
            j, cols = (idx, full) if isinstance(idx, int) else (idx[0], idx[2])
        start = self.off + j * self.width
        return self.ref[:, start + cols.start:start + cols.stop]


def _pack_vectors(named):
    layout, flat, off = {}, [], 0
    for name, a in named:
        n, width = (None, a.shape[0]) if a.ndim == 1 else a.shape
        layout[name] = (off, width, n)
        flat.append(a.reshape(-1))
        off += a.size
    block = jnp.broadcast_to(jnp.concatenate(flat)[None, :], (SUBLANES, off))
    return block, layout


def _rms(x, g):
    ms = jnp.mean(x * x, axis=-1, keepdims=True)
    return x * lax.rsqrt(ms + EPS) * _bc(g, x.shape[0])


def _layernorm(x, g, b):
    mu = jnp.mean(x, axis=-1, keepdims=True)
    xc = x - mu
    rows = x.shape[0]
    return xc * lax.rsqrt(jnp.mean(xc * xc, axis=-1, keepdims=True) + EPS) * _bc(g, rows) + _bc(b, rows)


def _sigmoid(x):
    return 1.0 / (1.0 + jnp.exp(-x))


def _softplus(x):
    return jnp.maximum(x, 0.0) + jnp.log(1.0 + jnp.exp(-jnp.abs(x)))


def _dot(a, b):
    return jnp.dot(a, b, preferred_element_type=F32)


def _fresh_rows(ref, salt, cols=slice(None)):
    zero = jnp.minimum(pl.program_id(0), 0) * salt
    return ref[pl.ds(pl.multiple_of(zero, 16), ref.shape[0]), cols]


def _wdot(a, w_words):
    return _dot(a, pltpu.bitcast(w_words, BF16))


def _ff_cols(c, half):
    start = half * D_FF + c * FF_CHUNK
    return slice(start, start + FF_CHUNK)


def _norm_to_bf16(src_ref, g_ref, dst_ref, rows):
    g = g_ref[...]
    for r0 in range(0, rows, SLAB):
        dst_ref[r0:r0 + SLAB, :] = _rms(src_ref[r0:r0 + SLAB, :], g).astype(BF16)


def _gate_stage(xc_ref, xcb_ref, g_ref, a_ref, u_ref, w_gate, ba, bx, apar, rows):
    for hf in range(2):
        c0 = hf * GATE_HALF
        g_ref[:, 2 * c0:2 * c0 + 2 * GATE_HALF] = _wdot(xcb_ref[:, c0:c0 + GATE_HALF], w_gate[hf])
    sp = _softplus(-apar[...])
    for r0 in range(0, rows, SLAB):
        for hf in range(2):
            c0 = hf * GATE_HALF
            cs = slice(c0, c0 + GATE_HALF)
            r = _sigmoid(g_ref[r0:r0 + SLAB, 2 * c0:2 * c0 + GATE_HALF] + _bc(ba[:, cs], SLAB))
            i = _sigmoid(g_ref[r0:r0 + SLAB, 2 * c0 + GATE_HALF:2 * c0 + 2 * GATE_HALF] + _bc(bx[:, cs], SLAB))
            a = jnp.exp((-C_RG) * r * _bc(sp[:, cs], SLAB))
            mult = jnp.sqrt(1.0 - a * a)
            a_ref[r0:r0 + SLAB, cs] = a
            u_ref[r0:r0 + SLAB, cs] = xc_ref[r0:r0 + SLAB, cs] * i * mult


def _branch_a_out(hs_ref, z_ref, g_oa, ymix_ref, rows):
    g = g_oa[...]
    for r0 in range(0, rows, SLAB):
        ga = z_ref[r0:r0 + SLAB, 0:W_A]
        ymix_ref[r0:r0 + SLAB, 0:W_A] = _rms(hs_ref[r0:r0 + SLAB, :] * jax.nn.gelu(ga), g).astype(BF16)


def _vn_stage(z_ref, lng, lnb, rows, write):
    g = lng[...]
    b = lnb[...]
    for r0 in range(0, rows, SLAB):
        vb = z_ref[r0:r0 + SLAB, W_A + W_B:W_A + 2 * W_B]
        write(r0, _layernorm(jax.nn.gelu(vb), g, b))


def _ffn_items(h_ref, n_ref, up2_ref, act2_ref, acc_ref, gz_ref, pe_ref, load_p, g_ffn, w_up, fcw, fcb,
               w_down, g_ple, w_pg, w_ple, g_fin, rows, shift, hist_rows, load_hist, store_hist, write_y):
    n_up = 2 * FF_CHUNK // MXU_PIECE
    n_down = D_MODEL // MXU_PIECE
    slabs = list(range(0, rows, SLAB))
    items = []

    def norm_in(r0):
        h = h_ref[r0:r0 + SLAB, :]
        acc_ref[r0:r0 + SLAB, :] = h
        n_ref[r0:r0 + SLAB, :] = _rms(h, g_ffn[...]).astype(BF16)

    def up_piece(c, j):
        up_ref = up2_ref[c % 2]
        cs = slice(j * MXU_PIECE, (j + 1) * MXU_PIECE)
        half, off = divmod(j * MXU_PIECE, FF_CHUNK)
        w0 = half * D_FF + c * FF_CHUNK + off
        if j == 0:
            load_hist(c, up_ref)
        lhs = _fresh_rows(n_ref, 1 + c * n_up + j)
        up_ref[hist_rows:hist_rows + rows, cs] = _wdot(lhs, w_up[:, w0:w0 + MXU_PIECE])

    def down_piece(c, j):
        cs = slice(j * MXU_PIECE, (j + 1) * MXU_PIECE)
        lhs = _fresh_rows(act2_ref[c % 2], 1 + c * n_down + j)
        acc_ref[:, cs] += _wdot(lhs, w_down[c * FF_CHUNK // 2:(c + 1) * FF_CHUNK // 2, cs])

    def conv_act_slab(c, r0, last):
        up_ref = up2_ref[c % 2]
        halves = []
        for half in range(2):
            cs = slice(half * FF_CHUNK, (half + 1) * FF_CHUNK)
            ws = _ff_cols(c, half)
            cv = _bc(fcb[:, ws], SLAB)
            for j in range(CONV_F):
                off = hist_rows + r0 - (CONV_F - 1 - j) * shift
                cv = cv + up_ref[off:off + SLAB, cs] * _bc(fcw[j, :, ws], SLAB)
            halves.append(cv)
        act2_ref[c % 2][r0:r0 + SLAB, :] = (jax.nn.gelu(halves[0]) * halves[1]).astype(BF16)
        if last:
            store_hist(c, up_ref)

    def embed():
        pe_ref[...] = _wdot(load_p().astype(BF16), w_ple[...])

    def norm_mid(r0):
        n_ref[r0:r0 + SLAB, :] = _rms(acc_ref[r0:r0 + SLAB, :], g_ple[...]).astype(BF16)

    def gate_piece(j):
        cs = slice(j * MXU_PIECE, (j + 1) * MXU_PIECE)
        gz_ref[:, cs] = _wdot(_fresh_rows(n_ref, 1 + N_FF_CHUNKS * n_up + j), w_pg[:, cs])

    def finish(r0):
        gate = _sigmoid(gz_ref[r0:r0 + SLAB, 0:D_MODEL])
        h3 = acc_ref[r0:r0 + SLAB, :] + pe_ref[r0:r0 + SLAB, :] * gate
        write_y(r0, _rms(h3, g_fin[...]))

    items += [functools.partial(norm_in, r0) for r0 in slabs]
    items += [functools.partial(up_piece, 0, j) for j in range(n_up)]
    for c in range(N_FF_CHUNKS + 1):
        pieces = []
        for j in range(max(n_up, n_down)):
            if c + 1 < N_FF_CHUNKS and j < n_up:
                pieces.append(functools.partial(up_piece, c + 1, j))
            if c >= 1 and j < n_down:
                pieces.append(functools.partial(down_piece, c - 1, j))
        if c < N_FF_CHUNKS:
            done = 0
            for i, r0 in enumerate(slabs):
                items.append(functools.partial(conv_act_slab, c, r0, i == len(slabs) - 1))
                upto = (i + 1) * len(pieces) // len(slabs)
                items += pieces[done:upto]
                done = upto
        else:
            items += pieces
    items += [functools.partial(norm_mid, r0) for r0 in slabs]
    items += [functools.partial(gate_piece, j) for j in range(n_down)]
    items += [functools.partial(finish, r0) for r0 in slabs]
    return embed, items


def _run(items):
    for item in items:
        item()


def _prompt_mixer_items(x_ref, h_mid_ref, n_ref, z_ref, xa_ref, xc_ref, xcb_ref, g_ref, a_ref, u_ref, hs_ref,
                        carry_ref, sp_ref, vn_ref, ymix_ref, g_mix, w_in, caw, cab, w_gate, ba, bx, apar,
                        g_oa, lng, lnb, sgw, sgb, g_ob, w_out, tm, reset_row, extra_matmul):
    assert tm % CHUNK == 0 and tm % SLAB == 0
    slabs = list(range(0, tm, SLAB))

    def norm_in(r0):
        n_ref[r0:r0 + SLAB, :] = _rms(x_ref[0, r0:r0 + SLAB, :], g_mix[...]).astype(BF16)

    def in_piece(j):
        c0 = j * MXU_PIECE
        res = _wdot(_fresh_rows(n_ref, 1 + j), w_in[:, c0:c0 + MXU_PIECE])
        if c0 < W_A:
            xa_ref[HIST:HIST + tm, c0:c0 + MXU_PIECE] = res
        else:
            z_ref[:, c0 - W_A:c0 - W_A + MXU_PIECE] = res

    def conv(r0):
        if r0 == 0:
            sp_ref[...] = _softplus(-apar[...])
        cv = _bc(cab[...], SLAB)
        for j in range(CONV_A):
            off = HIST + r0 - (CONV_A - 1 - j)
            cv = cv + xa_ref[off:off + SLAB, :] * _bc(caw[j], SLAB)
        xc_ref[r0:r0 + SLAB, :] = cv
        xcb_ref[r0:r0 + SLAB, :] = cv.astype(BF16)
        if r0 + SLAB == tm:
            xa_ref[0:HIST, :] = xa_ref[tm:tm + HIST, :]

    def gate_piece(hf):
        c0 = hf * GATE_HALF
        lhs = _fresh_rows(xcb_ref, 1 + hf, slice(c0, c0 + GATE_HALF))
        g_ref[:, 2 * c0:2 * c0 + 2 * GATE_HALF] = _wdot(lhs, w_gate[hf])

    def gate_ew(r0, hf):
        c0 = hf * GATE_HALF
        cs = slice(c0, c0 + GATE_HALF)
        r = _sigmoid(g_ref[r0:r0 + SLAB, 2 * c0:2 * c0 + GATE_HALF] + _bc(ba[:, cs], SLAB))
        i = _sigmoid(g_ref[r0:r0 + SLAB, 2 * c0 + GATE_HALF:2 * c0 + 2 * GATE_HALF] + _bc(bx[:, cs], SLAB))
        a = jnp.exp((-C_RG) * r * _bc(sp_ref[:, cs], SLAB))
        mult = jnp.sqrt(1.0 - a * a)
        if r0 == 0:
            row = lax.broadcasted_iota(jnp.int32, (SLAB, GATE_HALF), 0)
            mult = jnp.where(row == reset_row, 1.0, mult)
        a_ref[r0:r0 + SLAB, cs] = a
        u_ref[r0:r0 + SLAB, cs] = xc_ref[r0:r0 + SLAB, cs] * i * mult

    def scan_group(r0):
        row = lax.broadcasted_iota(jnp.int32, (SUBLANES, W_A), 0)
        a = a_ref[r0:r0 + SUBLANES, :]
        u = u_ref[r0:r0 + SUBLANES, :]
        for d in (1, 2, 4):
            keep = row >= d
            a_s = jnp.where(keep, pltpu.roll(a, d, 0), 1.0)
            u_s = jnp.where(keep, pltpu.roll(u, d, 0), 0.0)
            u = a * u_s + u
            a = a * a_s
        h = a * carry_ref[...] + u
        hs_ref[r0:r0 + SUBLANES, :] = h
        carry_ref[...] = jnp.broadcast_to(h[SUBLANES - 1:SUBLANES, :], (SUBLANES, W_A))

    def a_out(r0):
        ga = z_ref[r0:r0 + SLAB, 0:W_A]
        ymix_ref[r0:r0 + SLAB, 0:W_A] = _rms(hs_ref[r0:r0 + SLAB, :] * jax.nn.gelu(ga), g_oa[...]).astype(BF16)

    def vn(r0):
        vb = z_ref[r0:r0 + SLAB, W_A + W_B:W_A + 2 * W_B]
        vn_ref[r0:r0 + SLAB, :] = _layernorm(jax.nn.gelu(vb), lng[...], lnb[...]).astype(BF16)

    def sgu(c0):
        ti = lax.broadcasted_iota(jnp.int32, (CHUNK, CHUNK), 0)
        si = lax.broadcasted_iota(jnp.int32, (CHUNK, CHUNK), 1)
        heads = []
        for hd in range(H_B):
            cs = slice(hd * HD_B, (hd + 1) * HD_B)
            wmix = jnp.where(si <= ti, sgw[hd], 0.0).astype(BF16)
            mixed = _dot(wmix, vn_ref[c0:c0 + CHUNK, cs]) + sgb[:, cs]
            ub = z_ref[c0:c0 + CHUNK, W_A + hd * HD_B:W_A + (hd + 1) * HD_B]
            heads.append(jax.nn.gelu(ub) * mixed)
        yb = jnp.concatenate(heads, axis=-1)
        ymix_ref[c0:c0 + CHUNK, W_A:] = _rms(yb, g_ob[...]).astype(BF16)

    def out_piece(j):
        cs = slice(j * MXU_PIECE, (j + 1) * MXU_PIECE)
        h_mid_ref[:, cs] = x_ref[0, :, cs] + _wdot(_fresh_rows(ymix_ref, 1 + j), w_out[:, cs])

    def spread(work, matmuls):
        out, done = [], 0
        for i, item in enumerate(work):
            out.append(item)
            upto = (i + 1) * len(matmuls) // len(work)
            out += matmuls[done:upto]
            done = upto
        return out

    n_in = (2 * W_A + 2 * W_B) // MXU_PIECE
    n_a = W_A // MXU_PIECE
    later_in = [functools.partial(in_piece, j) for j in reversed(range(n_a, n_in))]
    items = [functools.partial(norm_in, r0) for r0 in slabs]
    items += [functools.partial(in_piece, j) for j in range(n_a)]
    items += [functools.partial(conv, r0) for r0 in slabs]
    items += [functools.partial(gate_piece, hf) for hf in range(2)] + [extra_matmul]
    items += spread([functools.partial(gate_ew, r0, hf) for r0 in slabs for hf in range(2)], later_in)
    items += [functools.partial(scan_group, r0) for r0 in range(0, tm, SUBLANES)]
    items += [functools.partial(a_out, r0) for r0 in slabs]
    items += [functools.partial(vn, r0) for r0 in slabs]
    items += [functools.partial(sgu, c0) for c0 in range(0, tm, CHUNK)]
    items += [functools.partial(out_piece, j) for j in range(D_MODEL // MXU_PIECE)]
    return items


def _prompt_kernel(x_ref, p_ref, vecs, w_in, w_gate, sgw, sgb, w_out, w_up, w_down, w_pg, w_ple,
                   y_ref, ht_ref, cnew_ref, fnew_ref,
                   h_mid_ref, n1_ref, z_ref, xa_ref, xc_ref, xcb_ref, g_ref, a_ref, u_ref, hs_ref, carry_ref,
                   sp_ref, vn_ref, ymix_ref, n2_ref, up_ref, upb_ref, fhist_ref, act_ref, actb_ref, acc_ref,
                   gz_ref, pe_ref, *, tm, layout):
    v = {name: _Vec(vecs, *spec) for name, spec in layout.items()}
    g_mix, caw, cab, ba, bx, apar = v['g_mix'], v['caw'], v['cab'], v['ba'], v['bx'], v['apar']
    g_oa, lng, lnb, g_ob = v['g_oa'], v['lng'], v['lnb'], v['g_ob']
    g_ffn, fcw, fcb, g_ple, g_fin = v['g_ffn'], v['fcw'], v['fcb'], v['g_ple'], v['g_fin']
    t = pl.program_id(1)

    @pl.when(t == 0)
    def _():
        xa_ref[0:HIST, :] = jnp.zeros((HIST, W_A), F32)
        carry_ref[...] = jnp.zeros((SUBLANES, W_A), F32)
        fhist_ref[...] = jnp.zeros(fhist_ref.shape, F32)

    def load_hist(c, up):
        for half in range(2):
            up[0:HIST, half * FF_CHUNK:(half + 1) * FF_CHUNK] = fhist_ref[:, _ff_cols(c, half)]

    def store_hist(c, up):
        for half in range(2):
            cs = slice(half * FF_CHUNK, (half + 1) * FF_CHUNK)
            fhist_ref[:, _ff_cols(c, half)] = up[tm:tm + HIST, cs]
            fnew_ref[0, :, _ff_cols(c, half)] = up[HIST + tm - (CONV_F - 1):HIST + tm, cs]

    def write_y(r0, y):
        y_ref[0, r0:r0 + SLAB, :] = y

    embed, ffn = _ffn_items(h_mid_ref, n2_ref, (up_ref, upb_ref), (act_ref, actb_ref), acc_ref, gz_ref, pe_ref,
                            lambda: p_ref[0], g_ffn, w_up, fcw, fcb, w_down, g_ple, w_pg, w_ple, g_fin, tm, 1,
                            HIST, load_hist, store_hist, write_y)
    reset_row = jnp.where(t == 0, 0, -1)
    mixer = _prompt_mixer_items(x_ref, h_mid_ref, n1_ref, z_ref, xa_ref, xc_ref, xcb_ref, g_ref, a_ref, u_ref,
                                hs_ref, carry_ref, sp_ref, vn_ref, ymix_ref, g_mix, w_in, caw, cab, w_gate,
                                ba, bx, apar, g_oa, lng, lnb, sgw, sgb, g_ob, w_out, tm, reset_row, embed)
    _run(mixer)
    _run(ffn)

    ht_ref[0] = carry_ref[0:1, :]
    cnew_ref[0] = xa_ref[HIST + tm - (CONV_A - 1):HIST + tm, :]


def _sample_kernel(x_ref, p_ref, h0_ref, chist_ref, fh_ref, vecs, w_in, w_gate, w_out, w_up, w_down,
                   w_pg, w_ple,
                   y_ref, ht_ref, cnew_ref, vn_out_ref, fnew_ref,
                   n_ref, z_ref, xa_ref, xc_ref, xcb_ref, g_ref, a_ref, u_ref, hs_ref, ymix_ref,
                   h_ref, p_buf, up_ref, upb_ref, act_ref, actb_ref, acc_ref, *, nb, steps, layout):
    v = {name: _Vec(vecs, *spec) for name, spec in layout.items()}
    g_mix, caw, cab, ba, bx, apar = v['g_mix'], v['caw'], v['cab'], v['ba'], v['bx'], v['apar']
    g_oa, lng, lnb, sgw, sgb, g_ob = v['g_oa'], v['lng'], v['lnb'], v['sgw'], v['sgb'], v['g_ob']
    g_ffn, fcw, fcb, g_ple, g_fin = v['g_ffn'], v['fcw'], v['fcb'], v['g_ple'], v['g_fin']
    rows = nb * steps
    ahist = (CONV_A - 1) * nb
    fhist = (CONV_F - 1) * nb

    for s in range(steps):
        h_ref[s * nb:(s + 1) * nb, :] = x_ref[s]
        p_buf[s * nb:(s + 1) * nb, :] = p_ref[s]
    _norm_to_bf16(h_ref, g_mix, n_ref, rows)
    for k in range(CONV_A - 1):
        xa_ref[k * nb:(k + 1) * nb, :] = chist_ref[k]
    xa_ref[ahist:ahist + rows, :] = _wdot(n_ref[...], w_in[:, 0:W_A])
    z_ref[...] = _wdot(n_ref[...], w_in[:, W_A:])

    for r0 in range(0, rows, SLAB):
        cv = _bc(cab[...], SLAB)
        for j in range(CONV_A):
            off = r0 + j * nb
            cv = cv + xa_ref[off:off + SLAB, :] * _bc(caw[j], SLAB)
        xc_ref[r0:r0 + SLAB, :] = cv
        xcb_ref[r0:r0 + SLAB, :] = cv.astype(BF16)
    for k in range(CONV_A - 1):
        cnew_ref[k] = xa_ref[rows + k * nb:rows + (k + 1) * nb, :]

    _gate_stage(xc_ref, xcb_ref, g_ref, a_ref, u_ref, w_gate, ba, bx, apar, rows)

    for b0 in range(0, nb, SLAB):
        h = h0_ref[b0:b0 + SLAB, :]
        for s in range(steps):
            r0 = s * nb + b0
            h = a_ref[r0:r0 + SLAB, :] * h + u_ref[r0:r0 + SLAB, :]
            hs_ref[r0:r0 + SLAB, :] = h
        ht_ref[b0:b0 + SLAB, :] = h

    _branch_a_out(hs_ref, z_ref, g_oa, ymix_ref, rows)

    def write_vn(r0, vn):
        vn_out_ref[r0 // nb, r0 % nb:r0 % nb + SLAB, :] = vn

    _vn_stage(z_ref, lng, lnb, rows, write_vn)

    gob = g_ob[...]
    for b0 in range(0, nb, SLAB):
        for tt in range(steps):
            r0 = tt * nb + b0
            mixed = _bc(sgb[tt], SLAB)
            for s in range(tt + 1):
                k = tt * steps + s
                mixed = mixed + vn_out_ref[s, b0:b0 + SLAB, :] * _bc(sgw[k], SLAB)
            yb = jax.nn.gelu(z_ref[r0:r0 + SLAB, W_A:W_A + W_B]) * mixed
            ymix_ref[r0:r0 + SLAB, W_A:] = _rms(yb, gob).astype(BF16)

    h_ref[...] += _wdot(ymix_ref[...], w_out[...])

    def load_hist(c, up):
        for k in range(CONV_F - 1):
            for half in range(2):
                cs = slice(half * FF_CHUNK, (half + 1) * FF_CHUNK)
                up[k * nb:(k + 1) * nb, cs] = fh_ref[k, :, _ff_cols(c, half)]

    def store_hist(c, up):
        for k in range(CONV_F - 1):
            for half in range(2):
                cs = slice(half * FF_CHUNK, (half + 1) * FF_CHUNK)
                fnew_ref[k, :, _ff_cols(c, half)] = up[rows + k * nb:rows + (k + 1) * nb, cs]

    def write_y(r0, y):
        y_ref[r0 // nb, r0 % nb:r0 % nb + SLAB, :] = y

    embed, items = _ffn_items(h_ref, n_ref, (up_ref, upb_ref), (act_ref, actb_ref), acc_ref, z_ref, g_ref,
                              lambda: p_buf[...], g_ffn, w_up, fcw, fcb, w_down, g_ple, w_pg, w_ple, g_fin,
                              rows, nb, fhist, load_hist, store_hist, write_y)
    _run([embed] + items)


def _const_spec(shape):
    zeros = (0,) * len(shape)
    return pl.BlockSpec(shape, lambda *_: zeros, pipeline_mode=pl.Buffered(1))


def _pack_bf16_rows(w):
    lead = w.shape[:-2]
    n = w.shape[-1]
    w2 = w.reshape((-1, n))
    k = w2.shape[0]
    bk = min(k, PACK_BLOCK_ROWS)
    bn = min(n, PACK_BLOCK_COLS)
    assert k % bk == 0 and n % bn == 0 and w.shape[-2] % 2 == 0
    words = pl.pallas_call(
        _pack_kernel,
        grid=(k // bk, n // bn),
        in_specs=[pl.BlockSpec((bk, bn), lambda i, j: (i, j))],
        out_specs=pl.BlockSpec((bk // 2, bn), lambda i, j: (i, j)),
        out_shape=jax.ShapeDtypeStruct((k // 2, n), jnp.uint32),
        name="pack_weight",
    )(w2)
    return words.reshape(lead + (w.shape[-2] // 2, n))


def _pack_kernel(w_ref, o_ref):
    o_ref[...] = pltpu.bitcast(w_ref[...].astype(BF16), jnp.uint32)


def _block_diag_gate(wa, wx):
    hh = H_A // 2
    eye = jnp.eye(hh, dtype=wa.dtype)

    def bd(w):
        return jnp.einsum('hij,hg->higj', w, eye).reshape(GATE_HALF, GATE_HALF)

    return jnp.stack([jnp.concatenate([bd(wa[h * hh:(h + 1) * hh]), bd(wx[h * hh:(h + 1) * hh])], axis=1)
                      for h in range(2)])


def kernel(x_prompt, x_sample, p_prompt, p_sample, state_rglru_h, state_rglru_conv, state_ffn_conv, g_mix_norm, w_in, conv_a_w, conv_a_b, lru_wa, lru_ba, lru_wx, lru_bx, lru_a_param, g_out_a, ln_v_g, ln_v_b, sgu_w, sgu_b, g_out_b, w_out, g_ffn_norm, w_up, ffn_conv_w, ffn_conv_b, w_down, g_ple_norm, w_ple_gate, w_ple, g_final):
    assert w_in.shape[0] == 1
    nbp, seq, _ = x_prompt.shape
    nbs, steps, _ = x_sample.shape
    ple = p_prompt.shape[-1]
    tm = PROMPT_BLOCK_ROWS
    nb = SAMPLE_GROUP
    assert seq % tm == 0 and tm % CHUNK == 0 and nbs % nb == 0 and nb % SLAB == 0 and steps <= CHUNK
    common = [('g_mix', g_mix_norm[0]), ('caw', conv_a_w[0]), ('cab', conv_a_b[0]), ('ba', lru_ba[0]),
              ('bx', lru_bx[0]), ('apar', lru_a_param[0]), ('g_oa', g_out_a[0]), ('lng', ln_v_g[0]),
              ('lnb', ln_v_b[0]), ('g_ob', g_out_b[0]), ('g_ffn', g_ffn_norm[0]), ('fcw', ffn_conv_w[0]),
              ('fcb', ffn_conv_b[0]), ('g_ple', g_ple_norm[0]), ('g_fin', g_final)]
    mats = [_pack_bf16_rows(w_in[0]), _pack_bf16_rows(_block_diag_gate(lru_wa[0], lru_wx[0]))]
    mats_tail = [_pack_bf16_rows(w_out[0]), _pack_bf16_rows(w_up[0]), _pack_bf16_rows(w_down[0]),
                 _pack_bf16_rows(w_ple_gate[0]), _pack_bf16_rows(w_ple[0])]

    sgb_p = jnp.repeat(jnp.transpose(sgu_b[0]), HD_B, axis=1)
    vecs_p, layout_p = _pack_vectors(common)
    p_args = [vecs_p] + mats + [sgu_w[0], sgb_p] + mats_tail
    scratch_p = [
        pltpu.VMEM((tm, D_MODEL), F32),
        pltpu.VMEM((tm, D_MODEL), BF16),
        pltpu.VMEM((tm, D_MODEL + W_A), F32),
        pltpu.VMEM((HIST + tm, W_A), F32),
        pltpu.VMEM((tm, W_A), F32),
        pltpu.VMEM((tm, W_A), BF16),
        pltpu.VMEM((tm, 2 * W_A), F32),
        pltpu.VMEM((tm, W_A), F32),
        pltpu.VMEM((tm, W_A), F32),
        pltpu.VMEM((tm, W_A), F32),
        pltpu.VMEM((SUBLANES, W_A), F32),
        pltpu.VMEM((SUBLANES, W_A), F32),
        pltpu.VMEM((tm, W_B), BF16),
        pltpu.VMEM((tm, W_A + W_B), BF16),
        pltpu.VMEM((tm, D_MODEL), BF16),
        pltpu.VMEM((HIST + tm, 2 * FF_CHUNK), F32),
        pltpu.VMEM((HIST + tm, 2 * FF_CHUNK), F32),
        pltpu.VMEM((HIST, 2 * D_FF), F32),
        pltpu.VMEM((tm, FF_CHUNK), BF16),
        pltpu.VMEM((tm, FF_CHUNK), BF16),
        pltpu.VMEM((tm, D_MODEL), F32),
        pltpu.VMEM((tm, D_MODEL), F32),
        pltpu.VMEM((tm, D_MODEL), F32),
    ]
    y_p, ht_p, cnew_p, fnew_p = pl.pallas_call(
        functools.partial(_prompt_kernel, tm=tm, layout=layout_p),
        grid=(nbp, seq // tm),
        in_specs=[pl.BlockSpec((1, tm, D_MODEL), lambda b, t: (b, t, 0)),
                  pl.BlockSpec((1, tm, ple), lambda b, t: (b, t, 0))]
                 + [_const_spec(a.shape) for a in p_args],
        out_specs=[pl.BlockSpec((1, tm, D_MODEL), lambda b, t: (b, t, 0)),
                   pl.BlockSpec((1, 1, W_A), lambda b, t: (b, 0, 0)),
                   pl.BlockSpec((1, CONV_A - 1, W_A), lambda b, t: (b, 0, 0)),
                   pl.BlockSpec((1, CONV_F - 1, 2 * D_FF), lambda b, t: (b, 0, 0))],
        out_shape=[jax.ShapeDtypeStruct((nbp, seq, D_MODEL), F32),
                   jax.ShapeDtypeStruct((nbp, 1, W_A), F32),
                   jax.ShapeDtypeStruct((nbp, CONV_A - 1, W_A), F32),
                   jax.ShapeDtypeStruct((nbp, CONV_F - 1, 2 * D_FF), F32)],
        scratch_shapes=scratch_p,
        compiler_params=pltpu.CompilerParams(dimension_semantics=("arbitrary", "arbitrary"),
                                             vmem_limit_bytes=VMEM_LIMIT_BYTES),
        name="prompt_layer",
    )(x_prompt, p_prompt[0], *p_args)

    rows = nb * steps
    ahist = (CONV_A - 1) * nb
    fhist = (CONV_F - 1) * nb
    tmaj = lambda a: jnp.swapaxes(a, 0, 1)
    sgw_s = jnp.repeat(jnp.transpose(sgu_w[0, :, :steps, :steps], (1, 2, 0)).reshape(steps * steps, H_B),
                       HD_B, axis=1)
    sgb_s = jnp.repeat(jnp.transpose(sgu_b[0, :, :steps]), HD_B, axis=1)
    vecs_s, layout_s = _pack_vectors(common + [('sgw', sgw_s), ('sgb', sgb_s)])
    w_args = [vecs_s] + mats + mats_tail
    scratch_s = [
        pltpu.VMEM((rows, D_MODEL), BF16),
        pltpu.VMEM((rows, D_MODEL + W_A), F32),
        pltpu.VMEM((ahist + rows, W_A), F32),
        pltpu.VMEM((rows, W_A), F32),
        pltpu.VMEM((rows, W_A), BF16),
        pltpu.VMEM((rows, 2 * W_A), F32),
        pltpu.VMEM((rows, W_A), F32),
        pltpu.VMEM((rows, W_A), F32),
        pltpu.VMEM((rows, W_A), F32),
        pltpu.VMEM((rows, W_A + W_B), BF16),
        pltpu.VMEM((rows, D_MODEL), F32),
        pltpu.VMEM((rows, ple), F32),
        pltpu.VMEM((fhist + rows, 2 * FF_CHUNK), F32),
        pltpu.VMEM((fhist + rows, 2 * FF_CHUNK), F32),
        pltpu.VMEM((rows, FF_CHUNK), BF16),
        pltpu.VMEM((rows, FF_CHUNK), BF16),
        pltpu.VMEM((rows, D_MODEL), F32),
    ]
    y_s, ht_s, cnew_s, vn_s, fnew_s = pl.pallas_call(
        functools.partial(_sample_kernel, nb=nb, steps=steps, layout=layout_s),
        grid=(nbs // nb,),
        in_specs=[pl.BlockSpec((steps, nb, D_MODEL), lambda i: (0, i, 0)),
                  pl.BlockSpec((steps, nb, ple), lambda i: (0, i, 0)),
                  pl.BlockSpec((nb, W_A), lambda i: (i, 0)),
                  pl.BlockSpec((CONV_A - 1, nb, W_A), lambda i: (0, i, 0)),
                  pl.BlockSpec((CONV_F - 1, nb, 2 * D_FF), lambda i: (0, i, 0))]
                 + [_const_spec(a.shape) for a in w_args],
        out_specs=[pl.BlockSpec((steps, nb, D_MODEL), lambda i: (0, i, 0)),
                   pl.BlockSpec((nb, W_A), lambda i: (i, 0)),
                   pl.BlockSpec((CONV_A - 1, nb, W_A), lambda i: (0, i, 0)),
                   pl.BlockSpec((steps, nb, W_B), lambda i: (0, i, 0)),
                   pl.BlockSpec((CONV_F - 1, nb, 2 * D_FF), lambda i: (0, i, 0))],
        out_shape=[jax.ShapeDtypeStruct((steps, nbs, D_MODEL), F32),
                   jax.ShapeDtypeStruct((nbs, W_A), F32),
                   jax.ShapeDtypeStruct((CONV_A - 1, nbs, W_A), F32),
                   jax.ShapeDtypeStruct((steps, nbs, W_B), F32),
                   jax.ShapeDtypeStruct((CONV_F - 1, nbs, 2 * D_FF), F32)],
        scratch_shapes=scratch_s,
        compiler_params=pltpu.CompilerParams(dimension_semantics=("arbitrary",),
                                             vmem_limit_bytes=VMEM_LIMIT_BYTES),
        name="sample_layer",
    )(tmaj(x_sample), tmaj(p_sample[0]), state_rglru_h[0], tmaj(state_rglru_conv[0]),
      tmaj(state_ffn_conv[0]), *w_args)

    return (y_p, tmaj(y_s), tmaj(ht_p), ht_s[None], cnew_p[None], tmaj(cnew_s)[None],
            tmaj(vn_s)[None], fnew_p[None], tmaj(fnew_s)[None])
```

```python
import functools

import jax
import jax.numpy as jnp
from jax import lax
from jax.experimental import pallas as pl
from jax.experimental.pallas import tpu as pltpu

F32 = jnp.float32
BF16 = jnp.bfloat16

D_MODEL = 1024
W_A = 512
W_B = 512
H_A = 8
BW_A = W_A // H_A
H_B = 4
HD_B = W_B // H_B
CHUNK = 128
D_FF = 3072
CONV_A = 4
CONV_F = 3
C_RG = 8.0
EPS = 1e-6

SUBLANES = 8
HIST = SUBLANES
GATE_HALF = W_A // 2
FF_CHUNK = 512
N_FF_CHUNKS = D_FF // FF_CHUNK
SLAB = 32
MXU_PIECE = 256
PACK_BLOCK_ROWS = 512
PACK_BLOCK_COLS = 2048
PROMPT_BLOCK_ROWS = 256
SAMPLE_GROUP = 32
VMEM_LIMIT_BYTES = 56 * 1024 * 1024


def _bc(tile, rows):
    return jnp.concatenate([tile] * (rows // SUBLANES), axis=0)


class _Vec:
    def __init__(self, ref, off, width, n):
        self.ref, self.off, self.width, self.n = ref, off, width, n

    def __getitem__(self, idx):
        full = slice(0, self.width)
        if self.n is None:
            j, cols = 0, (full if idx is Ellipsis else idx[1])
        else:
            j, cols = (idx, full) if isinstance(idx, int) else (idx[0], idx[2])
        start = self.off + j * self.width
        return self.ref[:, start + cols.start:start + cols.stop]


def _pack_vectors(named):
    layout, flat, off = {}, [], 0
    for name, a in named:
        n, width = (None, a.shape[0]) if a.ndim == 1 else a.shape
        layout[name] = (off, width, n)
        flat.append(a.reshape(-1))
        off += a.size
    block = jnp.broadcast_to(jnp.concatenate(flat)[None, :], (SUBLANES, off))
    return block, layout


def _rms(x, g):
    ms = jnp.mean(x * x, axis=-1, keepdims=True)
    return x * lax.rsqrt(ms + EPS) * _bc(g, x.shape[0])


def _layernorm(x, g, b):
    mu = jnp.mean(x, axis=-1, keepdims=True)
    xc = x - mu
    rows = x.shape[0]
    return xc * lax.rsqrt(jnp.mean(xc * xc, axis=-1, keepdims=True) + EPS) * _bc(g, rows) + _bc(b, rows)


def _sigmoid(x):
    return 1.0 / (1.0 + jnp.exp(-x))


def _softplus(x):
    return jnp.maximum(x, 0.0) + jnp.log(1.0 + jnp.exp(-jnp.abs(x)))


def _dot(a, b):
    return jnp.dot(a, b, preferred_element_type=F32)


def _fresh_rows(ref, salt, cols=slice(None)):
    zero = jnp.minimum(pl.program_id(0), 0) * salt
    return ref[pl.ds(pl.multiple_of(zero, 16), ref.shape[0]), cols]


def _wdot(a, w_words):
    return _dot(a, pltpu.bitcast(w_words, BF16))


def _ff_cols(c, half):
    start = half * D_FF + c * FF_CHUNK
    return slice(start, start + FF_CHUNK)


def _norm_to_bf16(src_ref, g_ref, dst_ref, rows):
    g = g_ref[...]
    for r0 in range(0, rows, SLAB):
        dst_ref[r0:r0 + SLAB, :] = _rms(src_ref[r0:r0 + SLAB, :], g).astype(BF16)


def _gate_stage(xc_ref, xcb_ref, g_ref, a_ref, u_ref, w_gate, ba, bx, apar, rows):
    for hf in range(2):
        c0 = hf * GATE_HALF
        g_ref[:, 2 * c0:2 * c0 + 2 * GATE_HALF] = _wdot(xcb_ref[:, c0:c0 + GATE_HALF], w_gate[hf])
    sp = _softplus(-apar[...])
    for r0 in range(0, rows, SLAB):
        for hf in range(2):
            c0 = hf * GATE_HALF
            cs = slice(c0, c0 + GATE_HALF)
            r = _sigmoid(g_ref[r0:r0 + SLAB, 2 * c0:2 * c0 + GATE_HALF] + _bc(ba[:, cs], SLAB))
            i = _sigmoid(g_ref[r0:r0 + SLAB, 2 * c0 + GATE_HALF:2 * c0 + 2 * GATE_HALF] + _bc(bx[:, cs], SLAB))
            a = jnp.exp((-C_RG) * r * _bc(sp[:, cs], SLAB))
            mult = jnp.sqrt(1.0 - a * a)
            a_ref[r0:r0 + SLAB, cs] = a
            u_ref[r0:r0 + SLAB, cs] = xc_ref[r0:r0 + SLAB, cs] * i * mult


def _branch_a_out(hs_ref, z_ref, g_oa, ymix_ref, rows):
    g = g_oa[...]
    for r0 in range(0, rows, SLAB):
        ga = z_ref[r0:r0 + SLAB, 0:W_A]
        ymix_ref[r0:r0 + SLAB, 0:W_A] = _rms(hs_ref[r0:r0 + SLAB, :] * jax.nn.gelu(ga), g).astype(BF16)


def _vn_stage(z_ref, lng, lnb, rows, write):
    g = lng[...]
    b = lnb[...]
    for r0 in range(0, rows, SLAB):
        vb = z_ref[r0:r0 + SLAB, W_A + W_B:W_A + 2 * W_B]
        write(r0, _layernorm(jax.nn.gelu(vb), g, b))


def _ffn_items(h_ref, n_ref, up2_ref, act2_ref, acc_ref, gz_ref, pe_ref, load_p, g_ffn, w_up, fcw, fcb,
               w_down, g_ple, w_pg, w_ple, g_fin, rows, shift, hist_rows, load_hist, store_hist, write_y):
    n_up = 2 * FF_CHUNK // MXU_PIECE
    n_down = D_MODEL // MXU_PIECE
    slabs = list(range(0, rows, SLAB))
    items = []

    def norm_in(r0):
        h = h_ref[r0:r0 + SLAB, :]
        acc_ref[r0:r0 + SLAB, :] = h
        n_ref[r0:r0 + SLAB, :] = _rms(h, g_ffn[...]).astype(BF16)

    def up_piece(c, j):
        up_ref = up2_ref[c % 2]
        cs = slice(j * MXU_PIECE, (j + 1) * MXU_PIECE)
        half, off = divmod(j * MXU_PIECE, FF_CHUNK)
        w0 = half * D_FF + c * FF_CHUNK + off
        if j == 0:
            load_hist(c, up_ref)
        lhs = _fresh_rows(n_ref, 1 + c * n_up + j)
        up_ref[hist_rows:hist_rows + rows, cs] = _wdot(lhs, w_up[:, w0:w0 + MXU_PIECE])

    def down_piece(c, j):
        cs = slice(j * MXU_PIECE, (j + 1) * MXU_PIECE)
        lhs = _fresh_rows(act2_ref[c % 2], 1 + c * n_down + j)
        acc_ref[:, cs] += _wdot(lhs, w_down[c * FF_CHUNK // 2:(c + 1) * FF_CHUNK // 2, cs])

    def conv_act_slab(c, r0, last):
        up_ref = up2_ref[c % 2]
        halves = []
        for half in range(2):
            cs = slice(half * FF_CHUNK, (half + 1) * FF_CHUNK)
            ws = _ff_cols(c, half)
            cv = _bc(fcb[:, ws], SLAB)
            for j in range(CONV_F):
                off = hist_rows + r0 - (CONV_F - 1 - j) * shift
                cv = cv + up_ref[off:off + SLAB, cs] * _bc(fcw[j, :, ws], SLAB)
            halves.append(cv)
        act2_ref[c % 2][r0:r0 + SLAB, :] = (jax.nn.gelu(halves[0]) * halves[1]).astype(BF16)
        if last:
            store_hist(c, up_ref)

    def embed():
        pe_ref[...] = _wdot(load_p().astype(BF16), w_ple[...])

    def norm_mid(r0):
        n_ref[r0:r0 + SLAB, :] = _rms(acc_ref[r0:r0 + SLAB, :], g_ple[...]).astype(BF16)

    def gate_piece(j):
        cs = slice(j * MXU_PIECE, (j + 1) * MXU_PIECE)
        gz_ref[:, cs] = _wdot(_fresh_rows(n_ref, 1 + N_FF_CHUNKS * n_up + j), w_pg[:, cs])

    def finish(r0):
        gate = _sigmoid(gz_ref[r0:r0 + SLAB, 0:D_MODEL])
        h3 = acc_ref[r0:r0 + SLAB, :] + pe_ref[r0:r0 + SLAB, :] * gate
        write_y(r0, _rms(h3, g_fin[...]))

    items += [functools.partial(norm_in, r0) for r0 in slabs]
    items += [functools.partial(up_piece, 0, j) for j in range(n_up)]
    for c in range(N_FF_CHUNKS + 1):
        pieces = []
        for j in range(max(n_up, n_down)):
            if c + 1 < N_FF_CHUNKS and j < n_up:
                pieces.append(functools.partial(up_piece, c + 1, j))
            if c >= 1 and j < n_down:
                pieces.append(functools.partial(down_piece, c - 1, j))
        if c < N_FF_CHUNKS:
            done = 0
            for i, r0 in enumerate(slabs):
                items.append(functools.partial(conv_act_slab, c, r0, i == len(slabs) - 1))
                upto = (i + 1) * len(pieces) // len(slabs)
                items += pieces[done:upto]
                done = upto
        else:
            items += pieces
    items += [functools.partial(norm_mid, r0) for r0 in slabs]
    items += [functools.partial(gate_piece, j) for j in range(n_down)]
    items += [functools.partial(finish, r0) for r0 in slabs]
    return embed, items


def _run(items):
    for item in items:
        item()


def _prompt_mixer_items(x_ref, h_mid_ref, n_ref, z_ref, xa_ref, xc_ref, xcb_ref, g_ref, a_ref, u_ref, hs_ref,
                        carry_ref, sp_ref, vn_ref, ymix_ref, g_mix, w_in, caw, cab, w_gate, ba, bx, apar,
                        g_oa, lng, lnb, sgw, sgb, g_ob, w_out, tm, reset_row, extra_matmul):
    assert tm % CHUNK == 0 and tm % SLAB == 0
    slabs = list(range(0, tm, SLAB))

    def norm_in(r0):
        n_ref[r0:r0 + SLAB, :] = _rms(x_ref[0, r0:r0 + SLAB, :], g_mix[...]).astype(BF16)

    def in_piece(j):
        c0 = j * MXU_PIECE
        res = _wdot(_fresh_rows(n_ref, 1 + j), w_in[:, c0:c0 + MXU_PIECE])
        if c0 < W_A:
            xa_ref[HIST:HIST + tm, c0:c0 + MXU_PIECE] = res
        else:
            z_ref[:, c0 - W_A:c0 - W_A + MXU_PIECE] = res

    def conv(r0):
        if r0 == 0:
            sp_ref[...] = _softplus(-apar[...])
        cv = _bc(cab[...], SLAB)
        for j in range(CONV_A):
            off = HIST + r0 - (CONV_A - 1 - j)
            cv = cv + xa_ref[off:off + SLAB, :] * _bc(caw[j], SLAB)
        xc_ref[r0:r0 + SLAB, :] = cv
        xcb_ref[r0:r0 + SLAB, :] = cv.astype(BF16)
        if r0 + SLAB == tm:
            xa_ref[0:HIST, :] = xa_ref[tm:tm + HIST, :]

    def gate_piece(hf):
        c0 = hf * GATE_HALF
        lhs = _fresh_rows(xcb_ref, 1 + hf, slice(c0, c0 + GATE_HALF))
        g_ref[:, 2 * c0:2 * c0 + 2 * GATE_HALF] = _wdot(lhs, w_gate[hf])

    def gate_ew(r0, hf):
        c0 = hf * GATE_HALF
        cs = slice(c0, c0 + GATE_HALF)
        r = _sigmoid(g_ref[r0:r0 + SLAB, 2 * c0:2 * c0 + GATE_HALF] + _bc(ba[:, cs], SLAB))
        i = _sigmoid(g_ref[r0:r0 + SLAB, 2 * c0 + GATE_HALF:2 * c0 + 2 * GATE_HALF] + _bc(bx[:, cs], SLAB))
        a = jnp.exp((-C_RG) * r * _bc(sp_ref[:, cs], SLAB))
        mult = jnp.sqrt(1.0 - a * a)
        if r0 == 0:
            row = lax.broadcasted_iota(jnp.int32, (SLAB, GATE_HALF), 0)
            mult = jnp.where(row == reset_row, 1.0, mult)
        a_ref[r0:r0 + SLAB, cs] = a
        u_ref[r0:r0 + SLAB, cs] = xc_ref[r0:r0 + SLAB, cs] * i * mult

    def scan_group(r0):
        row = lax.broadcasted_iota(jnp.int32, (SUBLANES, W_A), 0)
        a = a_ref[r0:r0 + SUBLANES, :]
        u = u_ref[r0:r0 + SUBLANES, :]
        for d in (1, 2, 4):
            keep = row >= d
            a_s = jnp.where(keep, pltpu.roll(a, d, 0), 1.0)
            u_s = jnp.where(keep, pltpu.roll(u, d, 0), 0.0)
            u = a * u_s + u
            a = a * a_s
        h = a * carry_ref[...] + u
        hs_ref[r0:r0 + SUBLANES, :] = h
        carry_ref[...] = jnp.broadcast_to(h[SUBLANES - 1:SUBLANES, :], (SUBLANES, W_A))

    def a_out(r0):
        ga = z_ref[r0:r0 + SLAB, 0:W_A]
        ymix_ref[r0:r0 + SLAB, 0:W_A] = _rms(hs_ref[r0:r0 + SLAB, :] * jax.nn.gelu(ga), g_oa[...]).astype(BF16)

    def vn(r0):
        vb = z_ref[r0:r0 + SLAB, W_A + W_B:W_A + 2 * W_B]
        vn_ref[r0:r0 + SLAB, :] = _layernorm(jax.nn.gelu(vb), lng[...], lnb[...]).astype(BF16)

    def sgu(c0):
        ti = lax.broadcasted_iota(jnp.int32, (CHUNK, CHUNK), 0)
        si = lax.broadcasted_iota(jnp.int32, (CHUNK, CHUNK), 1)
        heads = []
        for hd in range(H_B):
            cs = slice(hd * HD_B, (hd + 1) * HD_B)
            wmix = jnp.where(si <= ti, sgw[hd], 0.0).astype(BF16)
            mixed = _dot(wmix, vn_ref[c0:c0 + CHUNK, cs]) + sgb[:, cs]
            ub = z_ref[c0:c0 + CHUNK, W_A + hd * HD_B:W_A + (hd + 1) * HD_B]
            heads.append(jax.nn.gelu(ub) * mixed)
        yb = jnp.concatenate(heads, axis=-1)
        ymix_ref[c0:c0 + CHUNK, W_A:] = _rms(yb, g_ob[...]).astype(BF16)

    def out_piece(j):
        cs = slice(j * MXU_PIECE, (j + 1) * MXU_PIECE)
        h_mid_ref[:, cs] = x_ref[0, :, cs] + _wdot(_fresh_rows(ymix_ref, 1 + j), w_out[:, cs])

    def spread(work, matmuls):
        out, done = [], 0
        for i, item in enumerate(work):
            out.append(item)
            upto = (i + 1) * len(matmuls) // len(work)
            out += matmuls[done:upto]
            done = upto
        return out

    n_in = (2 * W_A + 2 * W_B) // MXU_PIECE
    n_a = W_A // MXU_PIECE
    later_in = [functools.partial(in_piece, j) for j in reversed(range(n_a, n_in))]
    items = [functools.partial(norm_in, r0) for r0 in slabs]
    items += [functools.partial(in_piece, j) for j in range(n_a)]
    items += spread([functools.partial(conv, r0) for r0 in slabs], later_in[:2])
    items += [functools.partial(gate_piece, hf) for hf in range(2)] + [extra_matmul]
    items += spread([functools.partial(gate_ew, r0, hf) for r0 in slabs for hf in range(2)], later_in[2:])
    items += [functools.partial(scan_group, r0) for r0 in range(0, tm, SUBLANES)]
    items += [functools.partial(a_out, r0) for r0 in slabs]
    items += [functools.partial(vn, r0) for r0 in slabs]
    items += [functools.partial(sgu, c0) for c0 in range(0, tm, CHUNK)]
    items += [functools.partial(out_piece, j) for j in range(D_MODEL // MXU_PIECE)]
    return items


def _prompt_kernel(x_ref, p_ref, vecs, w_in, w_gate, sgw, sgb, w_out, w_up, w_down, w_pg, w_ple,
                   y_ref, ht_ref, cnew_ref, fnew_ref,
                   h_mid_ref, n1_ref, z_ref, xa_ref, xc_ref, xcb_ref, g_ref, a_ref, u_ref, hs_ref, carry_ref,
                   sp_ref, vn_ref, ymix_ref, n2_ref, up_ref, upb_ref, fhist_ref, act_ref, actb_ref, acc_ref,
                   gz_ref, pe_ref, *, tm, layout):
    v = {name: _Vec(vecs, *spec) for name, spec in layout.items()}
    g_mix, caw, cab, ba, bx, apar = v['g_mix'], v['caw'], v['cab'], v['ba'], v['bx'], v['apar']
    g_oa, lng, lnb, g_ob = v['g_oa'], v['lng'], v['lnb'], v['g_ob']
    g_ffn, fcw, fcb, g_ple, g_fin = v['g_ffn'], v['fcw'], v['fcb'], v['g_ple'], v['g_fin']
    t = pl.program_id(1)

    @pl.when(t == 0)
    def _():
        xa_ref[0:HIST, :] = jnp.zeros((HIST, W_A), F32)
        carry_ref[...] = jnp.zeros((SUBLANES, W_A), F32)
        fhist_ref[...] = jnp.zeros(fhist_ref.shape, F32)

    def load_hist(c, up):
        for half in range(2):
            up[0:HIST, half * FF_CHUNK:(half + 1) * FF_CHUNK] = fhist_ref[:, _ff_cols(c, half)]

    def store_hist(c, up):
        for half in range(2):
            cs = slice(half * FF_CHUNK, (half + 1) * FF_CHUNK)
            fhist_ref[:, _ff_cols(c, half)] = up[tm:tm + HIST, cs]
            fnew_ref[0, :, _ff_cols(c, half)] = up[HIST + tm - (CONV_F - 1):HIST + tm, cs]

    def write_y(r0, y):
        y_ref[0, r0:r0 + SLAB, :] = y

    embed, ffn = _ffn_items(h_mid_ref, n2_ref, (up_ref, upb_ref), (act_ref, actb_ref), acc_ref, gz_ref, pe_ref,
                            lambda: p_ref[0], g_ffn, w_up, fcw, fcb, w_down, g_ple, w_pg, w_ple, g_fin, tm, 1,
                            HIST, load_hist, store_hist, write_y)
    reset_row = jnp.where(t == 0, 0, -1)
    mixer = _prompt_mixer_items(x_ref, h_mid_ref, n1_ref, z_ref, xa_ref, xc_ref, xcb_ref, g_ref, a_ref, u_ref,
                                hs_ref, carry_ref, sp_ref, vn_ref, ymix_ref, g_mix, w_in, caw, cab, w_gate,
                                ba, bx, apar, g_oa, lng, lnb, sgw, sgb, g_ob, w_out, tm, reset_row, embed)
    _run(mixer)
    _run(ffn)

    ht_ref[0] = carry_ref[0:1, :]
    cnew_ref[0] = xa_ref[HIST + tm - (CONV_A - 1):HIST + tm, :]


def _sample_kernel(x_ref, p_ref, h0_ref, chist_ref, fh_ref, vecs, w_in, w_gate, w_out, w_up, w_down,
                   w_pg, w_ple,
                   y_ref, ht_ref, cnew_ref, vn_out_ref, fnew_ref,
                   n_ref, z_ref, xa_ref, xc_ref, xcb_ref, g_ref, a_ref, u_ref, hs_ref, ymix_ref,
                   h_ref, p_buf, up_ref, upb_ref, act_ref, actb_ref, acc_ref, *, nb, steps, layout):
    v = {name: _Vec(vecs, *spec) for name, spec in layout.items()}
    g_mix, caw, cab, ba, bx, apar = v['g_mix'], v['caw'], v['cab'], v['ba'], v['bx'], v['apar']
    g_oa, lng, lnb, sgw, sgb, g_ob = v['g_oa'], v['lng'], v['lnb'], v['sgw'], v['sgb'], v['g_ob']
    g_ffn, fcw, fcb, g_ple, g_fin = v['g_ffn'], v['fcw'], v['fcb'], v['g_ple'], v['g_fin']
    rows = nb * steps
    ahist = (CONV_A - 1) * nb
    fhist = (CONV_F - 1) * nb

    for s in range(steps):
        h_ref[s * nb:(s + 1) * nb, :] = x_ref[s]
        p_buf[s * nb:(s + 1) * nb, :] = p_ref[s]
    _norm_to_bf16(h_ref, g_mix, n_ref, rows)
    for k in range(CONV_A - 1):
        xa_ref[k * nb:(k + 1) * nb, :] = chist_ref[k]
    xa_ref[ahist:ahist + rows, :] = _wdot(n_ref[...], w_in[:, 0:W_A])
    z_ref[...] = _wdot(n_ref[...], w_in[:, W_A:])

    for r0 in range(0, rows, SLAB):
        cv = _bc(cab[...], SLAB)
        for j in range(CONV_A):
            off = r0 + j * nb
            cv = cv + xa_ref[off:off + SLAB, :] * _bc(caw[j], SLAB)
        xc_ref[r0:r0 + SLAB, :] = cv
        xcb_ref[r0:r0 + SLAB, :] = cv.astype(BF16)
    for k in range(CONV_A - 1):
        cnew_ref[k] = xa_ref[rows + k * nb:rows + (k + 1) * nb, :]

    _gate_stage(xc_ref, xcb_ref, g_ref, a_ref, u_ref, w_gate, ba, bx, apar, rows)

    for b0 in range(0, nb, SLAB):
        h = h0_ref[b0:b0 + SLAB, :]
        for s in range(steps):
            r0 = s * nb + b0
            h = a_ref[r0:r0 + SLAB, :] * h + u_ref[r0:r0 + SLAB, :]
            hs_ref[r0:r0 + SLAB, :] = h
        ht_ref[b0:b0 + SLAB, :] = h

    _branch_a_out(hs_ref, z_ref, g_oa, ymix_ref, rows)

    def write_vn(r0, vn):
        vn_out_ref[r0 // nb, r0 % nb:r0 % nb + SLAB, :] = vn

    _vn_stage(z_ref, lng, lnb, rows, write_vn)

    gob = g_ob[...]
    for b0 in range(0, nb, SLAB):
        for tt in range(steps):
            r0 = tt * nb + b0
            mixed = _bc(sgb[tt], SLAB)
            for s in range(tt + 1):
                k = tt * steps + s
                mixed = mixed + vn_out_ref[s, b0:b0 + SLAB, :] * _bc(sgw[k], SLAB)
            yb = jax.nn.gelu(z_ref[r0:r0 + SLAB, W_A:W_A + W_B]) * mixed
            ymix_ref[r0:r0 + SLAB, W_A:] = _rms(yb, gob).astype(BF16)

    h_ref[...] += _wdot(ymix_ref[...], w_out[...])

    def load_hist(c, up):
        for k in range(CONV_F - 1):
            for half in range(2):
                cs = slice(half * FF_CHUNK, (half + 1) * FF_CHUNK)
                up[k * nb:(k + 1) * nb, cs] = fh_ref[k, :, _ff_cols(c, half)]

    def store_hist(c, up):
        for k in range(CONV_F - 1):
            for half in range(2):
                cs = slice(half * FF_CHUNK, (half + 1) * FF_CHUNK)
                fnew_ref[k, :, _ff_cols(c, half)] = up[rows + k * nb:rows + (k + 1) * nb, cs]

    def write_y(r0, y):
        y_ref[r0 // nb, r0 % nb:r0 % nb + SLAB, :] = y

    embed, items = _ffn_items(h_ref, n_ref, (up_ref, upb_ref), (act_ref, actb_ref), acc_ref, z_ref, g_ref,
                              lambda: p_buf[...], g_ffn, w_up, fcw, fcb, w_down, g_ple, w_pg, w_ple, g_fin,
                              rows, nb, fhist, load_hist, store_hist, write_y)
    _run([embed] + items)


def _const_spec(shape):
    zeros = (0,) * len(shape)
    return pl.BlockSpec(shape, lambda *_: zeros, pipeline_mode=pl.Buffered(1))


def _pack_bf16_rows(w):
    lead = w.shape[:-2]
    n = w.shape[-1]
    w2 = w.reshape((-1, n))
    k = w2.shape[0]
    bk = min(k, PACK_BLOCK_ROWS)
    bn = min(n, PACK_BLOCK_COLS)
    assert k % bk == 0 and n % bn == 0 and w.shape[-2] % 2 == 0
    words = pl.pallas_call(
        _pack_kernel,
        grid=(k // bk, n // bn),
        in_specs=[pl.BlockSpec((bk, bn), lambda i, j: (i, j))],
        out_specs=pl.BlockSpec((bk // 2, bn), lambda i, j: (i, j)),
        out_shape=jax.ShapeDtypeStruct((k // 2, n), jnp.uint32),
        name="pack_weight",
    )(w2)
    return words.reshape(lead + (w.shape[-2] // 2, n))


def _pack_kernel(w_ref, o_ref):
    o_ref[...] = pltpu.bitcast(w_ref[...].astype(BF16), jnp.uint32)


def _block_diag_gate(wa, wx):
    hh = H_A // 2
    eye = jnp.eye(hh, dtype=wa.dtype)

    def bd(w):
        return jnp.einsum('hij,hg->higj', w, eye).reshape(GATE_HALF, GATE_HALF)

    return jnp.stack([jnp.concatenate([bd(wa[h * hh:(h + 1) * hh]), bd(wx[h * hh:(h + 1) * hh])], axis=1)
                      for h in range(2)])


def kernel(x_prompt, x_sample, p_prompt, p_sample, state_rglru_h, state_rglru_conv, state_ffn_conv, g_mix_norm, w_in, conv_a_w, conv_a_b, lru_wa, lru_ba, lru_wx, lru_bx, lru_a_param, g_out_a, ln_v_g, ln_v_b, sgu_w, sgu_b, g_out_b, w_out, g_ffn_norm, w_up, ffn_conv_w, ffn_conv_b, w_down, g_ple_norm, w_ple_gate, w_ple, g_final):
    assert w_in.shape[0] == 1
    nbp, seq, _ = x_prompt.shape
    nbs, steps, _ = x_sample.shape
    ple = p_prompt.shape[-1]
    tm = PROMPT_BLOCK_ROWS
    nb = SAMPLE_GROUP
    assert seq % tm == 0 and tm % CHUNK == 0 and nbs % nb == 0 and nb % SLAB == 0 and steps <= CHUNK
    common = [('g_mix', g_mix_norm[0]), ('caw', conv_a_w[0]), ('cab', conv_a_b[0]), ('ba', lru_ba[0]),
              ('bx', lru_bx[0]), ('apar', lru_a_param[0]), ('g_oa', g_out_a[0]), ('lng', ln_v_g[0]),
              ('lnb', ln_v_b[0]), ('g_ob', g_out_b[0]), ('g_ffn', g_ffn_norm[0]), ('fcw', ffn_conv_w[0]),
              ('fcb', ffn_conv_b[0]), ('g_ple', g_ple_norm[0]), ('g_fin', g_final)]
    mats = [_pack_bf16_rows(w_in[0]), _pack_bf16_rows(_block_diag_gate(lru_wa[0], lru_wx[0]))]
    mats_tail = [_pack_bf16_rows(w_out[0]), _pack_bf16_rows(w_up[0]), _pack_bf16_rows(w_down[0]),
                 _pack_bf16_rows(w_ple_gate[0]), _pack_bf16_rows(w_ple[0])]

    sgb_p = jnp.repeat(jnp.transpose(sgu_b[0]), HD_B, axis=1)
    vecs_p, layout_p = _pack_vectors(common)
    p_args = [vecs_p] + mats + [sgu_w[0], sgb_p] + mats_tail
    scratch_p = [
        pltpu.VMEM((tm, D_MODEL), F32),
        pltpu.VMEM((tm, D_MODEL), BF16),
        pltpu.VMEM((tm, D_MODEL + W_A), F32),
        pltpu.VMEM((HIST + tm, W_A), F32),
        pltpu.VMEM((tm, W_A), F32),
        pltpu.VMEM((tm, W_A), BF16),
        pltpu.VMEM((tm, 2 * W_A), F32),
        pltpu.VMEM((tm, W_A), F32),
        pltpu.VMEM((tm, W_A), F32),
        pltpu.VMEM((tm, W_A), F32),
        pltpu.VMEM((SUBLANES, W_A), F32),
        pltpu.VMEM((SUBLANES, W_A), F32),
        pltpu.VMEM((tm, W_B), BF16),
        pltpu.VMEM((tm, W_A + W_B), BF16),
        pltpu.VMEM((tm, D_MODEL), BF16),
        pltpu.VMEM((HIST + tm, 2 * FF_CHUNK), F32),
        pltpu.VMEM((HIST + tm, 2 * FF_CHUNK), F32),
        pltpu.VMEM((HIST, 2 * D_FF), F32),
        pltpu.VMEM((tm, FF_CHUNK), BF16),
        pltpu.VMEM((tm, FF_CHUNK), BF16),
        pltpu.VMEM((tm, D_MODEL), F32),
        pltpu.VMEM((tm, D_MODEL), F32),
        pltpu.VMEM((tm, D_MODEL), F32),
    ]
    y_p, ht_p, cnew_p, fnew_p = pl.pallas_call(
        functools.partial(_prompt_kernel, tm=tm, layout=layout_p),
        grid=(nbp, seq // tm),
        in_specs=[pl.BlockSpec((1, tm, D_MODEL), lambda b, t: (b, t, 0)),
                  pl.BlockSpec((1, tm, ple), lambda b, t: (b, t, 0))]
                 + [_const_spec(a.shape) for a in p_args],
        out_specs=[pl.BlockSpec((1, tm, D_MODEL), lambda b, t: (b, t, 0)),
                   pl.BlockSpec((1, 1, W_A), lambda b, t: (b, 0, 0)),
                   pl.BlockSpec((1, CONV_A - 1, W_A), lambda b, t: (b, 0, 0)),
                   pl.BlockSpec((1, CONV_F - 1, 2 * D_FF), lambda b, t: (b, 0, 0))],
        out_shape=[jax.ShapeDtypeStruct((nbp, seq, D_MODEL), F32),
                   jax.ShapeDtypeStruct((nbp, 1, W_A), F32),
                   jax.ShapeDtypeStruct((nbp, CONV_A - 1, W_A), F32),
                   jax.ShapeDtypeStruct((nbp, CONV_F - 1, 2 * D_FF), F32)],
        scratch_shapes=scratch_p,
        compiler_params=pltpu.CompilerParams(dimension_semantics=("arbitrary", "arbitrary"),
                                             vmem_limit_bytes=VMEM_LIMIT_BYTES),
        name="prompt_layer",
    )(x_prompt, p_prompt[0], *p_args)

    rows = nb * steps
    ahist = (CONV_A - 1) * nb
    fhist = (CONV_F - 1) * nb
    tmaj = lambda a: jnp.swapaxes(a, 0, 1)
    sgw_s = jnp.repeat(jnp.transpose(sgu_w[0, :, :steps, :steps], (1, 2, 0)).reshape(steps * steps, H_B),
                       HD_B, axis=1)
    sgb_s = jnp.repeat(jnp.transpose(sgu_b[0, :, :steps]), HD_B, axis=1)
    vecs_s, layout_s = _pack_vectors(common + [('sgw', sgw_s), ('sgb', sgb_s)])
    w_args = [vecs_s] + mats + mats_tail
    scratch_s = [
        pltpu.VMEM((rows, D_MODEL), BF16),
        pltpu.VMEM((rows, D_MODEL + W_A), F32),
        pltpu.VMEM((ahist + rows, W_A), F32),
        pltpu.VMEM((rows, W_A), F32),
        pltpu.VMEM((rows, W_A), BF16),
        pltpu.VMEM((rows, 2 * W_A), F32),
        pltpu.VMEM((rows, W_A), F32),
        pltpu.VMEM((rows, W_A), F32),
        pltpu.VMEM((rows, W_A), F32),
        pltpu.VMEM((rows, W_A + W_B), BF16),
        pltpu.VMEM((rows, D_MODEL), F32),
        pltpu.VMEM((rows, ple), F32),
        pltpu.VMEM((fhist + rows, 2 * FF_CHUNK), F32),
        pltpu.VMEM((fhist + rows, 2 * FF_CHUNK), F32),
        pltpu.VMEM((rows, FF_CHUNK), BF16),
        pltpu.VMEM((rows, FF_CHUNK), BF16),
        pltpu.VMEM((rows, D_MODEL), F32),
    ]
    y_s, ht_s, cnew_s, vn_s, fnew_s = pl.pallas_call(
        functools.partial(_sample_kernel, nb=nb, steps=steps, layout=layout_s),
        grid=(nbs // nb,),
        in_specs=[pl.BlockSpec((steps, nb, D_MODEL), lambda i: (0, i, 0)),
                  pl.BlockSpec((steps, nb, ple), lambda i: (0, i, 0)),
                  pl.BlockSpec((nb, W_A), lambda i: (i, 0)),
                  pl.BlockSpec((CONV_A - 1, nb, W_A), lambda i: (0, i, 0)),
                  pl.BlockSpec((CONV_F - 1, nb, 2 * D_FF), lambda i: (0, i, 0))]
                 + [_const_spec(a.shape) for a in w_args],
        out_specs=[pl.BlockSpec((steps, nb, D_MODEL), lambda i: (0, i, 0)),
                   pl.BlockSpec((nb, W_A), lambda i: (i, 0)),
                   pl.BlockSpec((CONV_A - 1, nb, W_A), lambda i: (0, i, 0)),
                   pl.BlockSpec((steps, nb, W_B), lambda i: (0, i, 0)),
                   pl.BlockSpec((CONV_F - 1, nb, 2 * D_FF), lambda i: (0, i, 0))],
        out_shape=[jax.ShapeDtypeStruct((steps, nbs, D_MODEL), F32),
                   jax.ShapeDtypeStruct((nbs, W_A), F32),
                   jax.ShapeDtypeStruct((CONV_A - 1, nbs, W_A), F32),
                   jax.ShapeDtypeStruct((steps, nbs, W_B), F32),
                   jax.ShapeDtypeStruct((CONV_F - 1, nbs, 2 * D_FF), F32)],
        scratch_shapes=scratch_s,
        compiler_params=pltpu.CompilerParams(dimension_semantics=("arbitrary",),
                                             vmem_limit_bytes=VMEM_LIMIT_BYTES),
        name="sample_layer",
    )(tmaj(x_sample), tmaj(p_sample[0]), state_rglru_h[0], tmaj(state_rglru_conv[0]),
      tmaj(state_ffn_conv[0]), *w_args)

    return (y_p, tmaj(y_s), tmaj(ht_p), ht_s[None], cnew_p[None], tmaj(cnew_s)[None],
            tmaj(vn_s)[None], fnew_p[None], tmaj(fnew_s)[None])
```

```python
import functools

import jax
import jax.numpy as jnp
from jax import lax
from jax.experimental import pallas as pl
from jax.experimental.pallas import tpu as pltpu

F32 = jnp.float32
BF16 = jnp.bfloat16

D_MODEL = 1024
W_A = 512
W_B = 512
H_A = 8
BW_A = W_A // H_A
H_B = 4
HD_B = W_B // H_B
CHUNK = 128
D_FF = 3072
CONV_A = 4
CONV_F = 3
C_RG = 8.0
EPS = 1e-6

SUBLANES = 8
HIST = SUBLANES
GATE_HALF = W_A // 2
FF_CHUNK = 512
N_FF_CHUNKS = D_FF // FF_CHUNK
SLAB = 32
MXU_PIECE = 256
PACK_STEPS = 8
PROMPT_BLOCK_ROWS = 256
SAMPLE_GROUP = 32
VMEM_LIMIT_BYTES = 56 * 1024 * 1024


def _bc(tile, rows):
    return jnp.concatenate([tile] * (rows // SUBLANES), axis=0)


class _Vec:
    def __init__(self, ref, off, width, n):
        self.ref, self.off, self.width, self.n = ref, off, width, n

    def __getitem__(self, idx):
        full = slice(0, self.width)
        if self.n is None:
            j, cols = 0, (full if idx is Ellipsis else idx[1])
        else:
            j, cols = (idx, full) if isinstance(idx, int) else (idx[0], idx[2])
        start = self.off + j * self.width
        return self.ref[:, start + cols.start:start + cols.stop]


def _pack_vectors(named):
    layout, flat, off = {}, [], 0
    for name, a in named:
        n, width = (None, a.shape[0]) if a.ndim == 1 else a.shape
        layout[name] = (off, width, n)
        flat.append(a.reshape(-1))
        off += a.size
    block = jnp.broadcast_to(jnp.concatenate(flat)[None, :], (SUBLANES, off))
    return block, layout


def _rms(x, g):
    ms = jnp.mean(x * x, axis=-1, keepdims=True)
    return x * lax.rsqrt(ms + EPS) * _bc(g, x.shape[0])


def _layernorm(x, g, b):
    mu = jnp.mean(x, axis=-1, keepdims=True)
    xc = x - mu
    rows = x.shape[0]
    return xc * lax.rsqrt(jnp.mean(xc * xc, axis=-1, keepdims=True) + EPS) * _bc(g, rows) + _bc(b, rows)


def _sigmoid(x):
    return 1.0 / (1.0 + jnp.exp(-x))


def _softplus(x):
    return jnp.maximum(x, 0.0) + jnp.log(1.0 + jnp.exp(-jnp.abs(x)))


def _dot(a, b):
    return jnp.dot(a, b, preferred_element_type=F32)


def _fresh_rows(ref, salt, cols=slice(None)):
    zero = jnp.minimum(pl.program_id(0), 0) * salt
    return ref[pl.ds(pl.multiple_of(zero, 16), ref.shape[0]), cols]


def _wdot(a, w_words):
    return _dot(a, pltpu.bitcast(w_words, BF16))


def _ff_cols(c, half):
    start = half * D_FF + c * FF_CHUNK
    return slice(start, start + FF_CHUNK)


def _norm_to_bf16(src_ref, g_ref, dst_ref, rows):
    g = g_ref[...]
    for r0 in range(0, rows, SLAB):
        dst_ref[r0:r0 + SLAB, :] = _rms(src_ref[r0:r0 + SLAB, :], g).astype(BF16)


def _gate_stage(xc_ref, xcb_ref, g_ref, a_ref, u_ref, w_gate, ba, bx, apar, rows):
    for hf in range(2):
        c0 = hf * GATE_HALF
        g_ref[:, 2 * c0:2 * c0 + 2 * GATE_HALF] = _wdot(xcb_ref[:, c0:c0 + GATE_HALF], w_gate[hf])
    sp = _softplus(-apar[...])
    for r0 in range(0, rows, SLAB):
        for hf in range(2):
            c0 = hf * GATE_HALF
            cs = slice(c0, c0 + GATE_HALF)
            r = _sigmoid(g_ref[r0:r0 + SLAB, 2 * c0:2 * c0 + GATE_HALF] + _bc(ba[:, cs], SLAB))
            i = _sigmoid(g_ref[r0:r0 + SLAB, 2 * c0 + GATE_HALF:2 * c0 + 2 * GATE_HALF] + _bc(bx[:, cs], SLAB))
            a = jnp.exp((-C_RG) * r * _bc(sp[:, cs], SLAB))
            mult = jnp.sqrt(1.0 - a * a)
            a_ref[r0:r0 + SLAB, cs] = a
            u_ref[r0:r0 + SLAB, cs] = xc_ref[r0:r0 + SLAB, cs] * i * mult


def _branch_a_out(hs_ref, z_ref, g_oa, ymix_ref, rows):
    g = g_oa[...]
    for r0 in range(0, rows, SLAB):
        ga = z_ref[r0:r0 + SLAB, 0:W_A]
        ymix_ref[r0:r0 + SLAB, 0:W_A] = _rms(hs_ref[r0:r0 + SLAB, :] * jax.nn.gelu(ga), g).astype(BF16)


def _vn_stage(z_ref, lng, lnb, rows, write):
    g = lng[...]
    b = lnb[...]
    for r0 in range(0, rows, SLAB):
        vb = z_ref[r0:r0 + SLAB, W_A + W_B:W_A + 2 * W_B]
        write(r0, _layernorm(jax.nn.gelu(vb), g, b))


def _ffn_items(h_ref, n_ref, up2_ref, act2_ref, acc_ref, gz_ref, pe_ref, load_p, g_ffn, w_up, fcw, fcb,
               w_down, g_ple, w_pg, w_ple, g_fin, rows, shift, hist_rows, load_hist, store_hist, write_y):
    n_up = 2 * FF_CHUNK // MXU_PIECE
    n_down = D_MODEL // MXU_PIECE
    slabs = list(range(0, rows, SLAB))
    items = []

    def norm_in(r0):
        h = h_ref[r0:r0 + SLAB, :]
        acc_ref[r0:r0 + SLAB, :] = h
        n_ref[r0:r0 + SLAB, :] = _rms(h, g_ffn[...]).astype(BF16)

    def up_piece(c, j):
        up_ref = up2_ref[c % 2]
        cs = slice(j * MXU_PIECE, (j + 1) * MXU_PIECE)
        half, off = divmod(j * MXU_PIECE, FF_CHUNK)
        w0 = half * D_FF + c * FF_CHUNK + off
        if j == 0:
            load_hist(c, up_ref)
        lhs = _fresh_rows(n_ref, 1 + c * n_up + j)
        up_ref[hist_rows:hist_rows + rows, cs] = _wdot(lhs, w_up[:, w0:w0 + MXU_PIECE])

    def down_piece(c, j):
        cs = slice(j * MXU_PIECE, (j + 1) * MXU_PIECE)
        lhs = _fresh_rows(act2_ref[c % 2], 1 + c * n_down + j)
        acc_ref[:, cs] += _wdot(lhs, w_down[c * FF_CHUNK // 2:(c + 1) * FF_CHUNK // 2, cs])

    def conv_act_slab(c, r0, last):
        up_ref = up2_ref[c % 2]
        halves = []
        for half in range(2):
            cs = slice(half * FF_CHUNK, (half + 1) * FF_CHUNK)
            ws = _ff_cols(c, half)
            cv = _bc(fcb[:, ws], SLAB)
            for j in range(CONV_F):
                off = hist_rows + r0 - (CONV_F - 1 - j) * shift
                cv = cv + up_ref[off:off + SLAB, cs] * _bc(fcw[j, :, ws], SLAB)
            halves.append(cv)
        act2_ref[c % 2][r0:r0 + SLAB, :] = (jax.nn.gelu(halves[0]) * halves[1]).astype(BF16)
        if last:
            store_hist(c, up_ref)

    def embed():
        pe_ref[...] = _wdot(load_p().astype(BF16), w_ple[...])

    def norm_mid(r0):
        n_ref[r0:r0 + SLAB, :] = _rms(acc_ref[r0:r0 + SLAB, :], g_ple[...]).astype(BF16)

    def gate_piece(j):
        cs = slice(j * MXU_PIECE, (j + 1) * MXU_PIECE)
        gz_ref[:, cs] = _wdot(_fresh_rows(n_ref, 1 + N_FF_CHUNKS * n_up + j), w_pg[:, cs])

    def finish(r0):
        gate = _sigmoid(gz_ref[r0:r0 + SLAB, 0:D_MODEL])
        h3 = acc_ref[r0:r0 + SLAB, :] + pe_ref[r0:r0 + SLAB, :] * gate
        write_y(r0, _rms(h3, g_fin[...]))

    items += [functools.partial(norm_in, r0) for r0 in slabs]
    items += [functools.partial(up_piece, 0, j) for j in range(n_up)]
    for c in range(N_FF_CHUNKS + 1):
        pieces = []
        for j in range(max(n_up, n_down)):
            if c + 1 < N_FF_CHUNKS and j < n_up:
                pieces.append(functools.partial(up_piece, c + 1, j))
            if c >= 1 and j < n_down:
                pieces.append(functools.partial(down_piece, c - 1, j))
        if c < N_FF_CHUNKS:
            done = 0
            for i, r0 in enumerate(slabs):
                items.append(functools.partial(conv_act_slab, c, r0, i == len(slabs) - 1))
                upto = (i + 1) * len(pieces) // len(slabs)
                items += pieces[done:upto]
                done = upto
        else:
            items += pieces
    items += [functools.partial(norm_mid, r0) for r0 in slabs]
    items += [functools.partial(gate_piece, j) for j in range(n_down)]
    items += [functools.partial(finish, r0) for r0 in slabs]
    return embed, items


def _run(items):
    for item in items:
        item()


def _prompt_mixer_items(x_ref, h_mid_ref, n_ref, z_ref, xa_ref, xc_ref, xcb_ref, g_ref, a_ref, u_ref, hs_ref,
                        carry_ref, sp_ref, vn_ref, ymix_ref, g_mix, w_in, caw, cab, w_gate, ba, bx, apar,
                        g_oa, lng, lnb, sgw, sgb, g_ob, w_out, tm, reset_row, extra_matmul):
    assert tm % CHUNK == 0 and tm % SLAB == 0
    slabs = list(range(0, tm, SLAB))

    def norm_in(r0):
        n_ref[r0:r0 + SLAB, :] = _rms(x_ref[0, r0:r0 + SLAB, :], g_mix[...]).astype(BF16)

    def in_piece(j):
        c0 = j * MXU_PIECE
        res = _wdot(_fresh_rows(n_ref, 1 + j), w_in[:, c0:c0 + MXU_PIECE])
        if c0 < W_A:
            xa_ref[HIST:HIST + tm, c0:c0 + MXU_PIECE] = res
        else:
            z_ref[:, c0 - W_A:c0 - W_A + MXU_PIECE] = res

    def conv(r0):
        if r0 == 0:
            sp_ref[...] = _softplus(-apar[...])
        cv = _bc(cab[...], SLAB)
        for j in range(CONV_A):
            off = HIST + r0 - (CONV_A - 1 - j)
            cv = cv + xa_ref[off:off + SLAB, :] * _bc(caw[j], SLAB)
        xc_ref[r0:r0 + SLAB, :] = cv
        xcb_ref[r0:r0 + SLAB, :] = cv.astype(BF16)
        if r0 + SLAB == tm:
            xa_ref[0:HIST, :] = xa_ref[tm:tm + HIST, :]

    def gate_piece(hf):
        c0 = hf * GATE_HALF
        lhs = _fresh_rows(xcb_ref, 1 + hf, slice(c0, c0 + GATE_HALF))
        g_ref[:, 2 * c0:2 * c0 + 2 * GATE_HALF] = _wdot(lhs, w_gate[hf])

    def gate_ew(r0, hf):
        c0 = hf * GATE_HALF
        cs = slice(c0, c0 + GATE_HALF)
        r = _sigmoid(g_ref[r0:r0 + SLAB, 2 * c0:2 * c0 + GATE_HALF] + _bc(ba[:, cs], SLAB))
        i = _sigmoid(g_ref[r0:r0 + SLAB, 2 * c0 + GATE_HALF:2 * c0 + 2 * GATE_HALF] + _bc(bx[:, cs], SLAB))
        a = jnp.exp((-C_RG) * r * _bc(sp_ref[:, cs], SLAB))
        mult = jnp.sqrt(1.0 - a * a)
        if r0 == 0:
            row = lax.broadcasted_iota(jnp.int32, (SLAB, GATE_HALF), 0)
            mult = jnp.where(row == reset_row, 1.0, mult)
        a_ref[r0:r0 + SLAB, cs] = a
        u_ref[r0:r0 + SLAB, cs] = xc_ref[r0:r0 + SLAB, cs] * i * mult

    def scan_group(r0):
        row = lax.broadcasted_iota(jnp.int32, (SUBLANES, W_A), 0)
        a = a_ref[r0:r0 + SUBLANES, :]
        u = u_ref[r0:r0 + SUBLANES, :]
        for d in (1, 2, 4):
            keep = row >= d
            a_s = jnp.where(keep, pltpu.roll(a, d, 0), 1.0)
            u_s = jnp.where(keep, pltpu.roll(u, d, 0), 0.0)
            u = a * u_s + u
            a = a * a_s
        h = a * carry_ref[...] + u
        hs_ref[r0:r0 + SUBLANES, :] = h
        carry_ref[...] = jnp.broadcast_to(h[SUBLANES - 1:SUBLANES, :], (SUBLANES, W_A))

    def a_out(r0):
        ga = z_ref[r0:r0 + SLAB, 0:W_A]
        ymix_ref[r0:r0 + SLAB, 0:W_A] = _rms(hs_ref[r0:r0 + SLAB, :] * jax.nn.gelu(ga), g_oa[...]).astype(BF16)

    def vn(r0):
        vb = z_ref[r0:r0 + SLAB, W_A + W_B:W_A + 2 * W_B]
        vn_ref[r0:r0 + SLAB, :] = _layernorm(jax.nn.gelu(vb), lng[...], lnb[...]).astype(BF16)

    def sgu(c0):
        ti = lax.broadcasted_iota(jnp.int32, (CHUNK, CHUNK), 0)
        si = lax.broadcasted_iota(jnp.int32, (CHUNK, CHUNK), 1)
        heads = []
        for hd in range(H_B):
            cs = slice(hd * HD_B, (hd + 1) * HD_B)
            wmix = jnp.where(si <= ti, sgw[hd], 0.0).astype(BF16)
            mixed = _dot(wmix, vn_ref[c0:c0 + CHUNK, cs]) + sgb[:, cs]
            ub = z_ref[c0:c0 + CHUNK, W_A + hd * HD_B:W_A + (hd + 1) * HD_B]
            heads.append(jax.nn.gelu(ub) * mixed)
        yb = jnp.concatenate(heads, axis=-1)
        ymix_ref[c0:c0 + CHUNK, W_A:] = _rms(yb, g_ob[...]).astype(BF16)

    def out_piece(j):
        cs = slice(j * MXU_PIECE, (j + 1) * MXU_PIECE)
        h_mid_ref[:, cs] = x_ref[0, :, cs] + _wdot(_fresh_rows(ymix_ref, 1 + j), w_out[:, cs])

    def spread(work, matmuls):
        out, done = [], 0
        for i, item in enumerate(work):
            out.append(item)
            upto = (i + 1) * len(matmuls) // len(work)
            out += matmuls[done:upto]
            done = upto
        return out

    n_in = (2 * W_A + 2 * W_B) // MXU_PIECE
    n_a = W_A // MXU_PIECE
    later_in = [functools.partial(in_piece, j) for j in reversed(range(n_a, n_in))]
    items = [functools.partial(norm_in, r0) for r0 in slabs]
    items += [functools.partial(in_piece, j) for j in range(n_a)]
    items += spread([functools.partial(conv, r0) for r0 in slabs], later_in[:2])
    items += [functools.partial(gate_piece, hf) for hf in range(2)] + [extra_matmul]
    items += spread([functools.partial(gate_ew, r0, hf) for r0 in slabs for hf in range(2)], later_in[2:])
    items += [functools.partial(scan_group, r0) for r0 in range(0, tm, SUBLANES)]
    items += [functools.partial(a_out, r0) for r0 in slabs]
    items += [functools.partial(vn, r0) for r0 in slabs]
    items += [functools.partial(sgu, c0) for c0 in range(0, tm, CHUNK)]
    items += [functools.partial(out_piece, j) for j in range(D_MODEL // MXU_PIECE)]
    return items


def _prompt_kernel(x_ref, p_ref, vecs, w_in, w_gate, sgw, sgb, w_out, w_up, w_down, w_pg, w_ple,
                   y_ref, ht_ref, cnew_ref, fnew_ref,
                   h_mid_ref, n1_ref, z_ref, xa_ref, xc_ref, xcb_ref, g_ref, a_ref, u_ref, hs_ref, carry_ref,
                   sp_ref, vn_ref, ymix_ref, n2_ref, up_ref, upb_ref, fhist_ref, act_ref, actb_ref, acc_ref,
                   gz_ref, pe_ref, *, tm, layout):
    v = {name: _Vec(vecs, *spec) for name, spec in layout.items()}
    g_mix, caw, cab, ba, bx, apar = v['g_mix'], v['caw'], v['cab'], v['ba'], v['bx'], v['apar']
    g_oa, lng, lnb, g_ob = v['g_oa'], v['lng'], v['lnb'], v['g_ob']
    g_ffn, fcw, fcb, g_ple, g_fin = v['g_ffn'], v['fcw'], v['fcb'], v['g_ple'], v['g_fin']
    t = pl.program_id(1)

    @pl.when(t == 0)
    def _():
        xa_ref[0:HIST, :] = jnp.zeros((HIST, W_A), F32)
        carry_ref[...] = jnp.zeros((SUBLANES, W_A), F32)
        fhist_ref[...] = jnp.zeros(fhist_ref.shape, F32)

    def load_hist(c, up):
        for half in range(2):
            up[0:HIST, half * FF_CHUNK:(half + 1) * FF_CHUNK] = fhist_ref[:, _ff_cols(c, half)]

    def store_hist(c, up):
        for half in range(2):
            cs = slice(half * FF_CHUNK, (half + 1) * FF_CHUNK)
            fhist_ref[:, _ff_cols(c, half)] = up[tm:tm + HIST, cs]
            fnew_ref[0, :, _ff_cols(c, half)] = up[HIST + tm - (CONV_F - 1):HIST + tm, cs]

    def write_y(r0, y):
        y_ref[0, r0:r0 + SLAB, :] = y

    embed, ffn = _ffn_items(h_mid_ref, n2_ref, (up_ref, upb_ref), (act_ref, actb_ref), acc_ref, gz_ref, pe_ref,
                            lambda: p_ref[0], g_ffn, w_up, fcw, fcb, w_down, g_ple, w_pg, w_ple, g_fin, tm, 1,
                            HIST, load_hist, store_hist, write_y)
    reset_row = jnp.where(t == 0, 0, -1)
    mixer = _prompt_mixer_items(x_ref, h_mid_ref, n1_ref, z_ref, xa_ref, xc_ref, xcb_ref, g_ref, a_ref, u_ref,
                                hs_ref, carry_ref, sp_ref, vn_ref, ymix_ref, g_mix, w_in, caw, cab, w_gate,
                                ba, bx, apar, g_oa, lng, lnb, sgw, sgb, g_ob, w_out, tm, reset_row, embed)
    _run(mixer)
    _run(ffn)

    ht_ref[0] = carry_ref[0:1, :]
    cnew_ref[0] = xa_ref[HIST + tm - (CONV_A - 1):HIST + tm, :]


def _sample_kernel(x_ref, p_ref, h0_ref, chist_ref, fh_ref, vecs, w_in, w_gate, w_out, w_up, w_down,
                   w_pg, w_ple,
                   y_ref, ht_ref, cnew_ref, vn_out_ref, fnew_ref,
                   n_ref, z_ref, xa_ref, xc_ref, xcb_ref, g_ref, a_ref, u_ref, hs_ref, ymix_ref,
                   h_ref, p_buf, up_ref, upb_ref, act_ref, actb_ref, acc_ref, *, nb, steps, layout):
    v = {name: _Vec(vecs, *spec) for name, spec in layout.items()}
    g_mix, caw, cab, ba, bx, apar = v['g_mix'], v['caw'], v['cab'], v['ba'], v['bx'], v['apar']
    g_oa, lng, lnb, sgw, sgb, g_ob = v['g_oa'], v['lng'], v['lnb'], v['sgw'], v['sgb'], v['g_ob']
    g_ffn, fcw, fcb, g_ple, g_fin = v['g_ffn'], v['fcw'], v['fcb'], v['g_ple'], v['g_fin']
    rows = nb * steps
    ahist = (CONV_A - 1) * nb
    fhist = (CONV_F - 1) * nb

    for s in range(steps):
        h_ref[s * nb:(s + 1) * nb, :] = x_ref[s]
        p_buf[s * nb:(s + 1) * nb, :] = p_ref[s]
    _norm_to_bf16(h_ref, g_mix, n_ref, rows)
    for k in range(CONV_A - 1):
        xa_ref[k * nb:(k + 1) * nb, :] = chist_ref[k]
    xa_ref[ahist:ahist + rows, :] = _wdot(n_ref[...], w_in[:, 0:W_A])
    z_ref[...] = _wdot(n_ref[...], w_in[:, W_A:])

    for r0 in range(0, rows, SLAB):
        cv = _bc(cab[...], SLAB)
        for j in range(CONV_A):
            off = r0 + j * nb
            cv = cv + xa_ref[off:off + SLAB, :] * _bc(caw[j], SLAB)
        xc_ref[r0:r0 + SLAB, :] = cv
        xcb_ref[r0:r0 + SLAB, :] = cv.astype(BF16)
    for k in range(CONV_A - 1):
        cnew_ref[k] = xa_ref[rows + k * nb:rows + (k + 1) * nb, :]

    _gate_stage(xc_ref, xcb_ref, g_ref, a_ref, u_ref, w_gate, ba, bx, apar, rows)

    for b0 in range(0, nb, SLAB):
        h = h0_ref[b0:b0 + SLAB, :]
        for s in range(steps):
            r0 = s * nb + b0
            h = a_ref[r0:r0 + SLAB, :] * h + u_ref[r0:r0 + SLAB, :]
            hs_ref[r0:r0 + SLAB, :] = h
        ht_ref[b0:b0 + SLAB, :] = h

    _branch_a_out(hs_ref, z_ref, g_oa, ymix_ref, rows)

    def write_vn(r0, vn):
        vn_out_ref[r0 // nb, r0 % nb:r0 % nb + SLAB, :] = vn

    _vn_stage(z_ref, lng, lnb, rows, write_vn)

    gob = g_ob[...]
    for b0 in range(0, nb, SLAB):
        for tt in range(steps):
            r0 = tt * nb + b0
            mixed = _bc(sgb[tt], SLAB)
            for s in range(tt + 1):
                k = tt * steps + s
                mixed = mixed + vn_out_ref[s, b0:b0 + SLAB, :] * _bc(sgw[k], SLAB)
            yb = jax.nn.gelu(z_ref[r0:r0 + SLAB, W_A:W_A + W_B]) * mixed
            ymix_ref[r0:r0 + SLAB, W_A:] = _rms(yb, gob).astype(BF16)

    h_ref[...] += _wdot(ymix_ref[...], w_out[...])

    def load_hist(c, up):
        for k in range(CONV_F - 1):
            for half in range(2):
                cs = slice(half * FF_CHUNK, (half + 1) * FF_CHUNK)
                up[k * nb:(k + 1) * nb, cs] = fh_ref[k, :, _ff_cols(c, half)]

    def store_hist(c, up):
        for k in range(CONV_F - 1):
            for half in range(2):
                cs = slice(half * FF_CHUNK, (half + 1) * FF_CHUNK)
                fnew_ref[k, :, _ff_cols(c, half)] = up[rows + k * nb:rows + (k + 1) * nb, cs]

    def write_y(r0, y):
        y_ref[r0 // nb, r0 % nb:r0 % nb + SLAB, :] = y

    embed, items = _ffn_items(h_ref, n_ref, (up_ref, upb_ref), (act_ref, actb_ref), acc_ref, z_ref, g_ref,
                              lambda: p_buf[...], g_ffn, w_up, fcw, fcb, w_down, g_ple, w_pg, w_ple, g_fin,
                              rows, nb, fhist, load_hist, store_hist, write_y)
    _run([embed] + items)


def _const_spec(shape):
    zeros = (0,) * len(shape)
    return pl.BlockSpec(shape, lambda *_: zeros, pipeline_mode=pl.Buffered(1))


def _pack_bf16_rows(weights):
    flat = [w.reshape((-1, w.shape[-1])) for w in weights]
    for w, w2 in zip(weights, flat):
        assert w.shape[-2] % 2 == 0 and w2.shape[0] % (PACK_STEPS * 2 * SUBLANES) == 0
    words = pl.pallas_call(
        _pack_kernel,
        grid=(PACK_STEPS,),
        in_specs=[pl.BlockSpec((w2.shape[0] // PACK_STEPS, w2.shape[1]), lambda i: (i, 0)) for w2 in flat],
        out_specs=[pl.BlockSpec((w2.shape[0] // PACK_STEPS // 2, w2.shape[1]), lambda i: (i, 0)) for w2 in flat],
        out_shape=[jax.ShapeDtypeStruct((w2.shape[0] // 2, w2.shape[1]), jnp.uint32) for w2 in flat],
        compiler_params=pltpu.CompilerParams(vmem_limit_bytes=VMEM_LIMIT_BYTES),
        name="pack_weights",
    )(*flat)
    return [o.reshape(w.shape[:-2] + (w.shape[-2] // 2, w.shape[-1])) for o, w in zip(words, weights)]


def _pack_kernel(*refs):
    n = len(refs) // 2
    for w_ref, o_ref in zip(refs[:n], refs[n:]):
        o_ref[...] = pltpu.bitcast(w_ref[...].astype(BF16), jnp.uint32)


def _block_diag_gate(wa, wx):
    hh = H_A // 2
    eye = jnp.eye(hh, dtype=wa.dtype)

    def bd(w):
        return jnp.einsum('hij,hg->higj', w, eye).reshape(GATE_HALF, GATE_HALF)

    return jnp.stack([jnp.concatenate([bd(wa[h * hh:(h + 1) * hh]), bd(wx[h * hh:(h + 1) * hh])], axis=1)
                      for h in range(2)])


def kernel(x_prompt, x_sample, p_prompt, p_sample, state_rglru_h, state_rglru_conv, state_ffn_conv, g_mix_norm, w_in, conv_a_w, conv_a_b, lru_wa, lru_ba, lru_wx, lru_bx, lru_a_param, g_out_a, ln_v_g, ln_v_b, sgu_w, sgu_b, g_out_b, w_out, g_ffn_norm, w_up, ffn_conv_w, ffn_conv_b, w_down, g_ple_norm, w_ple_gate, w_ple, g_final):
    assert w_in.shape[0] == 1
    nbp, seq, _ = x_prompt.shape
    nbs, steps, _ = x_sample.shape
    ple = p_prompt.shape[-1]
    tm = PROMPT_BLOCK_ROWS
    nb = SAMPLE_GROUP
    assert seq % tm == 0 and tm % CHUNK == 0 and nbs % nb == 0 and nb % SLAB == 0 and steps <= CHUNK
    common = [('g_mix', g_mix_norm[0]), ('caw', conv_a_w[0]), ('cab', conv_a_b[0]), ('ba', lru_ba[0]),
              ('bx', lru_bx[0]), ('apar', lru_a_param[0]), ('g_oa', g_out_a[0]), ('lng', ln_v_g[0]),
              ('lnb', ln_v_b[0]), ('g_ob', g_out_b[0]), ('g_ffn', g_ffn_norm[0]), ('fcw', ffn_conv_w[0]),
              ('fcb', ffn_conv_b[0]), ('g_ple', g_ple_norm[0]), ('g_fin', g_final)]
    packed = _pack_bf16_rows([w_in[0], _block_diag_gate(lru_wa[0], lru_wx[0]), w_out[0], w_up[0], w_down[0],
                               w_ple_gate[0], w_ple[0]])
    mats, mats_tail = packed[:2], packed[2:]

    sgb_p = jnp.repeat(jnp.transpose(sgu_b[0]), HD_B, axis=1)
    vecs_p, layout_p = _pack_vectors(common)
    p_args = [vecs_p] + mats + [sgu_w[0], sgb_p] + mats_tail
    scratch_p = [
        pltpu.VMEM((tm, D_MODEL), F32),
        pltpu.VMEM((tm, D_MODEL), BF16),
        pltpu.VMEM((tm, D_MODEL + W_A), F32),
        pltpu.VMEM((HIST + tm, W_A), F32),
        pltpu.VMEM((tm, W_A), F32),
        pltpu.VMEM((tm, W_A), BF16),
        pltpu.VMEM((tm, 2 * W_A), F32),
        pltpu.VMEM((tm, W_A), F32),
        pltpu.VMEM((tm, W_A), F32),
        pltpu.VMEM((tm, W_A), F32),
        pltpu.VMEM((SUBLANES, W_A), F32),
        pltpu.VMEM((SUBLANES, W_A), F32),
        pltpu.VMEM((tm, W_B), BF16),
        pltpu.VMEM((tm, W_A + W_B), BF16),
        pltpu.VMEM((tm, D_MODEL), BF16),
        pltpu.VMEM((HIST + tm, 2 * FF_CHUNK), F32),
        pltpu.VMEM((HIST + tm, 2 * FF_CHUNK), F32),
        pltpu.VMEM((HIST, 2 * D_FF), F32),
        pltpu.VMEM((tm, FF_CHUNK), BF16),
        pltpu.VMEM((tm, FF_CHUNK), BF16),
        pltpu.VMEM((tm, D_MODEL), F32),
        pltpu.VMEM((tm, D_MODEL), F32),
        pltpu.VMEM((tm, D_MODEL), F32),
    ]
    y_p, ht_p, cnew_p, fnew_p = pl.pallas_call(
        functools.partial(_prompt_kernel, tm=tm, layout=layout_p),
        grid=(nbp, seq // tm),
        in_specs=[pl.BlockSpec((1, tm, D_MODEL), lambda b, t: (b, t, 0)),
                  pl.BlockSpec((1, tm, ple), lambda b, t: (b, t, 0))]
                 + [_const_spec(a.shape) for a in p_args],
        out_specs=[pl.BlockSpec((1, tm, D_MODEL), lambda b, t: (b, t, 0)),
                   pl.BlockSpec((1, 1, W_A), lambda b, t: (b, 0, 0)),
                   pl.BlockSpec((1, CONV_A - 1, W_A), lambda b, t: (b, 0, 0)),
                   pl.BlockSpec((1, CONV_F - 1, 2 * D_FF), lambda b, t: (b, 0, 0))],
        out_shape=[jax.ShapeDtypeStruct((nbp, seq, D_MODEL), F32),
                   jax.ShapeDtypeStruct((nbp, 1, W_A), F32),
                   jax.ShapeDtypeStruct((nbp, CONV_A - 1, W_A), F32),
                   jax.ShapeDtypeStruct((nbp, CONV_F - 1, 2 * D_FF), F32)],
        scratch_shapes=scratch_p,
        compiler_params=pltpu.CompilerParams(dimension_semantics=("arbitrary", "arbitrary"),
                                             vmem_limit_bytes=VMEM_LIMIT_BYTES),
        name="prompt_layer",
    )(x_prompt, p_prompt[0], *p_args)

    rows = nb * steps
    ahist = (CONV_A - 1) * nb
    fhist = (CONV_F - 1) * nb
    tmaj = lambda a: jnp.swapaxes(a, 0, 1)
    sgw_s = jnp.repeat(jnp.transpose(sgu_w[0, :, :steps, :steps], (1, 2, 0)).reshape(steps * steps, H_B),
                       HD_B, axis=1)
    sgb_s = jnp.repeat(jnp.transpose(sgu_b[0, :, :steps]), HD_B, axis=1)
    vecs_s, layout_s = _pack_vectors(common + [('sgw', sgw_s), ('sgb', sgb_s)])
    w_args = [vecs_s] + mats + mats_tail
    scratch_s = [
        pltpu.VMEM((rows, D_MODEL), BF16),
        pltpu.VMEM((rows, D_MODEL + W_A), F32),
        pltpu.VMEM((ahist + rows, W_A), F32),
        pltpu.VMEM((rows, W_A), F32),
        pltpu.VMEM((rows, W_A), BF16),
        pltpu.VMEM((rows, 2 * W_A), F32),
        pltpu.VMEM((rows, W_A), F32),
        pltpu.VMEM((rows, W_A), F32),
        pltpu.VMEM((rows, W_A), F32),
        pltpu.VMEM((rows, W_A + W_B), BF16),
        pltpu.VMEM((rows, D_MODEL), F32),
        pltpu.VMEM((rows, ple), F32),
        pltpu.VMEM((fhist + rows, 2 * FF_CHUNK), F32),
        pltpu.VMEM((fhist + rows, 2 * FF_CHUNK), F32),
        pltpu.VMEM((rows, FF_CHUNK), BF16),
        pltpu.VMEM((rows, FF_CHUNK), BF16),
        pltpu.VMEM((rows, D_MODEL), F32),
    ]
    y_s, ht_s, cnew_s, vn_s, fnew_s = pl.pallas_call(
        functools.partial(_sample_kernel, nb=nb, steps=steps, layout=layout_s),
        grid=(nbs // nb,),
        in_specs=[pl.BlockSpec((steps, nb, D_MODEL), lambda i: (0, i, 0)),
                  pl.BlockSpec((steps, nb, ple), lambda i: (0, i, 0)),
                  pl.BlockSpec((nb, W_A), lambda i: (i, 0)),
                  pl.BlockSpec((CONV_A - 1, nb, W_A), lambda i: (0, i, 0)),
                  pl.BlockSpec((CONV_F - 1, nb, 2 * D_FF), lambda i: (0, i, 0))]
                 + [_const_spec(a.shape) for a in w_args],
        out_specs=[pl.BlockSpec((steps, nb, D_MODEL), lambda i: (0, i, 0)),
                   pl.BlockSpec((nb, W_A), lambda i: (i, 0)),
                   pl.BlockSpec((CONV_A - 1, nb, W_A), lambda i: (0, i, 0)),
                   pl.BlockSpec((steps, nb, W_B), lambda i: (0, i, 0)),
                   pl.BlockSpec((CONV_F - 1, nb, 2 * D_FF), lambda i: (0, i, 0))],
        out_shape=[jax.ShapeDtypeStruct((steps, nbs, D_MODEL), F32),
                   jax.ShapeDtypeStruct((nbs, W_A), F32),
                   jax.ShapeDtypeStruct((CONV_A - 1, nbs, W_A), F32),
                   jax.ShapeDtypeStruct((steps, nbs, W_B), F32),
                   jax.ShapeDtypeStruct((CONV_F - 1, nbs, 2 * D_FF), F32)],
        scratch_shapes=scratch_s,
        compiler_params=pltpu.CompilerParams(dimension_semantics=("arbitrary",),
                                             vmem_limit_bytes=VMEM_LIMIT_BYTES),
        name="sample_layer",
    )(tmaj(x_sample), tmaj(p_sample[0]), state_rglru_h[0], tmaj(state_rglru_conv[0]),
      tmaj(state_ffn_conv[0]), *w_args)

    return (y_p, tmaj(y_s), tmaj(ht_p), ht_s[None], cnew_p[None], tmaj(cnew_s)[None],
            tmaj(vn_s)[None], fnew_p[None], tmaj(fnew_s)[None])
```

```python
import functools

import jax
import jax.numpy as jnp
from jax import lax
from jax.experimental import pallas as pl
from jax.experimental.pallas import tpu as pltpu

F32 = jnp.float32
BF16 = jnp.bfloat16

D_MODEL = 1024
W_A = 512
W_B = 512
H_A = 8
BW_A = W_A // H_A
H_B = 4
HD_B = W_B // H_B
CHUNK = 128
D_FF = 3072
CONV_A = 4
CONV_F = 3
C_RG = 8.0
EPS = 1e-6

SUBLANES = 8
HIST = SUBLANES
GATE_HALF = W_A // 2
FF_CHUNK = 512
N_FF_CHUNKS = D_FF // FF_CHUNK
SLAB = 32
MXU_PIECE = 256
PACK_STEPS = 8
PROMPT_BLOCK_ROWS = 256
SAMPLE_GROUP = 64
VMEM_LIMIT_BYTES = 56 * 1024 * 1024


def _bc(tile, rows):
    return jnp.concatenate([tile] * (rows // SUBLANES), axis=0)


class _Vec:
    def __init__(self, ref, off, width, n):
        self.ref, self.off, self.width, self.n = ref, off, width, n

    def __getitem__(self, idx):
        full = slice(0, self.width)
        if self.n is None:
            j, cols = 0, (full if idx is Ellipsis else idx[1])
        else:
            j, cols = (idx, full) if isinstance(idx, int) else (idx[0], idx[2])
        start = self.off + j * self.width
        return self.ref[:, start + cols.start:start + cols.stop]


def _pack_vectors(named):
    layout, flat, off = {}, [], 0
    for name, a in named:
        n, width = (None, a.shape[0]) if a.ndim == 1 else a.shape
        layout[name] = (off, width, n)
        flat.append(a.reshape(-1))
        off += a.size
    block = jnp.broadcast_to(jnp.concatenate(flat)[None, :], (SUBLANES, off))
    return block, layout


def _rms(x, g):
    ms = jnp.mean(x * x, axis=-1, keepdims=True)
    return x * lax.rsqrt(ms + EPS) * _bc(g, x.shape[0])


def _layernorm(x, g, b):
    mu = jnp.mean(x, axis=-1, keepdims=True)
    xc = x - mu
    rows = x.shape[0]
    return xc * lax.rsqrt(jnp.mean(xc * xc, axis=-1, keepdims=True) + EPS) * _bc(g, rows) + _bc(b, rows)


def _sigmoid(x):
    return 1.0 / (1.0 + jnp.exp(-x))


def _softplus(x):
    return jnp.maximum(x, 0.0) + jnp.log(1.0 + jnp.exp(-jnp.abs(x)))


def _dot(a, b):
    return jnp.dot(a, b, preferred_element_type=F32)


def _fresh_rows(ref, salt, cols=slice(None)):
    zero = jnp.minimum(pl.program_id(0), 0) * salt
    return ref[pl.ds(pl.multiple_of(zero, 16), ref.shape[0]), cols]


def _wdot(a, w_words):
    return _dot(a, pltpu.bitcast(w_words, BF16))


def _ff_cols(c, half):
    start = half * D_FF + c * FF_CHUNK
    return slice(start, start + FF_CHUNK)


def _norm_to_bf16(src_ref, g_ref, dst_ref, rows):
    g = g_ref[...]
    for r0 in range(0, rows, SLAB):
        dst_ref[r0:r0 + SLAB, :] = _rms(src_ref[r0:r0 + SLAB, :], g).astype(BF16)


def _gate_stage(xc_ref, xcb_ref, g_ref, a_ref, u_ref, w_gate, ba, bx, apar, rows):
    for hf in range(2):
        c0 = hf * GATE_HALF
        g_ref[:, 2 * c0:2 * c0 + 2 * GATE_HALF] = _wdot(xcb_ref[:, c0:c0 + GATE_HALF], w_gate[hf])
    sp = _softplus(-apar[...])
    for r0 in range(0, rows, SLAB):
        for hf in range(2):
            c0 = hf * GATE_HALF
            cs = slice(c0, c0 + GATE_HALF)
            r = _sigmoid(g_ref[r0:r0 + SLAB, 2 * c0:2 * c0 + GATE_HALF] + _bc(ba[:, cs], SLAB))
            i = _sigmoid(g_ref[r0:r0 + SLAB, 2 * c0 + GATE_HALF:2 * c0 + 2 * GATE_HALF] + _bc(bx[:, cs], SLAB))
            a = jnp.exp((-C_RG) * r * _bc(sp[:, cs], SLAB))
            mult = jnp.sqrt(1.0 - a * a)
            a_ref[r0:r0 + SLAB, cs] = a
            u_ref[r0:r0 + SLAB, cs] = xc_ref[r0:r0 + SLAB, cs] * i * mult


def _branch_a_out(hs_ref, z_ref, g_oa, ymix_ref, rows):
    g = g_oa[...]
    for r0 in range(0, rows, SLAB):
        ga = z_ref[r0:r0 + SLAB, 0:W_A]
        ymix_ref[r0:r0 + SLAB, 0:W_A] = _rms(hs_ref[r0:r0 + SLAB, :] * jax.nn.gelu(ga), g).astype(BF16)


def _vn_stage(z_ref, lng, lnb, rows, write):
    g = lng[...]
    b = lnb[...]
    for r0 in range(0, rows, SLAB):
        vb = z_ref[r0:r0 + SLAB, W_A + W_B:W_A + 2 * W_B]
        write(r0, _layernorm(jax.nn.gelu(vb), g, b))


def _ffn_items(h_ref, n_ref, up2_ref, act2_ref, acc_ref, gz_ref, pe_ref, load_p, g_ffn, w_up, fcw, fcb,
               w_down, g_ple, w_pg, w_ple, g_fin, rows, shift, hist_rows, load_hist, store_hist, write_y):
    n_up = 2 * FF_CHUNK // MXU_PIECE
    n_down = D_MODEL // MXU_PIECE
    slabs = list(range(0, rows, SLAB))
    items = []

    def norm_in(r0):
        h = h_ref[r0:r0 + SLAB, :]
        acc_ref[r0:r0 + SLAB, :] = h
        n_ref[r0:r0 + SLAB, :] = _rms(h, g_ffn[...]).astype(BF16)

    def up_piece(c, j):
        up_ref = up2_ref[c % 2]
        cs = slice(j * MXU_PIECE, (j + 1) * MXU_PIECE)
        half, off = divmod(j * MXU_PIECE, FF_CHUNK)
        w0 = half * D_FF + c * FF_CHUNK + off
        if j == 0:
            load_hist(c, up_ref)
        lhs = _fresh_rows(n_ref, 1 + c * n_up + j)
        up_ref[hist_rows:hist_rows + rows, cs] = _wdot(lhs, w_up[:, w0:w0 + MXU_PIECE])

    def down_piece(c, j):
        cs = slice(j * MXU_PIECE, (j + 1) * MXU_PIECE)
        lhs = _fresh_rows(act2_ref[c % 2], 1 + c * n_down + j)
        acc_ref[:, cs] += _wdot(lhs, w_down[c * FF_CHUNK // 2:(c + 1) * FF_CHUNK // 2, cs])

    def conv_act_slab(c, r0, last):
        up_ref = up2_ref[c % 2]
        halves = []
        for half in range(2):
            cs = slice(half * FF_CHUNK, (half + 1) * FF_CHUNK)
            ws = _ff_cols(c, half)
            cv = _bc(fcb[:, ws], SLAB)
            for j in range(CONV_F):
                off = hist_rows + r0 - (CONV_F - 1 - j) * shift
                cv = cv + up_ref[off:off + SLAB, cs] * _bc(fcw[j, :, ws], SLAB)
            halves.append(cv)
        act2_ref[c % 2][r0:r0 + SLAB, :] = (jax.nn.gelu(halves[0]) * halves[1]).astype(BF16)
        if last:
            store_hist(c, up_ref)

    def embed():
        pe_ref[...] = _wdot(load_p().astype(BF16), w_ple[...])

    def norm_mid(r0):
        n_ref[r0:r0 + SLAB, :] = _rms(acc_ref[r0:r0 + SLAB, :], g_ple[...]).astype(BF16)

    def gate_piece(j):
        cs = slice(j * MXU_PIECE, (j + 1) * MXU_PIECE)
        gz_ref[:, cs] = _wdot(_fresh_rows(n_ref, 1 + N_FF_CHUNKS * n_up + j), w_pg[:, cs])

    def finish(r0):
        gate = _sigmoid(gz_ref[r0:r0 + SLAB, 0:D_MODEL])
        h3 = acc_ref[r0:r0 + SLAB, :] + pe_ref[r0:r0 + SLAB, :] * gate
        write_y(r0, _rms(h3, g_fin[...]))

    items += [functools.partial(norm_in, r0) for r0 in slabs]
    items += [functools.partial(up_piece, 0, j) for j in range(n_up)]
    for c in range(N_FF_CHUNKS + 1):
        pieces = []
        for j in range(max(n_up, n_down)):
            if c + 1 < N_FF_CHUNKS and j < n_up:
                pieces.append(functools.partial(up_piece, c + 1, j))
            if c >= 1 and j < n_down:
                pieces.append(functools.partial(down_piece, c - 1, j))
        if c < N_FF_CHUNKS:
            done = 0
            for i, r0 in enumerate(slabs):
                items.append(functools.partial(conv_act_slab, c, r0, i == len(slabs) - 1))
                upto = (i + 1) * len(pieces) // len(slabs)
                items += pieces[done:upto]
                done = upto
        else:
            items += pieces
    items += [functools.partial(norm_mid, r0) for r0 in slabs]
    items += [functools.partial(gate_piece, j) for j in range(n_down)]
    items += [functools.partial(finish, r0) for r0 in slabs]
    return embed, items


def _run(items):
    for item in items:
        item()


def _prompt_mixer_items(x_ref, h_mid_ref, n_ref, z_ref, xa_ref, xc_ref, xcb_ref, g_ref, a_ref, u_ref, hs_ref,
                        carry_ref, sp_ref, vn_ref, ymix_ref, g_mix, w_in, caw, cab, w_gate, ba, bx, apar,
                        g_oa, lng, lnb, sgw, sgb, g_ob, w_out, tm, reset_row, extra_matmul):
    assert tm % CHUNK == 0 and tm % SLAB == 0
    slabs = list(range(0, tm, SLAB))

    def norm_in(r0):
        n_ref[r0:r0 + SLAB, :] = _rms(x_ref[0, r0:r0 + SLAB, :], g_mix[...]).astype(BF16)

    def in_piece(j):
        c0 = j * MXU_PIECE
        res = _wdot(_fresh_rows(n_ref, 1 + j), w_in[:, c0:c0 + MXU_PIECE])
        if c0 < W_A:
            xa_ref[HIST:HIST + tm, c0:c0 + MXU_PIECE] = res
        else:
            z_ref[:, c0 - W_A:c0 - W_A + MXU_PIECE] = res

    def conv(r0):
        if r0 == 0:
            sp_ref[...] = _softplus(-apar[...])
        cv = _bc(cab[...], SLAB)
        for j in range(CONV_A):
            off = HIST + r0 - (CONV_A - 1 - j)
            cv = cv + xa_ref[off:off + SLAB, :] * _bc(caw[j], SLAB)
        xc_ref[r0:r0 + SLAB, :] = cv
        xcb_ref[r0:r0 + SLAB, :] = cv.astype(BF16)
        if r0 + SLAB == tm:
            xa_ref[0:HIST, :] = xa_ref[tm:tm + HIST, :]

    def gate_piece(hf):
        c0 = hf * GATE_HALF
        lhs = _fresh_rows(xcb_ref, 1 + hf, slice(c0, c0 + GATE_HALF))
        g_ref[:, 2 * c0:2 * c0 + 2 * GATE_HALF] = _wdot(lhs, w_gate[hf])

    def gate_ew(r0, hf):
        c0 = hf * GATE_HALF
        cs = slice(c0, c0 + GATE_HALF)
        r = _sigmoid(g_ref[r0:r0 + SLAB, 2 * c0:2 * c0 + GATE_HALF] + _bc(ba[:, cs], SLAB))
        i = _sigmoid(g_ref[r0:r0 + SLAB, 2 * c0 + GATE_HALF:2 * c0 + 2 * GATE_HALF] + _bc(bx[:, cs], SLAB))
        a = jnp.exp((-C_RG) * r * _bc(sp_ref[:, cs], SLAB))
        mult = jnp.sqrt(1.0 - a * a)
        if r0 == 0:
            row = lax.broadcasted_iota(jnp.int32, (SLAB, GATE_HALF), 0)
            mult = jnp.where(row == reset_row, 1.0, mult)
        a_ref[r0:r0 + SLAB, cs] = a
        u_ref[r0:r0 + SLAB, cs] = xc_ref[r0:r0 + SLAB, cs] * i * mult

    def scan_group(r0):
        row = lax.broadcasted_iota(jnp.int32, (SUBLANES, W_A), 0)
        a = a_ref[r0:r0 + SUBLANES, :]
        u = u_ref[r0:r0 + SUBLANES, :]
        for d in (1, 2, 4):
            keep = row >= d
            a_s = jnp.where(keep, pltpu.roll(a, d, 0), 1.0)
            u_s = jnp.where(keep, pltpu.roll(u, d, 0), 0.0)
            u = a * u_s + u
            a = a * a_s
        h = a * carry_ref[...] + u
        hs_ref[r0:r0 + SUBLANES, :] = h
        carry_ref[...] = jnp.broadcast_to(h[SUBLANES - 1:SUBLANES, :], (SUBLANES, W_A))

    def a_out(r0):
        ga = z_ref[r0:r0 + SLAB, 0:W_A]
        ymix_ref[r0:r0 + SLAB, 0:W_A] = _rms(hs_ref[r0:r0 + SLAB, :] * jax.nn.gelu(ga), g_oa[...]).astype(BF16)

    def vn(r0):
        vb = z_ref[r0:r0 + SLAB, W_A + W_B:W_A + 2 * W_B]
        vn_ref[r0:r0 + SLAB, :] = _layernorm(jax.nn.gelu(vb), lng[...], lnb[...]).astype(BF16)

    def sgu(c0):
        ti = lax.broadcasted_iota(jnp.int32, (CHUNK, CHUNK), 0)
        si = lax.broadcasted_iota(jnp.int32, (CHUNK, CHUNK), 1)
        heads = []
        for hd in range(H_B):
            cs = slice(hd * HD_B, (hd + 1) * HD_B)
            wmix = jnp.where(si <= ti, sgw[hd], 0.0).astype(BF16)
            mixed = _dot(wmix, vn_ref[c0:c0 + CHUNK, cs]) + sgb[:, cs]
            ub = z_ref[c0:c0 + CHUNK, W_A + hd * HD_B:W_A + (hd + 1) * HD_B]
            heads.append(jax.nn.gelu(ub) * mixed)
        yb = jnp.concatenate(heads, axis=-1)
        ymix_ref[c0:c0 + CHUNK, W_A:] = _rms(yb, g_ob[...]).astype(BF16)

    def out_piece(j):
        cs = slice(j * MXU_PIECE, (j + 1) * MXU_PIECE)
        h_mid_ref[:, cs] = x_ref[0, :, cs] + _wdot(_fresh_rows(ymix_ref, 1 + j), w_out[:, cs])

    def spread(work, matmuls):
        out, done = [], 0
        for i, item in enumerate(work):
            out.append(item)
            upto = (i + 1) * len(matmuls) // len(work)
            out += matmuls[done:upto]
            done = upto
        return out

    n_in = (2 * W_A + 2 * W_B) // MXU_PIECE
    n_a = W_A // MXU_PIECE
    later_in = [functools.partial(in_piece, j) for j in reversed(range(n_a, n_in))]
    items = [functools.partial(norm_in, r0) for r0 in slabs]
    items += [functools.partial(in_piece, j) for j in range(n_a)]
    items += spread([functools.partial(conv, r0) for r0 in slabs], later_in[:2])
    items += [functools.partial(gate_piece, hf) for hf in range(2)] + [extra_matmul]
    items += spread([functools.partial(gate_ew, r0, hf) for r0 in slabs for hf in range(2)], later_in[2:])
    items += [functools.partial(scan_group, r0) for r0 in range(0, tm, SUBLANES)]
    items += [functools.partial(a_out, r0) for r0 in slabs]
    items += [functools.partial(vn, r0) for r0 in slabs]
    items += [functools.partial(sgu, c0) for c0 in range(0, tm, CHUNK)]
    items += [functools.partial(out_piece, j) for j in range(D_MODEL // MXU_PIECE)]
    return items


def _prompt_kernel(x_ref, p_ref, vecs, w_in, w_gate, sgw, sgb, w_out, w_up, w_down, w_pg, w_ple,
                   y_ref, ht_ref, cnew_ref, fnew_ref,
                   h_mid_ref, n1_ref, z_ref, xa_ref, xc_ref, xcb_ref, g_ref, a_ref, u_ref, hs_ref, carry_ref,
                   sp_ref, vn_ref, ymix_ref, n2_ref, up_ref, upb_ref, fhist_ref, act_ref, actb_ref, acc_ref,
                   gz_ref, pe_ref, *, tm, layout):
    v = {name: _Vec(vecs, *spec) for name, spec in layout.items()}
    g_mix, caw, cab, ba, bx, apar = v['g_mix'], v['caw'], v['cab'], v['ba'], v['bx'], v['apar']
    g_oa, lng, lnb, g_ob = v['g_oa'], v['lng'], v['lnb'], v['g_ob']
    g_ffn, fcw, fcb, g_ple, g_fin = v['g_ffn'], v['fcw'], v['fcb'], v['g_ple'], v['g_fin']
    t = pl.program_id(1)

    @pl.when(t == 0)
    def _():
        xa_ref[0:HIST, :] = jnp.zeros((HIST, W_A), F32)
        carry_ref[...] = jnp.zeros((SUBLANES, W_A), F32)
        fhist_ref[...] = jnp.zeros(fhist_ref.shape, F32)

    def load_hist(c, up):
        for half in range(2):
            up[0:HIST, half * FF_CHUNK:(half + 1) * FF_CHUNK] = fhist_ref[:, _ff_cols(c, half)]

    def store_hist(c, up):
        for half in range(2):
            cs = slice(half * FF_CHUNK, (half + 1) * FF_CHUNK)
            fhist_ref[:, _ff_cols(c, half)] = up[tm:tm + HIST, cs]
            fnew_ref[0, :, _ff_cols(c, half)] = up[HIST + tm - (CONV_F - 1):HIST + tm, cs]

    def write_y(r0, y):
        y_ref[0, r0:r0 + SLAB, :] = y

    embed, ffn = _ffn_items(h_mid_ref, n2_ref, (up_ref, upb_ref), (act_ref, actb_ref), acc_ref, gz_ref, pe_ref,
                            lambda: p_ref[0], g_ffn, w_up, fcw, fcb, w_down, g_ple, w_pg, w_ple, g_fin, tm, 1,
                            HIST, load_hist, store_hist, write_y)
    reset_row = jnp.where(t == 0, 0, -1)
    mixer = _prompt_mixer_items(x_ref, h_mid_ref, n1_ref, z_ref, xa_ref, xc_ref, xcb_ref, g_ref, a_ref, u_ref,
                                hs_ref, carry_ref, sp_ref, vn_ref, ymix_ref, g_mix, w_in, caw, cab, w_gate,
                                ba, bx, apar, g_oa, lng, lnb, sgw, sgb, g_ob, w_out, tm, reset_row, embed)
    _run(mixer)
    _run(ffn)

    ht_ref[0] = carry_ref[0:1, :]
    cnew_ref[0] = xa_ref[HIST + tm - (CONV_A - 1):HIST + tm, :]


def _sample_kernel(x_ref, p_ref, h0_ref, chist_ref, fh_ref, vecs, w_in, w_gate, w_out, w_up, w_down,
                   w_pg, w_ple,
                   y_ref, ht_ref, cnew_ref, vn_out_ref, fnew_ref,
                   n_ref, z_ref, xa_ref, xc_ref, xcb_ref, g_ref, a_ref, u_ref, hs_ref, ymix_ref,
                   h_ref, p_buf, up_ref, upb_ref, act_ref, actb_ref, acc_ref, *, nb, steps, layout):
    v = {name: _Vec(vecs, *spec) for name, spec in layout.items()}
    g_mix, caw, cab, ba, bx, apar = v['g_mix'], v['caw'], v['cab'], v['ba'], v['bx'], v['apar']
    g_oa, lng, lnb, sgw, sgb, g_ob = v['g_oa'], v['lng'], v['lnb'], v['sgw'], v['sgb'], v['g_ob']
    g_ffn, fcw, fcb, g_ple, g_fin = v['g_ffn'], v['fcw'], v['fcb'], v['g_ple'], v['g_fin']
    rows = nb * steps
    ahist = (CONV_A - 1) * nb
    fhist = (CONV_F - 1) * nb

    for s in range(steps):
        h_ref[s * nb:(s + 1) * nb, :] = x_ref[s]
        p_buf[s * nb:(s + 1) * nb, :] = p_ref[s]
    _norm_to_bf16(h_ref, g_mix, n_ref, rows)
    for k in range(CONV_A - 1):
        xa_ref[k * nb:(k + 1) * nb, :] = chist_ref[k]
    xa_ref[ahist:ahist + rows, :] = _wdot(n_ref[...], w_in[:, 0:W_A])
    z_ref[...] = _wdot(n_ref[...], w_in[:, W_A:])

    for r0 in range(0, rows, SLAB):
        cv = _bc(cab[...], SLAB)
        for j in range(CONV_A):
            off = r0 + j * nb
            cv = cv + xa_ref[off:off + SLAB, :] * _bc(caw[j], SLAB)
        xc_ref[r0:r0 + SLAB, :] = cv
        xcb_ref[r0:r0 + SLAB, :] = cv.astype(BF16)
    for k in range(CONV_A - 1):
        cnew_ref[k] = xa_ref[rows + k * nb:rows + (k + 1) * nb, :]

    _gate_stage(xc_ref, xcb_ref, g_ref, a_ref, u_ref, w_gate, ba, bx, apar, rows)

    for b0 in range(0, nb, SLAB):
        h = h0_ref[b0:b0 + SLAB, :]
        for s in range(steps):
            r0 = s * nb + b0
            h = a_ref[r0:r0 + SLAB, :] * h + u_ref[r0:r0 + SLAB, :]
            hs_ref[r0:r0 + SLAB, :] = h
        ht_ref[b0:b0 + SLAB, :] = h

    _branch_a_out(hs_ref, z_ref, g_oa, ymix_ref, rows)

    def write_vn(r0, vn):
        vn_out_ref[r0 // nb, r0 % nb:r0 % nb + SLAB, :] = vn

    _vn_stage(z_ref, lng, lnb, rows, write_vn)

    gob = g_ob[...]
    for b0 in range(0, nb, SLAB):
        for tt in range(steps):
            r0 = tt * nb + b0
            mixed = _bc(sgb[tt], SLAB)
            for s in range(tt + 1):
                k = tt * steps + s
                mixed = mixed + vn_out_ref[s, b0:b0 + SLAB, :] * _bc(sgw[k], SLAB)
            yb = jax.nn.gelu(z_ref[r0:r0 + SLAB, W_A:W_A + W_B]) * mixed
            ymix_ref[r0:r0 + SLAB, W_A:] = _rms(yb, gob).astype(BF16)

    h_ref[...] += _wdot(ymix_ref[...], w_out[...])

    def load_hist(c, up):
        for k in range(CONV_F - 1):
            for half in range(2):
                cs = slice(half * FF_CHUNK, (half + 1) * FF_CHUNK)
                up[k * nb:(k + 1) * nb, cs] = fh_ref[k, :, _ff_cols(c, half)]

    def store_hist(c, up):
        for k in range(CONV_F - 1):
            for half in range(2):
                cs = slice(half * FF_CHUNK, (half + 1) * FF_CHUNK)
                fnew_ref[k, :, _ff_cols(c, half)] = up[rows + k * nb:rows + (k + 1) * nb, cs]

    def write_y(r0, y):
        y_ref[r0 // nb, r0 % nb:r0 % nb + SLAB, :] = y

    embed, items = _ffn_items(h_ref, n_ref, (up_ref, upb_ref), (act_ref, actb_ref), acc_ref, z_ref, g_ref,
                              lambda: p_buf[...], g_ffn, w_up, fcw, fcb, w_down, g_ple, w_pg, w_ple, g_fin,
                              rows, nb, fhist, load_hist, store_hist, write_y)
    _run([embed] + items)


def _const_spec(shape):
    zeros = (0,) * len(shape)
    return pl.BlockSpec(shape, lambda *_: zeros, pipeline_mode=pl.Buffered(1))


def _pack_bf16_rows(weights):
    flat = [w.reshape((-1, w.shape[-1])) for w in weights]
    for w, w2 in zip(weights, flat):
        assert w.shape[-2] % 2 == 0 and w2.shape[0] % (PACK_STEPS * 2 * SUBLANES) == 0
    words = pl.pallas_call(
        _pack_kernel,
        grid=(PACK_STEPS,),
        in_specs=[pl.BlockSpec((w2.shape[0] // PACK_STEPS, w2.shape[1]), lambda i: (i, 0)) for w2 in flat],
        out_specs=[pl.BlockSpec((w2.shape[0] // PACK_STEPS // 2, w2.shape[1]), lambda i: (i, 0)) for w2 in flat],
        out_shape=[jax.ShapeDtypeStruct((w2.shape[0] // 2, w2.shape[1]), jnp.uint32) for w2 in flat],
        compiler_params=pltpu.CompilerParams(vmem_limit_bytes=VMEM_LIMIT_BYTES),
        name="pack_weights",
    )(*flat)
    return [o.reshape(w.shape[:-2] + (w.shape[-2] // 2, w.shape[-1])) for o, w in zip(words, weights)]


def _pack_kernel(*refs):
    n = len(refs) // 2
    for w_ref, o_ref in zip(refs[:n], refs[n:]):
        o_ref[...] = pltpu.bitcast(w_ref[...].astype(BF16), jnp.uint32)


def _block_diag_gate(wa, wx):
    hh = H_A // 2
    eye = jnp.eye(hh, dtype=wa.dtype)

    def bd(w):
        return jnp.einsum('hij,hg->higj', w, eye).reshape(GATE_HALF, GATE_HALF)

    return jnp.stack([jnp.concatenate([bd(wa[h * hh:(h + 1) * hh]), bd(wx[h * hh:(h + 1) * hh])], axis=1)
                      for h in range(2)])


def kernel(x_prompt, x_sample, p_prompt, p_sample, state_rglru_h, state_rglru_conv, state_ffn_conv, g_mix_norm, w_in, conv_a_w, conv_a_b, lru_wa, lru_ba, lru_wx, lru_bx, lru_a_param, g_out_a, ln_v_g, ln_v_b, sgu_w, sgu_b, g_out_b, w_out, g_ffn_norm, w_up, ffn_conv_w, ffn_conv_b, w_down, g_ple_norm, w_ple_gate, w_ple, g_final):
    assert w_in.shape[0] == 1
    nbp, seq, _ = x_prompt.shape
    nbs, steps, _ = x_sample.shape
    ple = p_prompt.shape[-1]
    tm = PROMPT_BLOCK_ROWS
    nb = SAMPLE_GROUP
    assert seq % tm == 0 and tm % CHUNK == 0 and nbs % nb == 0 and nb % SLAB == 0 and steps <= CHUNK
    common = [('g_mix', g_mix_norm[0]), ('caw', conv_a_w[0]), ('cab', conv_a_b[0]), ('ba', lru_ba[0]),
              ('bx', lru_bx[0]), ('apar', lru_a_param[0]), ('g_oa', g_out_a[0]), ('lng', ln_v_g[0]),
              ('lnb', ln_v_b[0]), ('g_ob', g_out_b[0]), ('g_ffn', g_ffn_norm[0]), ('fcw', ffn_conv_w[0]),
              ('fcb', ffn_conv_b[0]), ('g_ple', g_ple_norm[0]), ('g_fin', g_final)]
    packed = _pack_bf16_rows([w_in[0], _block_diag_gate(lru_wa[0], lru_wx[0]), w_out[0], w_up[0], w_down[0],
                               w_ple_gate[0], w_ple[0]])
    mats, mats_tail = packed[:2], packed[2:]

    sgb_p = jnp.repeat(jnp.transpose(sgu_b[0]), HD_B, axis=1)
    vecs_p, layout_p = _pack_vectors(common)
    p_args = [vecs_p] + mats + [sgu_w[0], sgb_p] + mats_tail
    scratch_p = [
        pltpu.VMEM((tm, D_MODEL), F32),
        pltpu.VMEM((tm, D_MODEL), BF16),
        pltpu.VMEM((tm, D_MODEL + W_A), F32),
        pltpu.VMEM((HIST + tm, W_A), F32),
        pltpu.VMEM((tm, W_A), F32),
        pltpu.VMEM((tm, W_A), BF16),
        pltpu.VMEM((tm, 2 * W_A), F32),
        pltpu.VMEM((tm, W_A), F32),
        pltpu.VMEM((tm, W_A), F32),
        pltpu.VMEM((tm, W_A), F32),
        pltpu.VMEM((SUBLANES, W_A), F32),
        pltpu.VMEM((SUBLANES, W_A), F32),
        pltpu.VMEM((tm, W_B), BF16),
        pltpu.VMEM((tm, W_A + W_B), BF16),
        pltpu.VMEM((tm, D_MODEL), BF16),
        pltpu.VMEM((HIST + tm, 2 * FF_CHUNK), F32),
        pltpu.VMEM((HIST + tm, 2 * FF_CHUNK), F32),
        pltpu.VMEM((HIST, 2 * D_FF), F32),
        pltpu.VMEM((tm, FF_CHUNK), BF16),
        pltpu.VMEM((tm, FF_CHUNK), BF16),
        pltpu.VMEM((tm, D_MODEL), F32),
        pltpu.VMEM((tm, D_MODEL), F32),
        pltpu.VMEM((tm, D_MODEL), F32),
    ]
    y_p, ht_p, cnew_p, fnew_p = pl.pallas_call(
        functools.partial(_prompt_kernel, tm=tm, layout=layout_p),
        grid=(nbp, seq // tm),
        in_specs=[pl.BlockSpec((1, tm, D_MODEL), lambda b, t: (b, t, 0)),
                  pl.BlockSpec((1, tm, ple), lambda b, t: (b, t, 0))]
                 + [_const_spec(a.shape) for a in p_args],
        out_specs=[pl.BlockSpec((1, tm, D_MODEL), lambda b, t: (b, t, 0)),
                   pl.BlockSpec((1, 1, W_A), lambda b, t: (b, 0, 0)),
                   pl.BlockSpec((1, CONV_A - 1, W_A), lambda b, t: (b, 0, 0)),
                   pl.BlockSpec((1, CONV_F - 1, 2 * D_FF), lambda b, t: (b, 0, 0))],
        out_shape=[jax.ShapeDtypeStruct((nbp, seq, D_MODEL), F32),
                   jax.ShapeDtypeStruct((nbp, 1, W_A), F32),
                   jax.ShapeDtypeStruct((nbp, CONV_A - 1, W_A), F32),
                   jax.ShapeDtypeStruct((nbp, CONV_F - 1, 2 * D_FF), F32)],
        scratch_shapes=scratch_p,
        compiler_params=pltpu.CompilerParams(dimension_semantics=("arbitrary", "arbitrary"),
                                             vmem_limit_bytes=VMEM_LIMIT_BYTES),
        name="prompt_layer",
    )(x_prompt, p_prompt[0], *p_args)

    rows = nb * steps
    ahist = (CONV_A - 1) * nb
    fhist = (CONV_F - 1) * nb
    tmaj = lambda a: jnp.swapaxes(a, 0, 1)
    sgw_s = jnp.repeat(jnp.transpose(sgu_w[0, :, :steps, :steps], (1, 2, 0)).reshape(steps * steps, H_B),
                       HD_B, axis=1)
    sgb_s = jnp.repeat(jnp.transpose(sgu_b[0, :, :steps]), HD_B, axis=1)
    vecs_s, layout_s = _pack_vectors(common + [('sgw', sgw_s), ('sgb', sgb_s)])
    w_args = [vecs_s] + mats + mats_tail
    scratch_s = [
        pltpu.VMEM((rows, D_MODEL), BF16),
        pltpu.VMEM((rows, D_MODEL + W_A), F32),
        pltpu.VMEM((ahist + rows, W_A), F32),
        pltpu.VMEM((rows, W_A), F32),
        pltpu.VMEM((rows, W_A), BF16),
        pltpu.VMEM((rows, 2 * W_A), F32),
        pltpu.VMEM((rows, W_A), F32),
        pltpu.VMEM((rows, W_A), F32),
        pltpu.VMEM((rows, W_A), F32),
        pltpu.VMEM((rows, W_A + W_B), BF16),
        pltpu.VMEM((rows, D_MODEL), F32),
        pltpu.VMEM((rows, ple), F32),
        pltpu.VMEM((fhist + rows, 2 * FF_CHUNK), F32),
        pltpu.VMEM((fhist + rows, 2 * FF_CHUNK), F32),
        pltpu.VMEM((rows, FF_CHUNK), BF16),
        pltpu.VMEM((rows, FF_CHUNK), BF16),
        pltpu.VMEM((rows, D_MODEL), F32),
    ]
    y_s, ht_s, cnew_s, vn_s, fnew_s = pl.pallas_call(
        functools.partial(_sample_kernel, nb=nb, steps=steps, layout=layout_s),
        grid=(nbs // nb,),
        in_specs=[pl.BlockSpec((steps, nb, D_MODEL), lambda i: (0, i, 0)),
                  pl.BlockSpec((steps, nb, ple), lambda i: (0, i, 0)),
                  pl.BlockSpec((nb, W_A), lambda i: (i, 0)),
                  pl.BlockSpec((CONV_A - 1, nb, W_A), lambda i: (0, i, 0)),
                  pl.BlockSpec((CONV_F - 1, nb, 2 * D_FF), lambda i: (0, i, 0))]
                 + [_const_spec(a.shape) for a in w_args],
        out_specs=[pl.BlockSpec((steps, nb, D_MODEL), lambda i: (0, i, 0)),
                   pl.BlockSpec((nb, W_A), lambda i: (i, 0)),
                   pl.BlockSpec((CONV_A - 1, nb, W_A), lambda i: (0, i, 0)),
                   pl.BlockSpec((steps, nb, W_B), lambda i: (0, i, 0)),
                   pl.BlockSpec((CONV_F - 1, nb, 2 * D_FF), lambda i: (0, i, 0))],
        out_shape=[jax.ShapeDtypeStruct((steps, nbs, D_MODEL), F32),
                   jax.ShapeDtypeStruct((nbs, W_A), F32),
                   jax.ShapeDtypeStruct((CONV_A - 1, nbs, W_A), F32),
                   jax.ShapeDtypeStruct((steps, nbs, W_B), F32),
                   jax.ShapeDtypeStruct((CONV_F - 1, nbs, 2 * D_FF), F32)],
        scratch_shapes=scratch_s,
        compiler_params=pltpu.CompilerParams(dimension_semantics=("arbitrary",),
                                             vmem_limit_bytes=VMEM_LIMIT_BYTES),
        name="sample_layer",
    )(tmaj(x_sample), tmaj(p_sample[0]), state_rglru_h[0], tmaj(state_rglru_conv[0]),
      tmaj(state_ffn_conv[0]), *w_args)

    return (y_p, tmaj(y_s), tmaj(ht_p), ht_s[None], cnew_p[None], tmaj(cnew_s)[None],
            tmaj(vn_s)[None], fnew_p[None], tmaj(fnew_s)[None])
```

```python
import functools

import jax
import jax.numpy as jnp
from jax import lax
from jax.experimental import pallas as pl
from jax.experimental.pallas import tpu as pltpu

F32 = jnp.float32
BF16 = jnp.bfloat16

D_MODEL = 1024
W_A = 512
W_B = 512
H_A = 8
BW_A = W_A // H_A
H_B = 4
HD_B = W_B // H_B
CHUNK = 128
D_FF = 3072
CONV_A = 4
CONV_F = 3
C_RG = 8.0
EPS = 1e-6

SUBLANES = 8
HIST = SUBLANES
GATE_HALF = W_A // 2
FF_CHUNK = 512
N_FF_CHUNKS = D_FF // FF_CHUNK
SLAB = 32
MXU_PIECE = 256
PACK_STEPS = 8
PROMPT_BLOCK_ROWS = 256
SAMPLE_GROUP = 32
VMEM_LIMIT_BYTES = 56 * 1024 * 1024


def _bc(tile, rows):
    return jnp.concatenate([tile] * (rows // SUBLANES), axis=0)


class _Vec:
    def __init__(self, ref, off, width, n):
        self.ref, self.off, self.width, self.n = ref, off, width, n

    def __getitem__(self, idx):
        full = slice(0, self.width)
        if self.n is None:
            j, cols = 0, (full if idx is Ellipsis else idx[1])
        else:
            j, cols = (idx, full) if isinstance(idx, int) else (idx[0], idx[2])
        start = self.off + j * self.width
        return self.ref[:, start + cols.start:start + cols.stop]


def _pack_vectors(named):
    layout, flat, off = {}, [], 0
    for name, a in named:
        n, width = (None, a.shape[0]) if a.ndim == 1 else a.shape
        layout[name] = (off, width, n)
        flat.append(a.reshape(-1))
        off += a.size
    block = jnp.broadcast_to(jnp.concatenate(flat)[None, :], (SUBLANES, off))
    return block, layout


def _rms(x, g):
    ms = jnp.mean(x * x, axis=-1, keepdims=True)
    return x * lax.rsqrt(ms + EPS) * _bc(g, x.shape[0])


def _layernorm(x, g, b):
    mu = jnp.mean(x, axis=-1, keepdims=True)
    xc = x - mu
    rows = x.shape[0]
    return xc * lax.rsqrt(jnp.mean(xc * xc, axis=-1, keepdims=True) + EPS) * _bc(g, rows) + _bc(b, rows)


def _sigmoid(x):
    return 1.0 / (1.0 + jnp.exp(-x))


def _softplus(x):
    return jnp.maximum(x, 0.0) + jnp.log(1.0 + jnp.exp(-jnp.abs(x)))


def _dot(a, b):
    return jnp.dot(a, b, preferred_element_type=F32)


def _fresh_rows(ref, salt, cols=slice(None)):
    zero = jnp.minimum(pl.program_id(0), 0) * salt
    return ref[pl.ds(pl.multiple_of(zero, 16), ref.shape[0]), cols]


def _wdot(a, w_words):
    return _dot(a, pltpu.bitcast(w_words, BF16))


def _ff_cols(c, half):
    start = half * D_FF + c * FF_CHUNK
    return slice(start, start + FF_CHUNK)


def _norm_to_bf16(src_ref, g_ref, dst_ref, rows):
    g = g_ref[...]
    for r0 in range(0, rows, SLAB):
        dst_ref[r0:r0 + SLAB, :] = _rms(src_ref[r0:r0 + SLAB, :], g).astype(BF16)


def _gate_stage(xc_ref, xcb_ref, g_ref, a_ref, u_ref, w_gate, ba, bx, apar, rows):
    for hf in range(2):
        c0 = hf * GATE_HALF
        g_ref[:, 2 * c0:2 * c0 + 2 * GATE_HALF] = _wdot(xcb_ref[:, c0:c0 + GATE_HALF], w_gate[hf])
    sp = _softplus(-apar[...])
    for r0 in range(0, rows, SLAB):
        for hf in range(2):
            c0 = hf * GATE_HALF
            cs = slice(c0, c0 + GATE_HALF)
            r = _sigmoid(g_ref[r0:r0 + SLAB, 2 * c0:2 * c0 + GATE_HALF] + _bc(ba[:, cs], SLAB))
            i = _sigmoid(g_ref[r0:r0 + SLAB, 2 * c0 + GATE_HALF:2 * c0 + 2 * GATE_HALF] + _bc(bx[:, cs], SLAB))
            a = jnp.exp((-C_RG) * r * _bc(sp[:, cs], SLAB))
            mult = jnp.sqrt(1.0 - a * a)
            a_ref[r0:r0 + SLAB, cs] = a
            u_ref[r0:r0 + SLAB, cs] = xc_ref[r0:r0 + SLAB, cs] * i * mult


def _branch_a_out(hs_ref, z_ref, g_oa, ymix_ref, rows):
    g = g_oa[...]
    for r0 in range(0, rows, SLAB):
        ga = z_ref[r0:r0 + SLAB, 0:W_A]
        ymix_ref[r0:r0 + SLAB, 0:W_A] = _rms(hs_ref[r0:r0 + SLAB, :] * jax.nn.gelu(ga), g).astype(BF16)


def _vn_stage(z_ref, lng, lnb, rows, write):
    g = lng[...]
    b = lnb[...]
    for r0 in range(0, rows, SLAB):
        vb = z_ref[r0:r0 + SLAB, W_A + W_B:W_A + 2 * W_B]
        write(r0, _layernorm(jax.nn.gelu(vb), g, b))


def _ffn_items(h_ref, n_ref, up2_ref, act2_ref, acc_ref, gz_ref, pe_ref, load_p, g_ffn, w_up, fcw, fcb,
               w_down, g_ple, w_pg, w_ple, g_fin, rows, shift, hist_rows, load_hist, store_hist, write_y):
    n_up = 2 * FF_CHUNK // MXU_PIECE
    n_down = D_MODEL // MXU_PIECE
    slabs = list(range(0, rows, SLAB))
    items = []

    def norm_in(r0):
        h = h_ref[r0:r0 + SLAB, :]
        acc_ref[r0:r0 + SLAB, :] = h
        n_ref[r0:r0 + SLAB, :] = _rms(h, g_ffn[...]).astype(BF16)

    def up_piece(c, j):
        up_ref = up2_ref[c % 2]
        cs = slice(j * MXU_PIECE, (j + 1) * MXU_PIECE)
        half, off = divmod(j * MXU_PIECE, FF_CHUNK)
        w0 = half * D_FF + c * FF_CHUNK + off
        if j == 0:
            load_hist(c, up_ref)
        lhs = _fresh_rows(n_ref, 1 + c * n_up + j)
        up_ref[hist_rows:hist_rows + rows, cs] = _wdot(lhs, w_up[:, w0:w0 + MXU_PIECE])

    def down_piece(c, j):
        cs = slice(j * MXU_PIECE, (j + 1) * MXU_PIECE)
        lhs = _fresh_rows(act2_ref[c % 2], 1 + c * n_down + j)
        acc_ref[:, cs] += _wdot(lhs, w_down[c * FF_CHUNK // 2:(c + 1) * FF_CHUNK // 2, cs])

    def conv_act_slab(c, r0, last):
        up_ref = up2_ref[c % 2]
        halves = []
        for half in range(2):
            cs = slice(half * FF_CHUNK, (half + 1) * FF_CHUNK)
            ws = _ff_cols(c, half)
            cv = _bc(fcb[:, ws], SLAB)
            for j in range(CONV_F):
                off = hist_rows + r0 - (CONV_F - 1 - j) * shift
                cv = cv + up_ref[off:off + SLAB, cs] * _bc(fcw[j, :, ws], SLAB)
            halves.append(cv)
        act2_ref[c % 2][r0:r0 + SLAB, :] = (jax.nn.gelu(halves[0]) * halves[1]).astype(BF16)
        if last:
            store_hist(c, up_ref)

    def embed():
        pe_ref[...] = _wdot(load_p().astype(BF16), w_ple[...])

    def norm_mid(r0):
        n_ref[r0:r0 + SLAB, :] = _rms(acc_ref[r0:r0 + SLAB, :], g_ple[...]).astype(BF16)

    def gate_piece(j):
        cs = slice(j * MXU_PIECE, (j + 1) * MXU_PIECE)
        gz_ref[:, cs] = _wdot(_fresh_rows(n_ref, 1 + N_FF_CHUNKS * n_up + j), w_pg[:, cs])

    def finish(r0):
        gate = _sigmoid(gz_ref[r0:r0 + SLAB, 0:D_MODEL])
        h3 = acc_ref[r0:r0 + SLAB, :] + pe_ref[r0:r0 + SLAB, :] * gate
        write_y(r0, _rms(h3, g_fin[...]))

    items += [functools.partial(norm_in, r0) for r0 in slabs]
    items += [functools.partial(up_piece, 0, j) for j in range(n_up)]
    for c in range(N_FF_CHUNKS + 1):
        pieces = []
        for j in range(max(n_up, n_down)):
            if c + 1 < N_FF_CHUNKS and j < n_up:
                pieces.append(functools.partial(up_piece, c + 1, j))
            if c >= 1 and j < n_down:
                pieces.append(functools.partial(down_piece, c - 1, j))
        if c < N_FF_CHUNKS:
            done = 0
            for i, r0 in enumerate(slabs):
                items.append(functools.partial(conv_act_slab, c, r0, i == len(slabs) - 1))
                upto = (i + 1) * len(pieces) // len(slabs)
                items += pieces[done:upto]
                done = upto
        else:
            items += pieces
    items += [functools.partial(norm_mid, r0) for r0 in slabs]
    items += [functools.partial(gate_piece, j) for j in range(n_down)]
    items += [functools.partial(finish, r0) for r0 in slabs]
    return embed, items


def _run(items):
    for item in items:
        item()


def _prompt_mixer_items(x_ref, h_mid_ref, n_ref, z_ref, xa_ref, xc_ref, xcb_ref, g_ref, a_ref, u_ref, hs_ref,
                        carry_ref, sp_ref, vn_ref, ymix_ref, g_mix, w_in, caw, cab, w_gate, ba, bx, apar,
                        g_oa, lng, lnb, sgw, sgb, g_ob, w_out, tm, reset_row, extra_matmul):
    assert tm % CHUNK == 0 and tm % SLAB == 0
    slabs = list(range(0, tm, SLAB))

    def norm_in(r0):
        n_ref[r0:r0 + SLAB, :] = _rms(x_ref[0, r0:r0 + SLAB, :], g_mix[...]).astype(BF16)

    def in_piece(j):
        c0 = j * MXU_PIECE
        res = _wdot(_fresh_rows(n_ref, 1 + j), w_in[:, c0:c0 + MXU_PIECE])
        if c0 < W_A:
            xa_ref[HIST:HIST + tm, c0:c0 + MXU_PIECE] = res
        else:
            z_ref[:, c0 - W_A:c0 - W_A + MXU_PIECE] = res

    def conv(r0):
        if r0 == 0:
            sp_ref[...] = _softplus(-apar[...])
        cv = _bc(cab[...], SLAB)
        for j in range(CONV_A):
            off = HIST + r0 - (CONV_A - 1 - j)
            cv = cv + xa_ref[off:off + SLAB, :] * _bc(caw[j], SLAB)
        xc_ref[r0:r0 + SLAB, :] = cv
        xcb_ref[r0:r0 + SLAB, :] = cv.astype(BF16)
        if r0 + SLAB == tm:
            xa_ref[0:HIST, :] = xa_ref[tm:tm + HIST, :]

    def gate_piece(hf):
        c0 = hf * GATE_HALF
        lhs = _fresh_rows(xcb_ref, 1 + hf, slice(c0, c0 + GATE_HALF))
        g_ref[:, 2 * c0:2 * c0 + 2 * GATE_HALF] = _wdot(lhs, w_gate[hf])

    def gate_ew(r0, hf):
        c0 = hf * GATE_HALF
        cs = slice(c0, c0 + GATE_HALF)
        r = _sigmoid(g_ref[r0:r0 + SLAB, 2 * c0:2 * c0 + GATE_HALF] + _bc(ba[:, cs], SLAB))
        i = _sigmoid(g_ref[r0:r0 + SLAB, 2 * c0 + GATE_HALF:2 * c0 + 2 * GATE_HALF] + _bc(bx[:, cs], SLAB))
        a = jnp.exp((-C_RG) * r * _bc(sp_ref[:, cs], SLAB))
        mult = jnp.sqrt(1.0 - a * a)
        if r0 == 0:
            row = lax.broadcasted_iota(jnp.int32, (SLAB, GATE_HALF), 0)
            mult = jnp.where(row == reset_row, 1.0, mult)
        a_ref[r0:r0 + SLAB, cs] = a
        u_ref[r0:r0 + SLAB, cs] = xc_ref[r0:r0 + SLAB, cs] * i * mult

    def scan_group(r0):
        row = lax.broadcasted_iota(jnp.int32, (SUBLANES, W_A), 0)
        a = a_ref[r0:r0 + SUBLANES, :]
        u = u_ref[r0:r0 + SUBLANES, :]
        for d in (1, 2, 4):
            keep = row >= d
            a_s = jnp.where(keep, pltpu.roll(a, d, 0), 1.0)
            u_s = jnp.where(keep, pltpu.roll(u, d, 0), 0.0)
            u = a * u_s + u
            a = a * a_s
        h = a * carry_ref[...] + u
        hs_ref[r0:r0 + SUBLANES, :] = h
        carry_ref[...] = jnp.broadcast_to(h[SUBLANES - 1:SUBLANES, :], (SUBLANES, W_A))

    def a_out(r0):
        ga = z_ref[r0:r0 + SLAB, 0:W_A]
        ymix_ref[r0:r0 + SLAB, 0:W_A] = _rms(hs_ref[r0:r0 + SLAB, :] * jax.nn.gelu(ga), g_oa[...]).astype(BF16)

    def vn(r0):
        vb = z_ref[r0:r0 + SLAB, W_A + W_B:W_A + 2 * W_B]
        vn_ref[r0:r0 + SLAB, :] = _layernorm(jax.nn.gelu(vb), lng[...], lnb[...]).astype(BF16)

    def sgu(c0):
        ti = lax.broadcasted_iota(jnp.int32, (CHUNK, CHUNK), 0)
        si = lax.broadcasted_iota(jnp.int32, (CHUNK, CHUNK), 1)
        heads = []
        for hd in range(H_B):
            cs = slice(hd * HD_B, (hd + 1) * HD_B)
            wmix = jnp.where(si <= ti, sgw[hd], 0.0).astype(BF16)
            mixed = _dot(wmix, vn_ref[c0:c0 + CHUNK, cs]) + sgb[:, cs]
            ub = z_ref[c0:c0 + CHUNK, W_A + hd * HD_B:W_A + (hd + 1) * HD_B]
            heads.append(jax.nn.gelu(ub) * mixed)
        yb = jnp.concatenate(heads, axis=-1)
        ymix_ref[c0:c0 + CHUNK, W_A:] = _rms(yb, g_ob[...]).astype(BF16)

    def out_piece(j):
        cs = slice(j * MXU_PIECE, (j + 1) * MXU_PIECE)
        h_mid_ref[:, cs] = x_ref[0, :, cs] + _wdot(_fresh_rows(ymix_ref, 1 + j), w_out[:, cs])

    def spread(work, matmuls):
        out, done = [], 0
        for i, item in enumerate(work):
            out.append(item)
            upto = (i + 1) * len(matmuls) // len(work)
            out += matmuls[done:upto]
            done = upto
        return out

    n_in = (2 * W_A + 2 * W_B) // MXU_PIECE
    n_a = W_A // MXU_PIECE
    later_in = [functools.partial(in_piece, j) for j in reversed(range(n_a, n_in))]
    items = [functools.partial(norm_in, r0) for r0 in slabs]
    items += [functools.partial(in_piece, j) for j in range(n_a)]
    items += spread([functools.partial(conv, r0) for r0 in slabs], later_in[:2])
    items += [functools.partial(gate_piece, hf) for hf in range(2)] + [extra_matmul]
    items += spread([functools.partial(gate_ew, r0, hf) for r0 in slabs for hf in range(2)], later_in[2:])
    items += [functools.partial(scan_group, r0) for r0 in range(0, tm, SUBLANES)]
    items += [functools.partial(a_out, r0) for r0 in slabs]
    items += [functools.partial(vn, r0) for r0 in slabs]
    items += [functools.partial(sgu, c0) for c0 in range(0, tm, CHUNK)]
    items += [functools.partial(out_piece, j) for j in range(D_MODEL // MXU_PIECE)]
    return items


def _prompt_kernel(x_ref, p_ref, vecs, w_in, w_gate, sgw, sgb, w_out, w_up, w_down, w_pg, w_ple,
                   y_ref, ht_ref, cnew_ref, fnew_ref,
                   h_mid_ref, n1_ref, z_ref, xa_ref, xc_ref, xcb_ref, g_ref, a_ref, u_ref, hs_ref, carry_ref,
                   sp_ref, vn_ref, ymix_ref, n2_ref, up_ref, upb_ref, fhist_ref, act_ref, actb_ref, acc_ref,
                   gz_ref, pe_ref, *, tm, layout):
    v = {name: _Vec(vecs, *spec) for name, spec in layout.items()}
    g_mix, caw, cab, ba, bx, apar = v['g_mix'], v['caw'], v['cab'], v['ba'], v['bx'], v['apar']
    g_oa, lng, lnb, g_ob = v['g_oa'], v['lng'], v['lnb'], v['g_ob']
    g_ffn, fcw, fcb, g_ple, g_fin = v['g_ffn'], v['fcw'], v['fcb'], v['g_ple'], v['g_fin']
    t = pl.program_id(1)

    @pl.when(t == 0)
    def _():
        xa_ref[0:HIST, :] = jnp.zeros((HIST, W_A), F32)
        carry_ref[...] = jnp.zeros((SUBLANES, W_A), F32)
        fhist_ref[...] = jnp.zeros(fhist_ref.shape, F32)

    def load_hist(c, up):
        for half in range(2):
            up[0:HIST, half * FF_CHUNK:(half + 1) * FF_CHUNK] = fhist_ref[:, _ff_cols(c, half)]

    def store_hist(c, up):
        for half in range(2):
            cs = slice(half * FF_CHUNK, (half + 1) * FF_CHUNK)
            fhist_ref[:, _ff_cols(c, half)] = up[tm:tm + HIST, cs]
            fnew_ref[0, :, _ff_cols(c, half)] = up[HIST + tm - (CONV_F - 1):HIST + tm, cs]

    def write_y(r0, y):
        y_ref[0, r0:r0 + SLAB, :] = y

    embed, ffn = _ffn_items(h_mid_ref, n2_ref, (up_ref, upb_ref), (act_ref, actb_ref), acc_ref, gz_ref, pe_ref,
                            lambda: p_ref[0], g_ffn, w_up, fcw, fcb, w_down, g_ple, w_pg, w_ple, g_fin, tm, 1,
                            HIST, load_hist, store_hist, write_y)
    reset_row = jnp.where(t == 0, 0, -1)
    mixer = _prompt_mixer_items(x_ref, h_mid_ref, n1_ref, z_ref, xa_ref, xc_ref, xcb_ref, g_ref, a_ref, u_ref,
                                hs_ref, carry_ref, sp_ref, vn_ref, ymix_ref, g_mix, w_in, caw, cab, w_gate,
                                ba, bx, apar, g_oa, lng, lnb, sgw, sgb, g_ob, w_out, tm, reset_row, embed)
    _run(mixer)
    _run(ffn)

    ht_ref[0] = carry_ref[0:1, :]
    cnew_ref[0] = xa_ref[HIST + tm - (CONV_A - 1):HIST + tm, :]


def _sample_kernel(x_ref, p_ref, h0_ref, chist_ref, fh_ref, vecs, w_in, w_gate, w_out, w_up, w_down,
                   w_pg, w_ple,
                   y_ref, ht_ref, cnew_ref, vn_out_ref, fnew_ref,
                   n_ref, z_ref, xa_ref, xc_ref, xcb_ref, g_ref, a_ref, u_ref, hs_ref, ymix_ref,
                   h_ref, p_buf, up_ref, upb_ref, act_ref, actb_ref, acc_ref, *, nb, steps, layout):
    v = {name: _Vec(vecs, *spec) for name, spec in layout.items()}
    g_mix, caw, cab, ba, bx, apar = v['g_mix'], v['caw'], v['cab'], v['ba'], v['bx'], v['apar']
    g_oa, lng, lnb, sgw, sgb, g_ob = v['g_oa'], v['lng'], v['lnb'], v['sgw'], v['sgb'], v['g_ob']
    g_ffn, fcw, fcb, g_ple, g_fin = v['g_ffn'], v['fcw'], v['fcb'], v['g_ple'], v['g_fin']
    rows = nb * steps
    ahist = (CONV_A - 1) * nb
    fhist = (CONV_F - 1) * nb

    for s in range(steps):
        h_ref[s * nb:(s + 1) * nb, :] = x_ref[s]
        p_buf[s * nb:(s + 1) * nb, :] = p_ref[s]
    _norm_to_bf16(h_ref, g_mix, n_ref, rows)
    for k in range(CONV_A - 1):
        xa_ref[k * nb:(k + 1) * nb, :] = chist_ref[k]
    xa_ref[ahist:ahist + rows, :] = _wdot(n_ref[...], w_in[:, 0:W_A])
    z_ref[...] = _wdot(n_ref[...], w_in[:, W_A:])

    for r0 in range(0, rows, SLAB):
        cv = _bc(cab[...], SLAB)
        for j in range(CONV_A):
            off = r0 + j * nb
            cv = cv + xa_ref[off:off + SLAB, :] * _bc(caw[j], SLAB)
        xc_ref[r0:r0 + SLAB, :] = cv
        xcb_ref[r0:r0 + SLAB, :] = cv.astype(BF16)
    for k in range(CONV_A - 1):
        cnew_ref[k] = xa_ref[rows + k * nb:rows + (k + 1) * nb, :]

    _gate_stage(xc_ref, xcb_ref, g_ref, a_ref, u_ref, w_gate, ba, bx, apar, rows)

    for b0 in range(0, nb, SLAB):
        h = h0_ref[b0:b0 + SLAB, :]
        for s in range(steps):
            r0 = s * nb + b0
            h = a_ref[r0:r0 + SLAB, :] * h + u_ref[r0:r0 + SLAB, :]
            hs_ref[r0:r0 + SLAB, :] = h
        ht_ref[b0:b0 + SLAB, :] = h

    _branch_a_out(hs_ref, z_ref, g_oa, ymix_ref, rows)

    def write_vn(r0, vn):
        vn_out_ref[r0 // nb, r0 % nb:r0 % nb + SLAB, :] = vn

    _vn_stage(z_ref, lng, lnb, rows, write_vn)

    gob = g_ob[...]
    for b0 in range(0, nb, SLAB):
        for tt in range(steps):
            r0 = tt * nb + b0
            mixed = _bc(sgb[tt], SLAB)
            for s in range(tt + 1):
                k = tt * steps + s
                mixed = mixed + vn_out_ref[s, b0:b0 + SLAB, :] * _bc(sgw[k], SLAB)
            yb = jax.nn.gelu(z_ref[r0:r0 + SLAB, W_A:W_A + W_B]) * mixed
            ymix_ref[r0:r0 + SLAB, W_A:] = _rms(yb, gob).astype(BF16)

    h_ref[...] += _wdot(ymix_ref[...], w_out[...])

    def load_hist(c, up):
        for k in range(CONV_F - 1):
            for half in range(2):
                cs = slice(half * FF_CHUNK, (half + 1) * FF_CHUNK)
                up[k * nb:(k + 1) * nb, cs] = fh_ref[:, k, _ff_cols(c, half)]

    def store_hist(c, up):
        for k in range(CONV_F - 1):
            for half in range(2):
                cs = slice(half * FF_CHUNK, (half + 1) * FF_CHUNK)
                fnew_ref[:, k, _ff_cols(c, half)] = up[rows + k * nb:rows + (k + 1) * nb, cs]

    def write_y(r0, y):
        y_ref[r0 // nb, r0 % nb:r0 % nb + SLAB, :] = y

    embed, items = _ffn_items(h_ref, n_ref, (up_ref, upb_ref), (act_ref, actb_ref), acc_ref, z_ref, g_ref,
                              lambda: p_buf[...], g_ffn, w_up, fcw, fcb, w_down, g_ple, w_pg, w_ple, g_fin,
                              rows, nb, fhist, load_hist, store_hist, write_y)
    _run([embed] + items)


def _const_spec(shape):
    zeros = (0,) * len(shape)
    return pl.BlockSpec(shape, lambda *_: zeros, pipeline_mode=pl.Buffered(1))


def _pack_bf16_rows(weights):
    flat = [w.reshape((-1, w.shape[-1])) for w in weights]
    for w, w2 in zip(weights, flat):
        assert w.shape[-2] % 2 == 0 and w2.shape[0] % (PACK_STEPS * 2 * SUBLANES) == 0
    words = pl.pallas_call(
        _pack_kernel,
        grid=(PACK_STEPS,),
        in_specs=[pl.BlockSpec((w2.shape[0] // PACK_STEPS, w2.shape[1]), lambda i: (i, 0)) for w2 in flat],
        out_specs=[pl.BlockSpec((w2.shape[0] // PACK_STEPS // 2, w2.shape[1]), lambda i: (i, 0)) for w2 in flat],
        out_shape=[jax.ShapeDtypeStruct((w2.shape[0] // 2, w2.shape[1]), jnp.uint32) for w2 in flat],
        compiler_params=pltpu.CompilerParams(vmem_limit_bytes=VMEM_LIMIT_BYTES),
        name="pack_weights",
    )(*flat)
    return [o.reshape(w.shape[:-2] + (w.shape[-2] // 2, w.shape[-1])) for o, w in zip(words, weights)]


def _pack_kernel(*refs):
    n = len(refs) // 2
    for w_ref, o_ref in zip(refs[:n], refs[n:]):
        o_ref[...] = pltpu.bitcast(w_ref[...].astype(BF16), jnp.uint32)


def _block_diag_gate(wa, wx):
    hh = H_A // 2
    eye = jnp.eye(hh, dtype=wa.dtype)

    def bd(w):
        return jnp.einsum('hij,hg->higj', w, eye).reshape(GATE_HALF, GATE_HALF)

    return jnp.stack([jnp.concatenate([bd(wa[h * hh:(h + 1) * hh]), bd(wx[h * hh:(h + 1) * hh])], axis=1)
                      for h in range(2)])


def kernel(x_prompt, x_sample, p_prompt, p_sample, state_rglru_h, state_rglru_conv, state_ffn_conv, g_mix_norm, w_in, conv_a_w, conv_a_b, lru_wa, lru_ba, lru_wx, lru_bx, lru_a_param, g_out_a, ln_v_g, ln_v_b, sgu_w, sgu_b, g_out_b, w_out, g_ffn_norm, w_up, ffn_conv_w, ffn_conv_b, w_down, g_ple_norm, w_ple_gate, w_ple, g_final):
    assert w_in.shape[0] == 1
    nbp, seq, _ = x_prompt.shape
    nbs, steps, _ = x_sample.shape
    ple = p_prompt.shape[-1]
    tm = PROMPT_BLOCK_ROWS
    nb = SAMPLE_GROUP
    assert seq % tm == 0 and tm % CHUNK == 0 and nbs % nb == 0 and nb % SLAB == 0 and steps <= CHUNK
    common = [('g_mix', g_mix_norm[0]), ('caw', conv_a_w[0]), ('cab', conv_a_b[0]), ('ba', lru_ba[0]),
              ('bx', lru_bx[0]), ('apar', lru_a_param[0]), ('g_oa', g_out_a[0]), ('lng', ln_v_g[0]),
              ('lnb', ln_v_b[0]), ('g_ob', g_out_b[0]), ('g_ffn', g_ffn_norm[0]), ('fcw', ffn_conv_w[0]),
              ('fcb', ffn_conv_b[0]), ('g_ple', g_ple_norm[0]), ('g_fin', g_final)]
    packed = _pack_bf16_rows([w_in[0], _block_diag_gate(lru_wa[0], lru_wx[0]), w_out[0], w_up[0], w_down[0],
                               w_ple_gate[0], w_ple[0]])
    mats, mats_tail = packed[:2], packed[2:]

    sgb_p = jnp.repeat(jnp.transpose(sgu_b[0]), HD_B, axis=1)
    vecs_p, layout_p = _pack_vectors(common)
    p_args = [vecs_p] + mats + [sgu_w[0], sgb_p] + mats_tail
    scratch_p = [
        pltpu.VMEM((tm, D_MODEL), F32),
        pltpu.VMEM((tm, D_MODEL), BF16),
        pltpu.VMEM((tm, D_MODEL + W_A), F32),
        pltpu.VMEM((HIST + tm, W_A), F32),
        pltpu.VMEM((tm, W_A), F32),
        pltpu.VMEM((tm, W_A), BF16),
        pltpu.VMEM((tm, 2 * W_A), F32),
        pltpu.VMEM((tm, W_A), F32),
        pltpu.VMEM((tm, W_A), F32),
        pltpu.VMEM((tm, W_A), F32),
        pltpu.VMEM((SUBLANES, W_A), F32),
        pltpu.VMEM((SUBLANES, W_A), F32),
        pltpu.VMEM((tm, W_B), BF16),
        pltpu.VMEM((tm, W_A + W_B), BF16),
        pltpu.VMEM((tm, D_MODEL), BF16),
        pltpu.VMEM((HIST + tm, 2 * FF_CHUNK), F32),
        pltpu.VMEM((HIST + tm, 2 * FF_CHUNK), F32),
        pltpu.VMEM((HIST, 2 * D_FF), F32),
        pltpu.VMEM((tm, FF_CHUNK), BF16),
        pltpu.VMEM((tm, FF_CHUNK), BF16),
        pltpu.VMEM((tm, D_MODEL), F32),
        pltpu.VMEM((tm, D_MODEL), F32),
        pltpu.VMEM((tm, D_MODEL), F32),
    ]
    y_p, ht_p, cnew_p, fnew_p = pl.pallas_call(
        functools.partial(_prompt_kernel, tm=tm, layout=layout_p),
        grid=(nbp, seq // tm),
        in_specs=[pl.BlockSpec((1, tm, D_MODEL), lambda b, t: (b, t, 0)),
                  pl.BlockSpec((1, tm, ple), lambda b, t: (b, t, 0))]
                 + [_const_spec(a.shape) for a in p_args],
        out_specs=[pl.BlockSpec((1, tm, D_MODEL), lambda b, t: (b, t, 0)),
                   pl.BlockSpec((1, 1, W_A), lambda b, t: (b, 0, 0)),
                   pl.BlockSpec((1, CONV_A - 1, W_A), lambda b, t: (b, 0, 0)),
                   pl.BlockSpec((1, CONV_F - 1, 2 * D_FF), lambda b, t: (b, 0, 0))],
        out_shape=[jax.ShapeDtypeStruct((nbp, seq, D_MODEL), F32),
                   jax.ShapeDtypeStruct((nbp, 1, W_A), F32),
                   jax.ShapeDtypeStruct((nbp, CONV_A - 1, W_A), F32),
                   jax.ShapeDtypeStruct((nbp, CONV_F - 1, 2 * D_FF), F32)],
        scratch_shapes=scratch_p,
        compiler_params=pltpu.CompilerParams(dimension_semantics=("arbitrary", "arbitrary"),
                                             vmem_limit_bytes=VMEM_LIMIT_BYTES),
        name="prompt_layer",
    )(x_prompt, p_prompt[0], *p_args)

    rows = nb * steps
    ahist = (CONV_A - 1) * nb
    fhist = (CONV_F - 1) * nb
    tmaj = lambda a: jnp.swapaxes(a, 0, 1)
    sgw_s = jnp.repeat(jnp.transpose(sgu_w[0, :, :steps, :steps], (1, 2, 0)).reshape(steps * steps, H_B),
                       HD_B, axis=1)
    sgb_s = jnp.repeat(jnp.transpose(sgu_b[0, :, :steps]), HD_B, axis=1)
    vecs_s, layout_s = _pack_vectors(common + [('sgw', sgw_s), ('sgb', sgb_s)])
    w_args = [vecs_s] + mats + mats_tail
    scratch_s = [
        pltpu.VMEM((rows, D_MODEL), BF16),
        pltpu.VMEM((rows, D_MODEL + W_A), F32),
        pltpu.VMEM((ahist + rows, W_A), F32),
        pltpu.VMEM((rows, W_A), F32),
        pltpu.VMEM((rows, W_A), BF16),
        pltpu.VMEM((rows, 2 * W_A), F32),
        pltpu.VMEM((rows, W_A), F32),
        pltpu.VMEM((rows, W_A), F32),
        pltpu.VMEM((rows, W_A), F32),
        pltpu.VMEM((rows, W_A + W_B), BF16),
        pltpu.VMEM((rows, D_MODEL), F32),
        pltpu.VMEM((rows, ple), F32),
        pltpu.VMEM((fhist + rows, 2 * FF_CHUNK), F32),
        pltpu.VMEM((fhist + rows, 2 * FF_CHUNK), F32),
        pltpu.VMEM((rows, FF_CHUNK), BF16),
        pltpu.VMEM((rows, FF_CHUNK), BF16),
        pltpu.VMEM((rows, D_MODEL), F32),
    ]
    y_s, ht_s, cnew_s, vn_s, fnew_s = pl.pallas_call(
        functools.partial(_sample_kernel, nb=nb, steps=steps, layout=layout_s),
        grid=(nbs // nb,),
        in_specs=[pl.BlockSpec((steps, nb, D_MODEL), lambda i: (0, i, 0)),
                  pl.BlockSpec((steps, nb, ple), lambda i: (0, i, 0)),
                  pl.BlockSpec((nb, W_A), lambda i: (i, 0)),
                  pl.BlockSpec((CONV_A - 1, nb, W_A), lambda i: (0, i, 0)),
                  pl.BlockSpec((nb, CONV_F - 1, 2 * D_FF), lambda i: (i, 0, 0))]
                 + [_const_spec(a.shape) for a in w_args],
        out_specs=[pl.BlockSpec((steps, nb, D_MODEL), lambda i: (0, i, 0)),
                   pl.BlockSpec((nb, W_A), lambda i: (i, 0)),
                   pl.BlockSpec((CONV_A - 1, nb, W_A), lambda i: (0, i, 0)),
                   pl.BlockSpec((steps, nb, W_B), lambda i: (0, i, 0)),
                   pl.BlockSpec((nb, CONV_F - 1, 2 * D_FF), lambda i: (i, 0, 0))],
        out_shape=[jax.ShapeDtypeStruct((steps, nbs, D_MODEL), F32),
                   jax.ShapeDtypeStruct((nbs, W_A), F32),
                   jax.ShapeDtypeStruct((CONV_A - 1, nbs, W_A), F32),
                   jax.ShapeDtypeStruct((steps, nbs, W_B), F32),
                   jax.ShapeDtypeStruct((nbs, CONV_F - 1, 2 * D_FF), F32)],
        scratch_shapes=scratch_s,
        compiler_params=pltpu.CompilerParams(dimension_semantics=("arbitrary",),
                                             vmem_limit_bytes=VMEM_LIMIT_BYTES),
        name="sample_layer",
    )(tmaj(x_sample), tmaj(p_sample[0]), state_rglru_h[0], tmaj(state_rglru_conv[0]),
      state_ffn_conv[0], *w_args)

    return (y_p, tmaj(y_s), tmaj(ht_p), ht_s[None], cnew_p[None], tmaj(cnew_s)[None],
            tmaj(vn_s)[None], fnew_p[None], fnew_s[None])
```

```python
import functools

import jax
import jax.numpy as jnp
from jax import lax
from jax.experimental import pallas as pl
from jax.experimental.pallas import tpu as pltpu

F32 = jnp.float32
BF16 = jnp.bfloat16

D_MODEL = 1024
W_A = 512
W_B = 512
H_A = 8
BW_A = W_A // H_A
H_B = 4
HD_B = W_B // H_B
CHUNK = 128
D_FF = 3072
CONV_A = 4
CONV_F = 3
C_RG = 8.0
EPS = 1e-6

SUBLANES = 8
HIST = SUBLANES
GATE_HALF = W_A // 2
FF_CHUNK = 512
N_FF_CHUNKS = D_FF // FF_CHUNK
SLAB = 32
MXU_PIECE = 256
PACK_STEPS = 8
PROMPT_BLOCK_ROWS = 256
SAMPLE_GROUP = 32
VMEM_LIMIT_BYTES = 56 * 1024 * 1024


def _bc(tile, rows):
    return jnp.concatenate([tile] * (rows // SUBLANES), axis=0)


class _Vec:
    def __init__(self, ref, off, width, n):
        self.ref, self.off, self.width, self.n = ref, off, width, n

    def __getitem__(self, idx):
        full = slice(0, self.width)
        if self.n is None:
            j, cols = 0, (full if idx is Ellipsis else idx[1])
        else:
            j, cols = (idx, full) if isinstance(idx, int) else (idx[0], idx[2])
        start = self.off + j * self.width
        return self.ref[:, start + cols.start:start + cols.stop]


def _pack_vectors(named):
    layout, flat, off = {}, [], 0
    for name, a in named:
        n, width = (None, a.shape[0]) if a.ndim == 1 else a.shape
        layout[name] = (off, width, n)
        flat.append(a.reshape(-1))
        off += a.size
    block = jnp.broadcast_to(jnp.concatenate(flat)[None, :], (SUBLANES, off))
    return block, layout


def _rms(x, g):
    ms = jnp.mean(x * x, axis=-1, keepdims=True)
    return x * lax.rsqrt(ms + EPS) * _bc(g, x.shape[0])


def _layernorm(x, g, b):
    mu = jnp.mean(x, axis=-1, keepdims=True)
    xc = x - mu
    rows = x.shape[0]
    return xc * lax.rsqrt(jnp.mean(xc * xc, axis=-1, keepdims=True) + EPS) * _bc(g, rows) + _bc(b, rows)


def _sigmoid(x):
    return 1.0 / (1.0 + jnp.exp(-x))


def _softplus(x):
    return jnp.maximum(x, 0.0) + jnp.log(1.0 + jnp.exp(-jnp.abs(x)))


def _dot(a, b):
    return jnp.dot(a, b, preferred_element_type=F32)


def _fresh_rows(ref, salt, cols=slice(None)):
    zero = jnp.minimum(pl.program_id(0), 0) * salt
    return ref[pl.ds(pl.multiple_of(zero, 16), ref.shape[0]), cols]


def _wdot(a, w_words):
    return _dot(a, pltpu.bitcast(w_words, BF16))


def _ff_cols(c, half):
    start = half * D_FF + c * FF_CHUNK
    return slice(start, start + FF_CHUNK)


def _norm_to_bf16(src_ref, g_ref, dst_ref, rows):
    g = g_ref[...]
    for r0 in range(0, rows, SLAB):
        dst_ref[r0:r0 + SLAB, :] = _rms(src_ref[r0:r0 + SLAB, :], g).astype(BF16)


def _gate_stage(xc_ref, xcb_ref, g_ref, a_ref, u_ref, w_gate, ba, bx, apar, rows):
    for hf in range(2):
        c0 = hf * GATE_HALF
        g_ref[:, 2 * c0:2 * c0 + 2 * GATE_HALF] = _wdot(xcb_ref[:, c0:c0 + GATE_HALF], w_gate[hf])
    sp = _softplus(-apar[...])
    for r0 in range(0, rows, SLAB):
        for hf in range(2):
            c0 = hf * GATE_HALF
            cs = slice(c0, c0 + GATE_HALF)
            r = _sigmoid(g_ref[r0:r0 + SLAB, 2 * c0:2 * c0 + GATE_HALF] + _bc(ba[:, cs], SLAB))
            i = _sigmoid(g_ref[r0:r0 + SLAB, 2 * c0 + GATE_HALF:2 * c0 + 2 * GATE_HALF] + _bc(bx[:, cs], SLAB))
            a = jnp.exp((-C_RG) * r * _bc(sp[:, cs], SLAB))
            mult = jnp.sqrt(1.0 - a * a)
            a_ref[r0:r0 + SLAB, cs] = a
            u_ref[r0:r0 + SLAB, cs] = xc_ref[r0:r0 + SLAB, cs] * i * mult


def _branch_a_out(hs_ref, z_ref, g_oa, ymix_ref, rows):
    g = g_oa[...]
    for r0 in range(0, rows, SLAB):
        ga = z_ref[r0:r0 + SLAB, 0:W_A]
        ymix_ref[r0:r0 + SLAB, 0:W_A] = _rms(hs_ref[r0:r0 + SLAB, :] * jax.nn.gelu(ga), g).astype(BF16)


def _vn_stage(z_ref, lng, lnb, rows, write):
    g = lng[...]
    b = lnb[...]
    for r0 in range(0, rows, SLAB):
        vb = z_ref[r0:r0 + SLAB, W_A + W_B:W_A + 2 * W_B]
        write(r0, _layernorm(jax.nn.gelu(vb), g, b))


def _ffn_items(h_ref, n_ref, up2_ref, act2_ref, acc_ref, gz_ref, pe_ref, load_p, g_ffn, w_up, fcw, fcb,
               w_down, g_ple, w_pg, w_ple, g_fin, rows, shift, hist_rows, load_hist, store_hist, write_y):
    n_up = 2 * FF_CHUNK // MXU_PIECE
    n_down = D_MODEL // MXU_PIECE
    slabs = list(range(0, rows, SLAB))
    items = []

    def norm_in(r0):
        h = h_ref[r0:r0 + SLAB, :]
        acc_ref[r0:r0 + SLAB, :] = h
        n_ref[r0:r0 + SLAB, :] = _rms(h, g_ffn[...]).astype(BF16)

    def up_piece(c, j):
        up_ref = up2_ref[c % 2]
        cs = slice(j * MXU_PIECE, (j + 1) * MXU_PIECE)
        half, off = divmod(j * MXU_PIECE, FF_CHUNK)
        w0 = half * D_FF + c * FF_CHUNK + off
        if j == 0:
            load_hist(c, up_ref)
        lhs = _fresh_rows(n_ref, 1 + c * n_up + j)
        up_ref[hist_rows:hist_rows + rows, cs] = _wdot(lhs, w_up[:, w0:w0 + MXU_PIECE])

    def down_piece(c, j):
        cs = slice(j * MXU_PIECE, (j + 1) * MXU_PIECE)
        lhs = _fresh_rows(act2_ref[c % 2], 1 + c * n_down + j)
        acc_ref[:, cs] += _wdot(lhs, w_down[c * FF_CHUNK // 2:(c + 1) * FF_CHUNK // 2, cs])

    def conv_act_slab(c, r0, last):
        up_ref = up2_ref[c % 2]
        halves = []
        for half in range(2):
            cs = slice(half * FF_CHUNK, (half + 1) * FF_CHUNK)
            ws = _ff_cols(c, half)
            cv = _bc(fcb[:, ws], SLAB)
            for j in range(CONV_F):
                off = hist_rows + r0 - (CONV_F - 1 - j) * shift
                cv = cv + up_ref[off:off + SLAB, cs] * _bc(fcw[j, :, ws], SLAB)
            halves.append(cv)
        act2_ref[c % 2][r0:r0 + SLAB, :] = (jax.nn.gelu(halves[0]) * halves[1]).astype(BF16)
        if last:
            store_hist(c, up_ref)

    def embed():
        pe_ref[...] = _wdot(load_p().astype(BF16), w_ple[...])

    def norm_mid(r0):
        n_ref[r0:r0 + SLAB, :] = _rms(acc_ref[r0:r0 + SLAB, :], g_ple[...]).astype(BF16)

    def gate_piece(j):
        cs = slice(j * MXU_PIECE, (j + 1) * MXU_PIECE)
        gz_ref[:, cs] = _wdot(_fresh_rows(n_ref, 1 + N_FF_CHUNKS * n_up + j), w_pg[:, cs])

    def finish(r0):
        gate = _sigmoid(gz_ref[r0:r0 + SLAB, 0:D_MODEL])
        h3 = acc_ref[r0:r0 + SLAB, :] + pe_ref[r0:r0 + SLAB, :] * gate
        write_y(r0, _rms(h3, g_fin[...]))

    items += [functools.partial(norm_in, r0) for r0 in slabs]
    items += [functools.partial(up_piece, 0, j) for j in range(n_up)]
    for c in range(N_FF_CHUNKS + 1):
        pieces = []
        for j in range(max(n_up, n_down)):
            if c + 1 < N_FF_CHUNKS and j < n_up:
                pieces.append(functools.partial(up_piece, c + 1, j))
            if c >= 1 and j < n_down:
                pieces.append(functools.partial(down_piece, c - 1, j))
        if c < N_FF_CHUNKS:
            done = 0
            for i, r0 in enumerate(slabs):
                items.append(functools.partial(conv_act_slab, c, r0, i == len(slabs) - 1))
                upto = (i + 1) * len(pieces) // len(slabs)
                items += pieces[done:upto]
                done = upto
        else:
            items += pieces
    items += [functools.partial(norm_mid, r0) for r0 in slabs]
    items += [functools.partial(gate_piece, j) for j in range(n_down)]
    items += [functools.partial(finish, r0) for r0 in slabs]
    return embed, items


def _run(items):
    for item in items:
        item()


def _prompt_mixer_items(x_ref, h_mid_ref, n_ref, z_ref, xa_ref, xc_ref, xcb_ref, g_ref, a_ref, u_ref, hs_ref,
                        carry_ref, sp_ref, vn_ref, ymix_ref, g_mix, w_in, caw, cab, w_gate, ba, bx, apar,
                        g_oa, lng, lnb, sgw, sgb, g_ob, w_out, tm, reset_row, extra_matmul):
    assert tm % CHUNK == 0 and tm % SLAB == 0
    slabs = list(range(0, tm, SLAB))

    def norm_in(r0):
        n_ref[r0:r0 + SLAB, :] = _rms(x_ref[0, r0:r0 + SLAB, :], g_mix[...]).astype(BF16)

    def in_piece(j):
        c0 = j * MXU_PIECE
        res = _wdot(_fresh_rows(n_ref, 1 + j), w_in[:, c0:c0 + MXU_PIECE])
        if c0 < W_A:
            xa_ref[HIST:HIST + tm, c0:c0 + MXU_PIECE] = res
        else:
            z_ref[:, c0 - W_A:c0 - W_A + MXU_PIECE] = res

    def conv(r0):
        if r0 == 0:
            sp_ref[...] = _softplus(-apar[...])
        cv = _bc(cab[...], SLAB)
        for j in range(CONV_A):
            off = HIST + r0 - (CONV_A - 1 - j)
            cv = cv + xa_ref[off:off + SLAB, :] * _bc(caw[j], SLAB)
        xc_ref[r0:r0 + SLAB, :] = cv
        xcb_ref[r0:r0 + SLAB, :] = cv.astype(BF16)
        if r0 + SLAB == tm:
            xa_ref[0:HIST, :] = xa_ref[tm:tm + HIST, :]

    def gate_piece(hf):
        c0 = hf * GATE_HALF
        lhs = _fresh_rows(xcb_ref, 1 + hf, slice(c0, c0 + GATE_HALF))
        g_ref[:, 2 * c0:2 * c0 + 2 * GATE_HALF] = _wdot(lhs, w_gate[hf])

    def gate_ew(r0, hf):
        c0 = hf * GATE_HALF
        cs = slice(c0, c0 + GATE_HALF)
        r = _sigmoid(g_ref[r0:r0 + SLAB, 2 * c0:2 * c0 + GATE_HALF] + _bc(ba[:, cs], SLAB))
        i = _sigmoid(g_ref[r0:r0 + SLAB, 2 * c0 + GATE_HALF:2 * c0 + 2 * GATE_HALF] + _bc(bx[:, cs], SLAB))
        a = jnp.exp((-C_RG) * r * _bc(sp_ref[:, cs], SLAB))
        mult = jnp.sqrt(1.0 - a * a)
        if r0 == 0:
            row = lax.broadcasted_iota(jnp.int32, (SLAB, GATE_HALF), 0)
            mult = jnp.where(row == reset_row, 1.0, mult)
        a_ref[r0:r0 + SLAB, cs] = a
        u_ref[r0:r0 + SLAB, cs] = xc_ref[r0:r0 + SLAB, cs] * i * mult

    def scan_group(r0):
        row = lax.broadcasted_iota(jnp.int32, (SUBLANES, W_A), 0)
        a = a_ref[r0:r0 + SUBLANES, :]
        u = u_ref[r0:r0 + SUBLANES, :]
        for d in (1, 2, 4):
            keep = row >= d
            a_s = jnp.where(keep, pltpu.roll(a, d, 0), 1.0)
            u_s = jnp.where(keep, pltpu.roll(u, d, 0), 0.0)
            u = a * u_s + u
            a = a * a_s
        h = a * carry_ref[...] + u
        hs_ref[r0:r0 + SUBLANES, :] = h
        carry_ref[...] = jnp.broadcast_to(h[SUBLANES - 1:SUBLANES, :], (SUBLANES, W_A))

    def a_out(r0):
        ga = z_ref[r0:r0 + SLAB, 0:W_A]
        ymix_ref[r0:r0 + SLAB, 0:W_A] = _rms(hs_ref[r0:r0 + SLAB, :] * jax.nn.gelu(ga), g_oa[...]).astype(BF16)

    def vn(r0):
        vb = z_ref[r0:r0 + SLAB, W_A + W_B:W_A + 2 * W_B]
        vn_ref[r0:r0 + SLAB, :] = _layernorm(jax.nn.gelu(vb), lng[...], lnb[...]).astype(BF16)

    def sgu(c0):
        ti = lax.broadcasted_iota(jnp.int32, (CHUNK, CHUNK), 0)
        si = lax.broadcasted_iota(jnp.int32, (CHUNK, CHUNK), 1)
        heads = []
        for hd in range(H_B):
            cs = slice(hd * HD_B, (hd + 1) * HD_B)
            wmix = jnp.where(si <= ti, sgw[hd], 0.0).astype(BF16)
            mixed = _dot(wmix, vn_ref[c0:c0 + CHUNK, cs]) + sgb[:, cs]
            ub = z_ref[c0:c0 + CHUNK, W_A + hd * HD_B:W_A + (hd + 1) * HD_B]
            heads.append(jax.nn.gelu(ub) * mixed)
        yb = jnp.concatenate(heads, axis=-1)
        ymix_ref[c0:c0 + CHUNK, W_A:] = _rms(yb, g_ob[...]).astype(BF16)

    def out_piece(j):
        cs = slice(j * MXU_PIECE, (j + 1) * MXU_PIECE)
        h_mid_ref[:, cs] = x_ref[0, :, cs] + _wdot(_fresh_rows(ymix_ref, 1 + j), w_out[:, cs])

    def spread(work, matmuls):
        out, done = [], 0
        for i, item in enumerate(work):
            out.append(item)
            upto = (i + 1) * len(matmuls) // len(work)
            out += matmuls[done:upto]
            done = upto
        return out

    n_in = (2 * W_A + 2 * W_B) // MXU_PIECE
    n_a = W_A // MXU_PIECE
    later_in = [functools.partial(in_piece, j) for j in reversed(range(n_a, n_in))]
    items = [functools.partial(norm_in, r0) for r0 in slabs]
    items += [functools.partial(in_piece, j) for j in range(n_a)]
    items += spread([functools.partial(conv, r0) for r0 in slabs], later_in[:2])
    items += [functools.partial(gate_piece, hf) for hf in range(2)] + [extra_matmul]
    items += spread([functools.partial(gate_ew, r0, hf) for r0 in slabs for hf in range(2)], later_in[2:])
    items += [functools.partial(scan_group, r0) for r0 in range(0, tm, SUBLANES)]
    items += [functools.partial(a_out, r0) for r0 in slabs]
    items += [functools.partial(vn, r0) for r0 in slabs]
    items += [functools.partial(sgu, c0) for c0 in range(0, tm, CHUNK)]
    items += [functools.partial(out_piece, j) for j in range(D_MODEL // MXU_PIECE)]
    return items


def _prompt_kernel(x_ref, p_ref, vecs, w_in, w_gate, sgw, sgb, w_out, w_up, w_down, w_pg, w_ple,
                   y_ref, ht_ref, cnew_ref, fnew_ref,
                   h_mid_ref, n1_ref, z_ref, xa_ref, xc_ref, xcb_ref, g_ref, a_ref, u_ref, hs_ref, carry_ref,
                   sp_ref, vn_ref, ymix_ref, n2_ref, up_ref, upb_ref, fhist_ref, act_ref, actb_ref, acc_ref,
                   gz_ref, pe_ref, *, tm, layout):
    v = {name: _Vec(vecs, *spec) for name, spec in layout.items()}
    g_mix, caw, cab, ba, bx, apar = v['g_mix'], v['caw'], v['cab'], v['ba'], v['bx'], v['apar']
    g_oa, lng, lnb, g_ob = v['g_oa'], v['lng'], v['lnb'], v['g_ob']
    g_ffn, fcw, fcb, g_ple, g_fin = v['g_ffn'], v['fcw'], v['fcb'], v['g_ple'], v['g_fin']
    t = pl.program_id(1)

    @pl.when(t == 0)
    def _():
        xa_ref[0:HIST, :] = jnp.zeros((HIST, W_A), F32)
        carry_ref[...] = jnp.zeros((SUBLANES, W_A), F32)
        fhist_ref[...] = jnp.zeros(fhist_ref.shape, F32)

    def load_hist(c, up):
        for half in range(2):
            up[0:HIST, half * FF_CHUNK:(half + 1) * FF_CHUNK] = fhist_ref[:, _ff_cols(c, half)]

    def store_hist(c, up):
        for half in range(2):
            cs = slice(half * FF_CHUNK, (half + 1) * FF_CHUNK)
            fhist_ref[:, _ff_cols(c, half)] = up[tm:tm + HIST, cs]
            fnew_ref[0, :, _ff_cols(c, half)] = up[HIST + tm - (CONV_F - 1):HIST + tm, cs]

    def write_y(r0, y):
        y_ref[0, r0:r0 + SLAB, :] = y

    embed, ffn = _ffn_items(h_mid_ref, n2_ref, (up_ref, upb_ref), (act_ref, actb_ref), acc_ref, gz_ref, pe_ref,
                            lambda: p_ref[0], g_ffn, w_up, fcw, fcb, w_down, g_ple, w_pg, w_ple, g_fin, tm, 1,
                            HIST, load_hist, store_hist, write_y)
    reset_row = jnp.where(t == 0, 0, -1)
    mixer = _prompt_mixer_items(x_ref, h_mid_ref, n1_ref, z_ref, xa_ref, xc_ref, xcb_ref, g_ref, a_ref, u_ref,
                                hs_ref, carry_ref, sp_ref, vn_ref, ymix_ref, g_mix, w_in, caw, cab, w_gate,
                                ba, bx, apar, g_oa, lng, lnb, sgw, sgb, g_ob, w_out, tm, reset_row, embed)
    _run(mixer)
    _run(ffn)

    ht_ref[0] = carry_ref[0:1, :]
    cnew_ref[0] = xa_ref[HIST + tm - (CONV_A - 1):HIST + tm, :]


def _sample_kernel(x_ref, p_ref, h0_ref, chist_ref, fh_ref, vecs, w_in, w_gate, w_out, w_up, w_down,
                   w_pg, w_ple,
                   y_ref, ht_ref, cnew_ref, vn_out_ref, fnew_ref,
                   n_ref, z_ref, xa_ref, xc_ref, xcb_ref, g_ref, a_ref, u_ref, hs_ref, ymix_ref,
                   h_ref, p_buf, up_ref, upb_ref, act_ref, actb_ref, acc_ref, *, nb, steps, layout):
    v = {name: _Vec(vecs, *spec) for name, spec in layout.items()}
    g_mix, caw, cab, ba, bx, apar = v['g_mix'], v['caw'], v['cab'], v['ba'], v['bx'], v['apar']
    g_oa, lng, lnb, sgw, sgb, g_ob = v['g_oa'], v['lng'], v['lnb'], v['sgw'], v['sgb'], v['g_ob']
    g_ffn, fcw, fcb, g_ple, g_fin = v['g_ffn'], v['fcw'], v['fcb'], v['g_ple'], v['g_fin']
    rows = nb * steps
    ahist = (CONV_A - 1) * nb
    fhist = (CONV_F - 1) * nb

    for s in range(steps):
        h_ref[s * nb:(s + 1) * nb, :] = x_ref[:, s, :]
        p_buf[s * nb:(s + 1) * nb, :] = p_ref[:, s, :]
    _norm_to_bf16(h_ref, g_mix, n_ref, rows)
    for k in range(CONV_A - 1):
        xa_ref[k * nb:(k + 1) * nb, :] = chist_ref[:, k, :]
    xa_ref[ahist:ahist + rows, :] = _wdot(n_ref[...], w_in[:, 0:W_A])
    z_ref[...] = _wdot(n_ref[...], w_in[:, W_A:])

    for r0 in range(0, rows, SLAB):
        cv = _bc(cab[...], SLAB)
        for j in range(CONV_A):
            off = r0 + j * nb
            cv = cv + xa_ref[off:off + SLAB, :] * _bc(caw[j], SLAB)
        xc_ref[r0:r0 + SLAB, :] = cv
        xcb_ref[r0:r0 + SLAB, :] = cv.astype(BF16)
    for k in range(CONV_A - 1):
        cnew_ref[:, k, :] = xa_ref[rows + k * nb:rows + (k + 1) * nb, :]

    _gate_stage(xc_ref, xcb_ref, g_ref, a_ref, u_ref, w_gate, ba, bx, apar, rows)

    for b0 in range(0, nb, SLAB):
        h = h0_ref[b0:b0 + SLAB, :]
        for s in range(steps):
            r0 = s * nb + b0
            h = a_ref[r0:r0 + SLAB, :] * h + u_ref[r0:r0 + SLAB, :]
            hs_ref[r0:r0 + SLAB, :] = h
        ht_ref[b0:b0 + SLAB, :] = h

    _branch_a_out(hs_ref, z_ref, g_oa, ymix_ref, rows)

    def write_vn(r0, vn):
        vn_out_ref[r0 % nb:r0 % nb + SLAB, r0 // nb, :] = vn

    _vn_stage(z_ref, lng, lnb, rows, write_vn)

    gob = g_ob[...]
    for b0 in range(0, nb, SLAB):
        for tt in range(steps):
            r0 = tt * nb + b0
            mixed = _bc(sgb[tt], SLAB)
            for s in range(tt + 1):
                k = tt * steps + s
                mixed = mixed + vn_out_ref[b0:b0 + SLAB, s, :] * _bc(sgw[k], SLAB)
            yb = jax.nn.gelu(z_ref[r0:r0 + SLAB, W_A:W_A + W_B]) * mixed
            ymix_ref[r0:r0 + SLAB, W_A:] = _rms(yb, gob).astype(BF16)

    h_ref[...] += _wdot(ymix_ref[...], w_out[...])

    def load_hist(c, up):
        for k in range(CONV_F - 1):
            for half in range(2):
                cs = slice(half * FF_CHUNK, (half + 1) * FF_CHUNK)
                up[k * nb:(k + 1) * nb, cs] = fh_ref[:, k, _ff_cols(c, half)]

    def store_hist(c, up):
        for k in range(CONV_F - 1):
            for half in range(2):
                cs = slice(half * FF_CHUNK, (half + 1) * FF_CHUNK)
                fnew_ref[:, k, _ff_cols(c, half)] = up[rows + k * nb:rows + (k + 1) * nb, cs]

    def write_y(r0, y):
        y_ref[r0 % nb:r0 % nb + SLAB, r0 // nb, :] = y

    embed, items = _ffn_items(h_ref, n_ref, (up_ref, upb_ref), (act_ref, actb_ref), acc_ref, z_ref, g_ref,
                              lambda: p_buf[...], g_ffn, w_up, fcw, fcb, w_down, g_ple, w_pg, w_ple, g_fin,
                              rows, nb, fhist, load_hist, store_hist, write_y)
    _run([embed] + items)


def _const_spec(shape):
    zeros = (0,) * len(shape)
    return pl.BlockSpec(shape, lambda *_: zeros, pipeline_mode=pl.Buffered(1))


def _pack_bf16_rows(weights):
    flat = [w.reshape((-1, w.shape[-1])) for w in weights]
    for w, w2 in zip(weights, flat):
        assert w.shape[-2] % 2 == 0 and w2.shape[0] % (PACK_STEPS * 2 * SUBLANES) == 0
    words = pl.pallas_call(
        _pack_kernel,
        grid=(PACK_STEPS,),
        in_specs=[pl.BlockSpec((w2.shape[0] // PACK_STEPS, w2.shape[1]), lambda i: (i, 0)) for w2 in flat],
        out_specs=[pl.BlockSpec((w2.shape[0] // PACK_STEPS // 2, w2.shape[1]), lambda i: (i, 0)) for w2 in flat],
        out_shape=[jax.ShapeDtypeStruct((w2.shape[0] // 2, w2.shape[1]), jnp.uint32) for w2 in flat],
        compiler_params=pltpu.CompilerParams(vmem_limit_bytes=VMEM_LIMIT_BYTES),
        name="pack_weights",
    )(*flat)
    return [o.reshape(w.shape[:-2] + (w.shape[-2] // 2, w.shape[-1])) for o, w in zip(words, weights)]


def _pack_kernel(*refs):
    n = len(refs) // 2
    for w_ref, o_ref in zip(refs[:n], refs[n:]):
        o_ref[...] = pltpu.bitcast(w_ref[...].astype(BF16), jnp.uint32)


def _block_diag_gate(wa, wx):
    hh = H_A // 2
    eye = jnp.eye(hh, dtype=wa.dtype)

    def bd(w):
        return jnp.einsum('hij,hg->higj', w, eye).reshape(GATE_HALF, GATE_HALF)

    return jnp.stack([jnp.concatenate([bd(wa[h * hh:(h + 1) * hh]), bd(wx[h * hh:(h + 1) * hh])], axis=1)
                      for h in range(2)])


def kernel(x_prompt, x_sample, p_prompt, p_sample, state_rglru_h, state_rglru_conv, state_ffn_conv, g_mix_norm, w_in, conv_a_w, conv_a_b, lru_wa, lru_ba, lru_wx, lru_bx, lru_a_param, g_out_a, ln_v_g, ln_v_b, sgu_w, sgu_b, g_out_b, w_out, g_ffn_norm, w_up, ffn_conv_w, ffn_conv_b, w_down, g_ple_norm, w_ple_gate, w_ple, g_final):
    assert w_in.shape[0] == 1
    nbp, seq, _ = x_prompt.shape
    nbs, steps, _ = x_sample.shape
    ple = p_prompt.shape[-1]
    tm = PROMPT_BLOCK_ROWS
    nb = SAMPLE_GROUP
    assert seq % tm == 0 and tm % CHUNK == 0 and nbs % nb == 0 and nb % SLAB == 0 and steps <= CHUNK
    common = [('g_mix', g_mix_norm[0]), ('caw', conv_a_w[0]), ('cab', conv_a_b[0]), ('ba', lru_ba[0]),
              ('bx', lru_bx[0]), ('apar', lru_a_param[0]), ('g_oa', g_out_a[0]), ('lng', ln_v_g[0]),
              ('lnb', ln_v_b[0]), ('g_ob', g_out_b[0]), ('g_ffn', g_ffn_norm[0]), ('fcw', ffn_conv_w[0]),
              ('fcb', ffn_conv_b[0]), ('g_ple', g_ple_norm[0]), ('g_fin', g_final)]
    packed = _pack_bf16_rows([w_in[0], _block_diag_gate(lru_wa[0], lru_wx[0]), w_out[0], w_up[0], w_down[0],
                               w_ple_gate[0], w_ple[0]])
    mats, mats_tail = packed[:2], packed[2:]

    sgb_p = jnp.repeat(jnp.transpose(sgu_b[0]), HD_B, axis=1)
    vecs_p, layout_p = _pack_vectors(common)
    p_args = [vecs_p] + mats + [sgu_w[0], sgb_p] + mats_tail
    scratch_p = [
        pltpu.VMEM((tm, D_MODEL), F32),
        pltpu.VMEM((tm, D_MODEL), BF16),
        pltpu.VMEM((tm, D_MODEL + W_A), F32),
        pltpu.VMEM((HIST + tm, W_A), F32),
        pltpu.VMEM((tm, W_A), F32),
        pltpu.VMEM((tm, W_A), BF16),
        pltpu.VMEM((tm, 2 * W_A), F32),
        pltpu.VMEM((tm, W_A), F32),
        pltpu.VMEM((tm, W_A), F32),
        pltpu.VMEM((tm, W_A), F32),
        pltpu.VMEM((SUBLANES, W_A), F32),
        pltpu.VMEM((SUBLANES, W_A), F32),
        pltpu.VMEM((tm, W_B), BF16),
        pltpu.VMEM((tm, W_A + W_B), BF16),
        pltpu.VMEM((tm, D_MODEL), BF16),
        pltpu.VMEM((HIST + tm, 2 * FF_CHUNK), F32),
        pltpu.VMEM((HIST + tm, 2 * FF_CHUNK), F32),
        pltpu.VMEM((HIST, 2 * D_FF), F32),
        pltpu.VMEM((tm, FF_CHUNK), BF16),
        pltpu.VMEM((tm, FF_CHUNK), BF16),
        pltpu.VMEM((tm, D_MODEL), F32),
        pltpu.VMEM((tm, D_MODEL), F32),
        pltpu.VMEM((tm, D_MODEL), F32),
    ]
    y_p, ht_p, cnew_p, fnew_p = pl.pallas_call(
        functools.partial(_prompt_kernel, tm=tm, layout=layout_p),
        grid=(nbp, seq // tm),
        in_specs=[pl.BlockSpec((1, tm, D_MODEL), lambda b, t: (b, t, 0)),
                  pl.BlockSpec((1, tm, ple), lambda b, t: (b, t, 0))]
                 + [_const_spec(a.shape) for a in p_args],
        out_specs=[pl.BlockSpec((1, tm, D_MODEL), lambda b, t: (b, t, 0)),
                   pl.BlockSpec((1, 1, W_A), lambda b, t: (b, 0, 0)),
                   pl.BlockSpec((1, CONV_A - 1, W_A), lambda b, t: (b, 0, 0)),
                   pl.BlockSpec((1, CONV_F - 1, 2 * D_FF), lambda b, t: (b, 0, 0))],
        out_shape=[jax.ShapeDtypeStruct((nbp, seq, D_MODEL), F32),
                   jax.ShapeDtypeStruct((nbp, 1, W_A), F32),
                   jax.ShapeDtypeStruct((nbp, CONV_A - 1, W_A), F32),
                   jax.ShapeDtypeStruct((nbp, CONV_F - 1, 2 * D_FF), F32)],
        scratch_shapes=scratch_p,
        compiler_params=pltpu.CompilerParams(dimension_semantics=("arbitrary", "arbitrary"),
                                             vmem_limit_bytes=VMEM_LIMIT_BYTES),
        name="prompt_layer",
    )(x_prompt, p_prompt[0], *p_args)

    rows = nb * steps
    ahist = (CONV_A - 1) * nb
    fhist = (CONV_F - 1) * nb
    tmaj = lambda a: jnp.swapaxes(a, 0, 1)
    sgw_s = jnp.repeat(jnp.transpose(sgu_w[0, :, :steps, :steps], (1, 2, 0)).reshape(steps * steps, H_B),
                       HD_B, axis=1)
    sgb_s = jnp.repeat(jnp.transpose(sgu_b[0, :, :steps]), HD_B, axis=1)
    vecs_s, layout_s = _pack_vectors(common + [('sgw', sgw_s), ('sgb', sgb_s)])
    w_args = [vecs_s] + mats + mats_tail
    scratch_s = [
        pltpu.VMEM((rows, D_MODEL), BF16),
        pltpu.VMEM((rows, D_MODEL + W_A), F32),
        pltpu.VMEM((ahist + rows, W_A), F32),
        pltpu.VMEM((rows, W_A), F32),
        pltpu.VMEM((rows, W_A), BF16),
        pltpu.VMEM((rows, 2 * W_A), F32),
        pltpu.VMEM((rows, W_A), F32),
        pltpu.VMEM((rows, W_A), F32),
        pltpu.VMEM((rows, W_A), F32),
        pltpu.VMEM((rows, W_A + W_B), BF16),
        pltpu.VMEM((rows, D_MODEL), F32),
        pltpu.VMEM((rows, ple), F32),
        pltpu.VMEM((fhist + rows, 2 * FF_CHUNK), F32),
        pltpu.VMEM((fhist + rows, 2 * FF_CHUNK), F32),
        pltpu.VMEM((rows, FF_CHUNK), BF16),
        pltpu.VMEM((rows, FF_CHUNK), BF16),
        pltpu.VMEM((rows, D_MODEL), F32),
    ]
    y_s, ht_s, cnew_s, vn_s, fnew_s = pl.pallas_call(
        functools.partial(_sample_kernel, nb=nb, steps=steps, layout=layout_s),
        grid=(nbs // nb,),
        in_specs=[pl.BlockSpec((nb, steps, D_MODEL), lambda i: (i, 0, 0)),
                  pl.BlockSpec((nb, steps, ple), lambda i: (i, 0, 0)),
                  pl.BlockSpec((nb, W_A), lambda i: (i, 0)),
                  pl.BlockSpec((nb, CONV_A - 1, W_A), lambda i: (i, 0, 0)),
                  pl.BlockSpec((nb, CONV_F - 1, 2 * D_FF), lambda i: (i, 0, 0))]
                 + [_const_spec(a.shape) for a in w_args],
        out_specs=[pl.BlockSpec((nb, steps, D_MODEL), lambda i: (i, 0, 0)),
                   pl.BlockSpec((nb, W_A), lambda i: (i, 0)),
                   pl.BlockSpec((nb, CONV_A - 1, W_A), lambda i: (i, 0, 0)),
                   pl.BlockSpec((nb, steps, W_B), lambda i: (i, 0, 0)),
                   pl.BlockSpec((nb, CONV_F - 1, 2 * D_FF), lambda i: (i, 0, 0))],
        out_shape=[jax.ShapeDtypeStruct((nbs, steps, D_MODEL), F32),
                   jax.ShapeDtypeStruct((nbs, W_A), F32),
                   jax.ShapeDtypeStruct((nbs, CONV_A - 1, W_A), F32),
                   jax.ShapeDtypeStruct((nbs, steps, W_B), F32),
                   jax.ShapeDtypeStruct((nbs, CONV_F - 1, 2 * D_FF), F32)],
        scratch_shapes=scratch_s,
        compiler_params=pltpu.CompilerParams(dimension_semantics=("arbitrary",),
                                             vmem_limit_bytes=VMEM_LIMIT_BYTES),
        name="sample_layer",
    )(x_sample, p_sample[0], state_rglru_h[0], state_rglru_conv[0], state_ffn_conv[0], *w_args)

    return (y_p, y_s, tmaj(ht_p), ht_s[None], cnew_p[None], cnew_s[None], vn_s[None], fnew_p[None], fnew_s[None])
```

```python
import functools

import jax
import jax.numpy as jnp
from jax import lax
from jax.experimental import pallas as pl
from jax.experimental.pallas import tpu as pltpu

F32 = jnp.float32
BF16 = jnp.bfloat16

D_MODEL = 1024
W_A = 512
W_B = 512
H_A = 8
BW_A = W_A // H_A
H_B = 4
HD_B = W_B // H_B
CHUNK = 128
D_FF = 3072
CONV_A = 4
CONV_F = 3
C_RG = 8.0
EPS = 1e-6

SUBLANES = 8
HIST = SUBLANES
GATE_HALF = W_A // 2
FF_CHUNK = 512
N_FF_CHUNKS = D_FF // FF_CHUNK
SLAB = 32
MXU_PIECE = 256
PACK_STEPS = 8
PROMPT_BLOCK_ROWS = 256
SAMPLE_GROUP = 32
VMEM_LIMIT_BYTES = 56 * 1024 * 1024


def _bc(tile, rows):
    return jnp.concatenate([tile] * (rows // SUBLANES), axis=0)


class _Vec:
    def __init__(self, ref, off, width, n):
        self.ref, self.off, self.width, self.n = ref, off, width, n

    def __getitem__(self, idx):
        full = slice(0, self.width)
        if self.n is None:
            j, cols = 0, (full if idx is Ellipsis else idx[1])
        else:
            j, cols = (idx, full) if isinstance(idx, int) else (idx[0], idx[2])
        start = self.off + j * self.width
        return self.ref[:, start + cols.start:start + cols.stop]


def _pack_vectors(named):
    layout, flat, off = {}, [], 0
    for name, a in named:
        n, width = (None, a.shape[0]) if a.ndim == 1 else a.shape
        layout[name] = (off, width, n)
        flat.append(a.reshape(-1))
        off += a.size
    block = jnp.broadcast_to(jnp.concatenate(flat)[None, :], (SUBLANES, off))
    return block, layout


def _rms(x, g):
    ms = jnp.mean(x * x, axis=-1, keepdims=True)
    return x * lax.rsqrt(ms + EPS) * _bc(g, x.shape[0])


def _layernorm(x, g, b):
    mu = jnp.mean(x, axis=-1, keepdims=True)
    xc = x - mu
    rows = x.shape[0]
    return xc * lax.rsqrt(jnp.mean(xc * xc, axis=-1, keepdims=True) + EPS) * _bc(g, rows) + _bc(b, rows)


def _sigmoid(x):
    return 1.0 / (1.0 + jnp.exp(-x))


def _softplus(x):
    return jnp.maximum(x, 0.0) + jnp.log(1.0 + jnp.exp(-jnp.abs(x)))


def _dot(a, b):
    return jnp.dot(a, b, preferred_element_type=F32)


def _fresh_rows(ref, salt, cols=slice(None)):
    zero = jnp.minimum(pl.program_id(0), 0) * salt
    return ref[pl.ds(pl.multiple_of(zero, 16), ref.shape[0]), cols]


def _wdot(a, w_words):
    return _dot(a, pltpu.bitcast(w_words, BF16))


def _ff_cols(c, half):
    start = half * D_FF + c * FF_CHUNK
    return slice(start, start + FF_CHUNK)


def _norm_to_bf16(src_ref, g_ref, dst_ref, rows):
    g = g_ref[...]
    for r0 in range(0, rows, SLAB):
        dst_ref[r0:r0 + SLAB, :] = _rms(src_ref[r0:r0 + SLAB, :], g).astype(BF16)


def _gate_stage(xc_ref, xcb_ref, g_ref, a_ref, u_ref, w_gate, ba, bx, apar, rows):
    for hf in range(2):
        c0 = hf * GATE_HALF
        g_ref[:, 2 * c0:2 * c0 + 2 * GATE_HALF] = _wdot(xcb_ref[:, c0:c0 + GATE_HALF], w_gate[hf])
    sp = _softplus(-apar[...])
    for r0 in range(0, rows, SLAB):
        for hf in range(2):
            c0 = hf * GATE_HALF
            cs = slice(c0, c0 + GATE_HALF)
            r = _sigmoid(g_ref[r0:r0 + SLAB, 2 * c0:2 * c0 + GATE_HALF] + _bc(ba[:, cs], SLAB))
            i = _sigmoid(g_ref[r0:r0 + SLAB, 2 * c0 + GATE_HALF:2 * c0 + 2 * GATE_HALF] + _bc(bx[:, cs], SLAB))
            a = jnp.exp((-C_RG) * r * _bc(sp[:, cs], SLAB))
            mult = jnp.sqrt(1.0 - a * a)
            a_ref[r0:r0 + SLAB, cs] = a
            u_ref[r0:r0 + SLAB, cs] = xc_ref[r0:r0 + SLAB, cs] * i * mult


def _branch_a_out(hs_ref, z_ref, g_oa, ymix_ref, rows):
    g = g_oa[...]
    for r0 in range(0, rows, SLAB):
        ga = z_ref[r0:r0 + SLAB, 0:W_A]
        ymix_ref[r0:r0 + SLAB, 0:W_A] = _rms(hs_ref[r0:r0 + SLAB, :] * jax.nn.gelu(ga), g).astype(BF16)


def _vn_stage(z_ref, lng, lnb, rows, write):
    g = lng[...]
    b = lnb[...]
    for r0 in range(0, rows, SLAB):
        vb = z_ref[r0:r0 + SLAB, W_A + W_B:W_A + 2 * W_B]
        write(r0, _layernorm(jax.nn.gelu(vb), g, b))


def _ffn_items(h_ref, n_ref, up2_ref, act2_ref, acc_ref, gz_ref, pe_ref, load_p, g_ffn, w_up, fcw, fcb,
               w_down, g_ple, w_pg, w_ple, g_fin, rows, shift, hist_rows, load_hist, store_hist, write_y):
    n_up = 2 * FF_CHUNK // MXU_PIECE
    n_down = D_MODEL // MXU_PIECE
    slabs = list(range(0, rows, SLAB))
    items = []

    def norm_in(r0):
        h = h_ref[r0:r0 + SLAB, :]
        acc_ref[r0:r0 + SLAB, :] = h
        n_ref[r0:r0 + SLAB, :] = _rms(h, g_ffn[...]).astype(BF16)

    def up_piece(c, j):
        up_ref = up2_ref[c % 2]
        cs = slice(j * MXU_PIECE, (j + 1) * MXU_PIECE)
        half, off = divmod(j * MXU_PIECE, FF_CHUNK)
        w0 = half * D_FF + c * FF_CHUNK + off
        if j == 0:
            load_hist(c, up_ref)
        lhs = _fresh_rows(n_ref, 1 + c * n_up + j)
        up_ref[hist_rows:hist_rows + rows, cs] = _wdot(lhs, w_up[:, w0:w0 + MXU_PIECE])

    def down_piece(c, j):
        cs = slice(j * MXU_PIECE, (j + 1) * MXU_PIECE)
        lhs = _fresh_rows(act2_ref[c % 2], 1 + c * n_down + j)
        acc_ref[:, cs] += _wdot(lhs, w_down[c * FF_CHUNK // 2:(c + 1) * FF_CHUNK // 2, cs])

    def conv_act_slab(c, r0, last):
        up_ref = up2_ref[c % 2]
        halves = []
        for half in range(2):
            cs = slice(half * FF_CHUNK, (half + 1) * FF_CHUNK)
            ws = _ff_cols(c, half)
            cv = _bc(fcb[:, ws], SLAB)
            for j in range(CONV_F):
                off = hist_rows + r0 - (CONV_F - 1 - j) * shift
                cv = cv + up_ref[off:off + SLAB, cs] * _bc(fcw[j, :, ws], SLAB)
            halves.append(cv)
        act2_ref[c % 2][r0:r0 + SLAB, :] = (jax.nn.gelu(halves[0]) * halves[1]).astype(BF16)
        if last:
            store_hist(c, up_ref)

    def embed():
        pe_ref[...] = _wdot(load_p().astype(BF16), w_ple[...])

    def norm_mid(r0):
        n_ref[r0:r0 + SLAB, :] = _rms(acc_ref[r0:r0 + SLAB, :], g_ple[...]).astype(BF16)

    def gate_piece(j):
        cs = slice(j * MXU_PIECE, (j + 1) * MXU_PIECE)
        gz_ref[:, cs] = _wdot(_fresh_rows(n_ref, 1 + N_FF_CHUNKS * n_up + j), w_pg[:, cs])

    def finish(r0):
        gate = _sigmoid(gz_ref[r0:r0 + SLAB, 0:D_MODEL])
        h3 = acc_ref[r0:r0 + SLAB, :] + pe_ref[r0:r0 + SLAB, :] * gate
        write_y(r0, _rms(h3, g_fin[...]))

    items += [functools.partial(norm_in, r0) for r0 in slabs]
    items += [functools.partial(up_piece, 0, j) for j in range(n_up)]
    for c in range(N_FF_CHUNKS + 1):
        pieces = []
        for j in range(max(n_up, n_down)):
            if c + 1 < N_FF_CHUNKS and j < n_up:
                pieces.append(functools.partial(up_piece, c + 1, j))
            if c >= 1 and j < n_down:
                pieces.append(functools.partial(down_piece, c - 1, j))
        if c < N_FF_CHUNKS:
            done = 0
            for i, r0 in enumerate(slabs):
                items.append(functools.partial(conv_act_slab, c, r0, i == len(slabs) - 1))
                upto = (i + 1) * len(pieces) // len(slabs)
                items += pieces[done:upto]
                done = upto
        else:
            items += pieces
    items += [functools.partial(norm_mid, r0) for r0 in slabs]
    items += [functools.partial(gate_piece, j) for j in range(n_down)]
    items += [functools.partial(finish, r0) for r0 in slabs]
    return embed, items


def _run(items):
    for item in items:
        item()


def _prompt_mixer_items(x_ref, h_mid_ref, n_ref, z_ref, xa_ref, xc_ref, xcb_ref, g_ref, a_ref, u_ref, hs_ref,
                        carry_ref, sp_ref, vn_ref, ymix_ref, g_mix, w_in, caw, cab, w_gate, ba, bx, apar,
                        g_oa, lng, lnb, sgw, sgb, g_ob, w_out, tm, reset_row, extra_matmul):
    assert tm % CHUNK == 0 and tm % SLAB == 0
    slabs = list(range(0, tm, SLAB))

    def norm_in(r0):
        n_ref[r0:r0 + SLAB, :] = _rms(x_ref[0, r0:r0 + SLAB, :], g_mix[...]).astype(BF16)

    def in_piece(j):
        c0 = j * MXU_PIECE
        res = _wdot(_fresh_rows(n_ref, 1 + j), w_in[:, c0:c0 + MXU_PIECE])
        if c0 < W_A:
            xa_ref[HIST:HIST + tm, c0:c0 + MXU_PIECE] = res
        else:
            z_ref[:, c0 - W_A:c0 - W_A + MXU_PIECE] = res

    def conv(r0):
        if r0 == 0:
            sp_ref[...] = _softplus(-apar[...])
        cv = _bc(cab[...], SLAB)
        for j in range(CONV_A):
            off = HIST + r0 - (CONV_A - 1 - j)
            cv = cv + xa_ref[off:off + SLAB, :] * _bc(caw[j], SLAB)
        xc_ref[r0:r0 + SLAB, :] = cv
        xcb_ref[r0:r0 + SLAB, :] = cv.astype(BF16)
        if r0 + SLAB == tm:
            xa_ref[0:HIST, :] = xa_ref[tm:tm + HIST, :]

    def gate_piece(hf):
        c0 = hf * GATE_HALF
        lhs = _fresh_rows(xcb_ref, 1 + hf, slice(c0, c0 + GATE_HALF))
        g_ref[:, 2 * c0:2 * c0 + 2 * GATE_HALF] = _wdot(lhs, w_gate[hf])

    def gate_ew(r0, hf):
        c0 = hf * GATE_HALF
        cs = slice(c0, c0 + GATE_HALF)
        r = _sigmoid(g_ref[r0:r0 + SLAB, 2 * c0:2 * c0 + GATE_HALF] + _bc(ba[:, cs], SLAB))
        i = _sigmoid(g_ref[r0:r0 + SLAB, 2 * c0 + GATE_HALF:2 * c0 + 2 * GATE_HALF] + _bc(bx[:, cs], SLAB))
        a = jnp.exp((-C_RG) * r * _bc(sp_ref[:, cs], SLAB))
        mult = jnp.sqrt(1.0 - a * a)
        if r0 == 0:
            row = lax.broadcasted_iota(jnp.int32, (SLAB, GATE_HALF), 0)
            mult = jnp.where(row == reset_row, 1.0, mult)
        a_ref[r0:r0 + SLAB, cs] = a
        u_ref[r0:r0 + SLAB, cs] = xc_ref[r0:r0 + SLAB, cs] * i * mult

    def scan_group(r0):
        row = lax.broadcasted_iota(jnp.int32, (SUBLANES, W_A), 0)
        a = a_ref[r0:r0 + SUBLANES, :]
        u = u_ref[r0:r0 + SUBLANES, :]
        for d in (1, 2, 4):
            keep = row >= d
            a_s = jnp.where(keep, pltpu.roll(a, d, 0), 1.0)
            u_s = jnp.where(keep, pltpu.roll(u, d, 0), 0.0)
            u = a * u_s + u
            a = a * a_s
        h = a * carry_ref[...] + u
        hs_ref[r0:r0 + SUBLANES, :] = h
        carry_ref[...] = jnp.broadcast_to(h[SUBLANES - 1:SUBLANES, :], (SUBLANES, W_A))

    def a_out(r0):
        ga = z_ref[r0:r0 + SLAB, 0:W_A]
        ymix_ref[r0:r0 + SLAB, 0:W_A] = _rms(hs_ref[r0:r0 + SLAB, :] * jax.nn.gelu(ga), g_oa[...]).astype(BF16)

    def vn(r0):
        vb = z_ref[r0:r0 + SLAB, W_A + W_B:W_A + 2 * W_B]
        vn_ref[r0:r0 + SLAB, :] = _layernorm(jax.nn.gelu(vb), lng[...], lnb[...]).astype(BF16)

    def sgu(c0):
        ti = lax.broadcasted_iota(jnp.int32, (CHUNK, CHUNK), 0)
        si = lax.broadcasted_iota(jnp.int32, (CHUNK, CHUNK), 1)
        heads = []
        for hd in range(H_B):
            cs = slice(hd * HD_B, (hd + 1) * HD_B)
            wmix = jnp.where(si <= ti, sgw[hd], 0.0).astype(BF16)
            mixed = _dot(wmix, vn_ref[c0:c0 + CHUNK, cs]) + sgb[:, cs]
            ub = z_ref[c0:c0 + CHUNK, W_A + hd * HD_B:W_A + (hd + 1) * HD_B]
            heads.append(jax.nn.gelu(ub) * mixed)
        yb = jnp.concatenate(heads, axis=-1)
        ymix_ref[c0:c0 + CHUNK, W_A:] = _rms(yb, g_ob[...]).astype(BF16)

    def out_piece(j):
        cs = slice(j * MXU_PIECE, (j + 1) * MXU_PIECE)
        h_mid_ref[:, cs] = x_ref[0, :, cs] + _wdot(_fresh_rows(ymix_ref, 1 + j), w_out[:, cs])

    def spread(work, matmuls):
        out, done = [], 0
        for i, item in enumerate(work):
            out.append(item)
            upto = (i + 1) * len(matmuls) // len(work)
            out += matmuls[done:upto]
            done = upto
        return out

    n_in = (2 * W_A + 2 * W_B) // MXU_PIECE
    n_a = W_A // MXU_PIECE
    later_in = [functools.partial(in_piece, j) for j in reversed(range(n_a, n_in))]
    items = [functools.partial(norm_in, r0) for r0 in slabs]
    items += [functools.partial(in_piece, j) for j in range(n_a)]
    items += spread([functools.partial(conv, r0) for r0 in slabs], later_in[:2])
    items += [functools.partial(gate_piece, hf) for hf in range(2)] + [extra_matmul]
    items += spread([functools.partial(gate_ew, r0, hf) for r0 in slabs for hf in range(2)], later_in[2:])
    items += [functools.partial(scan_group, r0) for r0 in range(0, tm, SUBLANES)]
    items += [functools.partial(a_out, r0) for r0 in slabs]
    items += [functools.partial(vn, r0) for r0 in slabs]
    items += [functools.partial(sgu, c0) for c0 in range(0, tm, CHUNK)]
    items += [functools.partial(out_piece, j) for j in range(D_MODEL // MXU_PIECE)]
    return items


def _prompt_kernel(x_ref, p_ref, vecs, w_in, w_gate, sgw, sgb, w_out, w_up_hbm, w_down_hbm, w_pg, w_ple,
                   y_ref, ht_ref, cnew_ref, fnew_ref,
                   h_mid_ref, n1_ref, z_ref, xa_ref, xc_ref, xcb_ref, g_ref, a_ref, u_ref, hs_ref, carry_ref,
                   sp_ref, vn_ref, ymix_ref, n2_ref, up_ref, upb_ref, fhist_ref, act_ref, actb_ref, acc_ref,
                   gz_ref, pe_ref, w_up, w_down, wsem, *, tm, layout):
    v = {name: _Vec(vecs, *spec) for name, spec in layout.items()}
    g_mix, caw, cab, ba, bx, apar = v['g_mix'], v['caw'], v['cab'], v['ba'], v['bx'], v['apar']
    g_oa, lng, lnb, g_ob = v['g_oa'], v['lng'], v['lnb'], v['g_ob']
    g_ffn, fcw, fcb, g_ple, g_fin = v['g_ffn'], v['fcw'], v['fcb'], v['g_ple'], v['g_fin']
    t = pl.program_id(1)
    first = jnp.logical_and(pl.program_id(0) == 0, t == 0)
    copies = _ffn_weight_copies((w_up_hbm, w_down_hbm), (w_up, w_down), wsem)

    @pl.when(first)
    def _():
        for c in copies:
            c.start()

    @pl.when(t == 0)
    def _():
        xa_ref[0:HIST, :] = jnp.zeros((HIST, W_A), F32)
        carry_ref[...] = jnp.zeros((SUBLANES, W_A), F32)
        fhist_ref[...] = jnp.zeros(fhist_ref.shape, F32)

    def load_hist(c, up):
        for half in range(2):
            up[0:HIST, half * FF_CHUNK:(half + 1) * FF_CHUNK] = fhist_ref[:, _ff_cols(c, half)]

    def store_hist(c, up):
        for half in range(2):
            cs = slice(half * FF_CHUNK, (half + 1) * FF_CHUNK)
            fhist_ref[:, _ff_cols(c, half)] = up[tm:tm + HIST, cs]
            fnew_ref[0, :, _ff_cols(c, half)] = up[HIST + tm - (CONV_F - 1):HIST + tm, cs]

    def write_y(r0, y):
        y_ref[0, r0:r0 + SLAB, :] = y

    embed, ffn = _ffn_items(h_mid_ref, n2_ref, (up_ref, upb_ref), (act_ref, actb_ref), acc_ref, gz_ref, pe_ref,
                            lambda: p_ref[0], g_ffn, w_up, fcw, fcb, w_down, g_ple, w_pg, w_ple, g_fin, tm, 1,
                            HIST, load_hist, store_hist, write_y)
    reset_row = jnp.where(t == 0, 0, -1)
    mixer = _prompt_mixer_items(x_ref, h_mid_ref, n1_ref, z_ref, xa_ref, xc_ref, xcb_ref, g_ref, a_ref, u_ref,
                                hs_ref, carry_ref, sp_ref, vn_ref, ymix_ref, g_mix, w_in, caw, cab, w_gate,
                                ba, bx, apar, g_oa, lng, lnb, sgw, sgb, g_ob, w_out, tm, reset_row, embed)
    _run(mixer)

    @pl.when(first)
    def _():
        for c in copies:
            c.wait()

    _run(ffn)

    ht_ref[0] = carry_ref[0:1, :]
    cnew_ref[0] = xa_ref[HIST + tm - (CONV_A - 1):HIST + tm, :]


def _sample_kernel(x_ref, p_ref, h0_ref, chist_ref, fh_ref, vecs, w_in, w_gate, w_out, w_up_hbm, w_down_hbm,
                   w_pg, w_ple,
                   y_ref, ht_ref, cnew_ref, vn_out_ref, fnew_ref,
                   n_ref, z_ref, xa_ref, xc_ref, xcb_ref, g_ref, a_ref, u_ref, hs_ref, ymix_ref,
                   h_ref, p_buf, up_ref, upb_ref, act_ref, actb_ref, acc_ref, w_up, w_down, wsem,
                   *, nb, steps, layout):
    v = {name: _Vec(vecs, *spec) for name, spec in layout.items()}
    g_mix, caw, cab, ba, bx, apar = v['g_mix'], v['caw'], v['cab'], v['ba'], v['bx'], v['apar']
    g_oa, lng, lnb, sgw, sgb, g_ob = v['g_oa'], v['lng'], v['lnb'], v['sgw'], v['sgb'], v['g_ob']
    g_ffn, fcw, fcb, g_ple, g_fin = v['g_ffn'], v['fcw'], v['fcb'], v['g_ple'], v['g_fin']
    rows = nb * steps
    ahist = (CONV_A - 1) * nb
    fhist = (CONV_F - 1) * nb
    first = pl.program_id(0) == 0
    copies = _ffn_weight_copies((w_up_hbm, w_down_hbm), (w_up, w_down), wsem)

    @pl.when(first)
    def _():
        for c in copies:
            c.start()

    for s in range(steps):
        h_ref[s * nb:(s + 1) * nb, :] = x_ref[:, s, :]
        p_buf[s * nb:(s + 1) * nb, :] = p_ref[:, s, :]
    _norm_to_bf16(h_ref, g_mix, n_ref, rows)
    for k in range(CONV_A - 1):
        xa_ref[k * nb:(k + 1) * nb, :] = chist_ref[:, k, :]
    xa_ref[ahist:ahist + rows, :] = _wdot(n_ref[...], w_in[:, 0:W_A])
    z_ref[...] = _wdot(n_ref[...], w_in[:, W_A:])

    for r0 in range(0, rows, SLAB):
        cv = _bc(cab[...], SLAB)
        for j in range(CONV_A):
            off = r0 + j * nb
            cv = cv + xa_ref[off:off + SLAB, :] * _bc(caw[j], SLAB)
        xc_ref[r0:r0 + SLAB, :] = cv
        xcb_ref[r0:r0 + SLAB, :] = cv.astype(BF16)
    for k in range(CONV_A - 1):
        cnew_ref[:, k, :] = xa_ref[rows + k * nb:rows + (k + 1) * nb, :]

    _gate_stage(xc_ref, xcb_ref, g_ref, a_ref, u_ref, w_gate, ba, bx, apar, rows)

    for b0 in range(0, nb, SLAB):
        h = h0_ref[b0:b0 + SLAB, :]
        for s in range(steps):
            r0 = s * nb + b0
            h = a_ref[r0:r0 + SLAB, :] * h + u_ref[r0:r0 + SLAB, :]
            hs_ref[r0:r0 + SLAB, :] = h
        ht_ref[b0:b0 + SLAB, :] = h

    _branch_a_out(hs_ref, z_ref, g_oa, ymix_ref, rows)

    def write_vn(r0, vn):
        vn_out_ref[r0 % nb:r0 % nb + SLAB, r0 // nb, :] = vn

    _vn_stage(z_ref, lng, lnb, rows, write_vn)

    gob = g_ob[...]
    for b0 in range(0, nb, SLAB):
        for tt in range(steps):
            r0 = tt * nb + b0
            mixed = _bc(sgb[tt], SLAB)
            for s in range(tt + 1):
                k = tt * steps + s
                mixed = mixed + vn_out_ref[b0:b0 + SLAB, s, :] * _bc(sgw[k], SLAB)
            yb = jax.nn.gelu(z_ref[r0:r0 + SLAB, W_A:W_A + W_B]) * mixed
            ymix_ref[r0:r0 + SLAB, W_A:] = _rms(yb, gob).astype(BF16)

    h_ref[...] += _wdot(ymix_ref[...], w_out[...])

    @pl.when(first)
    def _():
        for c in copies:
            c.wait()

    def load_hist(c, up):
        for k in range(CONV_F - 1):
            for half in range(2):
                cs = slice(half * FF_CHUNK, (half + 1) * FF_CHUNK)
                up[k * nb:(k + 1) * nb, cs] = fh_ref[:, k, _ff_cols(c, half)]

    def store_hist(c, up):
        for k in range(CONV_F - 1):
            for half in range(2):
                cs = slice(half * FF_CHUNK, (half + 1) * FF_CHUNK)
                fnew_ref[:, k, _ff_cols(c, half)] = up[rows + k * nb:rows + (k + 1) * nb, cs]

    def write_y(r0, y):
        y_ref[r0 % nb:r0 % nb + SLAB, r0 // nb, :] = y

    embed, items = _ffn_items(h_ref, n_ref, (up_ref, upb_ref), (act_ref, actb_ref), acc_ref, z_ref, g_ref,
                              lambda: p_buf[...], g_ffn, w_up, fcw, fcb, w_down, g_ple, w_pg, w_ple, g_fin,
                              rows, nb, fhist, load_hist, store_hist, write_y)
    _run([embed] + items)


def _ffn_weight_copies(srcs, dsts, sem):
    return [pltpu.make_async_copy(src, dst, sem.at[i]) for i, (src, dst) in enumerate(zip(srcs, dsts))]


def _const_spec(shape):
    zeros = (0,) * len(shape)
    return pl.BlockSpec(shape, lambda *_: zeros, pipeline_mode=pl.Buffered(1))


def _pack_bf16_rows(weights):
    flat = [w.reshape((-1, w.shape[-1])) for w in weights]
    for w, w2 in zip(weights, flat):
        assert w.shape[-2] % 2 == 0 and w2.shape[0] % (PACK_STEPS * 2 * SUBLANES) == 0
    words = pl.pallas_call(
        _pack_kernel,
        grid=(PACK_STEPS,),
        in_specs=[pl.BlockSpec((w2.shape[0] // PACK_STEPS, w2.shape[1]), lambda i: (i, 0)) for w2 in flat],
        out_specs=[pl.BlockSpec((w2.shape[0] // PACK_STEPS // 2, w2.shape[1]), lambda i: (i, 0)) for w2 in flat],
        out_shape=[jax.ShapeDtypeStruct((w2.shape[0] // 2, w2.shape[1]), jnp.uint32) for w2 in flat],
        compiler_params=pltpu.CompilerParams(vmem_limit_bytes=VMEM_LIMIT_BYTES),
        name="pack_weights",
    )(*flat)
    return [o.reshape(w.shape[:-2] + (w.shape[-2] // 2, w.shape[-1])) for o, w in zip(words, weights)]


def _pack_kernel(*refs):
    n = len(refs) // 2
    for w_ref, o_ref in zip(refs[:n], refs[n:]):
        o_ref[...] = pltpu.bitcast(w_ref[...].astype(BF16), jnp.uint32)


def _block_diag_gate(wa, wx):
    hh = H_A // 2
    eye = jnp.eye(hh, dtype=wa.dtype)

    def bd(w):
        return jnp.einsum('hij,hg->higj', w, eye).reshape(GATE_HALF, GATE_HALF)

    return jnp.stack([jnp.concatenate([bd(wa[h * hh:(h + 1) * hh]), bd(wx[h * hh:(h + 1) * hh])], axis=1)
                      for h in range(2)])


def kernel(x_prompt, x_sample, p_prompt, p_sample, state_rglru_h, state_rglru_conv, state_ffn_conv, g_mix_norm, w_in, conv_a_w, conv_a_b, lru_wa, lru_ba, lru_wx, lru_bx, lru_a_param, g_out_a, ln_v_g, ln_v_b, sgu_w, sgu_b, g_out_b, w_out, g_ffn_norm, w_up, ffn_conv_w, ffn_conv_b, w_down, g_ple_norm, w_ple_gate, w_ple, g_final):
    assert w_in.shape[0] == 1
    nbp, seq, _ = x_prompt.shape
    nbs, steps, _ = x_sample.shape
    ple = p_prompt.shape[-1]
    tm = PROMPT_BLOCK_ROWS
    nb = SAMPLE_GROUP
    assert seq % tm == 0 and tm % CHUNK == 0 and nbs % nb == 0 and nb % SLAB == 0 and steps <= CHUNK
    common = [('g_mix', g_mix_norm[0]), ('caw', conv_a_w[0]), ('cab', conv_a_b[0]), ('ba', lru_ba[0]),
              ('bx', lru_bx[0]), ('apar', lru_a_param[0]), ('g_oa', g_out_a[0]), ('lng', ln_v_g[0]),
              ('lnb', ln_v_b[0]), ('g_ob', g_out_b[0]), ('g_ffn', g_ffn_norm[0]), ('fcw', ffn_conv_w[0]),
              ('fcb', ffn_conv_b[0]), ('g_ple', g_ple_norm[0]), ('g_fin', g_final)]
    packed = _pack_bf16_rows([w_in[0], _block_diag_gate(lru_wa[0], lru_wx[0]), w_out[0], w_up[0], w_down[0],
                               w_ple_gate[0], w_ple[0]])
    mats, mats_tail = packed[:2], packed[2:]
    w_up_words, w_down_words = packed[3], packed[4]
    ffn_w_scratch = [pltpu.VMEM(w_up_words.shape, jnp.uint32), pltpu.VMEM(w_down_words.shape, jnp.uint32),
                     pltpu.SemaphoreType.DMA((2,))]

    def w_spec(a):
        if a is w_up_words or a is w_down_words:
            return pl.BlockSpec(memory_space=pltpu.HBM)
        return _const_spec(a.shape)

    sgb_p = jnp.repeat(jnp.transpose(sgu_b[0]), HD_B, axis=1)
    vecs_p, layout_p = _pack_vectors(common)
    p_args = [vecs_p] + mats + [sgu_w[0], sgb_p] + mats_tail
    scratch_p = [
        pltpu.VMEM((tm, D_MODEL), F32),
        pltpu.VMEM((tm, D_MODEL), BF16),
        pltpu.VMEM((tm, D_MODEL + W_A), F32),
        pltpu.VMEM((HIST + tm, W_A), F32),
        pltpu.VMEM((tm, W_A), F32),
        pltpu.VMEM((tm, W_A), BF16),
        pltpu.VMEM((tm, 2 * W_A), F32),
        pltpu.VMEM((tm, W_A), F32),
        pltpu.VMEM((tm, W_A), F32),
        pltpu.VMEM((tm, W_A), F32),
        pltpu.VMEM((SUBLANES, W_A), F32),
        pltpu.VMEM((SUBLANES, W_A), F32),
        pltpu.VMEM((tm, W_B), BF16),
        pltpu.VMEM((tm, W_A + W_B), BF16),
        pltpu.VMEM((tm, D_MODEL), BF16),
        pltpu.VMEM((HIST + tm, 2 * FF_CHUNK), F32),
        pltpu.VMEM((HIST + tm, 2 * FF_CHUNK), F32),
        pltpu.VMEM((HIST, 2 * D_FF), F32),
        pltpu.VMEM((tm, FF_CHUNK), BF16),
        pltpu.VMEM((tm, FF_CHUNK), BF16),
        pltpu.VMEM((tm, D_MODEL), F32),
        pltpu.VMEM((tm, D_MODEL), F32),
        pltpu.VMEM((tm, D_MODEL), F32),
    ] + ffn_w_scratch
    y_p, ht_p, cnew_p, fnew_p = pl.pallas_call(
        functools.partial(_prompt_kernel, tm=tm, layout=layout_p),
        grid=(nbp, seq // tm),
        in_specs=[pl.BlockSpec((1, tm, D_MODEL), lambda b, t: (b, t, 0)),
                  pl.BlockSpec((1, tm, ple), lambda b, t: (b, t, 0))]
                 + [w_spec(a) for a in p_args],
        out_specs=[pl.BlockSpec((1, tm, D_MODEL), lambda b, t: (b, t, 0)),
                   pl.BlockSpec((1, 1, W_A), lambda b, t: (b, 0, 0)),
                   pl.BlockSpec((1, CONV_A - 1, W_A), lambda b, t: (b, 0, 0)),
                   pl.BlockSpec((1, CONV_F - 1, 2 * D_FF), lambda b, t: (b, 0, 0))],
        out_shape=[jax.ShapeDtypeStruct((nbp, seq, D_MODEL), F32),
                   jax.ShapeDtypeStruct((nbp, 1, W_A), F32),
                   jax.ShapeDtypeStruct((nbp, CONV_A - 1, W_A), F32),
                   jax.ShapeDtypeStruct((nbp, CONV_F - 1, 2 * D_FF), F32)],
        scratch_shapes=scratch_p,
        compiler_params=pltpu.CompilerParams(dimension_semantics=("arbitrary", "arbitrary"),
                                             vmem_limit_bytes=VMEM_LIMIT_BYTES),
        name="prompt_layer",
    )(x_prompt, p_prompt[0], *p_args)

    rows = nb * steps
    ahist = (CONV_A - 1) * nb
    fhist = (CONV_F - 1) * nb
    tmaj = lambda a: jnp.swapaxes(a, 0, 1)
    sgw_s = jnp.repeat(jnp.transpose(sgu_w[0, :, :steps, :steps], (1, 2, 0)).reshape(steps * steps, H_B),
                       HD_B, axis=1)
    sgb_s = jnp.repeat(jnp.transpose(sgu_b[0, :, :steps]), HD_B, axis=1)
    vecs_s, layout_s = _pack_vectors(common + [('sgw', sgw_s), ('sgb', sgb_s)])
    w_args = [vecs_s] + mats + mats_tail
    scratch_s = [
        pltpu.VMEM((rows, D_MODEL), BF16),
        pltpu.VMEM((rows, D_MODEL + W_A), F32),
        pltpu.VMEM((ahist + rows, W_A), F32),
        pltpu.VMEM((rows, W_A), F32),
        pltpu.VMEM((rows, W_A), BF16),
        pltpu.VMEM((rows, 2 * W_A), F32),
        pltpu.VMEM((rows, W_A), F32),
        pltpu.VMEM((rows, W_A), F32),
        pltpu.VMEM((rows, W_A), F32),
        pltpu.VMEM((rows, W_A + W_B), BF16),
        pltpu.VMEM((rows, D_MODEL), F32),
        pltpu.VMEM((rows, ple), F32),
        pltpu.VMEM((fhist + rows, 2 * FF_CHUNK), F32),
        pltpu.VMEM((fhist + rows, 2 * FF_CHUNK), F32),
        pltpu.VMEM((rows, FF_CHUNK), BF16),
        pltpu.VMEM((rows, FF_CHUNK), BF16),
        pltpu.VMEM((rows, D_MODEL), F32),
    ] + ffn_w_scratch
    y_s, ht_s, cnew_s, vn_s, fnew_s = pl.pallas_call(
        functools.partial(_sample_kernel, nb=nb, steps=steps, layout=layout_s),
        grid=(nbs // nb,),
        in_specs=[pl.BlockSpec((nb, steps, D_MODEL), lambda i: (i, 0, 0)),
                  pl.BlockSpec((nb, steps, ple), lambda i: (i, 0, 0)),
                  pl.BlockSpec((nb, W_A), lambda i: (i, 0)),
                  pl.BlockSpec((nb, CONV_A - 1, W_A), lambda i: (i, 0, 0)),
                  pl.BlockSpec((nb, CONV_F - 1, 2 * D_FF), lambda i: (i, 0, 0))]
                 + [w_spec(a) for a in w_args],
        out_specs=[pl.BlockSpec((nb, steps, D_MODEL), lambda i: (i, 0, 0)),
                   pl.BlockSpec((nb, W_A), lambda i: (i, 0)),
                   pl.BlockSpec((nb, CONV_A - 1, W_A), lambda i: (i, 0, 0)),
                   pl.BlockSpec((nb, steps, W_B), lambda i: (i, 0, 0)),
                   pl.BlockSpec((nb, CONV_F - 1, 2 * D_FF), lambda i: (i, 0, 0))],
        out_shape=[jax.ShapeDtypeStruct((nbs, steps, D_MODEL), F32),
                   jax.ShapeDtypeStruct((nbs, W_A), F32),
                   jax.ShapeDtypeStruct((nbs, CONV_A - 1, W_A), F32),
                   jax.ShapeDtypeStruct((nbs, steps, W_B), F32),
                   jax.ShapeDtypeStruct((nbs, CONV_F - 1, 2 * D_FF), F32)],
        scratch_shapes=scratch_s,
        compiler_params=pltpu.CompilerParams(dimension_semantics=("arbitrary",),
                                             vmem_limit_bytes=VMEM_LIMIT_BYTES),
        name="sample_layer",
    )(x_sample, p_sample[0], state_rglru_h[0], state_rglru_conv[0], state_ffn_conv[0], *w_args)

    return (y_p, y_s, tmaj(ht_p), ht_s[None], cnew_p[None], cnew_s[None], vn_s[None], fnew_p[None], fnew_s[None])
```

```python
import functools

import jax
import jax.numpy as jnp
from jax import lax
from jax.experimental import pallas as pl
from jax.experimental.pallas import tpu as pltpu

F32 = jnp.float32
BF16 = jnp.bfloat16

D_MODEL = 1024
W_A = 512
W_B = 512
H_A = 8
BW_A = W_A // H_A
H_B = 4
HD_B = W_B // H_B
CHUNK = 128
D_FF = 3072
CONV_A = 4
CONV_F = 3
C_RG = 8.0
EPS = 1e-6

SUBLANES = 8
HIST = SUBLANES
GATE_HALF = W_A // 2
FF_CHUNK = 512
N_FF_CHUNKS = D_FF // FF_CHUNK
SLAB = 32
MXU_PIECE = 256
PACK_STEPS = 8
PROMPT_BLOCK_ROWS = 256
SAMPLE_GROUP = 32
VMEM_LIMIT_BYTES = 56 * 1024 * 1024


def _bc(tile, rows):
    return jnp.concatenate([tile] * (rows // SUBLANES), axis=0)


class _Vec:
    def __init__(self, ref, off, width, n):
        self.ref, self.off, self.width, self.n = ref, off, width, n

    def __getitem__(self, idx):
        full = slice(0, self.width)
        if self.n is None:
            j, cols = 0, (full if idx is Ellipsis else idx[1])
        else:
            j, cols = (idx, full) if isinstance(idx, int) else (idx[0], idx[2])
        start = self.off + j * self.width
        return self.ref[:, start + cols.start:start + cols.stop]


def _vector_layout(named):
    layout, off = {}, 0
    for name, a in named:
        n, width = (None, a.shape[0]) if a.ndim == 1 else a.shape
        layout[name] = (off, width, n)
        off += a.size
    return layout, off


def _rms(x, g):
    ms = jnp.mean(x * x, axis=-1, keepdims=True)
    return x * lax.rsqrt(ms + EPS) * _bc(g, x.shape[0])


def _layernorm(x, g, b):
    mu = jnp.mean(x, axis=-1, keepdims=True)
    xc = x - mu
    rows = x.shape[0]
    return xc * lax.rsqrt(jnp.mean(xc * xc, axis=-1, keepdims=True) + EPS) * _bc(g, rows) + _bc(b, rows)


def _sigmoid(x):
    return 1.0 / (1.0 + jnp.exp(-x))


def _softplus(x):
    return jnp.maximum(x, 0.0) + jnp.log(1.0 + jnp.exp(-jnp.abs(x)))


def _dot(a, b):
    return jnp.dot(a, b, preferred_element_type=F32)


def _fresh_rows(ref, salt, cols=slice(None)):
    zero = jnp.minimum(pl.program_id(0), 0) * salt
    return ref[pl.ds(pl.multiple_of(zero, 16), ref.shape[0]), cols]


def _wdot(a, w_words):
    return _dot(a, pltpu.bitcast(w_words, BF16))


def _ff_cols(c, half):
    start = half * D_FF + c * FF_CHUNK
    return slice(start, start + FF_CHUNK)


def _norm_to_bf16(src_ref, g_ref, dst_ref, rows):
    g = g_ref[...]
    for r0 in range(0, rows, SLAB):
        dst_ref[r0:r0 + SLAB, :] = _rms(src_ref[r0:r0 + SLAB, :], g).astype(BF16)


def _gate_stage(xc_ref, xcb_ref, g_ref, a_ref, u_ref, w_gate, ba, bx, apar, rows):
    for hf in range(2):
        c0 = hf * GATE_HALF
        g_ref[:, 2 * c0:2 * c0 + 2 * GATE_HALF] = _wdot(xcb_ref[:, c0:c0 + GATE_HALF], w_gate[hf])
    sp = _softplus(-apar[...])
    for r0 in range(0, rows, SLAB):
        for hf in range(2):
            c0 = hf * GATE_HALF
            cs = slice(c0, c0 + GATE_HALF)
            r = _sigmoid(g_ref[r0:r0 + SLAB, 2 * c0:2 * c0 + GATE_HALF] + _bc(ba[:, cs], SLAB))
            i = _sigmoid(g_ref[r0:r0 + SLAB, 2 * c0 + GATE_HALF:2 * c0 + 2 * GATE_HALF] + _bc(bx[:, cs], SLAB))
            a = jnp.exp((-C_RG) * r * _bc(sp[:, cs], SLAB))
            mult = jnp.sqrt(1.0 - a * a)
            a_ref[r0:r0 + SLAB, cs] = a
            u_ref[r0:r0 + SLAB, cs] = xc_ref[r0:r0 + SLAB, cs] * i * mult


def _branch_a_out(hs_ref, z_ref, g_oa, ymix_ref, rows):
    g = g_oa[...]
    for r0 in range(0, rows, SLAB):
        ga = z_ref[r0:r0 + SLAB, 0:W_A]
        ymix_ref[r0:r0 + SLAB, 0:W_A] = _rms(hs_ref[r0:r0 + SLAB, :] * jax.nn.gelu(ga), g).astype(BF16)


def _vn_stage(z_ref, lng, lnb, rows, write):
    g = lng[...]
    b = lnb[...]
    for r0 in range(0, rows, SLAB):
        vb = z_ref[r0:r0 + SLAB, W_A + W_B:W_A + 2 * W_B]
        write(r0, _layernorm(jax.nn.gelu(vb), g, b))


def _ffn_items(h_ref, n_ref, up2_ref, act2_ref, acc_ref, gz_ref, pe_ref, load_p, g_ffn, w_up, fcw, fcb,
               w_down, g_ple, w_pg, w_ple, g_fin, rows, shift, hist_rows, load_hist, store_hist, write_y):
    n_up = 2 * FF_CHUNK // MXU_PIECE
    n_down = D_MODEL // MXU_PIECE
    slabs = list(range(0, rows, SLAB))
    items = []

    def norm_in(r0):
        h = h_ref[r0:r0 + SLAB, :]
        acc_ref[r0:r0 + SLAB, :] = h
        n_ref[r0:r0 + SLAB, :] = _rms(h, g_ffn[...]).astype(BF16)

    def up_piece(c, j):
        up_ref = up2_ref[c % 2]
        cs = slice(j * MXU_PIECE, (j + 1) * MXU_PIECE)
        half, off = divmod(j * MXU_PIECE, FF_CHUNK)
        w0 = half * D_FF + c * FF_CHUNK + off
        if j == 0:
            load_hist(c, up_ref)
        lhs = _fresh_rows(n_ref, 1 + c * n_up + j)
        up_ref[hist_rows:hist_rows + rows, cs] = _wdot(lhs, w_up[:, w0:w0 + MXU_PIECE])

    def down_piece(c, j):
        cs = slice(j * MXU_PIECE, (j + 1) * MXU_PIECE)
        lhs = _fresh_rows(act2_ref[c % 2], 1 + c * n_down + j)
        acc_ref[:, cs] += _wdot(lhs, w_down[c * FF_CHUNK // 2:(c + 1) * FF_CHUNK // 2, cs])

    def conv_act_slab(c, r0, last):
        up_ref = up2_ref[c % 2]
        halves = []
        for half in range(2):
            cs = slice(half * FF_CHUNK, (half + 1) * FF_CHUNK)
            ws = _ff_cols(c, half)
            cv = _bc(fcb[:, ws], SLAB)
            for j in range(CONV_F):
                off = hist_rows + r0 - (CONV_F - 1 - j) * shift
                cv = cv + up_ref[off:off + SLAB, cs] * _bc(fcw[j, :, ws], SLAB)
            halves.append(cv)
        act2_ref[c % 2][r0:r0 + SLAB, :] = (jax.nn.gelu(halves[0]) * halves[1]).astype(BF16)
        if last:
            store_hist(c, up_ref)

    def embed():
        pe_ref[...] = _wdot(load_p().astype(BF16), w_ple[...])

    def norm_mid(r0):
        n_ref[r0:r0 + SLAB, :] = _rms(acc_ref[r0:r0 + SLAB, :], g_ple[...]).astype(BF16)

    def gate_piece(j):
        cs = slice(j * MXU_PIECE, (j + 1) * MXU_PIECE)
        gz_ref[:, cs] = _wdot(_fresh_rows(n_ref, 1 + N_FF_CHUNKS * n_up + j), w_pg[:, cs])

    def finish(r0):
        gate = _sigmoid(gz_ref[r0:r0 + SLAB, 0:D_MODEL])
        h3 = acc_ref[r0:r0 + SLAB, :] + pe_ref[r0:r0 + SLAB, :] * gate
        write_y(r0, _rms(h3, g_fin[...]))

    items += [functools.partial(norm_in, r0) for r0 in slabs]
    items += [functools.partial(up_piece, 0, j) for j in range(n_up)]
    for c in range(N_FF_CHUNKS + 1):
        pieces = []
        for j in range(max(n_up, n_down)):
            if c + 1 < N_FF_CHUNKS and j < n_up:
                pieces.append(functools.partial(up_piece, c + 1, j))
            if c >= 1 and j < n_down:
                pieces.append(functools.partial(down_piece, c - 1, j))
        if c < N_FF_CHUNKS:
            done = 0
            for i, r0 in enumerate(slabs):
                items.append(functools.partial(conv_act_slab, c, r0, i == len(slabs) - 1))
                upto = (i + 1) * len(pieces) // len(slabs)
                items += pieces[done:upto]
                done = upto
        else:
            items += pieces
    items += [functools.partial(norm_mid, r0) for r0 in slabs]
    items += [functools.partial(gate_piece, j) for j in range(n_down)]
    items += [functools.partial(finish, r0) for r0 in slabs]
    return embed, items


def _run(items):
    for item in items:
        item()


def _prompt_mixer_items(x_ref, h_mid_ref, n_ref, z_ref, xa_ref, xc_ref, xcb_ref, g_ref, a_ref, u_ref, hs_ref,
                        carry_ref, sp_ref, vn_ref, ymix_ref, g_mix, w_in, caw, cab, w_gate, ba, bx, apar,
                        g_oa, lng, lnb, sgw, sgb, g_ob, w_out, tm, reset_row, extra_matmul):
    assert tm % CHUNK == 0 and tm % SLAB == 0
    slabs = list(range(0, tm, SLAB))

    def norm_in(r0):
        n_ref[r0:r0 + SLAB, :] = _rms(x_ref[0, r0:r0 + SLAB, :], g_mix[...]).astype(BF16)

    def in_piece(j):
        c0 = j * MXU_PIECE
        res = _wdot(_fresh_rows(n_ref, 1 + j), w_in[:, c0:c0 + MXU_PIECE])
        if c0 < W_A:
            xa_ref[HIST:HIST + tm, c0:c0 + MXU_PIECE] = res
        else:
            z_ref[:, c0 - W_A:c0 - W_A + MXU_PIECE] = res

    def conv(r0):
        if r0 == 0:
            sp_ref[...] = _softplus(-apar[...])
        cv = _bc(cab[...], SLAB)
        for j in range(CONV_A):
            off = HIST + r0 - (CONV_A - 1 - j)
            cv = cv + xa_ref[off:off + SLAB, :] * _bc(caw[j], SLAB)
        xc_ref[r0:r0 + SLAB, :] = cv
        xcb_ref[r0:r0 + SLAB, :] = cv.astype(BF16)
        if r0 + SLAB == tm:
            xa_ref[0:HIST, :] = xa_ref[tm:tm + HIST, :]

    def gate_piece(hf):
        c0 = hf * GATE_HALF
        lhs = _fresh_rows(xcb_ref, 1 + hf, slice(c0, c0 + GATE_HALF))
        g_ref[:, 2 * c0:2 * c0 + 2 * GATE_HALF] = _wdot(lhs, w_gate[hf])

    def gate_ew(r0, hf):
        c0 = hf * GATE_HALF
        cs = slice(c0, c0 + GATE_HALF)
        r = _sigmoid(g_ref[r0:r0 + SLAB, 2 * c0:2 * c0 + GATE_HALF] + _bc(ba[:, cs], SLAB))
        i = _sigmoid(g_ref[r0:r0 + SLAB, 2 * c0 + GATE_HALF:2 * c0 + 2 * GATE_HALF] + _bc(bx[:, cs], SLAB))
        a = jnp.exp((-C_RG) * r * _bc(sp_ref[:, cs], SLAB))
        mult = jnp.sqrt(1.0 - a * a)
        if r0 == 0:
            row = lax.broadcasted_iota(jnp.int32, (SLAB, GATE_HALF), 0)
            mult = jnp.where(row == reset_row, 1.0, mult)
        a_ref[r0:r0 + SLAB, cs] = a
        u_ref[r0:r0 + SLAB, cs] = xc_ref[r0:r0 + SLAB, cs] * i * mult

    def scan_group(r0):
        row = lax.broadcasted_iota(jnp.int32, (SUBLANES, W_A), 0)
        a = a_ref[r0:r0 + SUBLANES, :]
        u = u_ref[r0:r0 + SUBLANES, :]
        for d in (1, 2, 4):
            keep = row >= d
            a_s = jnp.where(keep, pltpu.roll(a, d, 0), 1.0)
            u_s = jnp.where(keep, pltpu.roll(u, d, 0), 0.0)
            u = a * u_s + u
            a = a * a_s
        h = a * carry_ref[...] + u
        hs_ref[r0:r0 + SUBLANES, :] = h
        carry_ref[...] = jnp.broadcast_to(h[SUBLANES - 1:SUBLANES, :], (SUBLANES, W_A))

    def a_out(r0):
        ga = z_ref[r0:r0 + SLAB, 0:W_A]
        ymix_ref[r0:r0 + SLAB, 0:W_A] = _rms(hs_ref[r0:r0 + SLAB, :] * jax.nn.gelu(ga), g_oa[...]).astype(BF16)

    def vn(r0):
        vb = z_ref[r0:r0 + SLAB, W_A + W_B:W_A + 2 * W_B]
        vn_ref[r0:r0 + SLAB, :] = _layernorm(jax.nn.gelu(vb), lng[...], lnb[...]).astype(BF16)

    def sgu(c0):
        ti = lax.broadcasted_iota(jnp.int32, (CHUNK, CHUNK), 0)
        si = lax.broadcasted_iota(jnp.int32, (CHUNK, CHUNK), 1)
        heads = []
        for hd in range(H_B):
            cs = slice(hd * HD_B, (hd + 1) * HD_B)
            wmix = jnp.where(si <= ti, sgw[hd], 0.0).astype(BF16)
            mixed = _dot(wmix, vn_ref[c0:c0 + CHUNK, cs]) + sgb[:, cs]
            ub = z_ref[c0:c0 + CHUNK, W_A + hd * HD_B:W_A + (hd + 1) * HD_B]
            heads.append(jax.nn.gelu(ub) * mixed)
        yb = jnp.concatenate(heads, axis=-1)
        ymix_ref[c0:c0 + CHUNK, W_A:] = _rms(yb, g_ob[...]).astype(BF16)

    def out_piece(j):
        cs = slice(j * MXU_PIECE, (j + 1) * MXU_PIECE)
        h_mid_ref[:, cs] = x_ref[0, :, cs] + _wdot(_fresh_rows(ymix_ref, 1 + j), w_out[:, cs])

    def spread(work, matmuls):
        out, done = [], 0
        for i, item in enumerate(work):
            out.append(item)
            upto = (i + 1) * len(matmuls) // len(work)
            out += matmuls[done:upto]
            done = upto
        return out

    n_in = (2 * W_A + 2 * W_B) // MXU_PIECE
    n_a = W_A // MXU_PIECE
    later_in = [functools.partial(in_piece, j) for j in reversed(range(n_a, n_in))]
    items = [functools.partial(norm_in, r0) for r0 in slabs]
    items += [functools.partial(in_piece, j) for j in range(n_a)]
    items += spread([functools.partial(conv, r0) for r0 in slabs], later_in[:2])
    items += [functools.partial(gate_piece, hf) for hf in range(2)] + [extra_matmul]
    items += spread([functools.partial(gate_ew, r0, hf) for r0 in slabs for hf in range(2)], later_in[2:])
    items += [functools.partial(scan_group, r0) for r0 in range(0, tm, SUBLANES)]
    items += [functools.partial(a_out, r0) for r0 in slabs]
    items += [functools.partial(vn, r0) for r0 in slabs]
    items += [functools.partial(sgu, c0) for c0 in range(0, tm, CHUNK)]
    items += [functools.partial(out_piece, j) for j in range(D_MODEL // MXU_PIECE)]
    return items


def _prompt_kernel(x_ref, p_ref, vecs, w_in, w_gate, sgw, sgb, w_out, w_up_hbm, w_down_hbm, w_pg, w_ple,
                   y_ref, ht_ref, cnew_ref, fnew_ref,
                   h_mid_ref, n1_ref, z_ref, xa_ref, xc_ref, xcb_ref, g_ref, a_ref, u_ref, hs_ref, carry_ref,
                   sp_ref, vn_ref, ymix_ref, n2_ref, up_ref, upb_ref, fhist_ref, act_ref, actb_ref, acc_ref,
                   gz_ref, pe_ref, w_up, w_down, wsem, *, tm, layout):
    v = {name: _Vec(vecs, *spec) for name, spec in layout.items()}
    g_mix, caw, cab, ba, bx, apar = v['g_mix'], v['caw'], v['cab'], v['ba'], v['bx'], v['apar']
    g_oa, lng, lnb, g_ob = v['g_oa'], v['lng'], v['lnb'], v['g_ob']
    g_ffn, fcw, fcb, g_ple, g_fin = v['g_ffn'], v['fcw'], v['fcb'], v['g_ple'], v['g_fin']
    t = pl.program_id(1)
    first = jnp.logical_and(pl.program_id(0) == 0, t == 0)
    copies = _ffn_weight_copies((w_up_hbm, w_down_hbm), (w_up, w_down), wsem)

    @pl.when(first)
    def _():
        for c in copies:
            c.start()

    @pl.when(t == 0)
    def _():
        xa_ref[0:HIST, :] = jnp.zeros((HIST, W_A), F32)
        carry_ref[...] = jnp.zeros((SUBLANES, W_A), F32)
        fhist_ref[...] = jnp.zeros(fhist_ref.shape, F32)

    def load_hist(c, up):
        for half in range(2):
            up[0:HIST, half * FF_CHUNK:(half + 1) * FF_CHUNK] = fhist_ref[:, _ff_cols(c, half)]

    def store_hist(c, up):
        for half in range(2):
            cs = slice(half * FF_CHUNK, (half + 1) * FF_CHUNK)
            fhist_ref[:, _ff_cols(c, half)] = up[tm:tm + HIST, cs]
            fnew_ref[0, :, _ff_cols(c, half)] = up[HIST + tm - (CONV_F - 1):HIST + tm, cs]

    def write_y(r0, y):
        y_ref[0, r0:r0 + SLAB, :] = y

    embed, ffn = _ffn_items(h_mid_ref, n2_ref, (up_ref, upb_ref), (act_ref, actb_ref), acc_ref, gz_ref, pe_ref,
                            lambda: p_ref[0], g_ffn, w_up, fcw, fcb, w_down, g_ple, w_pg, w_ple, g_fin, tm, 1,
                            HIST, load_hist, store_hist, write_y)
    reset_row = jnp.where(t == 0, 0, -1)
    mixer = _prompt_mixer_items(x_ref, h_mid_ref, n1_ref, z_ref, xa_ref, xc_ref, xcb_ref, g_ref, a_ref, u_ref,
                                hs_ref, carry_ref, sp_ref, vn_ref, ymix_ref, g_mix, w_in, caw, cab, w_gate,
                                ba, bx, apar, g_oa, lng, lnb, sgw, sgb, g_ob, w_out, tm, reset_row, embed)
    _run(mixer)

    @pl.when(first)
    def _():
        for c in copies:
            c.wait()

    _run(ffn)

    ht_ref[0] = carry_ref[0:1, :]
    cnew_ref[0] = xa_ref[HIST + tm - (CONV_A - 1):HIST + tm, :]


def _sample_kernel(x_ref, p_ref, h0_ref, chist_ref, fh_ref, vecs, w_in, w_gate, w_out, w_up_hbm, w_down_hbm,
                   w_pg, w_ple,
                   y_ref, ht_ref, cnew_ref, vn_out_ref, fnew_ref,
                   n_ref, z_ref, xa_ref, xc_ref, xcb_ref, g_ref, a_ref, u_ref, hs_ref, ymix_ref,
                   h_ref, p_buf, up_ref, upb_ref, act_ref, actb_ref, acc_ref, w_up, w_down, wsem,
                   *, nb, steps, layout):
    v = {name: _Vec(vecs, *spec) for name, spec in layout.items()}
    g_mix, caw, cab, ba, bx, apar = v['g_mix'], v['caw'], v['cab'], v['ba'], v['bx'], v['apar']
    g_oa, lng, lnb, sgw, sgb, g_ob = v['g_oa'], v['lng'], v['lnb'], v['sgw'], v['sgb'], v['g_ob']
    g_ffn, fcw, fcb, g_ple, g_fin = v['g_ffn'], v['fcw'], v['fcb'], v['g_ple'], v['g_fin']
    rows = nb * steps
    ahist = (CONV_A - 1) * nb
    fhist = (CONV_F - 1) * nb
    first = pl.program_id(0) == 0
    copies = _ffn_weight_copies((w_up_hbm, w_down_hbm), (w_up, w_down), wsem)

    @pl.when(first)
    def _():
        for c in copies:
            c.start()

    for s in range(steps):
        h_ref[s * nb:(s + 1) * nb, :] = x_ref[:, s, :]
        p_buf[s * nb:(s + 1) * nb, :] = p_ref[:, s, :]
    _norm_to_bf16(h_ref, g_mix, n_ref, rows)
    for k in range(CONV_A - 1):
        xa_ref[k * nb:(k + 1) * nb, :] = chist_ref[:, k, :]
    xa_ref[ahist:ahist + rows, :] = _wdot(n_ref[...], w_in[:, 0:W_A])
    z_ref[...] = _wdot(n_ref[...], w_in[:, W_A:])

    for r0 in range(0, rows, SLAB):
        cv = _bc(cab[...], SLAB)
        for j in range(CONV_A):
            off = r0 + j * nb
            cv = cv + xa_ref[off:off + SLAB, :] * _bc(caw[j], SLAB)
        xc_ref[r0:r0 + SLAB, :] = cv
        xcb_ref[r0:r0 + SLAB, :] = cv.astype(BF16)
    for k in range(CONV_A - 1):
        cnew_ref[:, k, :] = xa_ref[rows + k * nb:rows + (k + 1) * nb, :]

    _gate_stage(xc_ref, xcb_ref, g_ref, a_ref, u_ref, w_gate, ba, bx, apar, rows)

    for b0 in range(0, nb, SLAB):
        h = h0_ref[b0:b0 + SLAB, :]
        for s in range(steps):
            r0 = s * nb + b0
            h = a_ref[r0:r0 + SLAB, :] * h + u_ref[r0:r0 + SLAB, :]
            hs_ref[r0:r0 + SLAB, :] = h
        ht_ref[b0:b0 + SLAB, :] = h

    _branch_a_out(hs_ref, z_ref, g_oa, ymix_ref, rows)

    def write_vn(r0, vn):
        vn_out_ref[r0 % nb:r0 % nb + SLAB, r0 // nb, :] = vn

    _vn_stage(z_ref, lng, lnb, rows, write_vn)

    gob = g_ob[...]
    for b0 in range(0, nb, SLAB):
        for tt in range(steps):
            r0 = tt * nb + b0
            mixed = _bc(sgb[tt], SLAB)
            for s in range(tt + 1):
                k = tt * steps + s
                mixed = mixed + vn_out_ref[b0:b0 + SLAB, s, :] * _bc(sgw[k], SLAB)
            yb = jax.nn.gelu(z_ref[r0:r0 + SLAB, W_A:W_A + W_B]) * mixed
            ymix_ref[r0:r0 + SLAB, W_A:] = _rms(yb, gob).astype(BF16)

    h_ref[...] += _wdot(ymix_ref[...], w_out[...])

    @pl.when(first)
    def _():
        for c in copies:
            c.wait()

    def load_hist(c, up):
        for k in range(CONV_F - 1):
            for half in range(2):
                cs = slice(half * FF_CHUNK, (half + 1) * FF_CHUNK)
                up[k * nb:(k + 1) * nb, cs] = fh_ref[:, k, _ff_cols(c, half)]

    def store_hist(c, up):
        for k in range(CONV_F - 1):
            for half in range(2):
                cs = slice(half * FF_CHUNK, (half + 1) * FF_CHUNK)
                fnew_ref[:, k, _ff_cols(c, half)] = up[rows + k * nb:rows + (k + 1) * nb, cs]

    def write_y(r0, y):
        y_ref[r0 % nb:r0 % nb + SLAB, r0 // nb, :] = y

    embed, items = _ffn_items(h_ref, n_ref, (up_ref, upb_ref), (act_ref, actb_ref), acc_ref, z_ref, g_ref,
                              lambda: p_buf[...], g_ffn, w_up, fcw, fcb, w_down, g_ple, w_pg, w_ple, g_fin,
                              rows, nb, fhist, load_hist, store_hist, write_y)
    _run([embed] + items)


def _ffn_weight_copies(srcs, dsts, sem):
    return [pltpu.make_async_copy(src, dst, sem.at[i]) for i, (src, dst) in enumerate(zip(srcs, dsts))]


def _const_spec(shape):
    zeros = (0,) * len(shape)
    return pl.BlockSpec(shape, lambda *_: zeros, pipeline_mode=pl.Buffered(1))


def _pack_params(weights, named):
    flat = [w.reshape((-1, w.shape[-1])) for w in weights]
    for w, w2 in zip(weights, flat):
        assert w.shape[-2] % 2 == 0 and w2.shape[0] % (PACK_STEPS * 2 * SUBLANES) == 0
    layout, total = _vector_layout(named)
    vecs = [a.reshape((-1, a.shape[-1])) for _, a in named]
    outs = pl.pallas_call(
        functools.partial(_pack_kernel, n_w=len(flat), n_v=len(vecs)),
        grid=(PACK_STEPS,),
        in_specs=[pl.BlockSpec((w2.shape[0] // PACK_STEPS, w2.shape[1]), lambda i: (i, 0)) for w2 in flat]
                 + [pl.BlockSpec(a.shape, lambda i: (0, 0)) for a in vecs],
        out_specs=[pl.BlockSpec((w2.shape[0] // PACK_STEPS // 2, w2.shape[1]), lambda i: (i, 0)) for w2 in flat]
                  + [pl.BlockSpec((SUBLANES, total), lambda i: (0, 0))],
        out_shape=[jax.ShapeDtypeStruct((w2.shape[0] // 2, w2.shape[1]), jnp.uint32) for w2 in flat]
                  + [jax.ShapeDtypeStruct((SUBLANES, total), F32)],
        compiler_params=pltpu.CompilerParams(dimension_semantics=("arbitrary",),
                                             vmem_limit_bytes=VMEM_LIMIT_BYTES),
        name="pack_weights",
    )(*flat, *vecs)
    words = [o.reshape(w.shape[:-2] + (w.shape[-2] // 2, w.shape[-1])) for o, w in zip(outs[:-1], weights)]
    return words, outs[-1], layout


def _pack_kernel(*refs, n_w, n_v):
    w_in, v_in = refs[:n_w], refs[n_w:n_w + n_v]
    w_out, v_out = refs[n_w + n_v:2 * n_w + n_v], refs[2 * n_w + n_v]
    for w_ref, o_ref in zip(w_in, w_out):
        o_ref[...] = pltpu.bitcast(w_ref[...].astype(BF16), jnp.uint32)
    off = 0
    for v_ref in v_in:
        n, width = v_ref.shape
        for j in range(n):
            v_out[:, off:off + width] = jnp.broadcast_to(v_ref[j:j + 1, :], (SUBLANES, width))
            off += width


def _block_diag_gate(wa, wx):
    hh = H_A // 2
    eye = jnp.eye(hh, dtype=wa.dtype)

    def bd(w):
        return jnp.einsum('hij,hg->higj', w, eye).reshape(GATE_HALF, GATE_HALF)

    return jnp.stack([jnp.concatenate([bd(wa[h * hh:(h + 1) * hh]), bd(wx[h * hh:(h + 1) * hh])], axis=1)
                      for h in range(2)])


def kernel(x_prompt, x_sample, p_prompt, p_sample, state_rglru_h, state_rglru_conv, state_ffn_conv, g_mix_norm, w_in, conv_a_w, conv_a_b, lru_wa, lru_ba, lru_wx, lru_bx, lru_a_param, g_out_a, ln_v_g, ln_v_b, sgu_w, sgu_b, g_out_b, w_out, g_ffn_norm, w_up, ffn_conv_w, ffn_conv_b, w_down, g_ple_norm, w_ple_gate, w_ple, g_final):
    assert w_in.shape[0] == 1
    nbp, seq, _ = x_prompt.shape
    nbs, steps, _ = x_sample.shape
    ple = p_prompt.shape[-1]
    tm = PROMPT_BLOCK_ROWS
    nb = SAMPLE_GROUP
    assert seq % tm == 0 and tm % CHUNK == 0 and nbs % nb == 0 and nb % SLAB == 0 and steps <= CHUNK
    common = [('g_mix', g_mix_norm[0]), ('caw', conv_a_w[0]), ('cab', conv_a_b[0]), ('ba', lru_ba[0]),
              ('bx', lru_bx[0]), ('apar', lru_a_param[0]), ('g_oa', g_out_a[0]), ('lng', ln_v_g[0]),
              ('lnb', ln_v_b[0]), ('g_ob', g_out_b[0]), ('g_ffn', g_ffn_norm[0]), ('fcw', ffn_conv_w[0]),
              ('fcb', ffn_conv_b[0]), ('g_ple', g_ple_norm[0]), ('g_fin', g_final)]
    sgw_s = jnp.repeat(jnp.transpose(sgu_w[0, :, :steps, :steps], (1, 2, 0)).reshape(steps * steps, H_B),
                       HD_B, axis=1)
    sgb_s = jnp.repeat(jnp.transpose(sgu_b[0, :, :steps]), HD_B, axis=1)
    packed, vecs, layout = _pack_params(
        [w_in[0], _block_diag_gate(lru_wa[0], lru_wx[0]), w_out[0], w_up[0], w_down[0], w_ple_gate[0], w_ple[0]],
        common + [('sgw', sgw_s), ('sgb', sgb_s)])
    vecs_p = vecs_s = vecs
    layout_p = layout_s = layout
    mats, mats_tail = packed[:2], packed[2:]
    w_up_words, w_down_words = packed[3], packed[4]
    ffn_w_scratch = [pltpu.VMEM(w_up_words.shape, jnp.uint32), pltpu.VMEM(w_down_words.shape, jnp.uint32),
                     pltpu.SemaphoreType.DMA((2,))]

    def w_spec(a):
        if a is w_up_words or a is w_down_words:
            return pl.BlockSpec(memory_space=pltpu.HBM)
        return _const_spec(a.shape)

    sgb_p = jnp.repeat(jnp.transpose(sgu_b[0]), HD_B, axis=1)
    p_args = [vecs_p] + mats + [sgu_w[0], sgb_p] + mats_tail
    scratch_p = [
        pltpu.VMEM((tm, D_MODEL), F32),
        pltpu.VMEM((tm, D_MODEL), BF16),
        pltpu.VMEM((tm, D_MODEL + W_A), F32),
        pltpu.VMEM((HIST + tm, W_A), F32),
        pltpu.VMEM((tm, W_A), F32),
        pltpu.VMEM((tm, W_A), BF16),
        pltpu.VMEM((tm, 2 * W_A), F32),
        pltpu.VMEM((tm, W_A), F32),
        pltpu.VMEM((tm, W_A), F32),
        pltpu.VMEM((tm, W_A), F32),
        pltpu.VMEM((SUBLANES, W_A), F32),
        pltpu.VMEM((SUBLANES, W_A), F32),
        pltpu.VMEM((tm, W_B), BF16),
        pltpu.VMEM((tm, W_A + W_B), BF16),
        pltpu.VMEM((tm, D_MODEL), BF16),
        pltpu.VMEM((HIST + tm, 2 * FF_CHUNK), F32),
        pltpu.VMEM((HIST + tm, 2 * FF_CHUNK), F32),
        pltpu.VMEM((HIST, 2 * D_FF), F32),
        pltpu.VMEM((tm, FF_CHUNK), BF16),
        pltpu.VMEM((tm, FF_CHUNK), BF16),
        pltpu.VMEM((tm, D_MODEL), F32),
        pltpu.VMEM((tm, D_MODEL), F32),
        pltpu.VMEM((tm, D_MODEL), F32),
    ] + ffn_w_scratch
    y_p, ht_p, cnew_p, fnew_p = pl.pallas_call(
        functools.partial(_prompt_kernel, tm=tm, layout=layout_p),
        grid=(nbp, seq // tm),
        in_specs=[pl.BlockSpec((1, tm, D_MODEL), lambda b, t: (b, t, 0)),
                  pl.BlockSpec((1, tm, ple), lambda b, t: (b, t, 0))]
                 + [w_spec(a) for a in p_args],
        out_specs=[pl.BlockSpec((1, tm, D_MODEL), lambda b, t: (b, t, 0)),
                   pl.BlockSpec((1, 1, W_A), lambda b, t: (b, 0, 0)),
                   pl.BlockSpec((1, CONV_A - 1, W_A), lambda b, t: (b, 0, 0)),
                   pl.BlockSpec((1, CONV_F - 1, 2 * D_FF), lambda b, t: (b, 0, 0))],
        out_shape=[jax.ShapeDtypeStruct((nbp, seq, D_MODEL), F32),
                   jax.ShapeDtypeStruct((nbp, 1, W_A), F32),
                   jax.ShapeDtypeStruct((nbp, CONV_A - 1, W_A), F32),
                   jax.ShapeDtypeStruct((nbp, CONV_F - 1, 2 * D_FF), F32)],
        scratch_shapes=scratch_p,
        compiler_params=pltpu.CompilerParams(dimension_semantics=("arbitrary", "arbitrary"),
                                             vmem_limit_bytes=VMEM_LIMIT_BYTES),
        name="prompt_layer",
    )(x_prompt, p_prompt[0], *p_args)

    rows = nb * steps
    ahist = (CONV_A - 1) * nb
    fhist = (CONV_F - 1) * nb
    tmaj = lambda a: jnp.swapaxes(a, 0, 1)
    w_args = [vecs_s] + mats + mats_tail
    scratch_s = [
        pltpu.VMEM((rows, D_MODEL), BF16),
        pltpu.VMEM((rows, D_MODEL + W_A), F32),
        pltpu.VMEM((ahist + rows, W_A), F32),
        pltpu.VMEM((rows, W_A), F32),
        pltpu.VMEM((rows, W_A), BF16),
        pltpu.VMEM((rows, 2 * W_A), F32),
        pltpu.VMEM((rows, W_A), F32),
        pltpu.VMEM((rows, W_A), F32),
        pltpu.VMEM((rows, W_A), F32),
        pltpu.VMEM((rows, W_A + W_B), BF16),
        pltpu.VMEM((rows, D_MODEL), F32),
        pltpu.VMEM((rows, ple), F32),
        pltpu.VMEM((fhist + rows, 2 * FF_CHUNK), F32),
        pltpu.VMEM((fhist + rows, 2 * FF_CHUNK), F32),
        pltpu.VMEM((rows, FF_CHUNK), BF16),
        pltpu.VMEM((rows, FF_CHUNK), BF16),
        pltpu.VMEM((rows, D_MODEL), F32),
    ] + ffn_w_scratch
    y_s, ht_s, cnew_s, vn_s, fnew_s = pl.pallas_call(
        functools.partial(_sample_kernel, nb=nb, steps=steps, layout=layout_s),
        grid=(nbs // nb,),
        in_specs=[pl.BlockSpec((nb, steps, D_MODEL), lambda i: (i, 0, 0)),
                  pl.BlockSpec((nb, steps, ple), lambda i: (i, 0, 0)),
                  pl.BlockSpec((nb, W_A), lambda i: (i, 0)),
                  pl.BlockSpec((nb, CONV_A - 1, W_A), lambda i: (i, 0, 0)),
                  pl.BlockSpec((nb, CONV_F - 1, 2 * D_FF), lambda i: (i, 0, 0))]
                 + [w_spec(a) for a in w_args],
        out_specs=[pl.BlockSpec((nb, steps, D_MODEL), lambda i: (i, 0, 0)),
                   pl.BlockSpec((nb, W_A), lambda i: (i, 0)),
                   pl.BlockSpec((nb, CONV_A - 1, W_A), lambda i: (i, 0, 0)),
                   pl.BlockSpec((nb, steps, W_B), lambda i: (i, 0, 0)),
                   pl.BlockSpec((nb, CONV_F - 1, 2 * D_FF), lambda i: (i, 0, 0))],
        out_shape=[jax.ShapeDtypeStruct((nbs, steps, D_MODEL), F32),
                   jax.ShapeDtypeStruct((nbs, W_A), F32),
                   jax.ShapeDtypeStruct((nbs, CONV_A - 1, W_A), F32),
                   jax.ShapeDtypeStruct((nbs, steps, W_B), F32),
                   jax.ShapeDtypeStruct((nbs, CONV_F - 1, 2 * D_FF), F32)],
        scratch_shapes=scratch_s,
        compiler_params=pltpu.CompilerParams(dimension_semantics=("arbitrary",),
                                             vmem_limit_bytes=VMEM_LIMIT_BYTES),
        name="sample_layer",
    )(x_sample, p_sample[0], state_rglru_h[0], state_rglru_conv[0], state_ffn_conv[0], *w_args)

    return (y_p, y_s, tmaj(ht_p), ht_s[None], cnew_p[None], cnew_s[None], vn_s[None], fnew_p[None], fnew_s[None])
```

```python
import functools

import jax
import jax.numpy as jnp
from jax import lax
from jax.experimental import pallas as pl
from jax.experimental.pallas import tpu as pltpu

F32 = jnp.float32
BF16 = jnp.bfloat16

D_MODEL = 1024
W_A = 512
W_B = 512
H_A = 8
BW_A = W_A // H_A
H_B = 4
HD_B = W_B // H_B
CHUNK = 128
D_FF = 3072
CONV_A = 4
CONV_F = 3
C_RG = 8.0
EPS = 1e-6

SUBLANES = 8
HIST = SUBLANES
GATE_HALF = W_A // 2
FF_CHUNK = 512
N_FF_CHUNKS = D_FF // FF_CHUNK
SLAB = 32
MXU_PIECE = 256
PACK_STEPS = 8
PROMPT_BLOCK_ROWS = 256
SAMPLE_GROUP = 32
VMEM_LIMIT_BYTES = 56 * 1024 * 1024


def _bc(tile, rows):
    return jnp.concatenate([tile] * (rows // SUBLANES), axis=0)


class _Vec:
    def __init__(self, ref, off, width, n):
        self.ref, self.off, self.width, self.n = ref, off, width, n

    def __getitem__(self, idx):
        full = slice(0, self.width)
        if self.n is None:
            j, cols = 0, (full if idx is Ellipsis else idx[1])
        else:
            j, cols = (idx, full) if isinstance(idx, int) else (idx[0], idx[2])
        start = self.off + j * self.width
        return self.ref[:, start + cols.start:start + cols.stop]


def _vector_layout(named):
    layout, off = {}, 0
    for name, a in named:
        n, width = (None, a.shape[0]) if a.ndim == 1 else a.shape
        layout[name] = (off, width, n)
        off += a.size
    return layout, off


def _rms(x, g):
    ms = jnp.mean(x * x, axis=-1, keepdims=True)
    return x * lax.rsqrt(ms + EPS) * _bc(g, x.shape[0])


def _layernorm(x, g, b):
    mu = jnp.mean(x, axis=-1, keepdims=True)
    xc = x - mu
    rows = x.shape[0]
    return xc * lax.rsqrt(jnp.mean(xc * xc, axis=-1, keepdims=True) + EPS) * _bc(g, rows) + _bc(b, rows)


def _sigmoid(x):
    return 1.0 / (1.0 + jnp.exp(-x))


def _softplus(x):
    return jnp.maximum(x, 0.0) + jnp.log(1.0 + jnp.exp(-jnp.abs(x)))


def _dot(a, b):
    return jnp.dot(a, b, preferred_element_type=F32)


def _fresh_rows(ref, salt, cols=slice(None)):
    zero = jnp.minimum(pl.program_id(0), 0) * salt
    return ref[pl.ds(pl.multiple_of(zero, 16), ref.shape[0]), cols]


def _wdot(a, w_words):
    return _dot(a, pltpu.bitcast(w_words, BF16))


def _ff_cols(c, half):
    start = half * D_FF + c * FF_CHUNK
    return slice(start, start + FF_CHUNK)


def _norm_to_bf16(src_ref, g_ref, dst_ref, rows):
    g = g_ref[...]
    for r0 in range(0, rows, SLAB):
        dst_ref[r0:r0 + SLAB, :] = _rms(src_ref[r0:r0 + SLAB, :], g).astype(BF16)


def _gate_stage(xc_ref, xcb_ref, g_ref, a_ref, u_ref, w_gate, ba, bx, apar, rows):
    for hf in range(2):
        c0 = hf * GATE_HALF
        g_ref[:, 2 * c0:2 * c0 + 2 * GATE_HALF] = _wdot(xcb_ref[:, c0:c0 + GATE_HALF], w_gate[hf])
    sp = _softplus(-apar[...])
    for r0 in range(0, rows, SLAB):
        for hf in range(2):
            c0 = hf * GATE_HALF
            cs = slice(c0, c0 + GATE_HALF)
            r = _sigmoid(g_ref[r0:r0 + SLAB, 2 * c0:2 * c0 + GATE_HALF] + _bc(ba[:, cs], SLAB))
            i = _sigmoid(g_ref[r0:r0 + SLAB, 2 * c0 + GATE_HALF:2 * c0 + 2 * GATE_HALF] + _bc(bx[:, cs], SLAB))
            a = jnp.exp((-C_RG) * r * _bc(sp[:, cs], SLAB))
            mult = jnp.sqrt(1.0 - a * a)
            a_ref[r0:r0 + SLAB, cs] = a
            u_ref[r0:r0 + SLAB, cs] = xc_ref[r0:r0 + SLAB, cs] * i * mult


def _branch_a_out(hs_ref, z_ref, g_oa, ymix_ref, rows):
    g = g_oa[...]
    for r0 in range(0, rows, SLAB):
        ga = z_ref[r0:r0 + SLAB, 0:W_A]
        ymix_ref[r0:r0 + SLAB, 0:W_A] = _rms(hs_ref[r0:r0 + SLAB, :] * jax.nn.gelu(ga), g).astype(BF16)


def _vn_stage(z_ref, lng, lnb, rows, write):
    g = lng[...]
    b = lnb[...]
    for r0 in range(0, rows, SLAB):
        vb = z_ref[r0:r0 + SLAB, W_A + W_B:W_A + 2 * W_B]
        write(r0, _layernorm(jax.nn.gelu(vb), g, b))


def _ffn_items(h_ref, n_ref, up2_ref, act2_ref, acc_ref, gz_ref, pe_ref, load_p, g_ffn, w_up, fcw, fcb,
               w_down, g_ple, w_pg, w_ple, g_fin, rows, shift, hist_rows, load_hist, store_hist, write_y):
    n_up = 2 * FF_CHUNK // MXU_PIECE
    n_down = D_MODEL // MXU_PIECE
    slabs = list(range(0, rows, SLAB))
    items = []

    def norm_in(r0):
        h = h_ref[r0:r0 + SLAB, :]
        acc_ref[r0:r0 + SLAB, :] = h
        n_ref[r0:r0 + SLAB, :] = _rms(h, g_ffn[...]).astype(BF16)

    def up_piece(c, j):
        up_ref = up2_ref[c % 2]
        cs = slice(j * MXU_PIECE, (j + 1) * MXU_PIECE)
        half, off = divmod(j * MXU_PIECE, FF_CHUNK)
        w0 = half * D_FF + c * FF_CHUNK + off
        if j == 0:
            load_hist(c, up_ref)
        lhs = _fresh_rows(n_ref, 1 + c * n_up + j)
        up_ref[hist_rows:hist_rows + rows, cs] = _wdot(lhs, w_up[:, w0:w0 + MXU_PIECE])

    def down_piece(c, j):
        cs = slice(j * MXU_PIECE, (j + 1) * MXU_PIECE)
        lhs = _fresh_rows(act2_ref[c % 2], 1 + c * n_down + j)
        acc_ref[:, cs] += _wdot(lhs, w_down[c * FF_CHUNK // 2:(c + 1) * FF_CHUNK // 2, cs])

    def conv_act_slab(c, r0, last):
        up_ref = up2_ref[c % 2]
        halves = []
        for half in range(2):
            cs = slice(half * FF_CHUNK, (half + 1) * FF_CHUNK)
            ws = _ff_cols(c, half)
            cv = _bc(fcb[:, ws], SLAB)
            for j in range(CONV_F):
                off = hist_rows + r0 - (CONV_F - 1 - j) * shift
                cv = cv + up_ref[off:off + SLAB, cs] * _bc(fcw[j, :, ws], SLAB)
            halves.append(cv)
        act2_ref[c % 2][r0:r0 + SLAB, :] = (jax.nn.gelu(halves[0]) * halves[1]).astype(BF16)
        if last:
            store_hist(c, up_ref)

    def embed():
        pe_ref[...] = _wdot(load_p().astype(BF16), w_ple[...])

    def norm_mid(r0):
        n_ref[r0:r0 + SLAB, :] = _rms(acc_ref[r0:r0 + SLAB, :], g_ple[...]).astype(BF16)

    def gate_piece(j):
        cs = slice(j * MXU_PIECE, (j + 1) * MXU_PIECE)
        gz_ref[:, cs] = _wdot(_fresh_rows(n_ref, 1 + N_FF_CHUNKS * n_up + j), w_pg[:, cs])

    def finish(r0):
        gate = _sigmoid(gz_ref[r0:r0 + SLAB, 0:D_MODEL])
        h3 = acc_ref[r0:r0 + SLAB, :] + pe_ref[r0:r0 + SLAB, :] * gate
        write_y(r0, _rms(h3, g_fin[...]))

    items += [functools.partial(norm_in, r0) for r0 in slabs]
    items += [functools.partial(up_piece, 0, j) for j in range(n_up)]
    for c in range(N_FF_CHUNKS + 1):
        pieces = []
        for j in range(max(n_up, n_down)):
            if c + 1 < N_FF_CHUNKS and j < n_up:
                pieces.append(functools.partial(up_piece, c + 1, j))
            if c >= 1 and j < n_down:
                pieces.append(functools.partial(down_piece, c - 1, j))
        if c < N_FF_CHUNKS:
            done = 0
            for i, r0 in enumerate(slabs):
                items.append(functools.partial(conv_act_slab, c, r0, i == len(slabs) - 1))
                upto = (i + 1) * len(pieces) // len(slabs)
                items += pieces[done:upto]
                done = upto
        else:
            items += pieces
    items += [functools.partial(norm_mid, r0) for r0 in slabs]
    items += [functools.partial(gate_piece, j) for j in range(n_down)]
    items += [functools.partial(finish, r0) for r0 in slabs]
    return embed, items


def _run(items):
    for item in items:
        item()


def _prompt_mixer_items(x_ref, h_mid_ref, n_ref, z_ref, xa_ref, xc_ref, xcb_ref, g_ref, a_ref, u_ref, hs_ref,
                        carry_ref, sp_ref, vn_ref, ymix_ref, g_mix, w_in, caw, cab, w_gate, ba, bx, apar,
                        g_oa, lng, lnb, sgw, sgb, g_ob, w_out, tm, reset_row, extra_matmul):
    assert tm % CHUNK == 0 and tm % SLAB == 0
    slabs = list(range(0, tm, SLAB))

    def norm_in(r0):
        n_ref[r0:r0 + SLAB, :] = _rms(x_ref[0, r0:r0 + SLAB, :], g_mix[...]).astype(BF16)

    def in_piece(j):
        c0 = j * MXU_PIECE
        res = _wdot(_fresh_rows(n_ref, 1 + j), w_in[:, c0:c0 + MXU_PIECE])
        if c0 < W_A:
            xa_ref[HIST:HIST + tm, c0:c0 + MXU_PIECE] = res
        else:
            z_ref[:, c0 - W_A:c0 - W_A + MXU_PIECE] = res

    def conv(r0):
        if r0 == 0:
            sp_ref[...] = _softplus(-apar[...])
        cv = _bc(cab[...], SLAB)
        for j in range(CONV_A):
            off = HIST + r0 - (CONV_A - 1 - j)
            cv = cv + xa_ref[off:off + SLAB, :] * _bc(caw[j], SLAB)
        xc_ref[r0:r0 + SLAB, :] = cv
        xcb_ref[r0:r0 + SLAB, :] = cv.astype(BF16)
        if r0 + SLAB == tm:
            xa_ref[0:HIST, :] = xa_ref[tm:tm + HIST, :]

    def gate_piece(hf):
        c0 = hf * GATE_HALF
        lhs = _fresh_rows(xcb_ref, 1 + hf, slice(c0, c0 + GATE_HALF))
        g_ref[:, 2 * c0:2 * c0 + 2 * GATE_HALF] = _wdot(lhs, w_gate[hf])

    def gate_ew(r0, hf):
        c0 = hf * GATE_HALF
        cs = slice(c0, c0 + GATE_HALF)
        r = _sigmoid(g_ref[r0:r0 + SLAB, 2 * c0:2 * c0 + GATE_HALF] + _bc(ba[:, cs], SLAB))
        i = _sigmoid(g_ref[r0:r0 + SLAB, 2 * c0 + GATE_HALF:2 * c0 + 2 * GATE_HALF] + _bc(bx[:, cs], SLAB))
        a = jnp.exp((-C_RG) * r * _bc(sp_ref[:, cs], SLAB))
        mult = jnp.sqrt(1.0 - a * a)
        if r0 == 0:
            row = lax.broadcasted_iota(jnp.int32, (SLAB, GATE_HALF), 0)
            mult = jnp.where(row == reset_row, 1.0, mult)
        a_ref[r0:r0 + SLAB, cs] = a
        u_ref[r0:r0 + SLAB, cs] = xc_ref[r0:r0 + SLAB, cs] * i * mult

    def scan_group(r0):
        row = lax.broadcasted_iota(jnp.int32, (SUBLANES, W_A), 0)
        a = a_ref[r0:r0 + SUBLANES, :]
        u = u_ref[r0:r0 + SUBLANES, :]
        for d in (1, 2, 4):
            keep = row >= d
            a_s = jnp.where(keep, pltpu.roll(a, d, 0), 1.0)
            u_s = jnp.where(keep, pltpu.roll(u, d, 0), 0.0)
            u = a * u_s + u
            a = a * a_s
        h = a * carry_ref[...] + u
        hs_ref[r0:r0 + SUBLANES, :] = h
        carry_ref[...] = jnp.broadcast_to(h[SUBLANES - 1:SUBLANES, :], (SUBLANES, W_A))

    def a_out(r0):
        ga = z_ref[r0:r0 + SLAB, 0:W_A]
        ymix_ref[r0:r0 + SLAB, 0:W_A] = _rms(hs_ref[r0:r0 + SLAB, :] * jax.nn.gelu(ga), g_oa[...]).astype(BF16)

    def vn(r0):
        vb = z_ref[r0:r0 + SLAB, W_A + W_B:W_A + 2 * W_B]
        vn_ref[r0:r0 + SLAB, :] = _layernorm(jax.nn.gelu(vb), lng[...], lnb[...]).astype(BF16)

    def sgu(c0):
        ti = lax.broadcasted_iota(jnp.int32, (CHUNK, CHUNK), 0)
        si = lax.broadcasted_iota(jnp.int32, (CHUNK, CHUNK), 1)
        heads = []
        for hd in range(H_B):
            cs = slice(hd * HD_B, (hd + 1) * HD_B)
            wmix = jnp.where(si <= ti, sgw[hd], 0.0).astype(BF16)
            mixed = _dot(wmix, vn_ref[c0:c0 + CHUNK, cs]) + sgb[:, cs]
            ub = z_ref[c0:c0 + CHUNK, W_A + hd * HD_B:W_A + (hd + 1) * HD_B]
            heads.append(jax.nn.gelu(ub) * mixed)
        yb = jnp.concatenate(heads, axis=-1)
        ymix_ref[c0:c0 + CHUNK, W_A:] = _rms(yb, g_ob[...]).astype(BF16)

    def out_piece(j):
        cs = slice(j * MXU_PIECE, (j + 1) * MXU_PIECE)
        h_mid_ref[:, cs] = x_ref[0, :, cs] + _wdot(_fresh_rows(ymix_ref, 1 + j), w_out[:, cs])

    def spread(work, matmuls):
        out, done = [], 0
        for i, item in enumerate(work):
            out.append(item)
            upto = (i + 1) * len(matmuls) // len(work)
            out += matmuls[done:upto]
            done = upto
        return out

    n_in = (2 * W_A + 2 * W_B) // MXU_PIECE
    n_a = W_A // MXU_PIECE
    later_in = [functools.partial(in_piece, j) for j in reversed(range(n_a, n_in))]
    items = [functools.partial(norm_in, r0) for r0 in slabs]
    items += [functools.partial(in_piece, j) for j in range(n_a)]
    items += spread([functools.partial(conv, r0) for r0 in slabs], later_in[:2])
    items += [functools.partial(gate_piece, hf) for hf in range(2)] + [extra_matmul]
    items += spread([functools.partial(gate_ew, r0, hf) for r0 in slabs for hf in range(2)], later_in[2:])
    items += [functools.partial(scan_group, r0) for r0 in range(0, tm, SUBLANES)]
    items += [functools.partial(a_out, r0) for r0 in slabs]
    items += [functools.partial(vn, r0) for r0 in slabs]
    items += [functools.partial(sgu, c0) for c0 in range(0, tm, CHUNK)]
    items += [functools.partial(out_piece, j) for j in range(D_MODEL // MXU_PIECE)]
    return items


def _prompt_kernel(x_ref, p_ref, vecs, w_in, w_gate, sgw, sgb, w_out, w_up_hbm, w_down_hbm, w_pg, w_ple,
                   y_ref, ht_ref, cnew_ref, fnew_ref,
                   h_mid_ref, n1_ref, z_ref, xa_ref, xc_ref, xcb_ref, g_ref, a_ref, u_ref, hs_ref, carry_ref,
                   sp_ref, vn_ref, ymix_ref, n2_ref, up_ref, upb_ref, fhist_ref, act_ref, actb_ref, acc_ref,
                   gz_ref, pe_ref, w_up, w_down, wsem, *, tm, layout):
    v = {name: _Vec(vecs, *spec) for name, spec in layout.items()}
    g_mix, caw, cab, ba, bx, apar = v['g_mix'], v['caw'], v['cab'], v['ba'], v['bx'], v['apar']
    g_oa, lng, lnb, g_ob = v['g_oa'], v['lng'], v['lnb'], v['g_ob']
    g_ffn, fcw, fcb, g_ple, g_fin = v['g_ffn'], v['fcw'], v['fcb'], v['g_ple'], v['g_fin']
    t = pl.program_id(1)
    first = jnp.logical_and(pl.program_id(0) == 0, t == 0)
    copies = _ffn_weight_copies((w_up_hbm, w_down_hbm), (w_up, w_down), wsem)

    @pl.when(first)
    def _():
        for i, c in enumerate(copies):
            c.start(priority=i % 2)

    @pl.when(t == 0)
    def _():
        xa_ref[0:HIST, :] = jnp.zeros((HIST, W_A), F32)
        carry_ref[...] = jnp.zeros((SUBLANES, W_A), F32)
        fhist_ref[...] = jnp.zeros(fhist_ref.shape, F32)

    def load_hist(c, up):
        for half in range(2):
            up[0:HIST, half * FF_CHUNK:(half + 1) * FF_CHUNK] = fhist_ref[:, _ff_cols(c, half)]

    def store_hist(c, up):
        for half in range(2):
            cs = slice(half * FF_CHUNK, (half + 1) * FF_CHUNK)
            fhist_ref[:, _ff_cols(c, half)] = up[tm:tm + HIST, cs]
            fnew_ref[0, :, _ff_cols(c, half)] = up[HIST + tm - (CONV_F - 1):HIST + tm, cs]

    def write_y(r0, y):
        y_ref[0, r0:r0 + SLAB, :] = y

    embed, ffn = _ffn_items(h_mid_ref, n2_ref, (up_ref, upb_ref), (act_ref, actb_ref), acc_ref, gz_ref, pe_ref,
                            lambda: p_ref[0], g_ffn, w_up, fcw, fcb, w_down, g_ple, w_pg, w_ple, g_fin, tm, 1,
                            HIST, load_hist, store_hist, write_y)
    reset_row = jnp.where(t == 0, 0, -1)
    mixer = _prompt_mixer_items(x_ref, h_mid_ref, n1_ref, z_ref, xa_ref, xc_ref, xcb_ref, g_ref, a_ref, u_ref,
                                hs_ref, carry_ref, sp_ref, vn_ref, ymix_ref, g_mix, w_in, caw, cab, w_gate,
                                ba, bx, apar, g_oa, lng, lnb, sgw, sgb, g_ob, w_out, tm, reset_row, embed)
    _run(mixer)

    @pl.when(first)
    def _():
        for c in copies:
            c.wait()

    _run(ffn)

    ht_ref[0] = carry_ref[0:1, :]
    cnew_ref[0] = xa_ref[HIST + tm - (CONV_A - 1):HIST + tm, :]


def _sample_kernel(x_ref, p_ref, h0_ref, chist_ref, fh_ref, vecs, w_in, w_gate, w_out, w_up_hbm, w_down_hbm,
                   w_pg, w_ple,
                   y_ref, ht_ref, cnew_ref, vn_out_ref, fnew_ref,
                   n_ref, z_ref, xa_ref, xc_ref, xcb_ref, g_ref, a_ref, u_ref, hs_ref, ymix_ref,
                   h_ref, p_buf, up_ref, upb_ref, act_ref, actb_ref, acc_ref, w_up, w_down, wsem,
                   *, nb, steps, layout):
    v = {name: _Vec(vecs, *spec) for name, spec in layout.items()}
    g_mix, caw, cab, ba, bx, apar = v['g_mix'], v['caw'], v['cab'], v['ba'], v['bx'], v['apar']
    g_oa, lng, lnb, sgw, sgb, g_ob = v['g_oa'], v['lng'], v['lnb'], v['sgw'], v['sgb'], v['g_ob']
    g_ffn, fcw, fcb, g_ple, g_fin = v['g_ffn'], v['fcw'], v['fcb'], v['g_ple'], v['g_fin']
    rows = nb * steps
    ahist = (CONV_A - 1) * nb
    fhist = (CONV_F - 1) * nb
    first = pl.program_id(0) == 0
    copies = _ffn_weight_copies((w_up_hbm, w_down_hbm), (w_up, w_down), wsem)

    @pl.when(first)
    def _():
        for i, c in enumerate(copies):
            c.start(priority=i % 2)

    for s in range(steps):
        h_ref[s * nb:(s + 1) * nb, :] = x_ref[:, s, :]
        p_buf[s * nb:(s + 1) * nb, :] = p_ref[:, s, :]
    _norm_to_bf16(h_ref, g_mix, n_ref, rows)
    for k in range(CONV_A - 1):
        xa_ref[k * nb:(k + 1) * nb, :] = chist_ref[:, k, :]
    xa_ref[ahist:ahist + rows, :] = _wdot(n_ref[...], w_in[:, 0:W_A])
    z_ref[...] = _wdot(n_ref[...], w_in[:, W_A:])

    for r0 in range(0, rows, SLAB):
        cv = _bc(cab[...], SLAB)
        for j in range(CONV_A):
            off = r0 + j * nb
            cv = cv + xa_ref[off:off + SLAB, :] * _bc(caw[j], SLAB)
        xc_ref[r0:r0 + SLAB, :] = cv
        xcb_ref[r0:r0 + SLAB, :] = cv.astype(BF16)
    for k in range(CONV_A - 1):
        cnew_ref[:, k, :] = xa_ref[rows + k * nb:rows + (k + 1) * nb, :]

    _gate_stage(xc_ref, xcb_ref, g_ref, a_ref, u_ref, w_gate, ba, bx, apar, rows)

    for b0 in range(0, nb, SLAB):
        h = h0_ref[b0:b0 + SLAB, :]
        for s in range(steps):
            r0 = s * nb + b0
            h = a_ref[r0:r0 + SLAB, :] * h + u_ref[r0:r0 + SLAB, :]
            hs_ref[r0:r0 + SLAB, :] = h
        ht_ref[b0:b0 + SLAB, :] = h

    _branch_a_out(hs_ref, z_ref, g_oa, ymix_ref, rows)

    def write_vn(r0, vn):
        vn_out_ref[r0 % nb:r0 % nb + SLAB, r0 // nb, :] = vn

    _vn_stage(z_ref, lng, lnb, rows, write_vn)

    gob = g_ob[...]
    for b0 in range(0, nb, SLAB):
        for tt in range(steps):
            r0 = tt * nb + b0
            mixed = _bc(sgb[tt], SLAB)
            for s in range(tt + 1):
                k = tt * steps + s
                mixed = mixed + vn_out_ref[b0:b0 + SLAB, s, :] * _bc(sgw[k], SLAB)
            yb = jax.nn.gelu(z_ref[r0:r0 + SLAB, W_A:W_A + W_B]) * mixed
            ymix_ref[r0:r0 + SLAB, W_A:] = _rms(yb, gob).astype(BF16)

    h_ref[...] += _wdot(ymix_ref[...], w_out[...])

    @pl.when(first)
    def _():
        for c in copies:
            c.wait()

    def load_hist(c, up):
        for k in range(CONV_F - 1):
            for half in range(2):
                cs = slice(half * FF_CHUNK, (half + 1) * FF_CHUNK)
                up[k * nb:(k + 1) * nb, cs] = fh_ref[:, k, _ff_cols(c, half)]

    def store_hist(c, up):
        for k in range(CONV_F - 1):
            for half in range(2):
                cs = slice(half * FF_CHUNK, (half + 1) * FF_CHUNK)
                fnew_ref[:, k, _ff_cols(c, half)] = up[rows + k * nb:rows + (k + 1) * nb, cs]

    def write_y(r0, y):
        y_ref[r0 % nb:r0 % nb + SLAB, r0 // nb, :] = y

    embed, items = _ffn_items(h_ref, n_ref, (up_ref, upb_ref), (act_ref, actb_ref), acc_ref, z_ref, g_ref,
                              lambda: p_buf[...], g_ffn, w_up, fcw, fcb, w_down, g_ple, w_pg, w_ple, g_fin,
                              rows, nb, fhist, load_hist, store_hist, write_y)
    _run([embed] + items)


def _ffn_weight_copies(srcs, dsts, sem):
    return [pltpu.make_async_copy(src, dst, sem.at[i]) for i, (src, dst) in enumerate(zip(srcs, dsts))]


def _const_spec(shape):
    zeros = (0,) * len(shape)
    return pl.BlockSpec(shape, lambda *_: zeros, pipeline_mode=pl.Buffered(1))


def _pack_params(weights, named):
    flat = [w.reshape((-1, w.shape[-1])) for w in weights]
    for w, w2 in zip(weights, flat):
        assert w.shape[-2] % 2 == 0 and w2.shape[0] % (PACK_STEPS * 2 * SUBLANES) == 0
    layout, total = _vector_layout(named)
    vecs = [a.reshape((-1, a.shape[-1])) for _, a in named]
    outs = pl.pallas_call(
        functools.partial(_pack_kernel, n_w=len(flat), n_v=len(vecs)),
        grid=(PACK_STEPS,),
        in_specs=[pl.BlockSpec((w2.shape[0] // PACK_STEPS, w2.shape[1]), lambda i: (i, 0)) for w2 in flat]
                 + [pl.BlockSpec(a.shape, lambda i: (0, 0)) for a in vecs],
        out_specs=[pl.BlockSpec((w2.shape[0] // PACK_STEPS // 2, w2.shape[1]), lambda i: (i, 0)) for w2 in flat]
                  + [pl.BlockSpec((SUBLANES, total), lambda i: (0, 0))],
        out_shape=[jax.ShapeDtypeStruct((w2.shape[0] // 2, w2.shape[1]), jnp.uint32) for w2 in flat]
                  + [jax.ShapeDtypeStruct((SUBLANES, total), F32)],
        compiler_params=pltpu.CompilerParams(dimension_semantics=("arbitrary",),
                                             vmem_limit_bytes=VMEM_LIMIT_BYTES),
        name="pack_weights",
    )(*flat, *vecs)
    words = [o.reshape(w.shape[:-2] + (w.shape[-2] // 2, w.shape[-1])) for o, w in zip(outs[:-1], weights)]
    return words, outs[-1], layout


def _pack_kernel(*refs, n_w, n_v):
    w_in, v_in = refs[:n_w], refs[n_w:n_w + n_v]
    w_out, v_out = refs[n_w + n_v:2 * n_w + n_v], refs[2 * n_w + n_v]
    for w_ref, o_ref in zip(w_in, w_out):
        o_ref[...] = pltpu.bitcast(w_ref[...].astype(BF16), jnp.uint32)
    @pl.when(pl.program_id(0) == 0)
    def _():
        off = 0
        for v_ref in v_in:
            n, width = v_ref.shape
            for j in range(n):
                v_out[:, off:off + width] = jnp.broadcast_to(v_ref[j:j + 1, :], (SUBLANES, width))
                off += width


def _block_diag_gate(wa, wx):
    hh = H_A // 2
    eye = jnp.eye(hh, dtype=wa.dtype)

    def bd(w):
        return jnp.einsum('hij,hg->higj', w, eye).reshape(GATE_HALF, GATE_HALF)

    return jnp.stack([jnp.concatenate([bd(wa[h * hh:(h + 1) * hh]), bd(wx[h * hh:(h + 1) * hh])], axis=1)
                      for h in range(2)])


def kernel(x_prompt, x_sample, p_prompt, p_sample, state_rglru_h, state_rglru_conv, state_ffn_conv, g_mix_norm, w_in, conv_a_w, conv_a_b, lru_wa, lru_ba, lru_wx, lru_bx, lru_a_param, g_out_a, ln_v_g, ln_v_b, sgu_w, sgu_b, g_out_b, w_out, g_ffn_norm, w_up, ffn_conv_w, ffn_conv_b, w_down, g_ple_norm, w_ple_gate, w_ple, g_final):
    assert w_in.shape[0] == 1
    nbp, seq, _ = x_prompt.shape
    nbs, steps, _ = x_sample.shape
    ple = p_prompt.shape[-1]
    tm = PROMPT_BLOCK_ROWS
    nb = SAMPLE_GROUP
    assert seq % tm == 0 and tm % CHUNK == 0 and nbs % nb == 0 and nb % SLAB == 0 and steps <= CHUNK
    common = [('g_mix', g_mix_norm[0]), ('caw', conv_a_w[0]), ('cab', conv_a_b[0]), ('ba', lru_ba[0]),
              ('bx', lru_bx[0]), ('apar', lru_a_param[0]), ('g_oa', g_out_a[0]), ('lng', ln_v_g[0]),
              ('lnb', ln_v_b[0]), ('g_ob', g_out_b[0]), ('g_ffn', g_ffn_norm[0]), ('fcw', ffn_conv_w[0]),
              ('fcb', ffn_conv_b[0]), ('g_ple', g_ple_norm[0]), ('g_fin', g_final)]
    sgw_s = jnp.repeat(jnp.transpose(sgu_w[0, :, :steps, :steps], (1, 2, 0)).reshape(steps * steps, H_B),
                       HD_B, axis=1)
    sgb_s = jnp.repeat(jnp.transpose(sgu_b[0, :, :steps]), HD_B, axis=1)
    packed, vecs, layout = _pack_params(
        [w_in[0], _block_diag_gate(lru_wa[0], lru_wx[0]), w_out[0], w_up[0], w_down[0], w_ple_gate[0], w_ple[0]],
        common + [('sgw', sgw_s), ('sgb', sgb_s)])
    vecs_p = vecs_s = vecs
    layout_p = layout_s = layout
    mats, mats_tail = packed[:2], packed[2:]
    w_up_words, w_down_words = packed[3], packed[4]
    ffn_w_scratch = [pltpu.VMEM(w_up_words.shape, jnp.uint32), pltpu.VMEM(w_down_words.shape, jnp.uint32),
                     pltpu.SemaphoreType.DMA((2,))]

    def w_spec(a):
        if a is w_up_words or a is w_down_words:
            return pl.BlockSpec(memory_space=pltpu.HBM)
        return _const_spec(a.shape)

    sgb_p = jnp.repeat(jnp.transpose(sgu_b[0]), HD_B, axis=1)
    p_args = [vecs_p] + mats + [sgu_w[0], sgb_p] + mats_tail
    scratch_p = [
        pltpu.VMEM((tm, D_MODEL), F32),
        pltpu.VMEM((tm, D_MODEL), BF16),
        pltpu.VMEM((tm, D_MODEL + W_A), F32),
        pltpu.VMEM((HIST + tm, W_A), F32),
        pltpu.VMEM((tm, W_A), F32),
        pltpu.VMEM((tm, W_A), BF16),
        pltpu.VMEM((tm, 2 * W_A), F32),
        pltpu.VMEM((tm, W_A), F32),
        pltpu.VMEM((tm, W_A), F32),
        pltpu.VMEM((tm, W_A), F32),
        pltpu.VMEM((SUBLANES, W_A), F32),
        pltpu.VMEM((SUBLANES, W_A), F32),
        pltpu.VMEM((tm, W_B), BF16),
        pltpu.VMEM((tm, W_A + W_B), BF16),
        pltpu.VMEM((tm, D_MODEL), BF16),
        pltpu.VMEM((HIST + tm, 2 * FF_CHUNK), F32),
        pltpu.VMEM((HIST + tm, 2 * FF_CHUNK), F32),
        pltpu.VMEM((HIST, 2 * D_FF), F32),
        pltpu.VMEM((tm, FF_CHUNK), BF16),
        pltpu.VMEM((tm, FF_CHUNK), BF16),
        pltpu.VMEM((tm, D_MODEL), F32),
        pltpu.VMEM((tm, D_MODEL), F32),
        pltpu.VMEM((tm, D_MODEL), F32),
    ] + ffn_w_scratch
    y_p, ht_p, cnew_p, fnew_p = pl.pallas_call(
        functools.partial(_prompt_kernel, tm=tm, layout=layout_p),
        grid=(nbp, seq // tm),
        in_specs=[pl.BlockSpec((1, tm, D_MODEL), lambda b, t: (b, t, 0)),
                  pl.BlockSpec((1, tm, ple), lambda b, t: (b, t, 0))]
                 + [w_spec(a) for a in p_args],
        out_specs=[pl.BlockSpec((1, tm, D_MODEL), lambda b, t: (b, t, 0)),
                   pl.BlockSpec((1, 1, W_A), lambda b, t: (b, 0, 0)),
                   pl.BlockSpec((1, CONV_A - 1, W_A), lambda b, t: (b, 0, 0)),
                   pl.BlockSpec((1, CONV_F - 1, 2 * D_FF), lambda b, t: (b, 0, 0))],
        out_shape=[jax.ShapeDtypeStruct((nbp, seq, D_MODEL), F32),
                   jax.ShapeDtypeStruct((nbp, 1, W_A), F32),
                   jax.ShapeDtypeStruct((nbp, CONV_A - 1, W_A), F32),
                   jax.ShapeDtypeStruct((nbp, CONV_F - 1, 2 * D_FF), F32)],
        scratch_shapes=scratch_p,
        compiler_params=pltpu.CompilerParams(dimension_semantics=("arbitrary", "arbitrary"),
                                             vmem_limit_bytes=VMEM_LIMIT_BYTES),
        name="prompt_layer",
    )(x_prompt, p_prompt[0], *p_args)

    rows = nb * steps
    ahist = (CONV_A - 1) * nb
    fhist = (CONV_F - 1) * nb
    tmaj = lambda a: jnp.swapaxes(a, 0, 1)
    w_args = [vecs_s] + mats + mats_tail
    scratch_s = [
        pltpu.VMEM((rows, D_MODEL), BF16),
        pltpu.VMEM((rows, D_MODEL + W_A), F32),
        pltpu.VMEM((ahist + rows, W_A), F32),
        pltpu.VMEM((rows, W_A), F32),
        pltpu.VMEM((rows, W_A), BF16),
        pltpu.VMEM((rows, 2 * W_A), F32),
        pltpu.VMEM((rows, W_A), F32),
        pltpu.VMEM((rows, W_A), F32),
        pltpu.VMEM((rows, W_A), F32),
        pltpu.VMEM((rows, W_A + W_B), BF16),
        pltpu.VMEM((rows, D_MODEL), F32),
        pltpu.VMEM((rows, ple), F32),
        pltpu.VMEM((fhist + rows, 2 * FF_CHUNK), F32),
        pltpu.VMEM((fhist + rows, 2 * FF_CHUNK), F32),
        pltpu.VMEM((rows, FF_CHUNK), BF16),
        pltpu.VMEM((rows, FF_CHUNK), BF16),
        pltpu.VMEM((rows, D_MODEL), F32),
    ] + ffn_w_scratch
    y_s, ht_s, cnew_s, vn_s, fnew_s = pl.pallas_call(
        functools.partial(_sample_kernel, nb=nb, steps=steps, layout=layout_s),
        grid=(nbs // nb,),
        in_specs=[pl.BlockSpec((nb, steps, D_MODEL), lambda i: (i, 0, 0)),
                  pl.BlockSpec((nb, steps, ple), lambda i: (i, 0, 0)),
                  pl.BlockSpec((nb, W_A), lambda i: (i, 0)),
                  pl.BlockSpec((nb, CONV_A - 1, W_A), lambda i: (i, 0, 0)),
                  pl.BlockSpec((nb, CONV_F - 1, 2 * D_FF), lambda i: (i, 0, 0))]
                 + [w_spec(a) for a in w_args],
        out_specs=[pl.BlockSpec((nb, steps, D_MODEL), lambda i: (i, 0, 0)),
                   pl.BlockSpec((nb, W_A), lambda i: (i, 0)),
                   pl.BlockSpec((nb, CONV_A - 1, W_A), lambda i: (i, 0, 0)),
                   pl.BlockSpec((nb, steps, W_B), lambda i: (i, 0, 0)),
                   pl.BlockSpec((nb, CONV_F - 1, 2 * D_FF), lambda i: (i, 0, 0))],
        out_shape=[jax.ShapeDtypeStruct((nbs, steps, D_MODEL), F32),
                   jax.ShapeDtypeStruct((nbs, W_A), F32),
                   jax.ShapeDtypeStruct((nbs, CONV_A - 1, W_A), F32),
                   jax.ShapeDtypeStruct((nbs, steps, W_B), F32),
                   jax.ShapeDtypeStruct((nbs, CONV_F - 1, 2 * D_FF), F32)],
        scratch_shapes=scratch_s,
        compiler_params=pltpu.CompilerParams(dimension_semantics=("arbitrary",),
                                             vmem_limit_bytes=VMEM_LIMIT_BYTES),
        name="sample_layer",
    )(x_sample, p_sample[0], state_rglru_h[0], state_rglru_conv[0], state_ffn_conv[0], *w_args)

    return (y_p, y_s, tmaj(ht_p), ht_s[None], cnew_p[None], cnew_s[None], vn_s[None], fnew_p[None], fnew_s[None])
```
